```python
import jax, jax.numpy as jnp
from jax import lax
import numpy as np

D_MODEL = 4096
BATCH = 16
SEQ = 2048
DEPTH = 2

N_MIXERS = 2
EXPAND = 2
D_INNER = EXPAND * D_MODEL
CHUNK = 128
A_HEADS = 16
A_HEAD_DIM = D_INNER // A_HEADS
POOL_WINDOWS = (2, 4, 8, 16)
B_GROUPS = len(POOL_WINDOWS)
B_GROUP_DIM = D_INNER // B_GROUPS
N_A_LAYERS = (DEPTH + 1) // 2
N_B_LAYERS = DEPTH // 2
EPS = 1e-6

kernel_name = "hybrid_gmlp_pool_sandwich_trunk"


def rms_norm(x, g):
    x32 = x.astype(jnp.float32)
    y = x32 * lax.rsqrt(jnp.mean(x32 * x32, axis=-1, keepdims=True) + EPS)
    return (y * g.astype(jnp.float32)).astype(x.dtype)


def layer_norm(x, g, b):
    x32 = x.astype(jnp.float32)
    mu = jnp.mean(x32, axis=-1, keepdims=True)
    xc = x32 - mu
    y = xc * lax.rsqrt(jnp.mean(xc * xc, axis=-1, keepdims=True) + EPS)
    return (y * g.astype(jnp.float32) + b.astype(jnp.float32)).astype(x.dtype)


def mixer_a(h, w_in, ln_g, ln_b, w_s, b_s, w_out):
    bsz, s, _ = h.shape
    proj = h @ w_in
    u, v, z = jnp.split(proj, 3, axis=-1)
    u = jax.nn.gelu(u, approximate=False)
    v = layer_norm(jax.nn.gelu(v, approximate=False), ln_g, ln_b)
    v = v.reshape(bsz, s // CHUNK, CHUNK, A_HEADS, A_HEAD_DIM)
    w_causal = jnp.tril(w_s)
    sv = jnp.einsum('hts,bnshd->bnthd', w_causal, v) + b_s.T[:, :, None]
    sgu = u * sv.reshape(bsz, s, D_INNER)
    y = sgu * jax.nn.silu(z)
    return y @ w_out


def mixer_b(h, w_in, w_grp, b_grp, scale, w_out):
    bsz, s, _ = h.shape
    proj = h @ w_in
    p, z = jnp.split(proj, 2, axis=-1)
    p = p.reshape(bsz, s, B_GROUPS, B_GROUP_DIM)
    p32 = p.astype(jnp.float32)
    csum = jnp.cumsum(p32, axis=1)
    pos = jnp.arange(1, s + 1, dtype=jnp.float32)
    means = []
    for k, w in enumerate(POOL_WINDOWS):
        ck = csum[:, :, k]
        lagged = jnp.pad(ck, ((0, 0), (w, 0), (0, 0)))[:, :s]
        count = jnp.minimum(pos, float(w))[:, None]
        means.append((ck - lagged) / count)
    pooled = (jnp.stack(means, axis=2) - p32).astype(p.dtype)
    mixed = jnp.einsum('bsgc,gcd->bsgd', pooled, w_grp) + b_grp
    mixed = mixed.reshape(bsz, s, D_INNER) * scale
    y = mixed * jax.nn.silu(z)
    return y @ w_out


def _fwd_setup_inputs(seed: int = 0) -> dict:
    key = jax.random.key(seed)
    ks = jax.random.split(key, 16)
    f32 = jnp.float32
    nrm = lambda k, shape, sc: jax.random.normal(k, shape, f32) * sc
    x = nrm(ks[0], (BATCH, SEQ, D_MODEL), 1.0)
    pre_norm = 1.0 + nrm(ks[1], (DEPTH, D_MODEL), 0.02)
    post_norm = 1.0 + nrm(ks[2], (DEPTH, D_MODEL), 0.02)
    a_w_in = nrm(ks[3], (N_A_LAYERS, D_MODEL, 3 * D_INNER), D_MODEL ** -0.5)
    a_ln_g = 1.0 + nrm(ks[4], (N_A_LAYERS, D_INNER), 0.02)
    a_ln_b = nrm(ks[5], (N_A_LAYERS, D_INNER), 0.02)
    a_w_s = nrm(ks[6], (N_A_LAYERS, A_HEADS, CHUNK, CHUNK), CHUNK ** -0.5)
    a_b_s = 1.0 + nrm(ks[7], (N_A_LAYERS, A_HEADS, CHUNK), 0.01)
    a_w_out = nrm(ks[8], (N_A_LAYERS, D_INNER, D_MODEL), D_INNER ** -0.5)
    b_w_in = nrm(ks[9], (N_B_LAYERS, D_MODEL, 2 * D_INNER), D_MODEL ** -0.5)
    b_w_grp = nrm(ks[10], (N_B_LAYERS, B_GROUPS, B_GROUP_DIM, B_GROUP_DIM), B_GROUP_DIM ** -0.5)
    b_b_grp = nrm(ks[11], (N_B_LAYERS, B_GROUPS, B_GROUP_DIM), 0.02)
    b_scale = 1.0 + nrm(ks[12], (N_B_LAYERS, D_INNER), 0.02)
    b_w_out = nrm(ks[13], (N_B_LAYERS, D_INNER, D_MODEL), D_INNER ** -0.5)
    return {"x": x, "pre_norm": pre_norm, "post_norm": post_norm,
            "a_w_in": a_w_in, "a_ln_g": a_ln_g, "a_ln_b": a_ln_b,
            "a_w_s": a_w_s, "a_b_s": a_b_s, "a_w_out": a_w_out,
            "b_w_in": b_w_in, "b_w_grp": b_w_grp, "b_b_grp": b_b_grp,
            "b_scale": b_scale, "b_w_out": b_w_out}


def _fwd_reference(x, pre_norm, post_norm, a_w_in, a_ln_g, a_ln_b, a_w_s, a_b_s, a_w_out,
              b_w_in, b_w_grp, b_b_grp, b_scale, b_w_out):
    for i in range(DEPTH):
        h = rms_norm(x, pre_norm[i])
        j = i // N_MIXERS
        if i % N_MIXERS == 0:
            m = mixer_a(h, a_w_in[j], a_ln_g[j], a_ln_b[j], a_w_s[j], a_b_s[j], a_w_out[j])
        else:
            m = mixer_b(h, b_w_in[j], b_w_grp[j], b_b_grp[j], b_scale[j], b_w_out[j])
        x = x + rms_norm(m, post_norm[i])
    return x


import jax as _jax
import jax.numpy as _jnp

TWIN_FORMAT = 'train_step'
FWD_PARAMS = ['x', 'pre_norm', 'post_norm', 'a_w_in', 'a_ln_g', 'a_ln_b', 'a_w_s', 'a_b_s', 'a_w_out', 'b_w_in', 'b_w_grp', 'b_b_grp', 'b_scale', 'b_w_out']
TWIN_WEIGHTS = ['pre_norm', 'post_norm', 'a_w_in', 'a_ln_g', 'a_ln_b', 'a_w_s', 'a_b_s', 'a_w_out', 'b_w_in', 'b_w_grp', 'b_b_grp', 'b_scale', 'b_w_out']
TWIN_DIFF_INPUT = 'x'
TWIN_INPUTS = ['x', 'pre_norm', 'post_norm', 'a_w_in', 'a_ln_g', 'a_ln_b', 'a_w_s', 'a_b_s', 'a_w_out', 'b_w_in', 'b_w_grp', 'b_b_grp', 'b_scale', 'b_w_out', 'loss_target', 'm_pre_norm', 'm_post_norm', 'm_a_w_in', 'm_a_ln_g', 'm_a_ln_b', 'm_a_w_s', 'm_a_b_s', 'm_a_w_out', 'm_b_w_in', 'm_b_w_grp', 'm_b_b_grp', 'm_b_scale', 'm_b_w_out', 'v_pre_norm', 'v_post_norm', 'v_a_w_in', 'v_a_ln_g', 'v_a_ln_b', 'v_a_w_s', 'v_a_b_s', 'v_a_w_out', 'v_b_w_in', 'v_b_w_grp', 'v_b_b_grp', 'v_b_scale', 'v_b_w_out']
TWIN_OUTPUTS = ['loss', 'grad_x', 'grad_pre_norm', 'grad_post_norm', 'grad_a_w_in', 'grad_a_ln_g', 'grad_a_ln_b', 'grad_a_w_s', 'grad_a_b_s', 'grad_a_w_out', 'grad_b_w_in', 'grad_b_w_grp', 'grad_b_b_grp', 'grad_b_scale', 'grad_b_w_out', 'delta_pre_norm', 'delta_post_norm', 'delta_a_w_in', 'delta_a_ln_g', 'delta_a_ln_b', 'delta_a_w_s', 'delta_a_b_s', 'delta_a_w_out', 'delta_b_w_in', 'delta_b_w_grp', 'delta_b_b_grp', 'delta_b_scale', 'delta_b_w_out', 'new_m_pre_norm', 'new_m_post_norm', 'new_m_a_w_in', 'new_m_a_ln_g', 'new_m_a_ln_b', 'new_m_a_w_s', 'new_m_a_b_s', 'new_m_a_w_out', 'new_m_b_w_in', 'new_m_b_w_grp', 'new_m_b_b_grp', 'new_m_b_scale', 'new_m_b_w_out', 'new_v_pre_norm', 'new_v_post_norm', 'new_v_a_w_in', 'new_v_a_ln_g', 'new_v_a_ln_b', 'new_v_a_w_s', 'new_v_a_b_s', 'new_v_a_w_out', 'new_v_b_w_in', 'new_v_b_w_grp', 'new_v_b_b_grp', 'new_v_b_scale', 'new_v_b_w_out']
TWIN_LEAF_KINDS = {'loss': 'loss', 'grad_x': 'grad_x', 'grad_pre_norm': 'grad_w', 'grad_post_norm': 'grad_w', 'grad_a_w_in': 'grad_w', 'grad_a_ln_g': 'grad_w', 'grad_a_ln_b': 'grad_w', 'grad_a_w_s': 'grad_w', 'grad_a_b_s': 'grad_w', 'grad_a_w_out': 'grad_w', 'grad_b_w_in': 'grad_w', 'grad_b_w_grp': 'grad_w', 'grad_b_b_grp': 'grad_w', 'grad_b_scale': 'grad_w', 'grad_b_w_out': 'grad_w', 'delta_pre_norm': 'delta_w', 'delta_post_norm': 'delta_w', 'delta_a_w_in': 'delta_w', 'delta_a_ln_g': 'delta_w', 'delta_a_ln_b': 'delta_w', 'delta_a_w_s': 'delta_w', 'delta_a_b_s': 'delta_w', 'delta_a_w_out': 'delta_w', 'delta_b_w_in': 'delta_w', 'delta_b_w_grp': 'delta_w', 'delta_b_b_grp': 'delta_w', 'delta_b_scale': 'delta_w', 'delta_b_w_out': 'delta_w', 'new_m_pre_norm': 'new_m', 'new_m_post_norm': 'new_m', 'new_m_a_w_in': 'new_m', 'new_m_a_ln_g': 'new_m', 'new_m_a_ln_b': 'new_m', 'new_m_a_w_s': 'new_m', 'new_m_a_b_s': 'new_m', 'new_m_a_w_out': 'new_m', 'new_m_b_w_in': 'new_m', 'new_m_b_w_grp': 'new_m', 'new_m_b_b_grp': 'new_m', 'new_m_b_scale': 'new_m', 'new_m_b_w_out': 'new_m', 'new_v_pre_norm': 'new_v', 'new_v_post_norm': 'new_v', 'new_v_a_w_in': 'new_v', 'new_v_a_ln_g': 'new_v', 'new_v_a_ln_b': 'new_v', 'new_v_a_w_s': 'new_v', 'new_v_a_b_s': 'new_v', 'new_v_a_w_out': 'new_v', 'new_v_b_w_in': 'new_v', 'new_v_b_w_grp': 'new_v', 'new_v_b_b_grp': 'new_v', 'new_v_b_scale': 'new_v', 'new_v_b_w_out': 'new_v'}


def _forward(args):
    return _fwd_reference(*[args[k] for k in FWD_PARAMS])


def _output_shape():
    def fwd():
        inp = _fwd_setup_inputs(0)
        return _fwd_reference(*[inp[k] for k in FWD_PARAMS])
    out = _jax.eval_shape(fwd)
    return out.shape, out.dtype

N_MICROBATCH = 1
ADAM_LR = 0.001
ADAM_B1 = 0.9
ADAM_B2 = 0.999
ADAM_EPS = 1e-08
ADAM_WD = 0.01
ADAM_STEP = 10
PER_EXAMPLE_BATCH_AXIS = {'x': 0, 'loss_target': 0}
SHARED_INPUTS = []
_WEIGHT_DTYPES = {'pre_norm': _jnp.float32, 'post_norm': _jnp.float32, 'a_w_in': _jnp.float32, 'a_ln_g': _jnp.float32, 'a_ln_b': _jnp.float32, 'a_w_s': _jnp.float32, 'a_b_s': _jnp.float32, 'a_w_out': _jnp.float32, 'b_w_in': _jnp.float32, 'b_w_grp': _jnp.float32, 'b_b_grp': _jnp.float32, 'b_scale': _jnp.float32, 'b_w_out': _jnp.float32}
MOMENT_SCALE = {'pre_norm': 1.496756e-01, 'post_norm': 7.994510e+00, 'a_w_in': 7.250793e-02, 'a_ln_g': 4.391875e-02, 'a_ln_b': 4.261255e-02, 'a_w_s': 8.612835e-02, 'a_b_s': 1.270096e-01, 'a_w_out': 1.163928e-01, 'b_w_in': 5.920537e-02, 'b_w_grp': 6.185334e-02, 'b_b_grp': 2.048567e-01, 'b_scale': 6.118107e-02, 'b_w_out': 8.771039e-02}


def _to_microbatches(a, axis):
    t = _jnp.moveaxis(a, axis, 0)
    t = t.reshape((N_MICROBATCH, t.shape[0] // N_MICROBATCH) + t.shape[1:])
    return _jnp.moveaxis(t, 1, axis + 1)


def setup_inputs(seed: int = 0) -> dict:
    inp = _fwd_setup_inputs(seed)
    key = _jax.random.fold_in(_jax.random.key(seed), 7919)
    shape, _ = _output_shape()
    out = dict(inp)
    out["loss_target"] = _jax.random.normal(_jax.random.fold_in(key, 0), shape, _jnp.float32)
    for i, name in enumerate(TWIN_WEIGHTS):
        w = inp[name].astype(_jnp.float32)
        if MOMENT_SCALE is None:
            s = _jnp.sqrt(_jnp.mean(_jnp.square(w)) + 1e-30)
        else:
            s = MOMENT_SCALE[name]
        km, kv = _jax.random.split(_jax.random.fold_in(key, i + 1))
        out[name] = w
        out["m_" + name] = s * _jax.random.normal(km, w.shape, _jnp.float32)
        out["v_" + name] = (s * s) * _jax.random.uniform(kv, w.shape, _jnp.float32, 0.5, 1.5)
    if N_MICROBATCH > 1:
        for name, axis in PER_EXAMPLE_BATCH_AXIS.items():
            out[name] = _to_microbatches(out[name], axis)
    return {'x': out['x'], 'pre_norm': out['pre_norm'], 'post_norm': out['post_norm'], 'a_w_in': out['a_w_in'], 'a_ln_g': out['a_ln_g'], 'a_ln_b': out['a_ln_b'], 'a_w_s': out['a_w_s'], 'a_b_s': out['a_b_s'], 'a_w_out': out['a_w_out'], 'b_w_in': out['b_w_in'], 'b_w_grp': out['b_w_grp'], 'b_b_grp': out['b_b_grp'], 'b_scale': out['b_scale'], 'b_w_out': out['b_w_out'], 'loss_target': out['loss_target'], 'm_pre_norm': out['m_pre_norm'], 'm_post_norm': out['m_post_norm'], 'm_a_w_in': out['m_a_w_in'], 'm_a_ln_g': out['m_a_ln_g'], 'm_a_ln_b': out['m_a_ln_b'], 'm_a_w_s': out['m_a_w_s'], 'm_a_b_s': out['m_a_b_s'], 'm_a_w_out': out['m_a_w_out'], 'm_b_w_in': out['m_b_w_in'], 'm_b_w_grp': out['m_b_w_grp'], 'm_b_b_grp': out['m_b_b_grp'], 'm_b_scale': out['m_b_scale'], 'm_b_w_out': out['m_b_w_out'], 'v_pre_norm': out['v_pre_norm'], 'v_post_norm': out['v_post_norm'], 'v_a_w_in': out['v_a_w_in'], 'v_a_ln_g': out['v_a_ln_g'], 'v_a_ln_b': out['v_a_ln_b'], 'v_a_w_s': out['v_a_w_s'], 'v_a_b_s': out['v_a_b_s'], 'v_a_w_out': out['v_a_w_out'], 'v_b_w_in': out['v_b_w_in'], 'v_b_w_grp': out['v_b_w_grp'], 'v_b_b_grp': out['v_b_b_grp'], 'v_b_scale': out['v_b_scale'], 'v_b_w_out': out['v_b_w_out']}


def _loss(weights, diff, rest, loss_target):
    with _jax.named_scope("forward"):
        args = {**rest, TWIN_DIFF_INPUT: diff, **{k: w.astype(_WEIGHT_DTYPES[k]) for k, w in weights.items()}}
        y = _forward(args)
    with _jax.named_scope("loss_head"):
        err = _jnp.square(y.astype(_jnp.float32) - loss_target)
        return 0.5 * _jnp.sum(_jnp.mean(err, axis=-1)) if err.ndim else 0.5 * err


def _adamw(w, g, m, v):
    m = ADAM_B1 * m + (1.0 - ADAM_B1) * g
    v = ADAM_B2 * v + (1.0 - ADAM_B2) * _jnp.square(g)
    m_hat = m / (1.0 - ADAM_B1 ** ADAM_STEP)
    v_hat = v / (1.0 - ADAM_B2 ** ADAM_STEP)
    delta = -ADAM_LR * (m_hat / (_jnp.sqrt(v_hat) + ADAM_EPS) + ADAM_WD * w)
    return delta, m, v


def reference(x, pre_norm, post_norm, a_w_in, a_ln_g, a_ln_b, a_w_s, a_b_s, a_w_out, b_w_in, b_w_grp, b_b_grp, b_scale, b_w_out, loss_target, m_pre_norm, m_post_norm, m_a_w_in, m_a_ln_g, m_a_ln_b, m_a_w_s, m_a_b_s, m_a_w_out, m_b_w_in, m_b_w_grp, m_b_b_grp, m_b_scale, m_b_w_out, v_pre_norm, v_post_norm, v_a_w_in, v_a_ln_g, v_a_ln_b, v_a_w_s, v_a_b_s, v_a_w_out, v_b_w_in, v_b_w_grp, v_b_b_grp, v_b_scale, v_b_w_out):
    given = dict(x=x, pre_norm=pre_norm, post_norm=post_norm, a_w_in=a_w_in, a_ln_g=a_ln_g, a_ln_b=a_ln_b, a_w_s=a_w_s, a_b_s=a_b_s, a_w_out=a_w_out, b_w_in=b_w_in, b_w_grp=b_w_grp, b_b_grp=b_b_grp, b_scale=b_scale, b_w_out=b_w_out, loss_target=loss_target, m_pre_norm=m_pre_norm, m_post_norm=m_post_norm, m_a_w_in=m_a_w_in, m_a_ln_g=m_a_ln_g, m_a_ln_b=m_a_ln_b, m_a_w_s=m_a_w_s, m_a_b_s=m_a_b_s, m_a_w_out=m_a_w_out, m_b_w_in=m_b_w_in, m_b_w_grp=m_b_w_grp, m_b_b_grp=m_b_b_grp, m_b_scale=m_b_scale, m_b_w_out=m_b_w_out, v_pre_norm=v_pre_norm, v_post_norm=v_post_norm, v_a_w_in=v_a_w_in, v_a_ln_g=v_a_ln_g, v_a_ln_b=v_a_ln_b, v_a_w_s=v_a_w_s, v_a_b_s=v_a_b_s, v_a_w_out=v_a_w_out, v_b_w_in=v_b_w_in, v_b_w_grp=v_b_w_grp, v_b_b_grp=v_b_b_grp, v_b_scale=v_b_scale, v_b_w_out=v_b_w_out)
    weights = {n: given[n] for n in TWIN_WEIGHTS}
    shared = {n: given[n] for n in SHARED_INPUTS}
    per_example = {n: given[n] for n in ['x']}
    grad_fn = _jax.value_and_grad(_loss, argnums=(0, 1))

    def one_microbatch(ex, loss_target):
        ex = dict(ex)
        diff = ex.pop(TWIN_DIFF_INPUT)
        return grad_fn(weights, diff, {**shared, **ex}, loss_target)

    if N_MICROBATCH == 1:
        loss, (grad_w, grad_x) = one_microbatch(per_example, given["loss_target"])
    else:
        def body(carry, xs):
            loss_sum, grad_sum = carry
            l_k, (gw_k, gx_k) = one_microbatch(xs[0], xs[1])
            with _jax.named_scope("update"):
                return (loss_sum + l_k, _jax.tree.map(_jnp.add, grad_sum, gw_k)), gx_k

        init = (_jnp.zeros((), _jnp.float32), _jax.tree.map(_jnp.zeros_like, weights))
        (loss, grad_w), grad_x = _jax.lax.scan(body, init, (per_example, given["loss_target"]))
    with _jax.named_scope("update"):
        delta_w, new_m, new_v = {}, {}, {}
        for n in TWIN_WEIGHTS:
            delta_w[n], new_m[n], new_v[n] = _adamw(weights[n], grad_w[n], given["m_" + n], given["v_" + n])
    return (loss, grad_x, *[grad_w[n] for n in TWIN_WEIGHTS], *[delta_w[n] for n in TWIN_WEIGHTS],
            *[new_m[n] for n in TWIN_WEIGHTS], *[new_v[n] for n in TWIN_WEIGHTS])
```

```python
import functools

import jax
import jax.numpy as jnp
from jax import lax
from jax.experimental import pallas as pl
from jax.experimental.pallas import tpu as pltpu

F32 = jnp.float32
BF16 = jnp.bfloat16
NORM_EPS = 1e-6
POOL_WINDOWS = (2, 4, 8, 16)
ADAM_LR = 0.001
ADAM_B1 = 0.9
ADAM_B2 = 0.999
ADAM_EPS = 1e-08
ADAM_WD = 0.01
ADAM_STEP = 10
N_CHIPS = 4
N_DEV = 8
V7X_VMEM_LIMIT_BYTES = 56 * 1024 * 1024
LANES = 128
MESH = pl.DeviceIdType.MESH
ANY = pl.BlockSpec(memory_space=pl.ANY)
SQRT_HALF = 0.7071067811865476
INV_SQRT_2PI = 0.3989422804014327


def _tile(dim, pref, mult=LANES):
    if dim <= pref:
        return dim
    t = (pref // mult) * mult
    while t >= mult:
        if dim % t == 0:
            return t
        t -= mult
    return dim


def _params(*sem):
    return pltpu.CompilerParams(dimension_semantics=sem or None, vmem_limit_bytes=V7X_VMEM_LIMIT_BYTES)


def _gelu(x):
    cdf = 0.5 * (1.0 + lax.erf(x * SQRT_HALF))
    pdf = jnp.exp(-0.5 * x * x) * INV_SQRT_2PI
    return x * cdf, cdf + x * pdf


def _silu(z):
    s = 1.0 / (1.0 + jnp.exp(-z))
    return z * s, s * (1.0 + z * (1.0 - s))


def _matmul(name, a, b, *, dims, grid, a_spec, b_spec, o_spec, out_shape, out_dtype):
    nk = grid[2]
    contract = {"nn": ((1,), (0,)), "nt": ((1,), (1,)), "tn": ((0,), (0,))}[dims]
    acc_in_out = out_dtype == F32
    blk = tuple(d for d in o_spec.block_shape if d is not None)

    def body(a_ref, b_ref, o_ref, *scratch):
        acc = o_ref if acc_in_out else scratch[0]
        k = pl.program_id(2)
        part = lax.dot_general(a_ref[...], b_ref[...], (contract, ((), ())), preferred_element_type=F32)

        @pl.when(k == 0)
        def _():
            acc[...] = part

        @pl.when(k > 0)
        def _():
            acc[...] += part

        if not acc_in_out:

            @pl.when(k == nk - 1)
            def _():
                o_ref[...] = acc[...].astype(out_dtype)

    return pl.pallas_call(
        body,
        name=name,
        grid=grid,
        in_specs=[a_spec, b_spec],
        out_specs=o_spec,
        out_shape=jax.ShapeDtypeStruct(out_shape, out_dtype),
        scratch_shapes=[] if acc_in_out else [pltpu.VMEM(blk, F32)],
        compiler_params=_params("parallel", "parallel", "arbitrary"),
    )(a, b)


TM, TN, TK = 1024, 2048, 512


def _mm_nn(name, a, b, out_dtype=F32):
    m, k = a.shape
    n = b.shape[1]
    tm, tn, tk = _tile(m, TM), _tile(n, TN), _tile(k, TK)
    return _matmul(
        name, a, b, dims="nn", grid=(m // tm, n // tn, k // tk),
        a_spec=pl.BlockSpec((tm, tk), lambda i, j, l: (i, l)),
        b_spec=pl.BlockSpec((tk, tn), lambda i, j, l: (l, j)),
        o_spec=pl.BlockSpec((tm, tn), lambda i, j, l: (i, j)),
        out_shape=(m, n), out_dtype=out_dtype)


def _mm_nt(name, a, b, out_dtype=F32):
    m, k = a.shape
    n = b.shape[0]
    tm, tn, tk = _tile(m, TM), _tile(n, TN), _tile(k, TK)
    return _matmul(
        name, a, b, dims="nt", grid=(m // tm, n // tn, k // tk),
        a_spec=pl.BlockSpec((tm, tk), lambda i, j, l: (i, l)),
        b_spec=pl.BlockSpec((tn, tk), lambda i, j, l: (j, l)),
        o_spec=pl.BlockSpec((tm, tn), lambda i, j, l: (i, j)),
        out_shape=(m, n), out_dtype=out_dtype)


def _mm_tn(name, a, b, out_dtype=F32):
    k, m = a.shape
    n = b.shape[1]
    tm, tn, tk = _tile(m, TM), _tile(n, TN), _tile(k, TK)
    return _matmul(
        name, a, b, dims="tn", grid=(m // tm, n // tn, k // tk),
        a_spec=pl.BlockSpec((tk, tm), lambda i, j, l: (l, i)),
        b_spec=pl.BlockSpec((tk, tn), lambda i, j, l: (l, j)),
        o_spec=pl.BlockSpec((tm, tn), lambda i, j, l: (i, j)),
        out_shape=(m, n), out_dtype=out_dtype)


def _mm_nn_cols(name, a, wg):
    m, k = a.shape
    ws = wg.shape[2]
    tm, tn, tk = _tile(m, TM), _tile(ws, TN), _tile(k, TK)
    npb = ws // tn
    return _matmul(
        name, a, wg, dims="nn", grid=(m // tm, N_CHIPS * npb, k // tk),
        a_spec=pl.BlockSpec((tm, tk), lambda i, j, l: (i, l)),
        b_spec=pl.BlockSpec((None, tk, tn), lambda i, j, l: (j // npb, l, j % npb)),
        o_spec=pl.BlockSpec((tm, tn), lambda i, j, l: (i, j)),
        out_shape=(m, N_CHIPS * ws), out_dtype=F32)


def _mm_nt_cols(name, a, wg):
    m = a.shape[0]
    n, ws = wg.shape[1], wg.shape[2]
    tm, tn, tk = _tile(m, TM), _tile(n, TN), _tile(ws, TK)
    kpb = ws // tk
    return _matmul(
        name, a, wg, dims="nt", grid=(m // tm, n // tn, N_CHIPS * kpb),
        a_spec=pl.BlockSpec((tm, tk), lambda i, j, l: (i, l)),
        b_spec=pl.BlockSpec((None, tn, tk), lambda i, j, l: (l // kpb, j, l % kpb)),
        o_spec=pl.BlockSpec((tm, tn), lambda i, j, l: (i, j)),
        out_shape=(m, n), out_dtype=F32)


def _mm_tn_cols(name, a, b, ws):
    k, m = a.shape
    tm, tn, tk = _tile(m, TM), _tile(ws, TN), _tile(k, TK)
    npb = ws // tn
    return _matmul(
        name, a, b, dims="tn", grid=(m // tm, N_CHIPS * npb, k // tk),
        a_spec=pl.BlockSpec((tk, tm), lambda i, j, l: (l, i)),
        b_spec=pl.BlockSpec((tk, tn), lambda i, j, l: (l, j)),
        o_spec=pl.BlockSpec((None, tm, tn), lambda i, j, l: (j // npb, i, j % npb)),
        out_shape=(N_CHIPS, m, ws), out_dtype=F32)


def _grp_fwd(name, pooled, wgg):
    t, e = pooled.shape
    _, ng, rs, gw = wgg.shape
    tm, tn, tk = _tile(t, TM), _tile(gw, TN), _tile(rs, TK)
    npb, kps = gw // tn, rs // tk
    return _matmul(
        name, pooled, wgg, dims="nn", grid=(t // tm, ng * npb, N_CHIPS * kps),
        a_spec=pl.BlockSpec((tm, tk), lambda i, j, l: (i, (j // npb) * (gw // tk) + l)),
        b_spec=pl.BlockSpec((None, None, tk, tn), lambda i, j, l: (l // kps, j // npb, l % kps, j % npb)),
        o_spec=pl.BlockSpec((tm, tn), lambda i, j, l: (i, j)),
        out_shape=(t, e), out_dtype=F32)


def _grp_bwd_x(name, dmm, wgg):
    t, e = dmm.shape
    _, ng, rs, gw = wgg.shape
    tm, tn, tk = _tile(t, TM), _tile(rs, TN), _tile(gw, TK)
    npr, kpg = rs // tn, gw // tk
    return _matmul(
        name, dmm, wgg, dims="nt", grid=(t // tm, ng * N_CHIPS * npr, kpg),
        a_spec=pl.BlockSpec((tm, tk), lambda i, j, l: (i, (j // (N_CHIPS * npr)) * kpg + l)),
        b_spec=pl.BlockSpec(
            (None, None, tn, tk),
            lambda i, j, l: ((j % (N_CHIPS * npr)) // npr, j // (N_CHIPS * npr), j % npr, l)),
        o_spec=pl.BlockSpec((tm, tn), lambda i, j, l: (i, j)),
        out_shape=(t, e), out_dtype=F32)


def _grp_bwd_w(name, pooled, dmm, ng):
    t, e = pooled.shape
    gw = e // ng
    rs = gw // N_CHIPS
    tn, tk = _tile(gw, TN), _tile(t, TK)
    npb = gw // tn
    return _matmul(
        name, pooled, dmm, dims="tn", grid=(ng * N_CHIPS, npb, t // tk),
        a_spec=pl.BlockSpec((tk, rs), lambda i, j, l: (l, i)),
        b_spec=pl.BlockSpec((tk, tn), lambda i, j, l: (l, (i // N_CHIPS) * npb + j)),
        o_spec=pl.BlockSpec((None, None, rs, tn), lambda i, j, l: (i % N_CHIPS, i // N_CHIPS, 0, j)),
        out_shape=(N_CHIPS, ng, rs, gw), out_dtype=F32)


def _cast_bf16(name, w):
    r, c = w.shape
    tr, tc = _tile(r, 512), _tile(c, 2048)

    def body(w_ref, o_ref):
        o_ref[...] = w_ref[...].astype(BF16)

    return pl.pallas_call(
        body, name=name, grid=(r // tr, c // tc),
        in_specs=[pl.BlockSpec((tr, tc), lambda i, j: (i, j))],
        out_specs=pl.BlockSpec((tr, tc), lambda i, j: (i, j)),
        out_shape=jax.ShapeDtypeStruct((r, c), BF16),
        compiler_params=_params("parallel", "parallel"))(w)


def _rms(v):
    return lax.rsqrt(jnp.mean(v * v, axis=-1, keepdims=True) + NORM_EPS)


def _rms_fwd(name, x, g):
    t, d = x.shape
    tr = _tile(t, 256)

    def body(x_ref, g_ref, h_ref):
        xv = x_ref[...]
        h_ref[...] = (xv * _rms(xv) * g_ref[...]).astype(BF16)

    row = pl.BlockSpec((tr, d), lambda i: (i, 0))
    vec = pl.BlockSpec((1, d), lambda i: (0, 0))
    return pl.pallas_call(
        body, name=name, grid=(t // tr,), in_specs=[row, vec], out_specs=row,
        out_shape=jax.ShapeDtypeStruct((t, d), BF16), compiler_params=_params("parallel"))(x, g)


def _post_pre_fwd(name, x, m, g_post, g_pre):
    t, d = x.shape
    tr = _tile(t, 256)

    def body(x_ref, m_ref, gp_ref, gn_ref, x1_ref, h_ref):
        mv = m_ref[...]
        x1 = x_ref[...] + mv * _rms(mv) * gp_ref[...]
        x1_ref[...] = x1
        h_ref[...] = (x1 * _rms(x1) * gn_ref[...]).astype(BF16)

    row = pl.BlockSpec((tr, d), lambda i: (i, 0))
    vec = pl.BlockSpec((1, d), lambda i: (0, 0))
    return pl.pallas_call(
        body, name=name, grid=(t // tr,), in_specs=[row, row, vec, vec], out_specs=[row, row],
        out_shape=[jax.ShapeDtypeStruct((t, d), F32), jax.ShapeDtypeStruct((t, d), BF16)],
        compiler_params=_params("parallel"))(x, m, g_post, g_pre)


def _norm_bwd(dout, nrm, r, g):
    gd = dout * g
    return r * (gd - nrm * jnp.mean(gd * nrm, axis=-1, keepdims=True))


def _loss_post_bwd(name, x1, m, g_post, target):
    t, d = x1.shape
    tr = _tile(t, 128)

    def body(x_ref, m_ref, g_ref, t_ref, loss_ref, dx_ref, dm_ref, dg_ref):
        i = pl.program_id(0)
        mv = m_ref[...]
        r = _rms(mv)
        nrm = mv * r
        err = x_ref[...] + nrm * g_ref[...] - t_ref[...]
        part = 0.5 * jnp.sum(jnp.mean(err * err, axis=-1, keepdims=True), axis=0, keepdims=True)
        dx = err / d
        dx_ref[...] = dx
        dm_ref[...] = _norm_bwd(dx, nrm, r, g_ref[...]).astype(BF16)
        dg = jnp.sum(dx * nrm, axis=0, keepdims=True)

        @pl.when(i == 0)
        def _():
            loss_ref[...] = part
            dg_ref[...] = dg

        @pl.when(i > 0)
        def _():
            loss_ref[...] += part
            dg_ref[...] += dg

    row = pl.BlockSpec((tr, d), lambda i: (i, 0))
    vec = pl.BlockSpec((1, d), lambda i: (0, 0))
    one = pl.BlockSpec((1, 1), lambda i: (0, 0))
    return pl.pallas_call(
        body, name=name, grid=(t // tr,), in_specs=[row, row, vec, row], out_specs=[one, row, row, vec],
        out_shape=[jax.ShapeDtypeStruct((1, 1), F32), jax.ShapeDtypeStruct((t, d), F32),
                   jax.ShapeDtypeStruct((t, d), BF16), jax.ShapeDtypeStruct((1, d), F32)],
        compiler_params=_params("arbitrary"))(x1, m, g_post, target)


def _mid_bwd(name, dx2, dh1, x1, g_pre, m0, g_post):
    t, d = x1.shape
    tr = _tile(t, 128)

    def body(dx2_ref, dh_ref, x_ref, gn_ref, m_ref, gp_ref, dx_ref, dm_ref, dgn_ref, dgp_ref):
        i = pl.program_id(0)
        xv = x_ref[...]
        r1 = _rms(xv)
        n1 = xv * r1
        dh = dh_ref[...]
        dx = dx2_ref[...] + _norm_bwd(dh, n1, r1, gn_ref[...])
        dx_ref[...] = dx
        mv = m_ref[...]
        r0 = _rms(mv)
        n0 = mv * r0
        dm_ref[...] = _norm_bwd(dx, n0, r0, gp_ref[...]).astype(BF16)
        dgn = jnp.sum(dh * n1, axis=0, keepdims=True)
        dgp = jnp.sum(dx * n0, axis=0, keepdims=True)

        @pl.when(i == 0)
        def _():
            dgn_ref[...] = dgn
            dgp_ref[...] = dgp

        @pl.when(i > 0)
        def _():
            dgn_ref[...] += dgn
            dgp_ref[...] += dgp

    row = pl.BlockSpec((tr, d), lambda i: (i, 0))
    vec = pl.BlockSpec((1, d), lambda i: (0, 0))
    return pl.pallas_call(
        body, name=name, grid=(t // tr,), in_specs=[row, row, row, vec, row, vec],
        out_specs=[row, row, vec, vec],
        out_shape=[jax.ShapeDtypeStruct((t, d), F32), jax.ShapeDtypeStruct((t, d), BF16),
                   jax.ShapeDtypeStruct((1, d), F32), jax.ShapeDtypeStruct((1, d), F32)],
        compiler_params=_params("arbitrary"))(dx2, dh1, x1, g_pre, m0, g_post)


def _pre_bwd(name, dx1, dh0, x, g_pre):
    t, d = x.shape
    tr = _tile(t, 128)

    def body(dx1_ref, dh_ref, x_ref, g_ref, dx_ref, dg_ref):
        i = pl.program_id(0)
        xv = x_ref[...]
        r = _rms(xv)
        nrm = xv * r
        dh = dh_ref[...]
        dx_ref[...] = dx1_ref[...] + _norm_bwd(dh, nrm, r, g_ref[...])
        dg = jnp.sum(dh * nrm, axis=0, keepdims=True)

        @pl.when(i == 0)
        def _():
            dg_ref[...] = dg

        @pl.when(i > 0)
        def _():
            dg_ref[...] += dg

    row = pl.BlockSpec((tr, d), lambda i: (i, 0))
    vec = pl.BlockSpec((1, d), lambda i: (0, 0))
    return pl.pallas_call(
        body, name=name, grid=(t // tr,), in_specs=[row, row, row, vec], out_specs=[row, vec],
        out_shape=[jax.ShapeDtypeStruct((t, d), F32), jax.ShapeDtypeStruct((1, d), F32)],
        compiler_params=_params("arbitrary"))(dx1, dh0, x, g_pre)


def _a_stats(name, pa, e):
    t = pa.shape[0]
    tr = _tile(t, 64, 8)

    def body(v_ref, mu_ref, rs_ref):
        vg, _ = _gelu(v_ref[...])
        mu = jnp.mean(vg, axis=-1, keepdims=True)
        xc = vg - mu
        mu_ref[...] = mu
        rs_ref[...] = lax.rsqrt(jnp.mean(xc * xc, axis=-1, keepdims=True) + NORM_EPS)

    col = pl.BlockSpec((tr, 1), lambda i: (i, 0))
    return pl.pallas_call(
        body, name=name, grid=(t // tr,), in_specs=[pl.BlockSpec((tr, e), lambda i: (i, 1))],
        out_specs=[col, col],
        out_shape=[jax.ShapeDtypeStruct((t, 1), F32), jax.ShapeDtypeStruct((t, 1), F32)],
        compiler_params=_params("parallel"))(pa)


def _causal(w):
    c = w.shape[0]
    keep = lax.broadcasted_iota(jnp.int32, (c, c), 0) >= lax.broadcasted_iota(jnp.int32, (c, c), 1)
    return jnp.where(keep, w, 0.0), keep


def _a_gate_fwd(name, pa, mu, rs, ln_g, ln_b, w_s, b_s3):
    t = pa.shape[0]
    nh, c, _ = w_s.shape
    e = ln_g.shape[1]
    dh = e // nh
    rb = 2 * c if t % (2 * c) == 0 else c

    def body(u_ref, v_ref, z_ref, mu_ref, rs_ref, g_ref, b_ref, w_ref, bs_ref, y_ref):
        wc = _causal(w_ref[...])[0].astype(BF16)
        vg, _ = _gelu(v_ref[...])
        vn = ((vg - mu_ref[...]) * rs_ref[...] * g_ref[...] + b_ref[...]).astype(BF16)
        for ci in range(rb // c):
            rows = pl.ds(ci * c, c)
            sv = jnp.dot(wc, vn[ci * c:(ci + 1) * c], preferred_element_type=F32) + bs_ref[...]
            u, _ = _gelu(u_ref[rows, :])
            sz, _ = _silu(z_ref[rows, :])
            y_ref[rows, :] = (u * sv * sz).astype(BF16)

    blk = lambda off: pl.BlockSpec((rb, dh), lambda i, h: (i, off + h))
    col = pl.BlockSpec((rb, 1), lambda i, h: (i, 0))
    vec = pl.BlockSpec((1, dh), lambda i, h: (0, h))
    return pl.pallas_call(
        body, name=name, grid=(t // rb, nh),
        in_specs=[blk(0), blk(nh), blk(2 * nh), col, col, vec, vec,
                  pl.BlockSpec((None, c, c), lambda i, h: (h, 0, 0)),
                  pl.BlockSpec((None, c, 1), lambda i, h: (h, 0, 0))],
        out_specs=pl.BlockSpec((rb, dh), lambda i, h: (i, h)),
        out_shape=jax.ShapeDtypeStruct((t, e), BF16),
        compiler_params=_params("parallel", "parallel"))(pa, pa, pa, mu, rs, ln_g, ln_b, w_s, b_s3)


def _a_gate_bwd1(name, pa, dy, mu, rs, ln_g, ln_b, w_s, b_s3):
    t = pa.shape[0]
    nh, c, _ = w_s.shape
    e = ln_g.shape[1]
    dh = e // nh
    rb = 2 * c if t % (2 * c) == 0 else c

    def body(u_ref, v_ref, z_ref, dy_ref, mu_ref, rs_ref, g_ref, b_ref, w_ref, bs_ref,
             du_ref, dz_ref, dsv_ref, c1_ref, c2_ref):
        h = pl.program_id(1)
        wc = _causal(w_ref[...])[0].astype(BF16)
        vg, _ = _gelu(v_ref[...])
        xh = (vg - mu_ref[...]) * rs_ref[...]
        vn = (xh * g_ref[...] + b_ref[...]).astype(BF16)
        s1 = []
        s2 = []
        for ci in range(rb // c):
            rows = pl.ds(ci * c, c)
            lo, hi = ci * c, (ci + 1) * c
            sv = jnp.dot(wc, vn[lo:hi], preferred_element_type=F32) + bs_ref[...]
            u, du = _gelu(u_ref[rows, :])
            zv = z_ref[rows, :]
            sz, dsz = _silu(zv)
            dyv = dy_ref[rows, :]
            du_ref[rows, :] = (dyv * sv * sz * du).astype(BF16)
            dz_ref[rows, :] = (dyv * u * sv * dsz).astype(BF16)
            dsv = (dyv * u * sz).astype(BF16)
            dsv_ref[rows, :] = dsv
            dvn = lax.dot_general(wc, dsv, (((0,), (0,)), ((), ())), preferred_element_type=F32)
            dxh = dvn * g_ref[...]
            s1.append(jnp.sum(dxh, axis=-1, keepdims=True))
            s2.append(jnp.sum(dxh * xh[lo:hi], axis=-1, keepdims=True))
        p1 = jnp.concatenate(s1, axis=0)
        p2 = jnp.concatenate(s2, axis=0)

        @pl.when(h == 0)
        def _():
            c1_ref[...] = p1
            c2_ref[...] = p2

        @pl.when(h > 0)
        def _():
            c1_ref[...] += p1
            c2_ref[...] += p2

    blk = lambda off: pl.BlockSpec((rb, dh), lambda i, h: (i, off + h))
    col = pl.BlockSpec((rb, 1), lambda i, h: (i, 0))
    vec = pl.BlockSpec((1, dh), lambda i, h: (0, h))
    act = jax.ShapeDtypeStruct((t, e), BF16)
    stat = jax.ShapeDtypeStruct((t, 1), F32)
    return pl.pallas_call(
        body, name=name, grid=(t // rb, nh),
        in_specs=[blk(0), blk(nh), blk(2 * nh), blk(0), col, col, vec, vec,
                  pl.BlockSpec((None, c, c), lambda i, h: (h, 0, 0)),
                  pl.BlockSpec((None, c, 1), lambda i, h: (h, 0, 0))],
        out_specs=[blk(0), blk(0), blk(0), col, col],
        out_shape=[act, act, act, stat, stat],
        compiler_params=_params("parallel", "arbitrary"))(pa, pa, pa, dy, mu, rs, ln_g, ln_b, w_s, b_s3)


def _a_gate_bwd2(name, pa, dsv, mu, rs, c1, c2, ln_g, ln_b, w_s):
    t = pa.shape[0]
    nh, c, _ = w_s.shape
    e = ln_g.shape[1]
    dh = e // nh
    rb = 2 * c if t % (2 * c) == 0 else c

    def body(v_ref, dsv_ref, mu_ref, rs_ref, c1_ref, c2_ref, g_ref, b_ref, w_ref,
             dv_ref, dw_ref, dbs_ref, dg_ref, db_ref):
        i = pl.program_id(1)
        wcf, keep = _causal(w_ref[...])
        wc = wcf.astype(BF16)
        vg, dvg = _gelu(v_ref[...])
        rsv = rs_ref[...]
        xh = (vg - mu_ref[...]) * rsv
        vn = (xh * g_ref[...] + b_ref[...]).astype(BF16)
        dw = jnp.zeros((c, c), F32)
        dbs = jnp.zeros((c, 1), F32)
        dvns = []
        for ci in range(rb // c):
            lo, hi = ci * c, (ci + 1) * c
            dsv = dsv_ref[pl.ds(lo, c), :]
            dvns.append(lax.dot_general(wc, dsv, (((0,), (0,)), ((), ())), preferred_element_type=F32))
            dw += lax.dot_general(dsv, vn[lo:hi], (((1,), (1,)), ((), ())), preferred_element_type=F32)
            dbs += jnp.sum(dsv.astype(F32), axis=-1, keepdims=True)
        dvn = jnp.concatenate(dvns, axis=0)
        dxh = dvn * g_ref[...]
        dvv = rsv * (dxh - c1_ref[...] * (1.0 / e) - xh * (c2_ref[...] * (1.0 / e)))
        dv_ref[...] = (dvv * dvg).astype(BF16)
        dw = jnp.where(keep, dw, 0.0)
        dg = jnp.sum(dvn * xh, axis=0, keepdims=True)
        db = jnp.sum(dvn, axis=0, keepdims=True)

        @pl.when(i == 0)
        def _():
            dw_ref[...] = dw
            dbs_ref[...] = dbs
            dg_ref[...] = dg
            db_ref[...] = db

        @pl.when(i > 0)
        def _():
            dw_ref[...] += dw
            dbs_ref[...] += dbs
            dg_ref[...] += dg
            db_ref[...] += db

    col = pl.BlockSpec((rb, 1), lambda h, i: (i, 0))
    vec = pl.BlockSpec((1, dh), lambda h, i: (0, h))
    hblk = pl.BlockSpec((rb, dh), lambda h, i: (i, h))
    return pl.pallas_call(
        body, name=name, grid=(nh, t // rb),
        in_specs=[pl.BlockSpec((rb, dh), lambda h, i: (i, nh + h)), hblk, col, col, col, col, vec, vec,
                  pl.BlockSpec((None, c, c), lambda h, i: (h, 0, 0))],
        out_specs=[hblk, pl.BlockSpec((None, c, c), lambda h, i: (h, 0, 0)),
                   pl.BlockSpec((None, c, 1), lambda h, i: (h, 0, 0)), vec, vec],
        out_shape=[jax.ShapeDtypeStruct((t, e), BF16), jax.ShapeDtypeStruct((nh, c, c), F32),
                   jax.ShapeDtypeStruct((nh, c, 1), F32), jax.ShapeDtypeStruct((1, e), F32),
                   jax.ShapeDtypeStruct((1, e), F32)],
        compiler_params=_params("parallel", "arbitrary"))(pa, dsv, mu, rs, c1, c2, ln_g, ln_b, w_s)


def _pool(name, src, e, seq, backward):
    t = src.shape[0]
    ng = len(POOL_WINDOWS)
    gw = e // ng
    cw = _tile(gw, 256)

    def shifted(a, j, pos):
        if backward:
            return jnp.where(pos < seq - j, pltpu.roll(a, seq - j, 0), 0.0)
        return jnp.where(pos >= j, pltpu.roll(a, j, 0), 0.0)

    def body(p_ref, o_ref):
        grp = (pl.program_id(1) * cw) // gw
        pos = lax.broadcasted_iota(jnp.int32, (seq, cw), 0)
        posf = (pos + 1).astype(F32)
        for k, w in enumerate(POOL_WINDOWS):

            @pl.when(grp == k)
            def _(w=w):
                pv = p_ref[...]
                cnt = jnp.minimum(posf, float(w))
                acc = pv / cnt if backward else pv
                j = 1
                while j < w:
                    acc = acc + shifted(acc, j, pos)
                    j *= 2
                out = acc - pv if backward else acc / cnt - pv
                o_ref[...] = out.astype(BF16)

    spec = pl.BlockSpec((seq, cw), lambda s, j: (s, j))
    return pl.pallas_call(
        body, name=name, grid=(t // seq, e // cw), in_specs=[spec], out_specs=spec,
        out_shape=jax.ShapeDtypeStruct((t, e), BF16),
        compiler_params=_params("parallel", "parallel"))(src)


def _b_gate_fwd(name, mm, pb, b_grp, scale):
    t, e = mm.shape
    tr, tc = _tile(t, 512), _tile(e, 1024)
    nc = e // tc

    def body(mm_ref, z_ref, b_ref, s_ref, y_ref):
        sz, _ = _silu(z_ref[...])
        y_ref[...] = ((mm_ref[...] + b_ref[...]) * s_ref[...] * sz).astype(BF16)

    blk = pl.BlockSpec((tr, tc), lambda i, j: (i, j))
    vec = pl.BlockSpec((1, tc), lambda i, j: (0, j))
    return pl.pallas_call(
        body, name=name, grid=(t // tr, nc),
        in_specs=[blk, pl.BlockSpec((tr, tc), lambda i, j: (i, nc + j)), vec, vec], out_specs=blk,
        out_shape=jax.ShapeDtypeStruct((t, e), BF16),
        compiler_params=_params("parallel", "parallel"))(mm, pb, b_grp, scale)


def _b_gate_bwd(name, dy, mm, pb, b_grp, scale):
    t, e = mm.shape
    tr, tc = _tile(t, 512), _tile(e, 1024)
    nc = e // tc

    def body(dy_ref, mm_ref, z_ref, b_ref, s_ref, dmm_ref, dz_ref, db_ref, ds_ref):
        i = pl.program_id(1)
        sz, dsz = _silu(z_ref[...])
        dyv = dy_ref[...]
        mb = mm_ref[...] + b_ref[...]
        dmixed = dyv * sz
        dmm = dmixed * s_ref[...]
        dmm_ref[...] = dmm.astype(BF16)
        dz_ref[...] = (dyv * (mb * s_ref[...]) * dsz).astype(BF16)
        db = jnp.sum(dmm, axis=0, keepdims=True)
        ds = jnp.sum(dmixed * mb, axis=0, keepdims=True)

        @pl.when(i == 0)
        def _():
            db_ref[...] = db
            ds_ref[...] = ds

        @pl.when(i > 0)
        def _():
            db_ref[...] += db
            ds_ref[...] += ds

    blk = pl.BlockSpec((tr, tc), lambda j, i: (i, j))
    vec = pl.BlockSpec((1, tc), lambda j, i: (0, j))
    act = jax.ShapeDtypeStruct((t, e), BF16)
    stat = jax.ShapeDtypeStruct((1, e), F32)
    return pl.pallas_call(
        body, name=name, grid=(nc, t // tr),
        in_specs=[blk, blk, pl.BlockSpec((tr, tc), lambda j, i: (i, nc + j)), vec, vec],
        out_specs=[blk, blk, vec, vec], out_shape=[act, act, stat, stat],
        compiler_params=_params("parallel", "arbitrary"))(dy, mm, pb, b_grp, scale)


def _position():
    return lax.axis_index("x"), lax.axis_index("y"), lax.axis_index("c")


def _other_chips(x, y):
    return [(1 - x, y), (x, 1 - y), (1 - x, 1 - y)]


def _half(ref, c):
    n = ref.shape[0] // 2
    return ref.at[pl.ds(c * n, n)]


def _gather_weights(shards, small):
    nw = len(shards)
    n_ici = 3 * nw + 3

    def body(*refs):
        srcs, sm = refs[:nw], refs[nw]
        dsts, gsm = refs[nw + 1:2 * nw + 1], refs[2 * nw + 1]
        send_sems, recv_sems, loc_sems = refs[2 * nw + 2:]
        x, y, c = _position()
        me = 2 * x + y
        sibling = (x, y, 1 - c)
        chips = _other_chips(x, y)
        local = [pltpu.make_async_copy(srcs[k], dsts[k].at[me], loc_sems.at[k]) for k in range(nw)]
        local.append(pltpu.make_async_copy(sm, gsm.at[me], loc_sems.at[nw]))
        for cp in local:
            cp.start()

        def ici(j, k, slot, to):
            return pltpu.make_async_remote_copy(
                src_ref=_half(srcs[k], c), dst_ref=_half(dsts[k].at[slot], c),
                send_sem=send_sems.at[j * nw + k], recv_sem=recv_sems.at[j * nw + k],
                device_id=to, device_id_type=MESH)

        def ici_small(j, slot, to):
            return pltpu.make_async_remote_copy(
                src_ref=sm, dst_ref=gsm.at[slot], send_sem=send_sems.at[3 * nw + j],
                recv_sem=recv_sems.at[3 * nw + j], device_id=to, device_id_type=MESH)

        def d2d(j, k, slot, half):
            return pltpu.make_async_remote_copy(
                src_ref=_half(dsts[k].at[slot], half), dst_ref=_half(dsts[k].at[slot], half),
                send_sem=send_sems.at[n_ici + j * nw + k], recv_sem=recv_sems.at[n_ici + j * nw + k],
                device_id=sibling, device_id_type=MESH)

        sent = []
        for j, (cx, cy) in enumerate(chips):
            for k in range(nw):
                sent.append(ici(j, k, me, (cx, cy, c)))
            sent.append(ici_small(j, me, (cx, cy, c)))
        for cp in sent:
            cp.start()
        passed = []
        for j, (cx, cy) in enumerate(chips):
            for k in range(nw):
                ici(j, k, 2 * cx + cy, sibling).wait_recv()
                cp = d2d(j, k, 2 * cx + cy, c)
                cp.start()
                passed.append(cp)
        for j, (cx, cy) in enumerate(chips):
            ici_small(j, 2 * cx + cy, sibling).wait_recv()
            for k in range(nw):
                d2d(j, k, 2 * cx + cy, 1 - c).wait_recv()
        for cp in sent + passed:
            cp.wait_send()
        for cp in local:
            cp.wait()

    out_shape = [jax.ShapeDtypeStruct((N_CHIPS,) + s.shape, s.dtype) for s in shards]
    out_shape.append(jax.ShapeDtypeStruct((N_CHIPS,) + small.shape, small.dtype))
    n_sem = n_ici + 3 * nw
    return pl.pallas_call(
        body, name="gather_weights", in_specs=[ANY] * (nw + 1), out_specs=[ANY] * (nw + 1), out_shape=out_shape,
        scratch_shapes=[pltpu.SemaphoreType.DMA((n_sem,)), pltpu.SemaphoreType.DMA((n_sem,)),
                        pltpu.SemaphoreType.DMA((nw + 1,))],
    )(*shards, small)


def _swap_halves(grads):
    nw = len(grads)

    def body(*refs):
        srcs = refs[:nw]
        mine, theirs = refs[nw:2 * nw], refs[2 * nw:3 * nw]
        send_sems, recv_sems, loc_sems = refs[3 * nw:]
        x, y, c = _position()
        local, swaps = [], []
        for k in range(nw):
            n = srcs[k].shape[1] // 2
            every = pl.ds(0, N_CHIPS)
            local.append(pltpu.make_async_copy(srcs[k].at[every, pl.ds(c * n, n)], mine[k], loc_sems.at[k]))
            swaps.append(pltpu.make_async_remote_copy(
                src_ref=srcs[k].at[every, pl.ds((1 - c) * n, n)], dst_ref=theirs[k],
                send_sem=send_sems.at[k], recv_sem=recv_sems.at[k],
                device_id=(x, y, 1 - c), device_id_type=MESH))
        for cp in swaps + local:
            cp.start()
        for cp in swaps + local:
            cp.wait()

    halves = [jax.ShapeDtypeStruct((g.shape[0], g.shape[1] // 2, g.shape[2]), F32) for g in grads]
    return pl.pallas_call(
        body, name="swap_grad_halves", in_specs=[ANY] * nw, out_specs=[ANY] * (2 * nw), out_shape=halves + halves,
        scratch_shapes=[pltpu.SemaphoreType.DMA((nw,)), pltpu.SemaphoreType.DMA((nw,)),
                        pltpu.SemaphoreType.DMA((nw,))],
    )(*grads)


def _add_halves(name, mine, theirs):
    s, r, w = mine.shape
    tr, tc = _tile(r, 256), _tile(w, 2048)

    def body(a_ref, b_ref, f_ref, h_ref):
        v = a_ref[...] + b_ref[...]
        f_ref[...] = v
        h_ref[...] = v.astype(BF16)

    blk = pl.BlockSpec((None, tr, tc), lambda q, i, j: (q, i, j))
    return pl.pallas_call(
        body, name=name, grid=(s, r // tr, w // tc), in_specs=[blk, blk], out_specs=[blk, blk],
        out_shape=[jax.ShapeDtypeStruct(mine.shape, F32), jax.ShapeDtypeStruct(mine.shape, BF16)],
        compiler_params=_params("parallel", "parallel", "parallel"))(mine, theirs)


def _scatter_grads(pf, pb):
    nw = len(pf)

    def body(*refs):
        pfs, pbs = refs[:nw], refs[nw:2 * nw]
        own, got = refs[2 * nw:3 * nw], refs[3 * nw:4 * nw]
        send_sems, recv_sems, loc_sems = refs[4 * nw:]
        x, y, c = _position()
        me = 2 * x + y
        sibling = (x, y, 1 - c)
        chips = _other_chips(x, y)
        local = [pltpu.make_async_copy(pfs[k].at[me], _half(own[k], c), loc_sems.at[k]) for k in range(nw)]
        for cp in local:
            cp.start()

        def to_sibling(k):
            return pltpu.make_async_remote_copy(
                src_ref=pfs[k].at[me], dst_ref=_half(own[k], c), send_sem=send_sems.at[k],
                recv_sem=recv_sems.at[k], device_id=sibling, device_id_type=MESH)

        def ici(j, k, shard, to):
            return pltpu.make_async_remote_copy(
                src_ref=pbs[k].at[shard], dst_ref=_half(got[k].at[j], c),
                send_sem=send_sems.at[nw + j * nw + k], recv_sem=recv_sems.at[nw + j * nw + k],
                device_id=to, device_id_type=MESH)

        def d2d(j, k, half):
            return pltpu.make_async_remote_copy(
                src_ref=_half(got[k].at[j], half), dst_ref=_half(got[k].at[j], half),
                send_sem=send_sems.at[4 * nw + j * nw + k], recv_sem=recv_sems.at[4 * nw + j * nw + k],
                device_id=sibling, device_id_type=MESH)

        sent = [to_sibling(k) for k in range(nw)]
        for j, (cx, cy) in enumerate(chips):
            for k in range(nw):
                sent.append(ici(j, k, 2 * cx + cy, (cx, cy, c)))
        for cp in sent:
            cp.start()
        passed = []
        for j in range(3):
            for k in range(nw):
                ici(j, k, me, sibling).wait_recv()
                cp = d2d(j, k, c)
                cp.start()
                passed.append(cp)
        for k in range(nw):
            pltpu.make_async_remote_copy(
                src_ref=pfs[k].at[me], dst_ref=_half(own[k], 1 - c), send_sem=send_sems.at[k],
                recv_sem=recv_sems.at[k], device_id=sibling, device_id_type=MESH).wait_recv()
        for j in range(3):
            for k in range(nw):
                d2d(j, k, 1 - c).wait_recv()
        for cp in sent + passed:
            cp.wait_send()
        for cp in local:
            cp.wait()

    own_shape = [jax.ShapeDtypeStruct((2 * p.shape[1], p.shape[2]), F32) for p in pf]
    got_shape = [jax.ShapeDtypeStruct((3, 2 * p.shape[1], p.shape[2]), BF16) for p in pf]
    n_sem = 7 * nw
    return pl.pallas_call(
        body, name="scatter_grads", in_specs=[ANY] * (2 * nw), out_specs=[ANY] * (2 * nw),
        out_shape=own_shape + got_shape,
        scratch_shapes=[pltpu.SemaphoreType.DMA((n_sem,)), pltpu.SemaphoreType.DMA((n_sem,)),
                        pltpu.SemaphoreType.DMA((nw,))],
    )(*pf, *pb)


def _gather_small(part):
    def body(p_ref, o_ref, send_sems, recv_sems, loc_sem):
        x, y, c = _position()
        me = 4 * x + 2 * y + c
        mine = pltpu.make_async_copy(p_ref, o_ref.at[me], loc_sem)
        mine.start()
        sent = []
        for mask in range(1, N_DEV):
            dx, dy, dc = (mask >> 2) & 1, (mask >> 1) & 1, mask & 1
            px, py, pc = x ^ dx, y ^ dy, c ^ dc
            cp = pltpu.make_async_remote_copy(
                src_ref=p_ref, dst_ref=o_ref.at[me], send_sem=send_sems.at[mask - 1],
                recv_sem=recv_sems.at[mask - 1], device_id=(px, py, pc), device_id_type=MESH)
            cp.start()
            sent.append((cp, 4 * px + 2 * py + pc))
        for mask in range(1, N_DEV):
            cp, peer = sent[mask - 1]
            pltpu.make_async_remote_copy(
                src_ref=p_ref, dst_ref=o_ref.at[peer], send_sem=send_sems.at[mask - 1],
                recv_sem=recv_sems.at[mask - 1], device_id=(x, y, c), device_id_type=MESH).wait_recv()
        for cp, _ in sent:
            cp.wait_send()
        mine.wait()

    return pl.pallas_call(
        body, name="gather_small_grads", in_specs=[ANY], out_specs=ANY,
        out_shape=jax.ShapeDtypeStruct((N_DEV,) + part.shape, F32),
        scratch_shapes=[pltpu.SemaphoreType.DMA((N_DEV - 1,)), pltpu.SemaphoreType.DMA((N_DEV - 1,)),
                        pltpu.SemaphoreType.DMA(())],
    )(part)


def _sum_slots(name, parts):
    s, r, c = parts.shape
    tr = _tile(r, 512, 8)

    def body(p_ref, o_ref):
        acc = p_ref[0]
        for q in range(1, s):
            acc = acc + p_ref[q]
        o_ref[...] = acc

    return pl.pallas_call(
        body, name=name, grid=(r // tr,), in_specs=[pl.BlockSpec((s, tr, c), lambda i: (0, i, 0))],
        out_specs=pl.BlockSpec((tr, c), lambda i: (i, 0)), out_shape=jax.ShapeDtypeStruct((r, c), F32),
        compiler_params=_params("parallel"))(parts)


def _adamw_math(w, g, m, v):
    m = ADAM_B1 * m + (1.0 - ADAM_B1) * g
    v = ADAM_B2 * v + (1.0 - ADAM_B2) * (g * g)
    m_hat = m / (1.0 - ADAM_B1 ** ADAM_STEP)
    v_hat = v / (1.0 - ADAM_B2 ** ADAM_STEP)
    delta = -ADAM_LR * (m_hat / (jnp.sqrt(v_hat) + ADAM_EPS) + ADAM_WD * w)
    return delta, m, v


def _adamw(name, w, m, v, own, got=None):
    r, c = w.shape
    tr, tc = _tile(r, 256, 8), _tile(c, 1024)

    def body(*refs):
        if got is None:
            w_ref, m_ref, v_ref, own_ref = refs[:4]
            outs = refs[4:]
            g = own_ref[...]
        else:
            w_ref, m_ref, v_ref, own_ref, got_ref = refs[:5]
            outs = refs[5:]
            g = own_ref[...]
            for j in range(3):
                g = g + got_ref[j].astype(F32)
        delta, mn, vn = _adamw_math(w_ref[...], g, m_ref[...], v_ref[...])
        outs[0][...] = g
        outs[1][...] = delta
        outs[2][...] = mn
        outs[3][...] = vn

    blk = pl.BlockSpec((tr, tc), lambda i, j: (i, j))
    ins, args = [blk] * 4, [w, m, v, own]
    if got is not None:
        ins.append(pl.BlockSpec((3, tr, tc), lambda i, j: (0, i, j)))
        args.append(got)
    return pl.pallas_call(
        body, name=name, grid=(r // tr, c // tc), in_specs=ins, out_specs=[blk] * 4,
        out_shape=[jax.ShapeDtypeStruct((r, c), F32)] * 4,
        compiler_params=_params("parallel", "parallel"))(*args)


def _pack(parts):
    return jnp.concatenate([p.reshape(-1) for p in parts]).reshape(-1, LANES)


def _unpack(packed, shapes):
    flat = packed.reshape(-1)
    out, off = [], 0
    for s in shapes:
        n = 1
        for d in s:
            n *= d
        out.append(flat[off:off + n].reshape(s))
        off += n
    return out


def kernel(x, pre_norm, post_norm, a_w_in, a_ln_g, a_ln_b, a_w_s, a_b_s, a_w_out, b_w_in, b_w_grp, b_b_grp, b_scale, b_w_out, loss_target, m_pre_norm, m_post_norm, m_a_w_in, m_a_ln_g, m_a_ln_b, m_a_w_s, m_a_b_s, m_a_w_out, m_b_w_in, m_b_w_grp, m_b_b_grp, m_b_scale, m_b_w_out, v_pre_norm, v_post_norm, v_a_w_in, v_a_ln_g, v_a_ln_b, v_a_w_s, v_a_b_s, v_a_w_out, v_b_w_in, v_b_w_grp, v_b_b_grp, v_b_scale, v_b_w_out):
    nb, seq, d = x.shape
    t = nb * seq
    e = a_ln_g.shape[1]
    nh, chunk = a_w_s.shape[1], a_w_s.shape[2]
    ng, rs, gw = b_w_grp.shape[1], b_w_grp.shape[2], b_w_grp.shape[3]
    wa, wb = a_w_in.shape[2], b_w_in.shape[2]
    cx, cy = lax.axis_index("x"), lax.axis_index("y")
    chip = 2 * cx + cy

    big_w = [a_w_in.reshape(d, wa), a_w_out.reshape(e // N_CHIPS, d), b_w_in.reshape(d, wb),
             b_w_grp.reshape(ng * rs, gw), b_w_out.reshape(e // N_CHIPS, d)]
    big_m = [m_a_w_in.reshape(d, wa), m_a_w_out.reshape(e // N_CHIPS, d), m_b_w_in.reshape(d, wb),
             m_b_w_grp.reshape(ng * rs, gw), m_b_w_out.reshape(e // N_CHIPS, d)]
    big_v = [v_a_w_in.reshape(d, wa), v_a_w_out.reshape(e // N_CHIPS, d), v_b_w_in.reshape(d, wb),
             v_b_w_grp.reshape(ng * rs, gw), v_b_w_out.reshape(e // N_CHIPS, d)]
    names = ["a_w_in", "a_w_out", "b_w_in", "b_w_grp", "b_w_out"]
    shards = [_cast_bf16("cast_" + n, w) for n, w in zip(names, big_w)]
    small_w = jnp.concatenate([b_b_grp.reshape(ng, rs), b_scale.reshape(ng, rs)], axis=0)
    *gathered, gsmall = _gather_weights(shards, small_w)
    wa_g, wao_g, wb_g, wg_g, wbo_g = gathered
    wao_g = wao_g.reshape(e, d)
    wbo_g = wbo_g.reshape(e, d)
    wg_g = wg_g.reshape(N_CHIPS, ng, rs, gw)
    b_grp_full = jnp.transpose(gsmall[:, :ng, :], (1, 0, 2)).reshape(1, e)
    scale_full = gsmall[:, ng:, :].reshape(1, e)

    xf = x.reshape(t, d)
    tgt = loss_target.reshape(t, d)
    g_pre0, g_pre1 = pre_norm[0:1], pre_norm[1:2]
    g_post0, g_post1 = post_norm[0:1], post_norm[1:2]
    w_s = a_w_s.reshape(nh, chunk, chunk)
    b_s3 = a_b_s.reshape(nh, chunk, 1)

    h0 = _rms_fwd("pre_norm0", xf, g_pre0)
    pa = _mm_nn_cols("a_in_proj", h0, wa_g)
    mu, rstd = _a_stats("a_ln_stats", pa, e)
    y0 = _a_gate_fwd("a_gate", pa, mu, rstd, a_ln_g, a_ln_b, w_s, b_s3)
    m0 = _mm_nn("a_out_proj", y0, wao_g)
    x1, h1 = _post_pre_fwd("post0_pre1", xf, m0, g_post0, g_pre1)
    pb = _mm_nn_cols("b_in_proj", h1, wb_g)
    pooled = _pool("b_pool", pb, e, seq, backward=False)
    mm = _grp_fwd("b_grp_proj", pooled, wg_g)
    y1 = _b_gate_fwd("b_gate", mm, pb, b_grp_full, scale_full)
    m1 = _mm_nn("b_out_proj", y1, wbo_g)
    loss, dx2, dm1, dg_post1 = _loss_post_bwd("loss_post1_bwd", x1, m1, g_post1, tgt)
    dy1 = _mm_nt("b_out_dx", dm1, wbo_g)
    g_wbo = _mm_tn("b_out_dw", y1, dm1)
    dmm, dzb, db_grp, dscale = _b_gate_bwd("b_gate_bwd", dy1, mm, pb, b_grp_full, scale_full)
    dpooled = _grp_bwd_x("b_grp_dx", dmm, wg_g)
    g_wg = _grp_bwd_w("b_grp_dw", pooled, dmm, ng)
    dp = _pool("b_pool_bwd", dpooled, e, seq, backward=True)
    dpb = jnp.concatenate([dp, dzb], axis=1)
    dh1 = _mm_nt_cols("b_in_dx", dpb, wb_g)
    g_wb = _mm_tn_cols("b_in_dw", h1, dpb, wb)
    dx1, dm0, dg_pre1, dg_post0 = _mid_bwd("pre1_post0_bwd", dx2, dh1, x1, g_pre1, m0, g_post0)
    dy0 = _mm_nt("a_out_dx", dm0, wao_g)
    g_wao = _mm_tn("a_out_dw", y0, dm0)
    du, dz, dsv, c1, c2 = _a_gate_bwd1("a_gate_bwd1", pa, dy0, mu, rstd, a_ln_g, a_ln_b, w_s, b_s3)
    dv, dw_s, db_s, dln_g, dln_b = _a_gate_bwd2("a_gate_bwd2", pa, dsv, mu, rstd, c1, c2, a_ln_g, a_ln_b, w_s)
    dpa = jnp.concatenate([du, dv, dz], axis=1)
    dh0 = _mm_nt_cols("a_in_dx", dpa, wa_g)
    g_wa = _mm_tn_cols("a_in_dw", h0, dpa, wa)
    grad_x, dg_pre0 = _pre_bwd("pre0_bwd", dx1, dh0, xf, g_pre0)

    grads = [g_wa, g_wao.reshape(N_CHIPS, e // N_CHIPS, d), g_wb, g_wg.reshape(N_CHIPS, ng * rs, gw),
             g_wbo.reshape(N_CHIPS, e // N_CHIPS, d)]
    halves = _swap_halves(grads)
    nw = len(grads)
    sums = [_add_halves("chip_sum_" + n, halves[k], halves[nw + k]) for k, n in enumerate(names)]
    scattered = _scatter_grads([s[0] for s in sums], [s[1] for s in sums])
    big_out = [_adamw("adamw_" + n, big_w[k], big_m[k], big_v[k], scattered[k], scattered[nw + k])
               for k, n in enumerate(names)]

    small_shapes = [(2, d), (2, d), (1, e), (1, e), (1, nh, chunk, chunk), (1, nh, chunk), (1, e), (1, e)]
    part = _pack([jnp.concatenate([dg_pre0, dg_pre1], axis=0), jnp.concatenate([dg_post0, dg_post1], axis=0),
                  dln_g, dln_b, dw_s, db_s, db_grp, dscale])
    g_small = _sum_slots("sum_small_grads", _gather_small(part))
    g_pre, g_post, g_lng, g_lnb, g_ws, g_bs, g_bgrp_full, g_scale_full = _unpack(g_small, small_shapes)
    g_bgrp = lax.dynamic_slice_in_dim(g_bgrp_full.reshape(ng, N_CHIPS, rs), chip, 1, axis=1).reshape(1, ng, rs)
    g_scale = lax.dynamic_slice_in_dim(g_scale_full.reshape(N_CHIPS, gw), chip, 1, axis=0)
    small_names = ["pre_norm", "post_norm", "a_ln_g", "a_ln_b", "a_w_s", "a_b_s", "b_b_grp", "b_scale"]
    small_g = [g_pre, g_post, g_lng, g_lnb, g_ws, g_bs, g_bgrp, g_scale]
    small_ws = [pre_norm, post_norm, a_ln_g, a_ln_b, a_w_s, a_b_s, b_b_grp, b_scale]
    small_ms = [m_pre_norm, m_post_norm, m_a_ln_g, m_a_ln_b, m_a_w_s, m_a_b_s, m_b_b_grp, m_b_scale]
    small_vs = [v_pre_norm, v_post_norm, v_a_ln_g, v_a_ln_b, v_a_w_s, v_a_b_s, v_b_b_grp, v_b_scale]
    packed = _adamw("adamw_small", _pack(small_ws), _pack(small_ms), _pack(small_vs), _pack(small_g))
    small_out = [_unpack(p, [w.shape for w in small_ws]) for p in packed]

    loss = lax.psum(loss[0, 0], ("x", "y", "c"))
    order = ["pre_norm", "post_norm", "a_w_in", "a_ln_g", "a_ln_b", "a_w_s", "a_b_s", "a_w_out", "b_w_in",
             "b_w_grp", "b_b_grp", "b_scale", "b_w_out"]
    big_shapes = dict(zip(names, [a_w_in.shape, a_w_out.shape, b_w_in.shape, b_w_grp.shape, b_w_out.shape]))
    outs = [loss, grad_x.reshape(nb, seq, d)]
    for kind in range(4):
        for n in order:
            if n in big_shapes:
                outs.append(big_out[names.index(n)][kind].reshape(big_shapes[n]))
            else:
                outs.append(small_out[kind][small_names.index(n)])
    return tuple(outs)
```

```python
import functools

import jax
import jax.numpy as jnp
from jax import lax
from jax.experimental import pallas as pl
from jax.experimental.pallas import tpu as pltpu

F32 = jnp.float32
BF16 = jnp.bfloat16
NORM_EPS = 1e-6
POOL_WINDOWS = (2, 4, 8, 16)
ADAM_LR = 0.001
ADAM_B1 = 0.9
ADAM_B2 = 0.999
ADAM_EPS = 1e-08
ADAM_WD = 0.01
ADAM_STEP = 10
N_CHIPS = 4
N_DEV = 8
V7X_VMEM_LIMIT_BYTES = 56 * 1024 * 1024
LANES = 128
MESH = pl.DeviceIdType.MESH
ANY = pl.BlockSpec(memory_space=pl.ANY)
SQRT_HALF = 0.7071067811865476
INV_SQRT_2PI = 0.3989422804014327


def _tile(dim, pref, mult=LANES):
    if dim <= pref:
        return dim
    t = (pref // mult) * mult
    while t >= mult:
        if dim % t == 0:
            return t
        t -= mult
    return dim


def _params(*sem):
    return pltpu.CompilerParams(dimension_semantics=sem or None, vmem_limit_bytes=V7X_VMEM_LIMIT_BYTES)


def _gelu(x):
    cdf = 0.5 * (1.0 + lax.erf(x * SQRT_HALF))
    pdf = jnp.exp(-0.5 * x * x) * INV_SQRT_2PI
    return x * cdf, cdf + x * pdf


def _silu(z):
    s = 1.0 / (1.0 + jnp.exp(-z))
    return z * s, s * (1.0 + z * (1.0 - s))


def _matmul(name, a, b, *, dims, grid, a_spec, b_spec, o_spec, out_shape, out_dtype):
    nk = grid[2]
    contract = {"nn": ((1,), (0,)), "nt": ((1,), (1,)), "tn": ((0,), (0,))}[dims]
    acc_in_out = out_dtype == F32 or nk == 1
    blk = tuple(d for d in o_spec.block_shape if d is not None)

    def body(a_ref, b_ref, o_ref, *scratch):
        part = lax.dot_general(a_ref[...], b_ref[...], (contract, ((), ())), preferred_element_type=F32)
        if nk == 1:
            o_ref[...] = part.astype(out_dtype)
            return
        acc = o_ref if acc_in_out else scratch[0]
        k = pl.program_id(2)

        @pl.when(k == 0)
        def _():
            acc[...] = part

        @pl.when(k > 0)
        def _():
            acc[...] += part

        if not acc_in_out:

            @pl.when(k == nk - 1)
            def _():
                o_ref[...] = acc[...].astype(out_dtype)

    return pl.pallas_call(
        body,
        name=name,
        grid=grid,
        in_specs=[a_spec, b_spec],
        out_specs=o_spec,
        out_shape=jax.ShapeDtypeStruct(out_shape, out_dtype),
        scratch_shapes=[] if acc_in_out else [pltpu.VMEM(blk, F32)],
        compiler_params=_params("parallel", "parallel", "arbitrary"),
    )(a, b)


MM_K_WHOLE = 4096


def _tiles(m, n, k, k_total=None):
    if (k if k_total is None else k_total) <= MM_K_WHOLE:
        return _tile(m, 1024), _tile(n, 1024), k
    return _tile(m, 1024), _tile(n, 2048), _tile(k, 2048)


def _mm_nn(name, a, b, out_dtype=F32):
    m, k = a.shape
    n = b.shape[1]
    tm, tn, tk = _tiles(m, n, k)
    return _matmul(
        name, a, b, dims="nn", grid=(m // tm, n // tn, k // tk),
        a_spec=pl.BlockSpec((tm, tk), lambda i, j, l: (i, l)),
        b_spec=pl.BlockSpec((tk, tn), lambda i, j, l: (l, j)),
        o_spec=pl.BlockSpec((tm, tn), lambda i, j, l: (i, j)),
        out_shape=(m, n), out_dtype=out_dtype)


def _mm_nt(name, a, b, out_dtype=F32):
    m, k = a.shape
    n = b.shape[0]
    tm, tn, tk = _tiles(m, n, k)
    return _matmul(
        name, a, b, dims="nt", grid=(m // tm, n // tn, k // tk),
        a_spec=pl.BlockSpec((tm, tk), lambda i, j, l: (i, l)),
        b_spec=pl.BlockSpec((tn, tk), lambda i, j, l: (j, l)),
        o_spec=pl.BlockSpec((tm, tn), lambda i, j, l: (i, j)),
        out_shape=(m, n), out_dtype=out_dtype)


def _mm_tn(name, a, b, out_dtype=F32):
    k, m = a.shape
    n = b.shape[1]
    tm, tn, tk = _tiles(m, n, k)
    return _matmul(
        name, a, b, dims="tn", grid=(m // tm, n // tn, k // tk),
        a_spec=pl.BlockSpec((tk, tm), lambda i, j, l: (l, i)),
        b_spec=pl.BlockSpec((tk, tn), lambda i, j, l: (l, j)),
        o_spec=pl.BlockSpec((tm, tn), lambda i, j, l: (i, j)),
        out_shape=(m, n), out_dtype=out_dtype)


def _mm_nn_cols(name, a, wg):
    m, k = a.shape
    ws = wg.shape[2]
    tm, tn, tk = _tiles(m, ws, k)
    npb = ws // tn
    return _matmul(
        name, a, wg, dims="nn", grid=(m // tm, N_CHIPS * npb, k // tk),
        a_spec=pl.BlockSpec((tm, tk), lambda i, j, l: (i, l)),
        b_spec=pl.BlockSpec((None, tk, tn), lambda i, j, l: (j // npb, l, j % npb)),
        o_spec=pl.BlockSpec((tm, tn), lambda i, j, l: (i, j)),
        out_shape=(m, N_CHIPS * ws), out_dtype=F32)


def _mm_nt_cols(name, a, wg):
    m = a.shape[0]
    n, ws = wg.shape[1], wg.shape[2]
    tm, tn, tk = _tiles(m, n, ws, N_CHIPS * ws)
    kpb = ws // tk
    return _matmul(
        name, a, wg, dims="nt", grid=(m // tm, n // tn, N_CHIPS * kpb),
        a_spec=pl.BlockSpec((tm, tk), lambda i, j, l: (i, l)),
        b_spec=pl.BlockSpec((None, tn, tk), lambda i, j, l: (l // kpb, j, l % kpb)),
        o_spec=pl.BlockSpec((tm, tn), lambda i, j, l: (i, j)),
        out_shape=(m, n), out_dtype=F32)


def _mm_tn_cols(name, a, b, ws):
    k, m = a.shape
    tm, tn, tk = _tiles(m, ws, k)
    npb = ws // tn
    return _matmul(
        name, a, b, dims="tn", grid=(m // tm, N_CHIPS * npb, k // tk),
        a_spec=pl.BlockSpec((tk, tm), lambda i, j, l: (l, i)),
        b_spec=pl.BlockSpec((tk, tn), lambda i, j, l: (l, j)),
        o_spec=pl.BlockSpec((None, tm, tn), lambda i, j, l: (j // npb, i, j % npb)),
        out_shape=(N_CHIPS, m, ws), out_dtype=F32)


def _grp_fwd(name, pooled, wgg):
    t, e = pooled.shape
    _, ng, rs, gw = wgg.shape
    tm, tn, tk = _tiles(t, gw, rs, gw)
    npb, kps = gw // tn, rs // tk
    return _matmul(
        name, pooled, wgg, dims="nn", grid=(t // tm, ng * npb, N_CHIPS * kps),
        a_spec=pl.BlockSpec((tm, tk), lambda i, j, l: (i, (j // npb) * (gw // tk) + l)),
        b_spec=pl.BlockSpec((None, None, tk, tn), lambda i, j, l: (l // kps, j // npb, l % kps, j % npb)),
        o_spec=pl.BlockSpec((tm, tn), lambda i, j, l: (i, j)),
        out_shape=(t, e), out_dtype=F32)


def _grp_bwd_x(name, dmm, wgg):
    t, e = dmm.shape
    _, ng, rs, gw = wgg.shape
    tm, tn, tk = _tiles(t, rs, gw)
    npr, kpg = rs // tn, gw // tk
    return _matmul(
        name, dmm, wgg, dims="nt", grid=(t // tm, ng * N_CHIPS * npr, kpg),
        a_spec=pl.BlockSpec((tm, tk), lambda i, j, l: (i, (j // (N_CHIPS * npr)) * kpg + l)),
        b_spec=pl.BlockSpec(
            (None, None, tn, tk),
            lambda i, j, l: ((j % (N_CHIPS * npr)) // npr, j // (N_CHIPS * npr), j % npr, l)),
        o_spec=pl.BlockSpec((tm, tn), lambda i, j, l: (i, j)),
        out_shape=(t, e), out_dtype=F32)


def _grp_bwd_w(name, pooled, dmm, ng):
    t, e = pooled.shape
    gw = e // ng
    rs = gw // N_CHIPS
    _, tn, tk = _tiles(rs, gw, t)
    npb = gw // tn
    return _matmul(
        name, pooled, dmm, dims="tn", grid=(ng * N_CHIPS, npb, t // tk),
        a_spec=pl.BlockSpec((tk, rs), lambda i, j, l: (l, i)),
        b_spec=pl.BlockSpec((tk, tn), lambda i, j, l: (l, (i // N_CHIPS) * npb + j)),
        o_spec=pl.BlockSpec((None, None, rs, tn), lambda i, j, l: (i % N_CHIPS, i // N_CHIPS, 0, j)),
        out_shape=(N_CHIPS, ng, rs, gw), out_dtype=F32)


def _cast_into_slot(name, w, place):
    r, c = w.shape
    tr, tc = _tile(r, 512), _tile(c, 2048)

    def body(place_ref, w_ref, o_ref):
        o_ref[...] = w_ref[...].astype(BF16)

    return pl.pallas_call(
        body, name=name,
        grid_spec=pltpu.PrefetchScalarGridSpec(
            num_scalar_prefetch=1, grid=(r // tr, c // tc),
            in_specs=[pl.BlockSpec((tr, tc), lambda i, j, p: (i, j))],
            out_specs=pl.BlockSpec((None, tr, tc), lambda i, j, p: (p[0], i, j))),
        out_shape=jax.ShapeDtypeStruct((N_CHIPS, r, c), BF16),
        compiler_params=_params("parallel", "parallel"))(place, w)


def _rms(v):
    return lax.rsqrt(jnp.mean(v * v, axis=-1, keepdims=True) + NORM_EPS)


def _rms_fwd(name, x, g):
    t, d = x.shape
    tr = _tile(t, 256)

    def body(x_ref, g_ref, h_ref):
        xv = x_ref[...]
        h_ref[...] = (xv * _rms(xv) * g_ref[...]).astype(BF16)

    row = pl.BlockSpec((tr, d), lambda i: (i, 0))
    vec = pl.BlockSpec((1, d), lambda i: (0, 0))
    return pl.pallas_call(
        body, name=name, grid=(t // tr,), in_specs=[row, vec], out_specs=row,
        out_shape=jax.ShapeDtypeStruct((t, d), BF16), compiler_params=_params("parallel"))(x, g)


def _post_pre_fwd(name, x, m, g_post, g_pre):
    t, d = x.shape
    tr = _tile(t, 256)

    def body(x_ref, m_ref, gp_ref, gn_ref, x1_ref, h_ref):
        mv = m_ref[...]
        x1 = x_ref[...] + mv * _rms(mv) * gp_ref[...]
        x1_ref[...] = x1
        h_ref[...] = (x1 * _rms(x1) * gn_ref[...]).astype(BF16)

    row = pl.BlockSpec((tr, d), lambda i: (i, 0))
    vec = pl.BlockSpec((1, d), lambda i: (0, 0))
    return pl.pallas_call(
        body, name=name, grid=(t // tr,), in_specs=[row, row, vec, vec], out_specs=[row, row],
        out_shape=[jax.ShapeDtypeStruct((t, d), F32), jax.ShapeDtypeStruct((t, d), BF16)],
        compiler_params=_params("parallel"))(x, m, g_post, g_pre)


def _norm_bwd(dout, nrm, r, g):
    gd = dout * g
    return r * (gd - nrm * jnp.mean(gd * nrm, axis=-1, keepdims=True))


def _loss_post_bwd(name, x1, m, g_post, target):
    t, d = x1.shape
    tr = _tile(t, 128)

    def body(x_ref, m_ref, g_ref, t_ref, loss_ref, dx_ref, dm_ref, dg_ref):
        i = pl.program_id(0)
        mv = m_ref[...]
        r = _rms(mv)
        nrm = mv * r
        err = x_ref[...] + nrm * g_ref[...] - t_ref[...]
        part = 0.5 * jnp.sum(jnp.mean(err * err, axis=-1, keepdims=True), axis=0, keepdims=True)
        dx = err / d
        dx_ref[...] = dx
        dm_ref[...] = _norm_bwd(dx, nrm, r, g_ref[...]).astype(BF16)
        dg = jnp.sum(dx * nrm, axis=0, keepdims=True)

        @pl.when(i == 0)
        def _():
            loss_ref[...] = part
            dg_ref[...] = dg

        @pl.when(i > 0)
        def _():
            loss_ref[...] += part
            dg_ref[...] += dg

    row = pl.BlockSpec((tr, d), lambda i: (i, 0))
    vec = pl.BlockSpec((1, d), lambda i: (0, 0))
    one = pl.BlockSpec((1, 1), lambda i: (0, 0))
    return pl.pallas_call(
        body, name=name, grid=(t // tr,), in_specs=[row, row, vec, row], out_specs=[one, row, row, vec],
        out_shape=[jax.ShapeDtypeStruct((1, 1), F32), jax.ShapeDtypeStruct((t, d), F32),
                   jax.ShapeDtypeStruct((t, d), BF16), jax.ShapeDtypeStruct((1, d), F32)],
        compiler_params=_params("arbitrary"))(x1, m, g_post, target)


def _mid_bwd(name, dx2, dh1, x1, g_pre, m0, g_post):
    t, d = x1.shape
    tr = _tile(t, 128)

    def body(dx2_ref, dh_ref, x_ref, gn_ref, m_ref, gp_ref, dx_ref, dm_ref, dgn_ref, dgp_ref):
        i = pl.program_id(0)
        xv = x_ref[...]
        r1 = _rms(xv)
        n1 = xv * r1
        dh = dh_ref[...]
        dx = dx2_ref[...] + _norm_bwd(dh, n1, r1, gn_ref[...])
        dx_ref[...] = dx
        mv = m_ref[...]
        r0 = _rms(mv)
        n0 = mv * r0
        dm_ref[...] = _norm_bwd(dx, n0, r0, gp_ref[...]).astype(BF16)
        dgn = jnp.sum(dh * n1, axis=0, keepdims=True)
        dgp = jnp.sum(dx * n0, axis=0, keepdims=True)

        @pl.when(i == 0)
        def _():
            dgn_ref[...] = dgn
            dgp_ref[...] = dgp

        @pl.when(i > 0)
        def _():
            dgn_ref[...] += dgn
            dgp_ref[...] += dgp

    row = pl.BlockSpec((tr, d), lambda i: (i, 0))
    vec = pl.BlockSpec((1, d), lambda i: (0, 0))
    return pl.pallas_call(
        body, name=name, grid=(t // tr,), in_specs=[row, row, row, vec, row, vec],
        out_specs=[row, row, vec, vec],
        out_shape=[jax.ShapeDtypeStruct((t, d), F32), jax.ShapeDtypeStruct((t, d), BF16),
                   jax.ShapeDtypeStruct((1, d), F32), jax.ShapeDtypeStruct((1, d), F32)],
        compiler_params=_params("arbitrary"))(dx2, dh1, x1, g_pre, m0, g_post)


def _pre_bwd(name, dx1, dh0, x, g_pre):
    t, d = x.shape
    tr = _tile(t, 128)

    def body(dx1_ref, dh_ref, x_ref, g_ref, dx_ref, dg_ref):
        i = pl.program_id(0)
        xv = x_ref[...]
        r = _rms(xv)
        nrm = xv * r
        dh = dh_ref[...]
        dx_ref[...] = dx1_ref[...] + _norm_bwd(dh, nrm, r, g_ref[...])
        dg = jnp.sum(dh * nrm, axis=0, keepdims=True)

        @pl.when(i == 0)
        def _():
            dg_ref[...] = dg

        @pl.when(i > 0)
        def _():
            dg_ref[...] += dg

    row = pl.BlockSpec((tr, d), lambda i: (i, 0))
    vec = pl.BlockSpec((1, d), lambda i: (0, 0))
    return pl.pallas_call(
        body, name=name, grid=(t // tr,), in_specs=[row, row, row, vec], out_specs=[row, vec],
        out_shape=[jax.ShapeDtypeStruct((t, d), F32), jax.ShapeDtypeStruct((1, d), F32)],
        compiler_params=_params("arbitrary"))(dx1, dh0, x, g_pre)


def _a_stats(name, pa, e):
    t = pa.shape[0]
    tr = _tile(t, 64, 8)

    def body(v_ref, mu_ref, rs_ref):
        vg, _ = _gelu(v_ref[...])
        mu = jnp.mean(vg, axis=-1, keepdims=True)
        xc = vg - mu
        mu_ref[...] = mu
        rs_ref[...] = lax.rsqrt(jnp.mean(xc * xc, axis=-1, keepdims=True) + NORM_EPS)

    col = pl.BlockSpec((tr, 1), lambda i: (i, 0))
    return pl.pallas_call(
        body, name=name, grid=(t // tr,), in_specs=[pl.BlockSpec((tr, e), lambda i: (i, 1))],
        out_specs=[col, col],
        out_shape=[jax.ShapeDtypeStruct((t, 1), F32), jax.ShapeDtypeStruct((t, 1), F32)],
        compiler_params=_params("parallel"))(pa)


def _causal(w):
    c = w.shape[0]
    keep = lax.broadcasted_iota(jnp.int32, (c, c), 0) >= lax.broadcasted_iota(jnp.int32, (c, c), 1)
    return jnp.where(keep, w, 0.0), keep


def _a_gate_fwd(name, pa, mu, rs, ln_g, ln_b, w_s, b_s3):
    t = pa.shape[0]
    nh, c, _ = w_s.shape
    e = ln_g.shape[1]
    dh = e // nh
    rb = 2 * c if t % (2 * c) == 0 else c

    def body(u_ref, v_ref, z_ref, mu_ref, rs_ref, g_ref, b_ref, w_ref, bs_ref, y_ref):
        wc = _causal(w_ref[...])[0].astype(BF16)
        vg, _ = _gelu(v_ref[...])
        vn = ((vg - mu_ref[...]) * rs_ref[...] * g_ref[...] + b_ref[...]).astype(BF16)
        for ci in range(rb // c):
            rows = pl.ds(ci * c, c)
            sv = jnp.dot(wc, vn[ci * c:(ci + 1) * c], preferred_element_type=F32) + bs_ref[...]
            u, _ = _gelu(u_ref[rows, :])
            sz, _ = _silu(z_ref[rows, :])
            y_ref[rows, :] = (u * sv * sz).astype(BF16)

    blk = lambda off: pl.BlockSpec((rb, dh), lambda i, h: (i, off + h))
    col = pl.BlockSpec((rb, 1), lambda i, h: (i, 0))
    vec = pl.BlockSpec((1, dh), lambda i, h: (0, h))
    return pl.pallas_call(
        body, name=name, grid=(t // rb, nh),
        in_specs=[blk(0), blk(nh), blk(2 * nh), col, col, vec, vec,
                  pl.BlockSpec((None, c, c), lambda i, h: (h, 0, 0)),
                  pl.BlockSpec((None, c, 1), lambda i, h: (h, 0, 0))],
        out_specs=pl.BlockSpec((rb, dh), lambda i, h: (i, h)),
        out_shape=jax.ShapeDtypeStruct((t, e), BF16),
        compiler_params=_params("parallel", "parallel"))(pa, pa, pa, mu, rs, ln_g, ln_b, w_s, b_s3)


def _a_gate_bwd1(name, pa, dy, mu, rs, ln_g, ln_b, w_s, b_s3):
    t = pa.shape[0]
    nh, c, _ = w_s.shape
    e = ln_g.shape[1]
    dh = e // nh
    rb = 2 * c if t % (2 * c) == 0 else c

    def body(u_ref, v_ref, z_ref, dy_ref, mu_ref, rs_ref, g_ref, b_ref, w_ref, bs_ref,
             du_ref, dz_ref, dsv_ref, c1_ref, c2_ref):
        h = pl.program_id(1)
        wc = _causal(w_ref[...])[0].astype(BF16)
        vg, _ = _gelu(v_ref[...])
        xh = (vg - mu_ref[...]) * rs_ref[...]
        vn = (xh * g_ref[...] + b_ref[...]).astype(BF16)
        s1 = []
        s2 = []
        for ci in range(rb // c):
            rows = pl.ds(ci * c, c)
            lo, hi = ci * c, (ci + 1) * c
            sv = jnp.dot(wc, vn[lo:hi], preferred_element_type=F32) + bs_ref[...]
            u, du = _gelu(u_ref[rows, :])
            zv = z_ref[rows, :]
            sz, dsz = _silu(zv)
            dyv = dy_ref[rows, :]
            du_ref[rows, :] = (dyv * sv * sz * du).astype(BF16)
            dz_ref[rows, :] = (dyv * u * sv * dsz).astype(BF16)
            dsv = (dyv * u * sz).astype(BF16)
            dsv_ref[rows, :] = dsv
            dvn = lax.dot_general(wc, dsv, (((0,), (0,)), ((), ())), preferred_element_type=F32)
            dxh = dvn * g_ref[...]
            s1.append(jnp.sum(dxh, axis=-1, keepdims=True))
            s2.append(jnp.sum(dxh * xh[lo:hi], axis=-1, keepdims=True))
        p1 = jnp.concatenate(s1, axis=0)
        p2 = jnp.concatenate(s2, axis=0)

        @pl.when(h == 0)
        def _():
            c1_ref[...] = p1
            c2_ref[...] = p2

        @pl.when(h > 0)
        def _():
            c1_ref[...] += p1
            c2_ref[...] += p2

    blk = lambda off: pl.BlockSpec((rb, dh), lambda i, h: (i, off + h))
    col = pl.BlockSpec((rb, 1), lambda i, h: (i, 0))
    vec = pl.BlockSpec((1, dh), lambda i, h: (0, h))
    act = jax.ShapeDtypeStruct((t, e), BF16)
    stat = jax.ShapeDtypeStruct((t, 1), F32)
    return pl.pallas_call(
        body, name=name, grid=(t // rb, nh),
        in_specs=[blk(0), blk(nh), blk(2 * nh), blk(0), col, col, vec, vec,
                  pl.BlockSpec((None, c, c), lambda i, h: (h, 0, 0)),
                  pl.BlockSpec((None, c, 1), lambda i, h: (h, 0, 0))],
        out_specs=[blk(0), blk(0), blk(0), col, col],
        out_shape=[act, act, act, stat, stat],
        compiler_params=_params("parallel", "arbitrary"))(pa, pa, pa, dy, mu, rs, ln_g, ln_b, w_s, b_s3)


def _a_gate_bwd2(name, pa, dsv, mu, rs, c1, c2, ln_g, ln_b, w_s):
    t = pa.shape[0]
    nh, c, _ = w_s.shape
    e = ln_g.shape[1]
    dh = e // nh
    rb = 2 * c if t % (2 * c) == 0 else c

    def body(v_ref, dsv_ref, mu_ref, rs_ref, c1_ref, c2_ref, g_ref, b_ref, w_ref,
             dv_ref, dw_ref, dbs_ref, dg_ref, db_ref):
        i = pl.program_id(1)
        wcf, keep = _causal(w_ref[...])
        wc = wcf.astype(BF16)
        vg, dvg = _gelu(v_ref[...])
        rsv = rs_ref[...]
        xh = (vg - mu_ref[...]) * rsv
        vn = (xh * g_ref[...] + b_ref[...]).astype(BF16)
        dw = jnp.zeros((c, c), F32)
        dbs = jnp.zeros((c, 1), F32)
        dvns = []
        for ci in range(rb // c):
            lo, hi = ci * c, (ci + 1) * c
            dsv = dsv_ref[pl.ds(lo, c), :]
            dvns.append(lax.dot_general(wc, dsv, (((0,), (0,)), ((), ())), preferred_element_type=F32))
            dw += lax.dot_general(dsv, vn[lo:hi], (((1,), (1,)), ((), ())), preferred_element_type=F32)
            dbs += jnp.sum(dsv.astype(F32), axis=-1, keepdims=True)
        dvn = jnp.concatenate(dvns, axis=0)
        dxh = dvn * g_ref[...]
        dvv = rsv * (dxh - c1_ref[...] * (1.0 / e) - xh * (c2_ref[...] * (1.0 / e)))
        dv_ref[...] = (dvv * dvg).astype(BF16)
        dw = jnp.where(keep, dw, 0.0)
        dg = jnp.sum(dvn * xh, axis=0, keepdims=True)
        db = jnp.sum(dvn, axis=0, keepdims=True)

        @pl.when(i == 0)
        def _():
            dw_ref[...] = dw
            dbs_ref[...] = dbs
            dg_ref[...] = dg
            db_ref[...] = db

        @pl.when(i > 0)
        def _():
            dw_ref[...] += dw
            dbs_ref[...] += dbs
            dg_ref[...] += dg
            db_ref[...] += db

    col = pl.BlockSpec((rb, 1), lambda h, i: (i, 0))
    vec = pl.BlockSpec((1, dh), lambda h, i: (0, h))
    hblk = pl.BlockSpec((rb, dh), lambda h, i: (i, h))
    return pl.pallas_call(
        body, name=name, grid=(nh, t // rb),
        in_specs=[pl.BlockSpec((rb, dh), lambda h, i: (i, nh + h)), hblk, col, col, col, col, vec, vec,
                  pl.BlockSpec((None, c, c), lambda h, i: (h, 0, 0))],
        out_specs=[hblk, pl.BlockSpec((None, c, c), lambda h, i: (h, 0, 0)),
                   pl.BlockSpec((None, c, 1), lambda h, i: (h, 0, 0)), vec, vec],
        out_shape=[jax.ShapeDtypeStruct((t, e), BF16), jax.ShapeDtypeStruct((nh, c, c), F32),
                   jax.ShapeDtypeStruct((nh, c, 1), F32), jax.ShapeDtypeStruct((1, e), F32),
                   jax.ShapeDtypeStruct((1, e), F32)],
        compiler_params=_params("parallel", "arbitrary"))(pa, dsv, mu, rs, c1, c2, ln_g, ln_b, w_s)


def _pool(name, src, e, seq, backward):
    t = src.shape[0]
    ng = len(POOL_WINDOWS)
    gw = e // ng
    cw = _tile(gw, 256)

    def shifted(a, j, pos):
        if backward:
            return jnp.where(pos < seq - j, pltpu.roll(a, seq - j, 0), 0.0)
        return jnp.where(pos >= j, pltpu.roll(a, j, 0), 0.0)

    def body(p_ref, o_ref):
        grp = (pl.program_id(1) * cw) // gw
        pos = lax.broadcasted_iota(jnp.int32, (seq, cw), 0)
        posf = (pos + 1).astype(F32)
        for k, w in enumerate(POOL_WINDOWS):

            @pl.when(grp == k)
            def _(w=w):
                pv = p_ref[...]
                cnt = jnp.minimum(posf, float(w))
                acc = pv / cnt if backward else pv
                j = 1
                while j < w:
                    acc = acc + shifted(acc, j, pos)
                    j *= 2
                out = acc - pv if backward else acc / cnt - pv
                o_ref[...] = out.astype(BF16)

    spec = pl.BlockSpec((seq, cw), lambda s, j: (s, j))
    return pl.pallas_call(
        body, name=name, grid=(t // seq, e // cw), in_specs=[spec], out_specs=spec,
        out_shape=jax.ShapeDtypeStruct((t, e), BF16),
        compiler_params=_params("parallel", "parallel"))(src)


def _b_gate_fwd(name, mm, pb, b_grp, scale):
    t, e = mm.shape
    tr, tc = _tile(t, 512), _tile(e, 1024)
    nc = e // tc

    def body(mm_ref, z_ref, b_ref, s_ref, y_ref):
        sz, _ = _silu(z_ref[...])
        y_ref[...] = ((mm_ref[...] + b_ref[...]) * s_ref[...] * sz).astype(BF16)

    blk = pl.BlockSpec((tr, tc), lambda i, j: (i, j))
    vec = pl.BlockSpec((1, tc), lambda i, j: (0, j))
    return pl.pallas_call(
        body, name=name, grid=(t // tr, nc),
        in_specs=[blk, pl.BlockSpec((tr, tc), lambda i, j: (i, nc + j)), vec, vec], out_specs=blk,
        out_shape=jax.ShapeDtypeStruct((t, e), BF16),
        compiler_params=_params("parallel", "parallel"))(mm, pb, b_grp, scale)


def _b_gate_bwd(name, dy, mm, pb, b_grp, scale):
    t, e = mm.shape
    tr, tc = _tile(t, 512), _tile(e, 1024)
    nc = e // tc

    def body(dy_ref, mm_ref, z_ref, b_ref, s_ref, dmm_ref, dz_ref, db_ref, ds_ref):
        i = pl.program_id(1)
        sz, dsz = _silu(z_ref[...])
        dyv = dy_ref[...]
        mb = mm_ref[...] + b_ref[...]
        dmixed = dyv * sz
        dmm = dmixed * s_ref[...]
        dmm_ref[...] = dmm.astype(BF16)
        dz_ref[...] = (dyv * (mb * s_ref[...]) * dsz).astype(BF16)
        db = jnp.sum(dmm, axis=0, keepdims=True)
        ds = jnp.sum(dmixed * mb, axis=0, keepdims=True)

        @pl.when(i == 0)
        def _():
            db_ref[...] = db
            ds_ref[...] = ds

        @pl.when(i > 0)
        def _():
            db_ref[...] += db
            ds_ref[...] += ds

    blk = pl.BlockSpec((tr, tc), lambda j, i: (i, j))
    vec = pl.BlockSpec((1, tc), lambda j, i: (0, j))
    act = jax.ShapeDtypeStruct((t, e), BF16)
    stat = jax.ShapeDtypeStruct((1, e), F32)
    return pl.pallas_call(
        body, name=name, grid=(nc, t // tr),
        in_specs=[blk, blk, pl.BlockSpec((tr, tc), lambda j, i: (i, nc + j)), vec, vec],
        out_specs=[blk, blk, vec, vec], out_shape=[act, act, stat, stat],
        compiler_params=_params("parallel", "arbitrary"))(dy, mm, pb, b_grp, scale)


def _position():
    return lax.axis_index("x"), lax.axis_index("y"), lax.axis_index("c")


def _other_chips(x, y):
    return [(1 - x, y), (x, 1 - y), (1 - x, 1 - y)]


def _half(ref, c):
    n = ref.shape[0] // 2
    return ref.at[pl.ds(c * n, n)]


def _gather_weights(slots, small):
    nw = len(slots)
    n_ici = 3 * nw + 3

    def body(*refs):
        srcs, sm = refs[:nw], refs[nw]
        dsts, gsm = refs[nw + 1:2 * nw + 1], refs[2 * nw + 1]
        send_sems, recv_sems, loc_sem = refs[2 * nw + 2:]
        x, y, c = _position()
        me = 2 * x + y
        sibling = (x, y, 1 - c)
        chips = _other_chips(x, y)
        local = pltpu.make_async_copy(sm, gsm.at[me], loc_sem)
        local.start()

        def ici(j, k, slot, to):
            return pltpu.make_async_remote_copy(
                src_ref=_half(srcs[k].at[slot], c), dst_ref=_half(dsts[k].at[slot], c),
                send_sem=send_sems.at[j * nw + k], recv_sem=recv_sems.at[j * nw + k],
                device_id=to, device_id_type=MESH)

        def ici_small(j, slot, to):
            return pltpu.make_async_remote_copy(
                src_ref=sm, dst_ref=gsm.at[slot], send_sem=send_sems.at[3 * nw + j],
                recv_sem=recv_sems.at[3 * nw + j], device_id=to, device_id_type=MESH)

        def d2d(j, k, slot, half):
            return pltpu.make_async_remote_copy(
                src_ref=_half(dsts[k].at[slot], half), dst_ref=_half(dsts[k].at[slot], half),
                send_sem=send_sems.at[n_ici + j * nw + k], recv_sem=recv_sems.at[n_ici + j * nw + k],
                device_id=sibling, device_id_type=MESH)

        sent = []
        for j, (cx, cy) in enumerate(chips):
            for k in range(nw):
                sent.append(ici(j, k, me, (cx, cy, c)))
            sent.append(ici_small(j, me, (cx, cy, c)))
        for cp in sent:
            cp.start()
        passed = []
        for j, (cx, cy) in enumerate(chips):
            for k in range(nw):
                ici(j, k, 2 * cx + cy, sibling).wait_recv()
                cp = d2d(j, k, 2 * cx + cy, c)
                cp.start()
                passed.append(cp)
        for j, (cx, cy) in enumerate(chips):
            ici_small(j, 2 * cx + cy, sibling).wait_recv()
            for k in range(nw):
                d2d(j, k, 2 * cx + cy, 1 - c).wait_recv()
        for cp in sent + passed:
            cp.wait_send()
        local.wait()

    out_shape = [jax.ShapeDtypeStruct(s.shape, s.dtype) for s in slots]
    out_shape.append(jax.ShapeDtypeStruct((N_CHIPS,) + small.shape, small.dtype))
    n_sem = n_ici + 3 * nw
    return pl.pallas_call(
        body, name="gather_weights", in_specs=[ANY] * (nw + 1), out_specs=[ANY] * (nw + 1), out_shape=out_shape,
        input_output_aliases={k: k for k in range(nw)},
        scratch_shapes=[pltpu.SemaphoreType.DMA((n_sem,)), pltpu.SemaphoreType.DMA((n_sem,)),
                        pltpu.SemaphoreType.DMA(())],
    )(*slots, small)


def _swap_halves(grads):
    nw = len(grads)

    def body(*refs):
        srcs, theirs = refs[:nw], refs[nw:2 * nw]
        send_sems, recv_sems = refs[2 * nw:]
        x, y, c = _position()
        swaps = []
        for k in range(nw):
            n = srcs[k].shape[1] // 2
            swaps.append(pltpu.make_async_remote_copy(
                src_ref=srcs[k].at[pl.ds(0, N_CHIPS), pl.ds((1 - c) * n, n)], dst_ref=theirs[k],
                send_sem=send_sems.at[k], recv_sem=recv_sems.at[k],
                device_id=(x, y, 1 - c), device_id_type=MESH))
        for cp in swaps:
            cp.start()
        for cp in swaps:
            cp.wait()

    halves = [jax.ShapeDtypeStruct((g.shape[0], g.shape[1] // 2, g.shape[2]), F32) for g in grads]
    return pl.pallas_call(
        body, name="swap_grad_halves", in_specs=[ANY] * nw, out_specs=[ANY] * nw, out_shape=halves,
        scratch_shapes=[pltpu.SemaphoreType.DMA((nw,)), pltpu.SemaphoreType.DMA((nw,))],
    )(*grads)


def _add_halves(name, grad, theirs, place):
    s, half, w = theirs.shape
    tr, tc = _tile(half, 256), _tile(w, 2048)
    nrt = half // tr

    def body(place_ref, a_ref, b_ref, h_ref, f_ref):
        v = a_ref[...] + b_ref[...]
        h_ref[...] = v.astype(BF16)

        @pl.when(pl.program_id(2) == place_ref[0])
        def _():
            f_ref[...] = v

    return pl.pallas_call(
        body, name=name,
        grid_spec=pltpu.PrefetchScalarGridSpec(
            num_scalar_prefetch=1, grid=(nrt, w // tc, s),
            in_specs=[pl.BlockSpec((None, tr, tc), lambda i, j, q, p: (q, p[1] * nrt + i, j)),
                      pl.BlockSpec((None, tr, tc), lambda i, j, q, p: (q, i, j))],
            out_specs=[pl.BlockSpec((None, tr, tc), lambda i, j, q, p: (q, i, j)),
                       pl.BlockSpec((tr, tc), lambda i, j, q, p: (p[1] * nrt + i, j))]),
        out_shape=[jax.ShapeDtypeStruct(theirs.shape, BF16), jax.ShapeDtypeStruct((2 * half, w), F32)],
        compiler_params=_params("parallel", "parallel", "arbitrary"))(place, grad, theirs)


def _scatter_grads(own, pb):
    nw = len(own)

    def body(*refs):
        own_in, pbs = refs[:nw], refs[nw:2 * nw]
        own_out, got = refs[2 * nw:3 * nw], refs[3 * nw:4 * nw]
        send_sems, recv_sems = refs[4 * nw:]
        x, y, c = _position()
        me = 2 * x + y
        sibling = (x, y, 1 - c)
        chips = _other_chips(x, y)

        def d2d_own(k, half):
            return pltpu.make_async_remote_copy(
                src_ref=_half(own_in[k], half), dst_ref=_half(own_out[k], half), send_sem=send_sems.at[k],
                recv_sem=recv_sems.at[k], device_id=sibling, device_id_type=MESH)

        def ici(j, k, shard, to):
            return pltpu.make_async_remote_copy(
                src_ref=pbs[k].at[shard], dst_ref=_half(got[k].at[j], c),
                send_sem=send_sems.at[nw + j * nw + k], recv_sem=recv_sems.at[nw + j * nw + k],
                device_id=to, device_id_type=MESH)

        def d2d(j, k, half):
            return pltpu.make_async_remote_copy(
                src_ref=_half(got[k].at[j], half), dst_ref=_half(got[k].at[j], half),
                send_sem=send_sems.at[4 * nw + j * nw + k], recv_sem=recv_sems.at[4 * nw + j * nw + k],
                device_id=sibling, device_id_type=MESH)

        sent = [d2d_own(k, c) for k in range(nw)]
        for j, (cx, cy) in enumerate(chips):
            for k in range(nw):
                sent.append(ici(j, k, 2 * cx + cy, (cx, cy, c)))
        for cp in sent:
            cp.start()
        passed = []
        for j in range(3):
            for k in range(nw):
                ici(j, k, me, sibling).wait_recv()
                cp = d2d(j, k, c)
                cp.start()
                passed.append(cp)
        for k in range(nw):
            d2d_own(k, 1 - c).wait_recv()
        for j in range(3):
            for k in range(nw):
                d2d(j, k, 1 - c).wait_recv()
        for cp in sent + passed:
            cp.wait_send()

    own_shape = [jax.ShapeDtypeStruct(o.shape, F32) for o in own]
    got_shape = [jax.ShapeDtypeStruct((3,) + o.shape, BF16) for o in own]
    n_sem = 7 * nw
    return pl.pallas_call(
        body, name="scatter_grads", in_specs=[ANY] * (2 * nw), out_specs=[ANY] * (2 * nw),
        out_shape=own_shape + got_shape, input_output_aliases={k: k for k in range(nw)},
        scratch_shapes=[pltpu.SemaphoreType.DMA((n_sem,)), pltpu.SemaphoreType.DMA((n_sem,))],
    )(*own, *pb)


def _gather_small(part):
    def body(p_ref, o_ref, send_sems, recv_sems, loc_sem):
        x, y, c = _position()
        me = 4 * x + 2 * y + c
        mine = pltpu.make_async_copy(p_ref, o_ref.at[me], loc_sem)
        mine.start()
        sent = []
        for mask in range(1, N_DEV):
            dx, dy, dc = (mask >> 2) & 1, (mask >> 1) & 1, mask & 1
            px, py, pc = x ^ dx, y ^ dy, c ^ dc
            cp = pltpu.make_async_remote_copy(
                src_ref=p_ref, dst_ref=o_ref.at[me], send_sem=send_sems.at[mask - 1],
                recv_sem=recv_sems.at[mask - 1], device_id=(px, py, pc), device_id_type=MESH)
            cp.start()
            sent.append((cp, 4 * px + 2 * py + pc))
        for mask in range(1, N_DEV):
            cp, peer = sent[mask - 1]
            pltpu.make_async_remote_copy(
                src_ref=p_ref, dst_ref=o_ref.at[peer], send_sem=send_sems.at[mask - 1],
                recv_sem=recv_sems.at[mask - 1], device_id=(x, y, c), device_id_type=MESH).wait_recv()
        for cp, _ in sent:
            cp.wait_send()
        mine.wait()

    return pl.pallas_call(
        body, name="gather_small_grads", in_specs=[ANY], out_specs=ANY,
        out_shape=jax.ShapeDtypeStruct((N_DEV,) + part.shape, F32),
        scratch_shapes=[pltpu.SemaphoreType.DMA((N_DEV - 1,)), pltpu.SemaphoreType.DMA((N_DEV - 1,)),
                        pltpu.SemaphoreType.DMA(())],
    )(part)


def _sum_slots(name, parts):
    s, r, c = parts.shape
    tr = _tile(r, 512, 8)

    def body(p_ref, o_ref):
        acc = p_ref[0]
        for q in range(1, s):
            acc = acc + p_ref[q]
        o_ref[...] = acc

    return pl.pallas_call(
        body, name=name, grid=(r // tr,), in_specs=[pl.BlockSpec((s, tr, c), lambda i: (0, i, 0))],
        out_specs=pl.BlockSpec((tr, c), lambda i: (i, 0)), out_shape=jax.ShapeDtypeStruct((r, c), F32),
        compiler_params=_params("parallel"))(parts)


def _adamw_math(w, g, m, v):
    m = ADAM_B1 * m + (1.0 - ADAM_B1) * g
    v = ADAM_B2 * v + (1.0 - ADAM_B2) * (g * g)
    m_hat = m / (1.0 - ADAM_B1 ** ADAM_STEP)
    v_hat = v / (1.0 - ADAM_B2 ** ADAM_STEP)
    delta = -ADAM_LR * (m_hat / (jnp.sqrt(v_hat) + ADAM_EPS) + ADAM_WD * w)
    return delta, m, v


def _adamw(name, w, m, v, own, got=None):
    r, c = w.shape
    tr, tc = _tile(r, 256, 8), _tile(c, 1024)

    def body(*refs):
        if got is None:
            w_ref, m_ref, v_ref, own_ref = refs[:4]
            outs = refs[4:]
            g = own_ref[...]
        else:
            w_ref, m_ref, v_ref, own_ref, got_ref = refs[:5]
            outs = refs[5:]
            g = own_ref[...]
            for j in range(3):
                g = g + got_ref[j].astype(F32)
        delta, mn, vn = _adamw_math(w_ref[...], g, m_ref[...], v_ref[...])
        outs[0][...] = g
        outs[1][...] = delta
        outs[2][...] = mn
        outs[3][...] = vn

    blk = pl.BlockSpec((tr, tc), lambda i, j: (i, j))
    ins, args = [blk] * 4, [w, m, v, own]
    if got is not None:
        ins.append(pl.BlockSpec((3, tr, tc), lambda i, j: (0, i, j)))
        args.append(got)
    return pl.pallas_call(
        body, name=name, grid=(r // tr, c // tc), in_specs=ins, out_specs=[blk] * 4,
        out_shape=[jax.ShapeDtypeStruct((r, c), F32)] * 4,
        compiler_params=_params("parallel", "parallel"))(*args)


def _pack(parts):
    return jnp.concatenate([p.reshape(-1) for p in parts]).reshape(-1, LANES)


def _unpack(packed, shapes):
    flat = packed.reshape(-1)
    out, off = [], 0
    for s in shapes:
        n = 1
        for d in s:
            n *= d
        out.append(flat[off:off + n].reshape(s))
        off += n
    return out


def kernel(x, pre_norm, post_norm, a_w_in, a_ln_g, a_ln_b, a_w_s, a_b_s, a_w_out, b_w_in, b_w_grp, b_b_grp, b_scale, b_w_out, loss_target, m_pre_norm, m_post_norm, m_a_w_in, m_a_ln_g, m_a_ln_b, m_a_w_s, m_a_b_s, m_a_w_out, m_b_w_in, m_b_w_grp, m_b_b_grp, m_b_scale, m_b_w_out, v_pre_norm, v_post_norm, v_a_w_in, v_a_ln_g, v_a_ln_b, v_a_w_s, v_a_b_s, v_a_w_out, v_b_w_in, v_b_w_grp, v_b_b_grp, v_b_scale, v_b_w_out):
    nb, seq, d = x.shape
    t = nb * seq
    e = a_ln_g.shape[1]
    nh, chunk = a_w_s.shape[1], a_w_s.shape[2]
    ng, rs, gw = b_w_grp.shape[1], b_w_grp.shape[2], b_w_grp.shape[3]
    wa, wb = a_w_in.shape[2], b_w_in.shape[2]
    cx, cy = lax.axis_index("x"), lax.axis_index("y")
    chip = 2 * cx + cy

    big_w = [a_w_in.reshape(d, wa), a_w_out.reshape(e // N_CHIPS, d), b_w_in.reshape(d, wb),
             b_w_grp.reshape(ng * rs, gw), b_w_out.reshape(e // N_CHIPS, d)]
    big_m = [m_a_w_in.reshape(d, wa), m_a_w_out.reshape(e // N_CHIPS, d), m_b_w_in.reshape(d, wb),
             m_b_w_grp.reshape(ng * rs, gw), m_b_w_out.reshape(e // N_CHIPS, d)]
    big_v = [v_a_w_in.reshape(d, wa), v_a_w_out.reshape(e // N_CHIPS, d), v_b_w_in.reshape(d, wb),
             v_b_w_grp.reshape(ng * rs, gw), v_b_w_out.reshape(e // N_CHIPS, d)]
    names = ["a_w_in", "a_w_out", "b_w_in", "b_w_grp", "b_w_out"]
    place = jnp.stack([chip, lax.axis_index("c")]).astype(jnp.int32)
    slots = [_cast_into_slot("cast_" + n, w, place) for n, w in zip(names, big_w)]
    small_w = jnp.concatenate([b_b_grp.reshape(ng, rs), b_scale.reshape(ng, rs)], axis=0)
    *gathered, gsmall = _gather_weights(slots, small_w)
    wa_g, wao_g, wb_g, wg_g, wbo_g = gathered
    wao_g = wao_g.reshape(e, d)
    wbo_g = wbo_g.reshape(e, d)
    wg_g = wg_g.reshape(N_CHIPS, ng, rs, gw)
    b_grp_full = jnp.transpose(gsmall[:, :ng, :], (1, 0, 2)).reshape(1, e)
    scale_full = gsmall[:, ng:, :].reshape(1, e)

    xf = x.reshape(t, d)
    tgt = loss_target.reshape(t, d)
    g_pre0, g_pre1 = pre_norm[0:1], pre_norm[1:2]
    g_post0, g_post1 = post_norm[0:1], post_norm[1:2]
    w_s = a_w_s.reshape(nh, chunk, chunk)
    b_s3 = a_b_s.reshape(nh, chunk, 1)

    h0 = _rms_fwd("pre_norm0", xf, g_pre0)
    pa = _mm_nn_cols("a_in_proj", h0, wa_g)
    mu, rstd = _a_stats("a_ln_stats", pa, e)
    y0 = _a_gate_fwd("a_gate", pa, mu, rstd, a_ln_g, a_ln_b, w_s, b_s3)
    m0 = _mm_nn("a_out_proj", y0, wao_g)
    x1, h1 = _post_pre_fwd("post0_pre1", xf, m0, g_post0, g_pre1)
    pb = _mm_nn_cols("b_in_proj", h1, wb_g)
    pooled = _pool("b_pool", pb, e, seq, backward=False)
    mm = _grp_fwd("b_grp_proj", pooled, wg_g)
    y1 = _b_gate_fwd("b_gate", mm, pb, b_grp_full, scale_full)
    m1 = _mm_nn("b_out_proj", y1, wbo_g)
    loss, dx2, dm1, dg_post1 = _loss_post_bwd("loss_post1_bwd", x1, m1, g_post1, tgt)
    dy1 = _mm_nt("b_out_dx", dm1, wbo_g)
    g_wbo = _mm_tn("b_out_dw", y1, dm1)
    dmm, dzb, db_grp, dscale = _b_gate_bwd("b_gate_bwd", dy1, mm, pb, b_grp_full, scale_full)
    dpooled = _grp_bwd_x("b_grp_dx", dmm, wg_g)
    g_wg = _grp_bwd_w("b_grp_dw", pooled, dmm, ng)
    dp = _pool("b_pool_bwd", dpooled, e, seq, backward=True)
    dpb = jnp.concatenate([dp, dzb], axis=1)
    dh1 = _mm_nt_cols("b_in_dx", dpb, wb_g)
    g_wb = _mm_tn_cols("b_in_dw", h1, dpb, wb)
    dx1, dm0, dg_pre1, dg_post0 = _mid_bwd("pre1_post0_bwd", dx2, dh1, x1, g_pre1, m0, g_post0)
    dy0 = _mm_nt("a_out_dx", dm0, wao_g)
    g_wao = _mm_tn("a_out_dw", y0, dm0)
    du, dz, dsv, c1, c2 = _a_gate_bwd1("a_gate_bwd1", pa, dy0, mu, rstd, a_ln_g, a_ln_b, w_s, b_s3)
    dv, dw_s, db_s, dln_g, dln_b = _a_gate_bwd2("a_gate_bwd2", pa, dsv, mu, rstd, c1, c2, a_ln_g, a_ln_b, w_s)
    dpa = jnp.concatenate([du, dv, dz], axis=1)
    dh0 = _mm_nt_cols("a_in_dx", dpa, wa_g)
    g_wa = _mm_tn_cols("a_in_dw", h0, dpa, wa)
    grad_x, dg_pre0 = _pre_bwd("pre0_bwd", dx1, dh0, xf, g_pre0)

    grads = [g_wa, g_wao.reshape(N_CHIPS, e // N_CHIPS, d), g_wb, g_wg.reshape(N_CHIPS, ng * rs, gw),
             g_wbo.reshape(N_CHIPS, e // N_CHIPS, d)]
    theirs = _swap_halves(grads)
    nw = len(grads)
    sums = [_add_halves("chip_sum_" + n, grads[k], theirs[k], place) for k, n in enumerate(names)]
    scattered = _scatter_grads([s[1] for s in sums], [s[0] for s in sums])
    big_out = [_adamw("adamw_" + n, big_w[k], big_m[k], big_v[k], scattered[k], scattered[nw + k])
               for k, n in enumerate(names)]

    small_shapes = [(2, d), (2, d), (1, e), (1, e), (1, nh, chunk, chunk), (1, nh, chunk), (1, e), (1, e)]
    part = _pack([jnp.concatenate([dg_pre0, dg_pre1], axis=0), jnp.concatenate([dg_post0, dg_post1], axis=0),
                  dln_g, dln_b, dw_s, db_s, db_grp, dscale])
    g_small = _sum_slots("sum_small_grads", _gather_small(part))
    g_pre, g_post, g_lng, g_lnb, g_ws, g_bs, g_bgrp_full, g_scale_full = _unpack(g_small, small_shapes)
    g_bgrp = lax.dynamic_slice_in_dim(g_bgrp_full.reshape(ng, N_CHIPS, rs), chip, 1, axis=1).reshape(1, ng, rs)
    g_scale = lax.dynamic_slice_in_dim(g_scale_full.reshape(N_CHIPS, gw), chip, 1, axis=0)
    small_names = ["pre_norm", "post_norm", "a_ln_g", "a_ln_b", "a_w_s", "a_b_s", "b_b_grp", "b_scale"]
    small_g = [g_pre, g_post, g_lng, g_lnb, g_ws, g_bs, g_bgrp, g_scale]
    small_ws = [pre_norm, post_norm, a_ln_g, a_ln_b, a_w_s, a_b_s, b_b_grp, b_scale]
    small_ms = [m_pre_norm, m_post_norm, m_a_ln_g, m_a_ln_b, m_a_w_s, m_a_b_s, m_b_b_grp, m_b_scale]
    small_vs = [v_pre_norm, v_post_norm, v_a_ln_g, v_a_ln_b, v_a_w_s, v_a_b_s, v_b_b_grp, v_b_scale]
    packed = _adamw("adamw_small", _pack(small_ws), _pack(small_ms), _pack(small_vs), _pack(small_g))
    small_out = [_unpack(p, [w.shape for w in small_ws]) for p in packed]

    loss = lax.psum(loss[0, 0], ("x", "y", "c"))
    order = ["pre_norm", "post_norm", "a_w_in", "a_ln_g", "a_ln_b", "a_w_s", "a_b_s", "a_w_out", "b_w_in",
             "b_w_grp", "b_b_grp", "b_scale", "b_w_out"]
    big_shapes = dict(zip(names, [a_w_in.shape, a_w_out.shape, b_w_in.shape, b_w_grp.shape, b_w_out.shape]))
    outs = [loss, grad_x.reshape(nb, seq, d)]
    for kind in range(4):
        for n in order:
            if n in big_shapes:
                outs.append(big_out[names.index(n)][kind].reshape(big_shapes[n]))
            else:
                outs.append(small_out[kind][small_names.index(n)])
    return tuple(outs)
```

```python
import functools

import jax
import jax.numpy as jnp
from jax import lax
from jax.experimental import pallas as pl
from jax.experimental.pallas import tpu as pltpu

F32 = jnp.float32
BF16 = jnp.bfloat16
NORM_EPS = 1e-6
POOL_WINDOWS = (2, 4, 8, 16)
ADAM_LR = 0.001
ADAM_B1 = 0.9
ADAM_B2 = 0.999
ADAM_EPS = 1e-08
ADAM_WD = 0.01
ADAM_STEP = 10
N_CHIPS = 4
N_DEV = 8
V7X_VMEM_LIMIT_BYTES = 56 * 1024 * 1024
LANES = 128
MESH = pl.DeviceIdType.MESH
ANY = pl.BlockSpec(memory_space=pl.ANY)
SQRT_HALF = 0.7071067811865476
INV_SQRT_2PI = 0.3989422804014327


def _tile(dim, pref, mult=LANES):
    if dim <= pref:
        return dim
    t = (pref // mult) * mult
    while t >= mult:
        if dim % t == 0:
            return t
        t -= mult
    return dim


def _params(*sem):
    return pltpu.CompilerParams(dimension_semantics=sem or None, vmem_limit_bytes=V7X_VMEM_LIMIT_BYTES)


def _gelu(x):
    cdf = 0.5 * (1.0 + lax.erf(x * SQRT_HALF))
    pdf = jnp.exp(-0.5 * x * x) * INV_SQRT_2PI
    return x * cdf, cdf + x * pdf


def _silu(z):
    s = 1.0 / (1.0 + jnp.exp(-z))
    return z * s, s * (1.0 + z * (1.0 - s))


class _Phase:
    def __init__(self, ins, outs, aliases, n_sem, start, finish):
        self.ins, self.outs, self.aliases, self.n_sem = list(ins), list(outs), dict(aliases), n_sem
        self.start, self.finish = start, finish


def _host(phases, n_in, n_out):
    ins, outs, aliases, sems = [], [], {}, []
    for ph in phases:
        for i, o in ph.aliases.items():
            aliases[n_in + len(ins) + i] = n_out + len(outs) + o
        ins += ph.ins
        outs += ph.outs
        sems += [pltpu.SemaphoreType.DMA((ph.n_sem,)), pltpu.SemaphoreType.DMA((ph.n_sem,))]
    return ins, outs, aliases, sems


def _run_phases(phases, which, in_refs, out_refs, sem_refs):
    i = o = 0
    for p, ph in enumerate(phases):
        fn = ph.start if which == "start" else ph.finish
        fn(in_refs[i:i + len(ph.ins)], out_refs[o:o + len(ph.outs)], sem_refs[2 * p], sem_refs[2 * p + 1])
        i += len(ph.ins)
        o += len(ph.outs)


def _comm_only(name, phases):
    ins, outs, aliases, sems = _host(phases, 0, 0)

    def body(*refs):
        in_refs, out_refs, sem_refs = refs[:len(ins)], refs[len(ins):len(ins) + len(outs)], refs[len(ins) + len(outs):]
        _run_phases(phases, "start", in_refs, out_refs, sem_refs)
        _run_phases(phases, "finish", in_refs, out_refs, sem_refs)

    return pl.pallas_call(
        body, name=name, in_specs=[ANY] * len(ins), out_specs=[ANY] * len(outs), out_shape=outs,
        input_output_aliases=aliases, scratch_shapes=sems)(*ins)


def _matmul(name, a, b, *, dims, grid, a_spec, b_spec, o_spec, out_shape, out_dtype, phases=()):
    nk = grid[2]
    contract = {"nn": ((1,), (0,)), "nt": ((1,), (1,)), "tn": ((0,), (0,))}[dims]
    acc_in_out = out_dtype == F32 or nk == 1
    blk = tuple(d for d in o_spec.block_shape if d is not None)
    x_ins, x_outs, aliases, sems = _host(phases, 2, 1)
    n_acc = 0 if acc_in_out else 1

    def body(*refs):
        a_ref, b_ref = refs[:2]
        in_refs = refs[2:2 + len(x_ins)]
        o_ref = refs[2 + len(x_ins)]
        out_refs = refs[3 + len(x_ins):3 + len(x_ins) + len(x_outs)]
        scratch = refs[3 + len(x_ins) + len(x_outs):]
        sem_refs = scratch[n_acc:]
        ids = [pl.program_id(d) for d in range(3)]
        if phases:

            @pl.when((ids[0] == 0) & (ids[1] == 0) & (ids[2] == 0))
            def _():
                _run_phases(phases, "start", in_refs, out_refs, sem_refs)

        part = lax.dot_general(a_ref[...], b_ref[...], (contract, ((), ())), preferred_element_type=F32)
        if nk == 1:
            o_ref[...] = part.astype(out_dtype)
        else:
            acc = o_ref if acc_in_out else scratch[0]
            k = ids[2]

            @pl.when(k == 0)
            def _():
                acc[...] = part

            @pl.when(k > 0)
            def _():
                acc[...] += part

            if not acc_in_out:

                @pl.when(k == nk - 1)
                def _():
                    o_ref[...] = acc[...].astype(out_dtype)

        if phases:

            @pl.when((ids[0] == grid[0] - 1) & (ids[1] == grid[1] - 1) & (ids[2] == grid[2] - 1))
            def _():
                _run_phases(phases, "finish", in_refs, out_refs, sem_refs)

    sem = ("arbitrary",) * 3 if phases else ("parallel", "parallel", "arbitrary")
    res = pl.pallas_call(
        body,
        name=name,
        grid=grid,
        in_specs=[a_spec, b_spec] + [ANY] * len(x_ins),
        out_specs=[o_spec] + [ANY] * len(x_outs),
        out_shape=[jax.ShapeDtypeStruct(out_shape, out_dtype)] + x_outs,
        input_output_aliases=aliases,
        scratch_shapes=([] if acc_in_out else [pltpu.VMEM(blk, F32)]) + sems,
        compiler_params=_params(*sem),
    )(a, b, *x_ins)
    return (res[0], list(res[1:])) if phases else res[0]


MM_K_WHOLE = 4096


def _tiles(m, n, k, k_total=None):
    if (k if k_total is None else k_total) <= MM_K_WHOLE:
        return _tile(m, 1024), _tile(n, 1024), k
    return _tile(m, 1024), _tile(n, 2048), _tile(k, 2048)


def _mm_nn(name, a, b, out_dtype=F32):
    m, k = a.shape
    n = b.shape[1]
    tm, tn, tk = _tiles(m, n, k)
    return _matmul(
        name, a, b, dims="nn", grid=(m // tm, n // tn, k // tk),
        a_spec=pl.BlockSpec((tm, tk), lambda i, j, l: (i, l)),
        b_spec=pl.BlockSpec((tk, tn), lambda i, j, l: (l, j)),
        o_spec=pl.BlockSpec((tm, tn), lambda i, j, l: (i, j)),
        out_shape=(m, n), out_dtype=out_dtype)


def _mm_nt(name, a, b, out_dtype=F32, phases=()):
    m, k = a.shape
    n = b.shape[0]
    tm, tn, tk = _tiles(m, n, k)
    return _matmul(
        name, a, b, dims="nt", grid=(m // tm, n // tn, k // tk),
        a_spec=pl.BlockSpec((tm, tk), lambda i, j, l: (i, l)),
        b_spec=pl.BlockSpec((tn, tk), lambda i, j, l: (j, l)),
        o_spec=pl.BlockSpec((tm, tn), lambda i, j, l: (i, j)),
        out_shape=(m, n), out_dtype=out_dtype, phases=phases)


def _mm_tn(name, a, b, out_dtype=F32):
    k, m = a.shape
    n = b.shape[1]
    tm, tn, tk = _tiles(m, n, k)
    return _matmul(
        name, a, b, dims="tn", grid=(m // tm, n // tn, k // tk),
        a_spec=pl.BlockSpec((tk, tm), lambda i, j, l: (l, i)),
        b_spec=pl.BlockSpec((tk, tn), lambda i, j, l: (l, j)),
        o_spec=pl.BlockSpec((tm, tn), lambda i, j, l: (i, j)),
        out_shape=(m, n), out_dtype=out_dtype)


def _mm_nn_cols(name, a, wg, phases=()):
    m, k = a.shape
    ws = wg.shape[2]
    tm, tn, tk = _tiles(m, ws, k)
    npb = ws // tn
    return _matmul(
        name, a, wg, dims="nn", grid=(m // tm, N_CHIPS * npb, k // tk),
        a_spec=pl.BlockSpec((tm, tk), lambda i, j, l: (i, l)),
        b_spec=pl.BlockSpec((None, tk, tn), lambda i, j, l: (j // npb, l, j % npb)),
        o_spec=pl.BlockSpec((tm, tn), lambda i, j, l: (i, j)),
        out_shape=(m, N_CHIPS * ws), out_dtype=F32, phases=phases)


def _mm_nt_cols(name, a, wg, phases=()):
    m = a.shape[0]
    n, ws = wg.shape[1], wg.shape[2]
    tm, tn, tk = _tiles(m, n, ws, N_CHIPS * ws)
    kpb = ws // tk
    return _matmul(
        name, a, wg, dims="nt", grid=(m // tm, n // tn, N_CHIPS * kpb),
        a_spec=pl.BlockSpec((tm, tk), lambda i, j, l: (i, l)),
        b_spec=pl.BlockSpec((None, tn, tk), lambda i, j, l: (l // kpb, j, l % kpb)),
        o_spec=pl.BlockSpec((tm, tn), lambda i, j, l: (i, j)),
        out_shape=(m, n), out_dtype=F32, phases=phases)


def _mm_tn_cols(name, a, b, ws, phases=()):
    k, m = a.shape
    tm, tn, tk = _tiles(m, ws, k)
    npb = ws // tn
    return _matmul(
        name, a, b, dims="tn", grid=(m // tm, N_CHIPS * npb, k // tk),
        a_spec=pl.BlockSpec((tk, tm), lambda i, j, l: (l, i)),
        b_spec=pl.BlockSpec((tk, tn), lambda i, j, l: (l, j)),
        o_spec=pl.BlockSpec((None, tm, tn), lambda i, j, l: (j // npb, i, j % npb)),
        out_shape=(N_CHIPS, m, ws), out_dtype=F32, phases=phases)


def _grp_fwd(name, pooled, wgg):
    t, e = pooled.shape
    _, ng, rs, gw = wgg.shape
    tm, tn, tk = _tiles(t, gw, rs, gw)
    npb, kps = gw // tn, rs // tk
    return _matmul(
        name, pooled, wgg, dims="nn", grid=(t // tm, ng * npb, N_CHIPS * kps),
        a_spec=pl.BlockSpec((tm, tk), lambda i, j, l: (i, (j // npb) * (gw // tk) + l)),
        b_spec=pl.BlockSpec((None, None, tk, tn), lambda i, j, l: (l // kps, j // npb, l % kps, j % npb)),
        o_spec=pl.BlockSpec((tm, tn), lambda i, j, l: (i, j)),
        out_shape=(t, e), out_dtype=F32)


def _grp_bwd_x(name, dmm, wgg):
    t, e = dmm.shape
    _, ng, rs, gw = wgg.shape
    tm, tn, tk = _tiles(t, rs, gw)
    npr, kpg = rs // tn, gw // tk
    return _matmul(
        name, dmm, wgg, dims="nt", grid=(t // tm, ng * N_CHIPS * npr, kpg),
        a_spec=pl.BlockSpec((tm, tk), lambda i, j, l: (i, (j // (N_CHIPS * npr)) * kpg + l)),
        b_spec=pl.BlockSpec(
            (None, None, tn, tk),
            lambda i, j, l: ((j % (N_CHIPS * npr)) // npr, j // (N_CHIPS * npr), j % npr, l)),
        o_spec=pl.BlockSpec((tm, tn), lambda i, j, l: (i, j)),
        out_shape=(t, e), out_dtype=F32)


def _grp_bwd_w(name, pooled, dmm, ng):
    t, e = pooled.shape
    gw = e // ng
    rs = gw // N_CHIPS
    _, tn, tk = _tiles(rs, gw, t)
    npb = gw // tn
    return _matmul(
        name, pooled, dmm, dims="tn", grid=(ng * N_CHIPS, npb, t // tk),
        a_spec=pl.BlockSpec((tk, rs), lambda i, j, l: (l, i)),
        b_spec=pl.BlockSpec((tk, tn), lambda i, j, l: (l, (i // N_CHIPS) * npb + j)),
        o_spec=pl.BlockSpec((None, None, rs, tn), lambda i, j, l: (i % N_CHIPS, i // N_CHIPS, 0, j)),
        out_shape=(N_CHIPS, ng, rs, gw), out_dtype=F32)


def _cast_into_slot(name, w, place):
    r, c = w.shape
    tr, tc = _tile(r, 512), _tile(c, 2048)

    def body(place_ref, w_ref, o_ref):
        o_ref[...] = w_ref[...].astype(BF16)

    return pl.pallas_call(
        body, name=name,
        grid_spec=pltpu.PrefetchScalarGridSpec(
            num_scalar_prefetch=1, grid=(r // tr, c // tc),
            in_specs=[pl.BlockSpec((tr, tc), lambda i, j, p: (i, j))],
            out_specs=pl.BlockSpec((None, tr, tc), lambda i, j, p: (p[0], i, j))),
        out_shape=jax.ShapeDtypeStruct((N_CHIPS, r, c), BF16),
        compiler_params=_params("parallel", "parallel"))(place, w)


def _rms(v):
    return lax.rsqrt(jnp.mean(v * v, axis=-1, keepdims=True) + NORM_EPS)


def _rms_fwd(name, x, g):
    t, d = x.shape
    tr = _tile(t, 256)

    def body(x_ref, g_ref, h_ref):
        xv = x_ref[...]
        h_ref[...] = (xv * _rms(xv) * g_ref[...]).astype(BF16)

    row = pl.BlockSpec((tr, d), lambda i: (i, 0))
    vec = pl.BlockSpec((1, d), lambda i: (0, 0))
    return pl.pallas_call(
        body, name=name, grid=(t // tr,), in_specs=[row, vec], out_specs=row,
        out_shape=jax.ShapeDtypeStruct((t, d), BF16), compiler_params=_params("parallel"))(x, g)


def _post_pre_fwd(name, x, m, g_post, g_pre):
    t, d = x.shape
    tr = _tile(t, 256)

    def body(x_ref, m_ref, gp_ref, gn_ref, x1_ref, h_ref):
        mv = m_ref[...]
        x1 = x_ref[...] + mv * _rms(mv) * gp_ref[...]
        x1_ref[...] = x1
        h_ref[...] = (x1 * _rms(x1) * gn_ref[...]).astype(BF16)

    row = pl.BlockSpec((tr, d), lambda i: (i, 0))
    vec = pl.BlockSpec((1, d), lambda i: (0, 0))
    return pl.pallas_call(
        body, name=name, grid=(t // tr,), in_specs=[row, row, vec, vec], out_specs=[row, row],
        out_shape=[jax.ShapeDtypeStruct((t, d), F32), jax.ShapeDtypeStruct((t, d), BF16)],
        compiler_params=_params("parallel"))(x, m, g_post, g_pre)


def _norm_bwd(dout, nrm, r, g):
    gd = dout * g
    return r * (gd - nrm * jnp.mean(gd * nrm, axis=-1, keepdims=True))


def _loss_post_bwd(name, x1, m, g_post, target):
    t, d = x1.shape
    tr = _tile(t, 128)

    def body(x_ref, m_ref, g_ref, t_ref, loss_ref, dx_ref, dm_ref, dg_ref):
        i = pl.program_id(0)
        mv = m_ref[...]
        r = _rms(mv)
        nrm = mv * r
        err = x_ref[...] + nrm * g_ref[...] - t_ref[...]
        part = 0.5 * jnp.sum(jnp.mean(err * err, axis=-1, keepdims=True), axis=0, keepdims=True)
        dx = err / d
        dx_ref[...] = dx
        dm_ref[...] = _norm_bwd(dx, nrm, r, g_ref[...]).astype(BF16)
        dg = jnp.sum(dx * nrm, axis=0, keepdims=True)

        @pl.when(i == 0)
        def _():
            loss_ref[...] = part
            dg_ref[...] = dg

        @pl.when(i > 0)
        def _():
            loss_ref[...] += part
            dg_ref[...] += dg

    row = pl.BlockSpec((tr, d), lambda i: (i, 0))
    vec = pl.BlockSpec((1, d), lambda i: (0, 0))
    one = pl.BlockSpec((1, 1), lambda i: (0, 0))
    return pl.pallas_call(
        body, name=name, grid=(t // tr,), in_specs=[row, row, vec, row], out_specs=[one, row, row, vec],
        out_shape=[jax.ShapeDtypeStruct((1, 1), F32), jax.ShapeDtypeStruct((t, d), F32),
                   jax.ShapeDtypeStruct((t, d), BF16), jax.ShapeDtypeStruct((1, d), F32)],
        compiler_params=_params("arbitrary"))(x1, m, g_post, target)


def _mid_bwd(name, dx2, dh1, x1, g_pre, m0, g_post):
    t, d = x1.shape
    tr = _tile(t, 128)

    def body(dx2_ref, dh_ref, x_ref, gn_ref, m_ref, gp_ref, dx_ref, dm_ref, dgn_ref, dgp_ref):
        i = pl.program_id(0)
        xv = x_ref[...]
        r1 = _rms(xv)
        n1 = xv * r1
        dh = dh_ref[...]
        dx = dx2_ref[...] + _norm_bwd(dh, n1, r1, gn_ref[...])
        dx_ref[...] = dx
        mv = m_ref[...]
        r0 = _rms(mv)
        n0 = mv * r0
        dm_ref[...] = _norm_bwd(dx, n0, r0, gp_ref[...]).astype(BF16)
        dgn = jnp.sum(dh * n1, axis=0, keepdims=True)
        dgp = jnp.sum(dx * n0, axis=0, keepdims=True)

        @pl.when(i == 0)
        def _():
            dgn_ref[...] = dgn
            dgp_ref[...] = dgp

        @pl.when(i > 0)
        def _():
            dgn_ref[...] += dgn
            dgp_ref[...] += dgp

    row = pl.BlockSpec((tr, d), lambda i: (i, 0))
    vec = pl.BlockSpec((1, d), lambda i: (0, 0))
    return pl.pallas_call(
        body, name=name, grid=(t // tr,), in_specs=[row, row, row, vec, row, vec],
        out_specs=[row, row, vec, vec],
        out_shape=[jax.ShapeDtypeStruct((t, d), F32), jax.ShapeDtypeStruct((t, d), BF16),
                   jax.ShapeDtypeStruct((1, d), F32), jax.ShapeDtypeStruct((1, d), F32)],
        compiler_params=_params("arbitrary"))(dx2, dh1, x1, g_pre, m0, g_post)


def _pre_bwd(name, dx1, dh0, x, g_pre):
    t, d = x.shape
    tr = _tile(t, 128)

    def body(dx1_ref, dh_ref, x_ref, g_ref, dx_ref, dg_ref):
        i = pl.program_id(0)
        xv = x_ref[...]
        r = _rms(xv)
        nrm = xv * r
        dh = dh_ref[...]
        dx_ref[...] = dx1_ref[...] + _norm_bwd(dh, nrm, r, g_ref[...])
        dg = jnp.sum(dh * nrm, axis=0, keepdims=True)

        @pl.when(i == 0)
        def _():
            dg_ref[...] = dg

        @pl.when(i > 0)
        def _():
            dg_ref[...] += dg

    row = pl.BlockSpec((tr, d), lambda i: (i, 0))
    vec = pl.BlockSpec((1, d), lambda i: (0, 0))
    return pl.pallas_call(
        body, name=name, grid=(t // tr,), in_specs=[row, row, row, vec], out_specs=[row, vec],
        out_shape=[jax.ShapeDtypeStruct((t, d), F32), jax.ShapeDtypeStruct((1, d), F32)],
        compiler_params=_params("arbitrary"))(dx1, dh0, x, g_pre)


def _a_stats(name, pa, e):
    t = pa.shape[0]
    tr = _tile(t, 64, 8)

    def body(v_ref, mu_ref, rs_ref):
        vg, _ = _gelu(v_ref[...])
        mu = jnp.mean(vg, axis=-1, keepdims=True)
        xc = vg - mu
        mu_ref[...] = mu
        rs_ref[...] = lax.rsqrt(jnp.mean(xc * xc, axis=-1, keepdims=True) + NORM_EPS)

    col = pl.BlockSpec((tr, 1), lambda i: (i, 0))
    return pl.pallas_call(
        body, name=name, grid=(t // tr,), in_specs=[pl.BlockSpec((tr, e), lambda i: (i, 1))],
        out_specs=[col, col],
        out_shape=[jax.ShapeDtypeStruct((t, 1), F32), jax.ShapeDtypeStruct((t, 1), F32)],
        compiler_params=_params("parallel"))(pa)


def _causal(w):
    c = w.shape[0]
    keep = lax.broadcasted_iota(jnp.int32, (c, c), 0) >= lax.broadcasted_iota(jnp.int32, (c, c), 1)
    return jnp.where(keep, w, 0.0), keep


def _a_gate_fwd(name, pa, mu, rs, ln_g, ln_b, w_s, b_s3):
    t = pa.shape[0]
    nh, c, _ = w_s.shape
    e = ln_g.shape[1]
    dh = e // nh
    rb = 2 * c if t % (2 * c) == 0 else c

    def body(u_ref, v_ref, z_ref, mu_ref, rs_ref, g_ref, b_ref, w_ref, bs_ref, y_ref):
        wc = _causal(w_ref[...])[0].astype(BF16)
        vg, _ = _gelu(v_ref[...])
        vn = ((vg - mu_ref[...]) * rs_ref[...] * g_ref[...] + b_ref[...]).astype(BF16)
        for ci in range(rb // c):
            rows = pl.ds(ci * c, c)
            sv = jnp.dot(wc, vn[ci * c:(ci + 1) * c], preferred_element_type=F32) + bs_ref[...]
            u, _ = _gelu(u_ref[rows, :])
            sz, _ = _silu(z_ref[rows, :])
            y_ref[rows, :] = (u * sv * sz).astype(BF16)

    blk = lambda off: pl.BlockSpec((rb, dh), lambda i, h: (i, off + h))
    col = pl.BlockSpec((rb, 1), lambda i, h: (i, 0))
    vec = pl.BlockSpec((1, dh), lambda i, h: (0, h))
    return pl.pallas_call(
        body, name=name, grid=(t // rb, nh),
        in_specs=[blk(0), blk(nh), blk(2 * nh), col, col, vec, vec,
                  pl.BlockSpec((None, c, c), lambda i, h: (h, 0, 0)),
                  pl.BlockSpec((None, c, 1), lambda i, h: (h, 0, 0))],
        out_specs=pl.BlockSpec((rb, dh), lambda i, h: (i, h)),
        out_shape=jax.ShapeDtypeStruct((t, e), BF16),
        compiler_params=_params("parallel", "parallel"))(pa, pa, pa, mu, rs, ln_g, ln_b, w_s, b_s3)


def _a_gate_bwd1(name, pa, dy, mu, rs, ln_g, ln_b, w_s, b_s3):
    t = pa.shape[0]
    nh, c, _ = w_s.shape
    e = ln_g.shape[1]
    dh = e // nh
    rb = 2 * c if t % (2 * c) == 0 else c

    def body(u_ref, v_ref, z_ref, dy_ref, mu_ref, rs_ref, g_ref, b_ref, w_ref, bs_ref,
             du_ref, dz_ref, dsv_ref, c1_ref, c2_ref):
        h = pl.program_id(1)
        wc = _causal(w_ref[...])[0].astype(BF16)
        vg, _ = _gelu(v_ref[...])
        xh = (vg - mu_ref[...]) * rs_ref[...]
        vn = (xh * g_ref[...] + b_ref[...]).astype(BF16)
        s1 = []
        s2 = []
        for ci in range(rb // c):
            rows = pl.ds(ci * c, c)
            lo, hi = ci * c, (ci + 1) * c
            sv = jnp.dot(wc, vn[lo:hi], preferred_element_type=F32) + bs_ref[...]
            u, du = _gelu(u_ref[rows, :])
            zv = z_ref[rows, :]
            sz, dsz = _silu(zv)
            dyv = dy_ref[rows, :]
            du_ref[rows, :] = (dyv * sv * sz * du).astype(BF16)
            dz_ref[rows, :] = (dyv * u * sv * dsz).astype(BF16)
            dsv = (dyv * u * sz).astype(BF16)
            dsv_ref[rows, :] = dsv
            dvn = lax.dot_general(wc, dsv, (((0,), (0,)), ((), ())), preferred_element_type=F32)
            dxh = dvn * g_ref[...]
            s1.append(jnp.sum(dxh, axis=-1, keepdims=True))
            s2.append(jnp.sum(dxh * xh[lo:hi], axis=-1, keepdims=True))
        p1 = jnp.concatenate(s1, axis=0)
        p2 = jnp.concatenate(s2, axis=0)

        @pl.when(h == 0)
        def _():
            c1_ref[...] = p1
            c2_ref[...] = p2

        @pl.when(h > 0)
        def _():
            c1_ref[...] += p1
            c2_ref[...] += p2

    blk = lambda off: pl.BlockSpec((rb, dh), lambda i, h: (i, off + h))
    col = pl.BlockSpec((rb, 1), lambda i, h: (i, 0))
    vec = pl.BlockSpec((1, dh), lambda i, h: (0, h))
    act = jax.ShapeDtypeStruct((t, e), BF16)
    stat = jax.ShapeDtypeStruct((t, 1), F32)
    return pl.pallas_call(
        body, name=name, grid=(t // rb, nh),
        in_specs=[blk(0), blk(nh), blk(2 * nh), blk(0), col, col, vec, vec,
                  pl.BlockSpec((None, c, c), lambda i, h: (h, 0, 0)),
                  pl.BlockSpec((None, c, 1), lambda i, h: (h, 0, 0))],
        out_specs=[blk(0), blk(0), blk(0), col, col],
        out_shape=[act, act, act, stat, stat],
        compiler_params=_params("parallel", "arbitrary"))(pa, pa, pa, dy, mu, rs, ln_g, ln_b, w_s, b_s3)


def _a_gate_bwd2(name, pa, dsv, mu, rs, c1, c2, ln_g, ln_b, w_s):
    t = pa.shape[0]
    nh, c, _ = w_s.shape
    e = ln_g.shape[1]
    dh = e // nh
    rb = 2 * c if t % (2 * c) == 0 else c

    def body(v_ref, dsv_ref, mu_ref, rs_ref, c1_ref, c2_ref, g_ref, b_ref, w_ref,
             dv_ref, dw_ref, dbs_ref, dg_ref, db_ref):
        i = pl.program_id(1)
        wcf, keep = _causal(w_ref[...])
        wc = wcf.astype(BF16)
        vg, dvg = _gelu(v_ref[...])
        rsv = rs_ref[...]
        xh = (vg - mu_ref[...]) * rsv
        vn = (xh * g_ref[...] + b_ref[...]).astype(BF16)
        dw = jnp.zeros((c, c), F32)
        dbs = jnp.zeros((c, 1), F32)
        dvns = []
        for ci in range(rb // c):
            lo, hi = ci * c, (ci + 1) * c
            dsv = dsv_ref[pl.ds(lo, c), :]
            dvns.append(lax.dot_general(wc, dsv, (((0,), (0,)), ((), ())), preferred_element_type=F32))
            dw += lax.dot_general(dsv, vn[lo:hi], (((1,), (1,)), ((), ())), preferred_element_type=F32)
            dbs += jnp.sum(dsv.astype(F32), axis=-1, keepdims=True)
        dvn = jnp.concatenate(dvns, axis=0)
        dxh = dvn * g_ref[...]
        dvv = rsv * (dxh - c1_ref[...] * (1.0 / e) - xh * (c2_ref[...] * (1.0 / e)))
        dv_ref[...] = (dvv * dvg).astype(BF16)
        dw = jnp.where(keep, dw, 0.0)
        dg = jnp.sum(dvn * xh, axis=0, keepdims=True)
        db = jnp.sum(dvn, axis=0, keepdims=True)

        @pl.when(i == 0)
        def _():
            dw_ref[...] = dw
            dbs_ref[...] = dbs
            dg_ref[...] = dg
            db_ref[...] = db

        @pl.when(i > 0)
        def _():
            dw_ref[...] += dw
            dbs_ref[...] += dbs
            dg_ref[...] += dg
            db_ref[...] += db

    col = pl.BlockSpec((rb, 1), lambda h, i: (i, 0))
    vec = pl.BlockSpec((1, dh), lambda h, i: (0, h))
    hblk = pl.BlockSpec((rb, dh), lambda h, i: (i, h))
    return pl.pallas_call(
        body, name=name, grid=(nh, t // rb),
        in_specs=[pl.BlockSpec((rb, dh), lambda h, i: (i, nh + h)), hblk, col, col, col, col, vec, vec,
                  pl.BlockSpec((None, c, c), lambda h, i: (h, 0, 0))],
        out_specs=[hblk, pl.BlockSpec((None, c, c), lambda h, i: (h, 0, 0)),
                   pl.BlockSpec((None, c, 1), lambda h, i: (h, 0, 0)), vec, vec],
        out_shape=[jax.ShapeDtypeStruct((t, e), BF16), jax.ShapeDtypeStruct((nh, c, c), F32),
                   jax.ShapeDtypeStruct((nh, c, 1), F32), jax.ShapeDtypeStruct((1, e), F32),
                   jax.ShapeDtypeStruct((1, e), F32)],
        compiler_params=_params("parallel", "arbitrary"))(pa, dsv, mu, rs, c1, c2, ln_g, ln_b, w_s)


def _pool(name, src, e, seq, backward):
    t = src.shape[0]
    ng = len(POOL_WINDOWS)
    gw = e // ng
    cw = _tile(gw, 256)

    def shifted(a, j, pos):
        if backward:
            return jnp.where(pos < seq - j, pltpu.roll(a, seq - j, 0), 0.0)
        return jnp.where(pos >= j, pltpu.roll(a, j, 0), 0.0)

    def body(p_ref, o_ref):
        grp = (pl.program_id(1) * cw) // gw
        pos = lax.broadcasted_iota(jnp.int32, (seq, cw), 0)
        posf = (pos + 1).astype(F32)
        for k, w in enumerate(POOL_WINDOWS):

            @pl.when(grp == k)
            def _(w=w):
                pv = p_ref[...]
                cnt = jnp.minimum(posf, float(w))
                acc = pv / cnt if backward else pv
                j = 1
                while j < w:
                    acc = acc + shifted(acc, j, pos)
                    j *= 2
                out = acc - pv if backward else acc / cnt - pv
                o_ref[...] = out.astype(BF16)

    spec = pl.BlockSpec((seq, cw), lambda s, j: (s, j))
    return pl.pallas_call(
        body, name=name, grid=(t // seq, e // cw), in_specs=[spec], out_specs=spec,
        out_shape=jax.ShapeDtypeStruct((t, e), BF16),
        compiler_params=_params("parallel", "parallel"))(src)


def _b_gate_fwd(name, mm, pb, b_grp, scale):
    t, e = mm.shape
    tr, tc = _tile(t, 512), _tile(e, 1024)
    nc = e // tc

    def body(mm_ref, z_ref, b_ref, s_ref, y_ref):
        sz, _ = _silu(z_ref[...])
        y_ref[...] = ((mm_ref[...] + b_ref[...]) * s_ref[...] * sz).astype(BF16)

    blk = pl.BlockSpec((tr, tc), lambda i, j: (i, j))
    vec = pl.BlockSpec((1, tc), lambda i, j: (0, j))
    return pl.pallas_call(
        body, name=name, grid=(t // tr, nc),
        in_specs=[blk, pl.BlockSpec((tr, tc), lambda i, j: (i, nc + j)), vec, vec], out_specs=blk,
        out_shape=jax.ShapeDtypeStruct((t, e), BF16),
        compiler_params=_params("parallel", "parallel"))(mm, pb, b_grp, scale)


def _b_gate_bwd(name, dy, mm, pb, b_grp, scale):
    t, e = mm.shape
    tr, tc = _tile(t, 512), _tile(e, 1024)
    nc = e // tc

    def body(dy_ref, mm_ref, z_ref, b_ref, s_ref, dmm_ref, dz_ref, db_ref, ds_ref):
        i = pl.program_id(1)
        sz, dsz = _silu(z_ref[...])
        dyv = dy_ref[...]
        mb = mm_ref[...] + b_ref[...]
        dmixed = dyv * sz
        dmm = dmixed * s_ref[...]
        dmm_ref[...] = dmm.astype(BF16)
        dz_ref[...] = (dyv * (mb * s_ref[...]) * dsz).astype(BF16)
        db = jnp.sum(dmm, axis=0, keepdims=True)
        ds = jnp.sum(dmixed * mb, axis=0, keepdims=True)

        @pl.when(i == 0)
        def _():
            db_ref[...] = db
            ds_ref[...] = ds

        @pl.when(i > 0)
        def _():
            db_ref[...] += db
            ds_ref[...] += ds

    blk = pl.BlockSpec((tr, tc), lambda j, i: (i, j))
    vec = pl.BlockSpec((1, tc), lambda j, i: (0, j))
    act = jax.ShapeDtypeStruct((t, e), BF16)
    stat = jax.ShapeDtypeStruct((1, e), F32)
    return pl.pallas_call(
        body, name=name, grid=(nc, t // tr),
        in_specs=[blk, blk, pl.BlockSpec((tr, tc), lambda j, i: (i, nc + j)), vec, vec],
        out_specs=[blk, blk, vec, vec], out_shape=[act, act, stat, stat],
        compiler_params=_params("parallel", "arbitrary"))(dy, mm, pb, b_grp, scale)


def _position():
    return lax.axis_index("x"), lax.axis_index("y"), lax.axis_index("c")


def _other_chips(x, y):
    return [(1 - x, y), (x, 1 - y), (1 - x, 1 - y)]


def _half(ref, c):
    n = ref.shape[0] // 2
    return ref.at[pl.ds(c * n, n)]


def _gather_phase(slots, small=None):
    nw = len(slots)
    ns = 0 if small is None else 1
    n_ici = 3 * (nw + ns)
    n_sem = n_ici + 3 * nw + ns

    def copies(ins, outs, send_sems, recv_sems):
        x, y, c = _position()
        sibling = (x, y, 1 - c)

        def ici(j, k, slot, to):
            return pltpu.make_async_remote_copy(
                src_ref=_half(ins[k].at[slot], c), dst_ref=_half(outs[k].at[slot], c),
                send_sem=send_sems.at[j * nw + k], recv_sem=recv_sems.at[j * nw + k],
                device_id=to, device_id_type=MESH)

        def ici_small(j, slot, to):
            return pltpu.make_async_remote_copy(
                src_ref=ins[nw], dst_ref=outs[nw].at[slot], send_sem=send_sems.at[3 * nw + j],
                recv_sem=recv_sems.at[3 * nw + j], device_id=to, device_id_type=MESH)

        def d2d(j, k, slot, half):
            return pltpu.make_async_remote_copy(
                src_ref=_half(outs[k].at[slot], half), dst_ref=_half(outs[k].at[slot], half),
                send_sem=send_sems.at[n_ici + j * nw + k], recv_sem=recv_sems.at[n_ici + j * nw + k],
                device_id=sibling, device_id_type=MESH)

        def local():
            return pltpu.make_async_copy(ins[nw], outs[nw].at[2 * x + y], send_sems.at[n_sem - 1])

        return x, y, c, ici, ici_small, d2d, local

    def start(ins, outs, send_sems, recv_sems):
        x, y, c, ici, ici_small, _, local = copies(ins, outs, send_sems, recv_sems)
        if ns:
            local().start()
        for j, (cx, cy) in enumerate(_other_chips(x, y)):
            for k in range(nw):
                ici(j, k, 2 * x + y, (cx, cy, c)).start()
            if ns:
                ici_small(j, 2 * x + y, (cx, cy, c)).start()

    def finish(ins, outs, send_sems, recv_sems):
        x, y, c, ici, ici_small, d2d, local = copies(ins, outs, send_sems, recv_sems)
        chips = _other_chips(x, y)
        for j, (cx, cy) in enumerate(chips):
            for k in range(nw):
                ici(j, k, 2 * cx + cy, (x, y, c)).wait_recv()
                d2d(j, k, 2 * cx + cy, c).start()
        for j, (cx, cy) in enumerate(chips):
            if ns:
                ici_small(j, 2 * cx + cy, (x, y, c)).wait_recv()
            for k in range(nw):
                d2d(j, k, 2 * cx + cy, 1 - c).wait_recv()
        for j, (cx, cy) in enumerate(chips):
            for k in range(nw):
                ici(j, k, 2 * x + y, (cx, cy, c)).wait_send()
                d2d(j, k, 2 * cx + cy, c).wait_send()
            if ns:
                ici_small(j, 2 * x + y, (cx, cy, c)).wait_send()
        if ns:
            local().wait()

    outs = [jax.ShapeDtypeStruct(s.shape, s.dtype) for s in slots]
    if ns:
        outs.append(jax.ShapeDtypeStruct((N_CHIPS,) + small.shape, small.dtype))
    return _Phase(list(slots) + ([small] if ns else []), outs, {k: k for k in range(nw)}, n_sem, start, finish)


def _swap_phase(grads):
    nw = len(grads)

    def swaps(ins, outs, send_sems, recv_sems):
        x, y, c = _position()
        cps = []
        for k in range(nw):
            n = ins[k].shape[1] // 2
            cps.append(pltpu.make_async_remote_copy(
                src_ref=ins[k].at[pl.ds(0, N_CHIPS), pl.ds((1 - c) * n, n)], dst_ref=outs[k],
                send_sem=send_sems.at[k], recv_sem=recv_sems.at[k],
                device_id=(x, y, 1 - c), device_id_type=MESH))
        return cps

    def start(*refs):
        for cp in swaps(*refs):
            cp.start()

    def finish(*refs):
        for cp in swaps(*refs):
            cp.wait()

    halves = [jax.ShapeDtypeStruct((g.shape[0], g.shape[1] // 2, g.shape[2]), F32) for g in grads]
    return _Phase(grads, halves, {}, nw, start, finish)


def _add_halves(name, grad, theirs, place):
    s, half, w = theirs.shape
    tr, tc = _tile(half, 256), _tile(w, 2048)
    nrt = half // tr

    def body(place_ref, a_ref, b_ref, h_ref, f_ref):
        v = a_ref[...] + b_ref[...]
        h_ref[...] = v.astype(BF16)

        @pl.when(pl.program_id(2) == place_ref[0])
        def _():
            f_ref[...] = v

    return pl.pallas_call(
        body, name=name,
        grid_spec=pltpu.PrefetchScalarGridSpec(
            num_scalar_prefetch=1, grid=(nrt, w // tc, s),
            in_specs=[pl.BlockSpec((None, tr, tc), lambda i, j, q, p: (q, p[1] * nrt + i, j)),
                      pl.BlockSpec((None, tr, tc), lambda i, j, q, p: (q, i, j))],
            out_specs=[pl.BlockSpec((None, tr, tc), lambda i, j, q, p: (q, i, j)),
                       pl.BlockSpec((tr, tc), lambda i, j, q, p: (p[1] * nrt + i, j))]),
        out_shape=[jax.ShapeDtypeStruct(theirs.shape, BF16), jax.ShapeDtypeStruct((2 * half, w), F32)],
        compiler_params=_params("parallel", "parallel", "arbitrary"))(place, grad, theirs)


def _scatter_phase(own, pb):
    nw = len(own)

    def copies(ins, outs, send_sems, recv_sems):
        own_in, pbs = ins[:nw], ins[nw:]
        own_out, got = outs[:nw], outs[nw:]
        x, y, c = _position()
        sibling = (x, y, 1 - c)

        def d2d_own(k, half):
            return pltpu.make_async_remote_copy(
                src_ref=_half(own_in[k], half), dst_ref=_half(own_out[k], half), send_sem=send_sems.at[k],
                recv_sem=recv_sems.at[k], device_id=sibling, device_id_type=MESH)

        def ici(j, k, shard, to):
            return pltpu.make_async_remote_copy(
                src_ref=pbs[k].at[shard], dst_ref=_half(got[k].at[j], c),
                send_sem=send_sems.at[nw + j * nw + k], recv_sem=recv_sems.at[nw + j * nw + k],
                device_id=to, device_id_type=MESH)

        def d2d(j, k, half):
            return pltpu.make_async_remote_copy(
                src_ref=_half(got[k].at[j], half), dst_ref=_half(got[k].at[j], half),
                send_sem=send_sems.at[4 * nw + j * nw + k], recv_sem=recv_sems.at[4 * nw + j * nw + k],
                device_id=sibling, device_id_type=MESH)

        return x, y, c, d2d_own, ici, d2d

    def start(*refs):
        x, y, c, d2d_own, ici, _ = copies(*refs)
        for k in range(nw):
            d2d_own(k, c).start()
        for j, (cx, cy) in enumerate(_other_chips(x, y)):
            for k in range(nw):
                ici(j, k, 2 * cx + cy, (cx, cy, c)).start()

    def finish(*refs):
        x, y, c, d2d_own, ici, d2d = copies(*refs)
        for j in range(3):
            for k in range(nw):
                ici(j, k, 2 * x + y, (x, y, c)).wait_recv()
                d2d(j, k, c).start()
        for k in range(nw):
            d2d_own(k, 1 - c).wait_recv()
        for j in range(3):
            for k in range(nw):
                d2d(j, k, 1 - c).wait_recv()
        for k in range(nw):
            d2d_own(k, c).wait_send()
        for j, (cx, cy) in enumerate(_other_chips(x, y)):
            for k in range(nw):
                ici(j, k, 2 * cx + cy, (cx, cy, c)).wait_send()
                d2d(j, k, c).wait_send()

    own_shape = [jax.ShapeDtypeStruct(o.shape, F32) for o in own]
    got_shape = [jax.ShapeDtypeStruct((3,) + o.shape, BF16) for o in own]
    return _Phase(list(own) + list(pb), own_shape + got_shape, {k: k for k in range(nw)}, 7 * nw, start, finish)


def _gather_small(part):
    def body(p_ref, o_ref, send_sems, recv_sems, loc_sem):
        x, y, c = _position()
        me = 4 * x + 2 * y + c
        mine = pltpu.make_async_copy(p_ref, o_ref.at[me], loc_sem)
        mine.start()
        sent = []
        for mask in range(1, N_DEV):
            dx, dy, dc = (mask >> 2) & 1, (mask >> 1) & 1, mask & 1
            px, py, pc = x ^ dx, y ^ dy, c ^ dc
            cp = pltpu.make_async_remote_copy(
                src_ref=p_ref, dst_ref=o_ref.at[me], send_sem=send_sems.at[mask - 1],
                recv_sem=recv_sems.at[mask - 1], device_id=(px, py, pc), device_id_type=MESH)
            cp.start()
            sent.append((cp, 4 * px + 2 * py + pc))
        for mask in range(1, N_DEV):
            cp, peer = sent[mask - 1]
            pltpu.make_async_remote_copy(
                src_ref=p_ref, dst_ref=o_ref.at[peer], send_sem=send_sems.at[mask - 1],
                recv_sem=recv_sems.at[mask - 1], device_id=(x, y, c), device_id_type=MESH).wait_recv()
        for cp, _ in sent:
            cp.wait_send()
        mine.wait()

    return pl.pallas_call(
        body, name="gather_small_grads", in_specs=[ANY], out_specs=ANY,
        out_shape=jax.ShapeDtypeStruct((N_DEV,) + part.shape, F32),
        scratch_shapes=[pltpu.SemaphoreType.DMA((N_DEV - 1,)), pltpu.SemaphoreType.DMA((N_DEV - 1,)),
                        pltpu.SemaphoreType.DMA(())],
    )(part)


def _sum_slots(name, parts):
    s, r, c = parts.shape
    tr = _tile(r, 512, 8)

    def body(p_ref, o_ref):
        acc = p_ref[0]
        for q in range(1, s):
            acc = acc + p_ref[q]
        o_ref[...] = acc

    return pl.pallas_call(
        body, name=name, grid=(r // tr,), in_specs=[pl.BlockSpec((s, tr, c), lambda i: (0, i, 0))],
        out_specs=pl.BlockSpec((tr, c), lambda i: (i, 0)), out_shape=jax.ShapeDtypeStruct((r, c), F32),
        compiler_params=_params("parallel"))(parts)


def _adamw_math(w, g, m, v):
    m = ADAM_B1 * m + (1.0 - ADAM_B1) * g
    v = ADAM_B2 * v + (1.0 - ADAM_B2) * (g * g)
    m_hat = m / (1.0 - ADAM_B1 ** ADAM_STEP)
    v_hat = v / (1.0 - ADAM_B2 ** ADAM_STEP)
    delta = -ADAM_LR * (m_hat / (jnp.sqrt(v_hat) + ADAM_EPS) + ADAM_WD * w)
    return delta, m, v


def _adamw(name, w, m, v, own, got=None):
    r, c = w.shape
    tr, tc = _tile(r, 256, 8), _tile(c, 1024)

    def body(*refs):
        if got is None:
            w_ref, m_ref, v_ref, own_ref = refs[:4]
            outs = refs[4:]
            g = own_ref[...]
        else:
            w_ref, m_ref, v_ref, own_ref, got_ref = refs[:5]
            outs = refs[5:]
            g = own_ref[...]
            for j in range(3):
                g = g + got_ref[j].astype(F32)
        delta, mn, vn = _adamw_math(w_ref[...], g, m_ref[...], v_ref[...])
        outs[0][...] = g
        outs[1][...] = delta
        outs[2][...] = mn
        outs[3][...] = vn

    blk = pl.BlockSpec((tr, tc), lambda i, j: (i, j))
    ins, args = [blk] * 4, [w, m, v, own]
    if got is not None:
        ins.append(pl.BlockSpec((3, tr, tc), lambda i, j: (0, i, j)))
        args.append(got)
    return pl.pallas_call(
        body, name=name, grid=(r // tr, c // tc), in_specs=ins, out_specs=[blk] * 4,
        out_shape=[jax.ShapeDtypeStruct((r, c), F32)] * 4,
        compiler_params=_params("parallel", "parallel"))(*args)


def _pack(parts):
    return jnp.concatenate([p.reshape(-1) for p in parts]).reshape(-1, LANES)


def _unpack(packed, shapes):
    flat = packed.reshape(-1)
    out, off = [], 0
    for s in shapes:
        n = 1
        for d in s:
            n *= d
        out.append(flat[off:off + n].reshape(s))
        off += n
    return out


def kernel(x, pre_norm, post_norm, a_w_in, a_ln_g, a_ln_b, a_w_s, a_b_s, a_w_out, b_w_in, b_w_grp, b_b_grp, b_scale, b_w_out, loss_target, m_pre_norm, m_post_norm, m_a_w_in, m_a_ln_g, m_a_ln_b, m_a_w_s, m_a_b_s, m_a_w_out, m_b_w_in, m_b_w_grp, m_b_b_grp, m_b_scale, m_b_w_out, v_pre_norm, v_post_norm, v_a_w_in, v_a_ln_g, v_a_ln_b, v_a_w_s, v_a_b_s, v_a_w_out, v_b_w_in, v_b_w_grp, v_b_b_grp, v_b_scale, v_b_w_out):
    nb, seq, d = x.shape
    t = nb * seq
    e = a_ln_g.shape[1]
    nh, chunk = a_w_s.shape[1], a_w_s.shape[2]
    ng, rs, gw = b_w_grp.shape[1], b_w_grp.shape[2], b_w_grp.shape[3]
    wa, wb = a_w_in.shape[2], b_w_in.shape[2]
    cx, cy = lax.axis_index("x"), lax.axis_index("y")
    chip = 2 * cx + cy

    big_w = [a_w_in.reshape(d, wa), a_w_out.reshape(e // N_CHIPS, d), b_w_in.reshape(d, wb),
             b_w_grp.reshape(ng * rs, gw), b_w_out.reshape(e // N_CHIPS, d)]
    big_m = [m_a_w_in.reshape(d, wa), m_a_w_out.reshape(e // N_CHIPS, d), m_b_w_in.reshape(d, wb),
             m_b_w_grp.reshape(ng * rs, gw), m_b_w_out.reshape(e // N_CHIPS, d)]
    big_v = [v_a_w_in.reshape(d, wa), v_a_w_out.reshape(e // N_CHIPS, d), v_b_w_in.reshape(d, wb),
             v_b_w_grp.reshape(ng * rs, gw), v_b_w_out.reshape(e // N_CHIPS, d)]
    names = ["a_w_in", "a_w_out", "b_w_in", "b_w_grp", "b_w_out"]
    place = jnp.stack([chip, lax.axis_index("c")]).astype(jnp.int32)
    slots = [_cast_into_slot("cast_" + n, w, place) for n, w in zip(names, big_w)]
    small_w = jnp.concatenate([b_b_grp.reshape(ng, rs), b_scale.reshape(ng, rs)], axis=0)
    wa_g, gsmall = _comm_only("gather_a_w_in", [_gather_phase(slots[:1], small_w)])
    b_grp_full = jnp.transpose(gsmall[:, :ng, :], (1, 0, 2)).reshape(1, e)
    scale_full = gsmall[:, ng:, :].reshape(1, e)

    xf = x.reshape(t, d)
    tgt = loss_target.reshape(t, d)
    g_pre0, g_pre1 = pre_norm[0:1], pre_norm[1:2]
    g_post0, g_post1 = post_norm[0:1], post_norm[1:2]
    w_s = a_w_s.reshape(nh, chunk, chunk)
    b_s3 = a_b_s.reshape(nh, chunk, 1)

    h0 = _rms_fwd("pre_norm0", xf, g_pre0)
    pa, (wao_g, wb_g) = _mm_nn_cols("a_in_proj", h0, wa_g, [_gather_phase(slots[1:3])])
    wao_g = wao_g.reshape(e, d)
    mu, rstd = _a_stats("a_ln_stats", pa, e)
    y0 = _a_gate_fwd("a_gate", pa, mu, rstd, a_ln_g, a_ln_b, w_s, b_s3)
    m0 = _mm_nn("a_out_proj", y0, wao_g)
    x1, h1 = _post_pre_fwd("post0_pre1", xf, m0, g_post0, g_pre1)
    pb, (wg_g, wbo_g) = _mm_nn_cols("b_in_proj", h1, wb_g, [_gather_phase(slots[3:5])])
    wg_g = wg_g.reshape(N_CHIPS, ng, rs, gw)
    wbo_g = wbo_g.reshape(e, d)
    pooled = _pool("b_pool", pb, e, seq, backward=False)
    mm = _grp_fwd("b_grp_proj", pooled, wg_g)
    y1 = _b_gate_fwd("b_gate", mm, pb, b_grp_full, scale_full)
    m1 = _mm_nn("b_out_proj", y1, wbo_g)
    loss, dx2, dm1, dg_post1 = _loss_post_bwd("loss_post1_bwd", x1, m1, g_post1, tgt)
    dy1 = _mm_nt("b_out_dx", dm1, wbo_g)
    g_wbo = _mm_tn("b_out_dw", y1, dm1).reshape(N_CHIPS, e // N_CHIPS, d)
    dmm, dzb, db_grp, dscale = _b_gate_bwd("b_gate_bwd", dy1, mm, pb, b_grp_full, scale_full)
    dpooled = _grp_bwd_x("b_grp_dx", dmm, wg_g)
    g_wg = _grp_bwd_w("b_grp_dw", pooled, dmm, ng).reshape(N_CHIPS, ng * rs, gw)
    dp = _pool("b_pool_bwd", dpooled, e, seq, backward=True)
    dpb = jnp.concatenate([dp, dzb], axis=1)
    dh1 = _mm_nt_cols("b_in_dx", dpb, wb_g)
    g_wb = _mm_tn_cols("b_in_dw", h1, dpb, wb)
    dx1, dm0, dg_pre1, dg_post0 = _mid_bwd("pre1_post0_bwd", dx2, dh1, x1, g_pre1, m0, g_post0)
    layer1 = ["b_w_out", "b_w_grp", "b_w_in"]
    dy0, theirs_1 = _mm_nt("a_out_dx", dm0, wao_g, phases=[_swap_phase([g_wbo, g_wg, g_wb])])
    sums_1 = [_add_halves("chip_sum_" + n, g, th, place) for n, g, th in zip(layer1, [g_wbo, g_wg, g_wb], theirs_1)]
    g_wao = _mm_tn("a_out_dw", y0, dm0).reshape(N_CHIPS, e // N_CHIPS, d)
    du, dz, dsv, c1, c2 = _a_gate_bwd1("a_gate_bwd1", pa, dy0, mu, rstd, a_ln_g, a_ln_b, w_s, b_s3)
    dv, dw_s, db_s, dln_g, dln_b = _a_gate_bwd2("a_gate_bwd2", pa, dsv, mu, rstd, c1, c2, a_ln_g, a_ln_b, w_s)
    dpa = jnp.concatenate([du, dv, dz], axis=1)
    g_wa, moved = _mm_tn_cols(
        "a_in_dw", h0, dpa, wa,
        [_scatter_phase([s[1] for s in sums_1], [s[0] for s in sums_1]), _swap_phase([g_wao])])
    reduced = {n: (moved[k], moved[3 + k]) for k, n in enumerate(layer1)}
    sum_ao = _add_halves("chip_sum_a_w_out", g_wao, moved[6], place)
    dh0, moved = _mm_nt_cols("a_in_dx", dpa, wa_g, [_swap_phase([g_wa]), _scatter_phase([sum_ao[1]], [sum_ao[0]])])
    reduced["a_w_out"] = (moved[1], moved[2])
    sum_ai = _add_halves("chip_sum_a_w_in", g_wa, moved[0], place)
    grad_x, dg_pre0 = _pre_bwd("pre0_bwd", dx1, dh0, xf, g_pre0)
    reduced["a_w_in"] = tuple(_comm_only("scatter_a_w_in", [_scatter_phase([sum_ai[1]], [sum_ai[0]])]))
    big_out = [_adamw("adamw_" + n, big_w[k], big_m[k], big_v[k], *reduced[n]) for k, n in enumerate(names)]

    small_shapes = [(2, d), (2, d), (1, e), (1, e), (1, nh, chunk, chunk), (1, nh, chunk), (1, e), (1, e)]
    part = _pack([jnp.concatenate([dg_pre0, dg_pre1], axis=0), jnp.concatenate([dg_post0, dg_post1], axis=0),
                  dln_g, dln_b, dw_s, db_s, db_grp, dscale])
    g_small = _sum_slots("sum_small_grads", _gather_small(part))
    g_pre, g_post, g_lng, g_lnb, g_ws, g_bs, g_bgrp_full, g_scale_full = _unpack(g_small, small_shapes)
    g_bgrp = lax.dynamic_slice_in_dim(g_bgrp_full.reshape(ng, N_CHIPS, rs), chip, 1, axis=1).reshape(1, ng, rs)
    g_scale = lax.dynamic_slice_in_dim(g_scale_full.reshape(N_CHIPS, gw), chip, 1, axis=0)
    small_names = ["pre_norm", "post_norm", "a_ln_g", "a_ln_b", "a_w_s", "a_b_s", "b_b_grp", "b_scale"]
    small_g = [g_pre, g_post, g_lng, g_lnb, g_ws, g_bs, g_bgrp, g_scale]
    small_ws = [pre_norm, post_norm, a_ln_g, a_ln_b, a_w_s, a_b_s, b_b_grp, b_scale]
    small_ms = [m_pre_norm, m_post_norm, m_a_ln_g, m_a_ln_b, m_a_w_s, m_a_b_s, m_b_b_grp, m_b_scale]
    small_vs = [v_pre_norm, v_post_norm, v_a_ln_g, v_a_ln_b, v_a_w_s, v_a_b_s, v_b_b_grp, v_b_scale]
    packed = _adamw("adamw_small", _pack(small_ws), _pack(small_ms), _pack(small_vs), _pack(small_g))
    small_out = [_unpack(p, [w.shape for w in small_ws]) for p in packed]

    loss = lax.psum(loss[0, 0], ("x", "y", "c"))
    order = ["pre_norm", "post_norm", "a_w_in", "a_ln_g", "a_ln_b", "a_w_s", "a_b_s", "a_w_out", "b_w_in",
             "b_w_grp", "b_b_grp", "b_scale", "b_w_out"]
    big_shapes = dict(zip(names, [a_w_in.shape, a_w_out.shape, b_w_in.shape, b_w_grp.shape, b_w_out.shape]))
    outs = [loss, grad_x.reshape(nb, seq, d)]
    for kind in range(4):
        for n in order:
            if n in big_shapes:
                outs.append(big_out[names.index(n)][kind].reshape(big_shapes[n]))
            else:
                outs.append(small_out[kind][small_names.index(n)])
    return tuple(outs)
```

```python
import functools

import jax
import jax.numpy as jnp
from jax import lax
from jax.experimental import pallas as pl
from jax.experimental.pallas import tpu as pltpu

F32 = jnp.float32
BF16 = jnp.bfloat16
NORM_EPS = 1e-6
POOL_WINDOWS = (2, 4, 8, 16)
ADAM_LR = 0.001
ADAM_B1 = 0.9
ADAM_B2 = 0.999
ADAM_EPS = 1e-08
ADAM_WD = 0.01
ADAM_STEP = 10
N_CHIPS = 4
N_DEV = 8
V7X_VMEM_LIMIT_BYTES = 56 * 1024 * 1024
LANES = 128
MESH = pl.DeviceIdType.MESH
ANY = pl.BlockSpec(memory_space=pl.ANY)
SQRT_HALF = 0.7071067811865476
INV_SQRT_2PI = 0.3989422804014327


def _tile(dim, pref, mult=LANES):
    if dim <= pref:
        return dim
    t = (pref // mult) * mult
    while t >= mult:
        if dim % t == 0:
            return t
        t -= mult
    return dim


def _params(*sem):
    return pltpu.CompilerParams(dimension_semantics=sem or None, vmem_limit_bytes=V7X_VMEM_LIMIT_BYTES)


def _gelu(x):
    cdf = 0.5 * (1.0 + lax.erf(x * SQRT_HALF))
    pdf = jnp.exp(-0.5 * x * x) * INV_SQRT_2PI
    return x * cdf, cdf + x * pdf


def _silu(z):
    s = 1.0 / (1.0 + jnp.exp(-z))
    return z * s, s * (1.0 + z * (1.0 - s))


class _Phase:
    def __init__(self, ins, outs, aliases, n_sem, start, finish):
        self.ins, self.outs, self.aliases, self.n_sem = list(ins), list(outs), dict(aliases), n_sem
        self.start, self.finish = start, finish


def _host(phases, n_in, n_out):
    ins, outs, aliases, sems = [], [], {}, []
    for ph in phases:
        for i, o in ph.aliases.items():
            aliases[n_in + len(ins) + i] = n_out + len(outs) + o
        ins += ph.ins
        outs += ph.outs
        sems += [pltpu.SemaphoreType.DMA((ph.n_sem,)), pltpu.SemaphoreType.DMA((ph.n_sem,))]
    return ins, outs, aliases, sems


def _run_phases(phases, which, in_refs, out_refs, sem_refs):
    i = o = 0
    for p, ph in enumerate(phases):
        fn = ph.start if which == "start" else ph.finish
        fn(in_refs[i:i + len(ph.ins)], out_refs[o:o + len(ph.outs)], sem_refs[2 * p], sem_refs[2 * p + 1])
        i += len(ph.ins)
        o += len(ph.outs)


def _comm_only(name, phases):
    ins, outs, aliases, sems = _host(phases, 0, 0)

    def body(*refs):
        in_refs, out_refs, sem_refs = refs[:len(ins)], refs[len(ins):len(ins) + len(outs)], refs[len(ins) + len(outs):]
        _run_phases(phases, "start", in_refs, out_refs, sem_refs)
        _run_phases(phases, "finish", in_refs, out_refs, sem_refs)

    return pl.pallas_call(
        body, name=name, in_specs=[ANY] * len(ins), out_specs=[ANY] * len(outs), out_shape=outs,
        input_output_aliases=aliases, scratch_shapes=sems)(*ins)


def _matmul(name, a, b, *, dims, grid, a_spec, b_spec, o_spec, out_shape, out_dtype, phases=()):
    nk = grid[2]
    contract = {"nn": ((1,), (0,)), "nt": ((1,), (1,)), "tn": ((0,), (0,))}[dims]
    acc_in_out = out_dtype == F32 or nk == 1
    blk = tuple(d for d in o_spec.block_shape if d is not None)
    x_ins, x_outs, aliases, sems = _host(phases, 2, 1)
    n_acc = 0 if acc_in_out else 1

    def body(*refs):
        a_ref, b_ref = refs[:2]
        in_refs = refs[2:2 + len(x_ins)]
        o_ref = refs[2 + len(x_ins)]
        out_refs = refs[3 + len(x_ins):3 + len(x_ins) + len(x_outs)]
        scratch = refs[3 + len(x_ins) + len(x_outs):]
        sem_refs = scratch[n_acc:]
        ids = [pl.program_id(d) for d in range(3)]
        if phases:

            @pl.when((ids[0] == 0) & (ids[1] == 0) & (ids[2] == 0))
            def _():
                _run_phases(phases, "start", in_refs, out_refs, sem_refs)

        part = lax.dot_general(a_ref[...], b_ref[...], (contract, ((), ())), preferred_element_type=F32)
        if nk == 1:
            o_ref[...] = part.astype(out_dtype)
        else:
            acc = o_ref if acc_in_out else scratch[0]
            k = ids[2]

            @pl.when(k == 0)
            def _():
                acc[...] = part

            @pl.when(k > 0)
            def _():
                acc[...] += part

            if not acc_in_out:

                @pl.when(k == nk - 1)
                def _():
                    o_ref[...] = acc[...].astype(out_dtype)

        if phases:

            @pl.when((ids[0] == grid[0] - 1) & (ids[1] == grid[1] - 1) & (ids[2] == grid[2] - 1))
            def _():
                _run_phases(phases, "finish", in_refs, out_refs, sem_refs)

    sem = ("arbitrary",) * 3 if phases else ("parallel", "parallel", "arbitrary")
    res = pl.pallas_call(
        body,
        name=name,
        grid=grid,
        in_specs=[a_spec, b_spec] + [ANY] * len(x_ins),
        out_specs=[o_spec] + [ANY] * len(x_outs),
        out_shape=[jax.ShapeDtypeStruct(out_shape, out_dtype)] + x_outs,
        input_output_aliases=aliases,
        scratch_shapes=([] if acc_in_out else [pltpu.VMEM(blk, F32)]) + sems,
        compiler_params=_params(*sem),
    )(a, b, *x_ins)
    return (res[0], list(res[1:])) if phases else res[0]


MM_K_WHOLE = 4096


def _tiles(m, n, k, k_total=None):
    if (k if k_total is None else k_total) <= MM_K_WHOLE:
        return _tile(m, 1024), _tile(n, 1024), k
    return _tile(m, 1024), _tile(n, 2048), _tile(k, 2048)


def _mm_nn(name, a, b, out_dtype=F32, phases=()):
    m, k = a.shape
    n = b.shape[1]
    tm, tn, tk = _tiles(m, n, k)
    return _matmul(
        name, a, b, dims="nn", grid=(m // tm, n // tn, k // tk),
        a_spec=pl.BlockSpec((tm, tk), lambda i, j, l: (i, l)),
        b_spec=pl.BlockSpec((tk, tn), lambda i, j, l: (l, j)),
        o_spec=pl.BlockSpec((tm, tn), lambda i, j, l: (i, j)),
        out_shape=(m, n), out_dtype=out_dtype, phases=phases)


def _mm_nt(name, a, b, out_dtype=F32, phases=()):
    m, k = a.shape
    n = b.shape[0]
    tm, tn, tk = _tiles(m, n, k)
    return _matmul(
        name, a, b, dims="nt", grid=(m // tm, n // tn, k // tk),
        a_spec=pl.BlockSpec((tm, tk), lambda i, j, l: (i, l)),
        b_spec=pl.BlockSpec((tn, tk), lambda i, j, l: (j, l)),
        o_spec=pl.BlockSpec((tm, tn), lambda i, j, l: (i, j)),
        out_shape=(m, n), out_dtype=out_dtype, phases=phases)


def _mm_tn(name, a, b, out_dtype=F32, phases=()):
    k, m = a.shape
    n = b.shape[1]
    tm, tn, tk = _tiles(m, n, k)
    return _matmul(
        name, a, b, dims="tn", grid=(m // tm, n // tn, k // tk),
        a_spec=pl.BlockSpec((tk, tm), lambda i, j, l: (l, i)),
        b_spec=pl.BlockSpec((tk, tn), lambda i, j, l: (l, j)),
        o_spec=pl.BlockSpec((tm, tn), lambda i, j, l: (i, j)),
        out_shape=(m, n), out_dtype=out_dtype, phases=phases)


def _mm_nn_cols(name, a, wg, phases=()):
    m, k = a.shape
    ws = wg.shape[2]
    tm, tn, tk = _tiles(m, ws, k)
    npb = ws // tn
    return _matmul(
        name, a, wg, dims="nn", grid=(m // tm, N_CHIPS * npb, k // tk),
        a_spec=pl.BlockSpec((tm, tk), lambda i, j, l: (i, l)),
        b_spec=pl.BlockSpec((None, tk, tn), lambda i, j, l: (j // npb, l, j % npb)),
        o_spec=pl.BlockSpec((tm, tn), lambda i, j, l: (i, j)),
        out_shape=(m, N_CHIPS * ws), out_dtype=F32, phases=phases)


def _mm_nt_cols(name, a, wg, phases=()):
    m = a.shape[0]
    n, ws = wg.shape[1], wg.shape[2]
    tm, tn, tk = _tiles(m, n, ws, N_CHIPS * ws)
    kpb = ws // tk
    return _matmul(
        name, a, wg, dims="nt", grid=(m // tm, n // tn, N_CHIPS * kpb),
        a_spec=pl.BlockSpec((tm, tk), lambda i, j, l: (i, l)),
        b_spec=pl.BlockSpec((None, tn, tk), lambda i, j, l: (l // kpb, j, l % kpb)),
        o_spec=pl.BlockSpec((tm, tn), lambda i, j, l: (i, j)),
        out_shape=(m, n), out_dtype=F32, phases=phases)


def _mm_tn_cols(name, a, b, ws, phases=(), odd=None):
    k, m = a.shape
    rows = m if odd is None else m // 2
    tm, tn, tk = _tiles(m // 4, ws, k)
    npb, bpq = ws // tn, (m // 4) // tm
    pick = (lambda i: i) if odd is None else (lambda i: (2 * (i // bpq) + odd) * bpq + i % bpq)
    return _matmul(
        name, a, b, dims="tn", grid=(rows // tm, N_CHIPS * npb, k // tk),
        a_spec=pl.BlockSpec((tk, tm), lambda i, j, l: (l, pick(i))),
        b_spec=pl.BlockSpec((tk, tn), lambda i, j, l: (l, j)),
        o_spec=pl.BlockSpec((None, tm, tn), lambda i, j, l: (j // npb, i, j % npb)),
        out_shape=(N_CHIPS, rows, ws), out_dtype=F32, phases=phases)


def _grp_fwd(name, pooled, wgg):
    t, e = pooled.shape
    _, ng, rs, gw = wgg.shape
    tm, tn, tk = _tiles(t, gw, rs, gw)
    npb, kps = gw // tn, rs // tk
    return _matmul(
        name, pooled, wgg, dims="nn", grid=(t // tm, ng * npb, N_CHIPS * kps),
        a_spec=pl.BlockSpec((tm, tk), lambda i, j, l: (i, (j // npb) * (gw // tk) + l)),
        b_spec=pl.BlockSpec((None, None, tk, tn), lambda i, j, l: (l // kps, j // npb, l % kps, j % npb)),
        o_spec=pl.BlockSpec((tm, tn), lambda i, j, l: (i, j)),
        out_shape=(t, e), out_dtype=F32)


def _grp_bwd_x(name, dmm, wgg, phases=()):
    t, e = dmm.shape
    _, ng, rs, gw = wgg.shape
    tm, tn, tk = _tiles(t, rs, gw)
    npr, kpg = rs // tn, gw // tk
    return _matmul(
        name, dmm, wgg, dims="nt", grid=(t // tm, ng * N_CHIPS * npr, kpg),
        a_spec=pl.BlockSpec((tm, tk), lambda i, j, l: (i, (j // (N_CHIPS * npr)) * kpg + l)),
        b_spec=pl.BlockSpec(
            (None, None, tn, tk),
            lambda i, j, l: ((j % (N_CHIPS * npr)) // npr, j // (N_CHIPS * npr), j % npr, l)),
        o_spec=pl.BlockSpec((tm, tn), lambda i, j, l: (i, j)),
        out_shape=(t, e), out_dtype=F32, phases=phases)


def _grp_bwd_w(name, pooled, dmm, ng):
    t, e = pooled.shape
    gw = e // ng
    rs = gw // N_CHIPS
    _, tn, tk = _tiles(rs, gw, t)
    npb = gw // tn
    return _matmul(
        name, pooled, dmm, dims="tn", grid=(ng * N_CHIPS, npb, t // tk),
        a_spec=pl.BlockSpec((tk, rs), lambda i, j, l: (l, i)),
        b_spec=pl.BlockSpec((tk, tn), lambda i, j, l: (l, (i // N_CHIPS) * npb + j)),
        o_spec=pl.BlockSpec((None, None, rs, tn), lambda i, j, l: (i % N_CHIPS, i // N_CHIPS, 0, j)),
        out_shape=(N_CHIPS, ng, rs, gw), out_dtype=F32)


def _cast_into_slot(name, w, place):
    r, c = w.shape
    tr, tc = _tile(r, 512), _tile(c, 2048)

    def body(place_ref, w_ref, o_ref):
        o_ref[...] = w_ref[...].astype(BF16)

    return pl.pallas_call(
        body, name=name,
        grid_spec=pltpu.PrefetchScalarGridSpec(
            num_scalar_prefetch=1, grid=(r // tr, c // tc),
            in_specs=[pl.BlockSpec((tr, tc), lambda i, j, p: (i, j))],
            out_specs=pl.BlockSpec((None, tr, tc), lambda i, j, p: (p[0], i, j))),
        out_shape=jax.ShapeDtypeStruct((N_CHIPS, r, c), BF16),
        compiler_params=_params("parallel", "parallel"))(place, w)


def _rms(v):
    return lax.rsqrt(jnp.mean(v * v, axis=-1, keepdims=True) + NORM_EPS)


def _rms_fwd(name, x, g):
    t, d = x.shape
    tr = _tile(t, 256)

    def body(x_ref, g_ref, h_ref):
        xv = x_ref[...]
        h_ref[...] = (xv * _rms(xv) * g_ref[...]).astype(BF16)

    row = pl.BlockSpec((tr, d), lambda i: (i, 0))
    vec = pl.BlockSpec((1, d), lambda i: (0, 0))
    return pl.pallas_call(
        body, name=name, grid=(t // tr,), in_specs=[row, vec], out_specs=row,
        out_shape=jax.ShapeDtypeStruct((t, d), BF16), compiler_params=_params("parallel"))(x, g)


def _post_pre_fwd(name, x, m, g_post, g_pre):
    t, d = x.shape
    tr = _tile(t, 256)

    def body(x_ref, m_ref, gp_ref, gn_ref, x1_ref, h_ref):
        mv = m_ref[...]
        x1 = x_ref[...] + mv * _rms(mv) * gp_ref[...]
        x1_ref[...] = x1
        h_ref[...] = (x1 * _rms(x1) * gn_ref[...]).astype(BF16)

    row = pl.BlockSpec((tr, d), lambda i: (i, 0))
    vec = pl.BlockSpec((1, d), lambda i: (0, 0))
    return pl.pallas_call(
        body, name=name, grid=(t // tr,), in_specs=[row, row, vec, vec], out_specs=[row, row],
        out_shape=[jax.ShapeDtypeStruct((t, d), F32), jax.ShapeDtypeStruct((t, d), BF16)],
        compiler_params=_params("parallel"))(x, m, g_post, g_pre)


def _norm_bwd(dout, nrm, r, g):
    gd = dout * g
    return r * (gd - nrm * jnp.mean(gd * nrm, axis=-1, keepdims=True))


def _loss_post_bwd(name, x1, m, g_post, target):
    t, d = x1.shape
    tr = _tile(t, 128)

    def body(x_ref, m_ref, g_ref, t_ref, loss_ref, dx_ref, dm_ref, dg_ref):
        i = pl.program_id(0)
        mv = m_ref[...]
        r = _rms(mv)
        nrm = mv * r
        err = x_ref[...] + nrm * g_ref[...] - t_ref[...]
        part = 0.5 * jnp.sum(jnp.mean(err * err, axis=-1, keepdims=True), axis=0, keepdims=True)
        dx = err / d
        dx_ref[...] = dx
        dm_ref[...] = _norm_bwd(dx, nrm, r, g_ref[...]).astype(BF16)
        dg = jnp.sum(dx * nrm, axis=0, keepdims=True)

        @pl.when(i == 0)
        def _():
            loss_ref[...] = part
            dg_ref[...] = dg

        @pl.when(i > 0)
        def _():
            loss_ref[...] += part
            dg_ref[...] += dg

    row = pl.BlockSpec((tr, d), lambda i: (i, 0))
    vec = pl.BlockSpec((1, d), lambda i: (0, 0))
    one = pl.BlockSpec((1, 1), lambda i: (0, 0))
    return pl.pallas_call(
        body, name=name, grid=(t // tr,), in_specs=[row, row, vec, row], out_specs=[one, row, row, vec],
        out_shape=[jax.ShapeDtypeStruct((1, 1), F32), jax.ShapeDtypeStruct((t, d), F32),
                   jax.ShapeDtypeStruct((t, d), BF16), jax.ShapeDtypeStruct((1, d), F32)],
        compiler_params=_params("arbitrary"))(x1, m, g_post, target)


def _mid_bwd(name, dx2, dh1, x1, g_pre, m0, g_post):
    t, d = x1.shape
    tr = _tile(t, 128)

    def body(dx2_ref, dh_ref, x_ref, gn_ref, m_ref, gp_ref, dx_ref, dm_ref, dgn_ref, dgp_ref):
        i = pl.program_id(0)
        xv = x_ref[...]
        r1 = _rms(xv)
        n1 = xv * r1
        dh = dh_ref[...]
        dx = dx2_ref[...] + _norm_bwd(dh, n1, r1, gn_ref[...])
        dx_ref[...] = dx
        mv = m_ref[...]
        r0 = _rms(mv)
        n0 = mv * r0
        dm_ref[...] = _norm_bwd(dx, n0, r0, gp_ref[...]).astype(BF16)
        dgn = jnp.sum(dh * n1, axis=0, keepdims=True)
        dgp = jnp.sum(dx * n0, axis=0, keepdims=True)

        @pl.when(i == 0)
        def _():
            dgn_ref[...] = dgn
            dgp_ref[...] = dgp

        @pl.when(i > 0)
        def _():
            dgn_ref[...] += dgn
            dgp_ref[...] += dgp

    row = pl.BlockSpec((tr, d), lambda i: (i, 0))
    vec = pl.BlockSpec((1, d), lambda i: (0, 0))
    return pl.pallas_call(
        body, name=name, grid=(t // tr,), in_specs=[row, row, row, vec, row, vec],
        out_specs=[row, row, vec, vec],
        out_shape=[jax.ShapeDtypeStruct((t, d), F32), jax.ShapeDtypeStruct((t, d), BF16),
                   jax.ShapeDtypeStruct((1, d), F32), jax.ShapeDtypeStruct((1, d), F32)],
        compiler_params=_params("arbitrary"))(dx2, dh1, x1, g_pre, m0, g_post)


def _pre_bwd(name, dx1, dh0, x, g_pre):
    t, d = x.shape
    tr = _tile(t, 128)

    def body(dx1_ref, dh_ref, x_ref, g_ref, dx_ref, dg_ref):
        i = pl.program_id(0)
        xv = x_ref[...]
        r = _rms(xv)
        nrm = xv * r
        dh = dh_ref[...]
        dx_ref[...] = dx1_ref[...] + _norm_bwd(dh, nrm, r, g_ref[...])
        dg = jnp.sum(dh * nrm, axis=0, keepdims=True)

        @pl.when(i == 0)
        def _():
            dg_ref[...] = dg

        @pl.when(i > 0)
        def _():
            dg_ref[...] += dg

    row = pl.BlockSpec((tr, d), lambda i: (i, 0))
    vec = pl.BlockSpec((1, d), lambda i: (0, 0))
    return pl.pallas_call(
        body, name=name, grid=(t // tr,), in_specs=[row, row, row, vec], out_specs=[row, vec],
        out_shape=[jax.ShapeDtypeStruct((t, d), F32), jax.ShapeDtypeStruct((1, d), F32)],
        compiler_params=_params("arbitrary"))(dx1, dh0, x, g_pre)


def _a_stats(name, pa, e):
    t = pa.shape[0]
    tr = _tile(t, 64, 8)

    def body(v_ref, mu_ref, rs_ref):
        vg, _ = _gelu(v_ref[...])
        mu = jnp.mean(vg, axis=-1, keepdims=True)
        xc = vg - mu
        mu_ref[...] = mu
        rs_ref[...] = lax.rsqrt(jnp.mean(xc * xc, axis=-1, keepdims=True) + NORM_EPS)

    col = pl.BlockSpec((tr, 1), lambda i: (i, 0))
    return pl.pallas_call(
        body, name=name, grid=(t // tr,), in_specs=[pl.BlockSpec((tr, e), lambda i: (i, 1))],
        out_specs=[col, col],
        out_shape=[jax.ShapeDtypeStruct((t, 1), F32), jax.ShapeDtypeStruct((t, 1), F32)],
        compiler_params=_params("parallel"))(pa)


def _causal(w):
    c = w.shape[0]
    keep = lax.broadcasted_iota(jnp.int32, (c, c), 0) >= lax.broadcasted_iota(jnp.int32, (c, c), 1)
    return jnp.where(keep, w, 0.0), keep


def _a_gate_fwd(name, pa, mu, rs, ln_g, ln_b, w_s, b_s3):
    t = pa.shape[0]
    nh, c, _ = w_s.shape
    e = ln_g.shape[1]
    dh = e // nh
    rb = 2 * c if t % (2 * c) == 0 else c

    def body(u_ref, v_ref, z_ref, mu_ref, rs_ref, g_ref, b_ref, w_ref, bs_ref, y_ref):
        wc = _causal(w_ref[...])[0].astype(BF16)
        vg, _ = _gelu(v_ref[...])
        vn = ((vg - mu_ref[...]) * rs_ref[...] * g_ref[...] + b_ref[...]).astype(BF16)
        for ci in range(rb // c):
            rows = pl.ds(ci * c, c)
            sv = jnp.dot(wc, vn[ci * c:(ci + 1) * c], preferred_element_type=F32) + bs_ref[...]
            u, _ = _gelu(u_ref[rows, :])
            sz, _ = _silu(z_ref[rows, :])
            y_ref[rows, :] = (u * sv * sz).astype(BF16)

    blk = lambda off: pl.BlockSpec((rb, dh), lambda i, h: (i, off + h))
    col = pl.BlockSpec((rb, 1), lambda i, h: (i, 0))
    vec = pl.BlockSpec((1, dh), lambda i, h: (0, h))
    return pl.pallas_call(
        body, name=name, grid=(t // rb, nh),
        in_specs=[blk(0), blk(nh), blk(2 * nh), col, col, vec, vec,
                  pl.BlockSpec((None, c, c), lambda i, h: (h, 0, 0)),
                  pl.BlockSpec((None, c, 1), lambda i, h: (h, 0, 0))],
        out_specs=pl.BlockSpec((rb, dh), lambda i, h: (i, h)),
        out_shape=jax.ShapeDtypeStruct((t, e), BF16),
        compiler_params=_params("parallel", "parallel"))(pa, pa, pa, mu, rs, ln_g, ln_b, w_s, b_s3)


def _a_gate_bwd1(name, pa, dy, mu, rs, ln_g, ln_b, w_s, b_s3):
    t = pa.shape[0]
    nh, c, _ = w_s.shape
    e = ln_g.shape[1]
    dh = e // nh
    rb = 2 * c if t % (2 * c) == 0 else c

    def body(u_ref, v_ref, z_ref, dy_ref, mu_ref, rs_ref, g_ref, b_ref, w_ref, bs_ref,
             du_ref, dz_ref, dsv_ref, c1_ref, c2_ref):
        h = pl.program_id(1)
        wc = _causal(w_ref[...])[0].astype(BF16)
        vg, _ = _gelu(v_ref[...])
        xh = (vg - mu_ref[...]) * rs_ref[...]
        vn = (xh * g_ref[...] + b_ref[...]).astype(BF16)
        s1 = []
        s2 = []
        for ci in range(rb // c):
            rows = pl.ds(ci * c, c)
            lo, hi = ci * c, (ci + 1) * c
            sv = jnp.dot(wc, vn[lo:hi], preferred_element_type=F32) + bs_ref[...]
            u, du = _gelu(u_ref[rows, :])
            zv = z_ref[rows, :]
            sz, dsz = _silu(zv)
            dyv = dy_ref[rows, :]
            du_ref[rows, :] = (dyv * sv * sz * du).astype(BF16)
            dz_ref[rows, :] = (dyv * u * sv * dsz).astype(BF16)
            dsv = (dyv * u * sz).astype(BF16)
            dsv_ref[rows, :] = dsv
            dvn = lax.dot_general(wc, dsv, (((0,), (0,)), ((), ())), preferred_element_type=F32)
            dxh = dvn * g_ref[...]
            s1.append(jnp.sum(dxh, axis=-1, keepdims=True))
            s2.append(jnp.sum(dxh * xh[lo:hi], axis=-1, keepdims=True))
        p1 = jnp.concatenate(s1, axis=0)
        p2 = jnp.concatenate(s2, axis=0)

        @pl.when(h == 0)
        def _():
            c1_ref[...] = p1
            c2_ref[...] = p2

        @pl.when(h > 0)
        def _():
            c1_ref[...] += p1
            c2_ref[...] += p2

    blk = lambda off: pl.BlockSpec((rb, dh), lambda i, h: (i, off + h))
    col = pl.BlockSpec((rb, 1), lambda i, h: (i, 0))
    vec = pl.BlockSpec((1, dh), lambda i, h: (0, h))
    act = jax.ShapeDtypeStruct((t, e), BF16)
    stat = jax.ShapeDtypeStruct((t, 1), F32)
    return pl.pallas_call(
        body, name=name, grid=(t // rb, nh),
        in_specs=[blk(0), blk(nh), blk(2 * nh), blk(0), col, col, vec, vec,
                  pl.BlockSpec((None, c, c), lambda i, h: (h, 0, 0)),
                  pl.BlockSpec((None, c, 1), lambda i, h: (h, 0, 0))],
        out_specs=[blk(0), blk(0), blk(0), col, col],
        out_shape=[act, act, act, stat, stat],
        compiler_params=_params("parallel", "arbitrary"))(pa, pa, pa, dy, mu, rs, ln_g, ln_b, w_s, b_s3)


def _a_gate_bwd2(name, pa, dsv, mu, rs, c1, c2, ln_g, ln_b, w_s):
    t = pa.shape[0]
    nh, c, _ = w_s.shape
    e = ln_g.shape[1]
    dh = e // nh
    rb = 2 * c if t % (2 * c) == 0 else c

    def body(v_ref, dsv_ref, mu_ref, rs_ref, c1_ref, c2_ref, g_ref, b_ref, w_ref,
             dv_ref, dw_ref, dbs_ref, dg_ref, db_ref):
        i = pl.program_id(1)
        wcf, keep = _causal(w_ref[...])
        wc = wcf.astype(BF16)
        vg, dvg = _gelu(v_ref[...])
        rsv = rs_ref[...]
        xh = (vg - mu_ref[...]) * rsv
        vn = (xh * g_ref[...] + b_ref[...]).astype(BF16)
        dw = jnp.zeros((c, c), F32)
        dbs = jnp.zeros((c, 1), F32)
        dvns = []
        for ci in range(rb // c):
            lo, hi = ci * c, (ci + 1) * c
            dsv = dsv_ref[pl.ds(lo, c), :]
            dvns.append(lax.dot_general(wc, dsv, (((0,), (0,)), ((), ())), preferred_element_type=F32))
            dw += lax.dot_general(dsv, vn[lo:hi], (((1,), (1,)), ((), ())), preferred_element_type=F32)
            dbs += jnp.sum(dsv.astype(F32), axis=-1, keepdims=True)
        dvn = jnp.concatenate(dvns, axis=0)
        dxh = dvn * g_ref[...]
        dvv = rsv * (dxh - c1_ref[...] * (1.0 / e) - xh * (c2_ref[...] * (1.0 / e)))
        dv_ref[...] = (dvv * dvg).astype(BF16)
        dw = jnp.where(keep, dw, 0.0)
        dg = jnp.sum(dvn * xh, axis=0, keepdims=True)
        db = jnp.sum(dvn, axis=0, keepdims=True)

        @pl.when(i == 0)
        def _():
            dw_ref[...] = dw
            dbs_ref[...] = dbs
            dg_ref[...] = dg
            db_ref[...] = db

        @pl.when(i > 0)
        def _():
            dw_ref[...] += dw
            dbs_ref[...] += dbs
            dg_ref[...] += dg
            db_ref[...] += db

    col = pl.BlockSpec((rb, 1), lambda h, i: (i, 0))
    vec = pl.BlockSpec((1, dh), lambda h, i: (0, h))
    hblk = pl.BlockSpec((rb, dh), lambda h, i: (i, h))
    return pl.pallas_call(
        body, name=name, grid=(nh, t // rb),
        in_specs=[pl.BlockSpec((rb, dh), lambda h, i: (i, nh + h)), hblk, col, col, col, col, vec, vec,
                  pl.BlockSpec((None, c, c), lambda h, i: (h, 0, 0))],
        out_specs=[hblk, pl.BlockSpec((None, c, c), lambda h, i: (h, 0, 0)),
                   pl.BlockSpec((None, c, 1), lambda h, i: (h, 0, 0)), vec, vec],
        out_shape=[jax.ShapeDtypeStruct((t, e), BF16), jax.ShapeDtypeStruct((nh, c, c), F32),
                   jax.ShapeDtypeStruct((nh, c, 1), F32), jax.ShapeDtypeStruct((1, e), F32),
                   jax.ShapeDtypeStruct((1, e), F32)],
        compiler_params=_params("parallel", "arbitrary"))(pa, dsv, mu, rs, c1, c2, ln_g, ln_b, w_s)


def _pool(name, src, e, seq, backward):
    t = src.shape[0]
    ng = len(POOL_WINDOWS)
    gw = e // ng
    cw = _tile(gw, 256)

    def shifted(a, j, pos):
        if backward:
            return jnp.where(pos < seq - j, pltpu.roll(a, seq - j, 0), 0.0)
        return jnp.where(pos >= j, pltpu.roll(a, j, 0), 0.0)

    def body(p_ref, o_ref):
        grp = (pl.program_id(1) * cw) // gw
        pos = lax.broadcasted_iota(jnp.int32, (seq, cw), 0)
        posf = (pos + 1).astype(F32)
        for k, w in enumerate(POOL_WINDOWS):

            @pl.when(grp == k)
            def _(w=w):
                pv = p_ref[...]
                cnt = jnp.minimum(posf, float(w))
                acc = pv / cnt if backward else pv
                j = 1
                while j < w:
                    acc = acc + shifted(acc, j, pos)
                    j *= 2
                out = acc - pv if backward else acc / cnt - pv
                o_ref[...] = out.astype(BF16)

    spec = pl.BlockSpec((seq, cw), lambda s, j: (s, j))
    return pl.pallas_call(
        body, name=name, grid=(t // seq, e // cw), in_specs=[spec], out_specs=spec,
        out_shape=jax.ShapeDtypeStruct((t, e), BF16),
        compiler_params=_params("parallel", "parallel"))(src)


def _b_gate_fwd(name, mm, pb, b_grp, scale):
    t, e = mm.shape
    tr, tc = _tile(t, 512), _tile(e, 1024)
    nc = e // tc

    def body(mm_ref, z_ref, b_ref, s_ref, y_ref):
        sz, _ = _silu(z_ref[...])
        y_ref[...] = ((mm_ref[...] + b_ref[...]) * s_ref[...] * sz).astype(BF16)

    blk = pl.BlockSpec((tr, tc), lambda i, j: (i, j))
    vec = pl.BlockSpec((1, tc), lambda i, j: (0, j))
    return pl.pallas_call(
        body, name=name, grid=(t // tr, nc),
        in_specs=[blk, pl.BlockSpec((tr, tc), lambda i, j: (i, nc + j)), vec, vec], out_specs=blk,
        out_shape=jax.ShapeDtypeStruct((t, e), BF16),
        compiler_params=_params("parallel", "parallel"))(mm, pb, b_grp, scale)


def _b_gate_bwd(name, dy, mm, pb, b_grp, scale):
    t, e = mm.shape
    tr, tc = _tile(t, 512), _tile(e, 1024)
    nc = e // tc

    def body(dy_ref, mm_ref, z_ref, b_ref, s_ref, dmm_ref, dz_ref, db_ref, ds_ref):
        i = pl.program_id(1)
        sz, dsz = _silu(z_ref[...])
        dyv = dy_ref[...]
        mb = mm_ref[...] + b_ref[...]
        dmixed = dyv * sz
        dmm = dmixed * s_ref[...]
        dmm_ref[...] = dmm.astype(BF16)
        dz_ref[...] = (dyv * (mb * s_ref[...]) * dsz).astype(BF16)
        db = jnp.sum(dmm, axis=0, keepdims=True)
        ds = jnp.sum(dmixed * mb, axis=0, keepdims=True)

        @pl.when(i == 0)
        def _():
            db_ref[...] = db
            ds_ref[...] = ds

        @pl.when(i > 0)
        def _():
            db_ref[...] += db
            ds_ref[...] += ds

    blk = pl.BlockSpec((tr, tc), lambda j, i: (i, j))
    vec = pl.BlockSpec((1, tc), lambda j, i: (0, j))
    act = jax.ShapeDtypeStruct((t, e), BF16)
    stat = jax.ShapeDtypeStruct((1, e), F32)
    return pl.pallas_call(
        body, name=name, grid=(nc, t // tr),
        in_specs=[blk, blk, pl.BlockSpec((tr, tc), lambda j, i: (i, nc + j)), vec, vec],
        out_specs=[blk, blk, vec, vec], out_shape=[act, act, stat, stat],
        compiler_params=_params("parallel", "arbitrary"))(dy, mm, pb, b_grp, scale)


def _position():
    return lax.axis_index("x"), lax.axis_index("y"), lax.axis_index("c")


def _other_chips(x, y):
    return [(1 - x, y), (x, 1 - y), (1 - x, 1 - y)]


def _half(ref, c):
    n = ref.shape[0] // 2
    return ref.at[pl.ds(c * n, n)]


def _part(ref, c, part):
    q, nq = part
    n = ref.shape[0] // (2 * nq)
    return ref.at[pl.ds(c * nq * n + q * n, n)]


def _gather_phase(slots, small=None, part=(0, 1)):
    nw = len(slots)
    ns = 0 if small is None else 1
    n_ici = 3 * (nw + ns)
    n_sem = n_ici + 3 * nw + ns

    def copies(ins, outs, send_sems, recv_sems):
        x, y, c = _position()
        sibling = (x, y, 1 - c)

        def ici(j, k, slot, to):
            return pltpu.make_async_remote_copy(
                src_ref=_part(ins[k].at[slot], c, part), dst_ref=_part(outs[k].at[slot], c, part),
                send_sem=send_sems.at[j * nw + k], recv_sem=recv_sems.at[j * nw + k],
                device_id=to, device_id_type=MESH)

        def ici_small(j, slot, to):
            return pltpu.make_async_remote_copy(
                src_ref=ins[nw], dst_ref=outs[nw].at[slot], send_sem=send_sems.at[3 * nw + j],
                recv_sem=recv_sems.at[3 * nw + j], device_id=to, device_id_type=MESH)

        def d2d(j, k, slot, half):
            return pltpu.make_async_remote_copy(
                src_ref=_part(outs[k].at[slot], half, part), dst_ref=_part(outs[k].at[slot], half, part),
                send_sem=send_sems.at[n_ici + j * nw + k], recv_sem=recv_sems.at[n_ici + j * nw + k],
                device_id=sibling, device_id_type=MESH)

        def local():
            return pltpu.make_async_copy(ins[nw], outs[nw].at[2 * x + y], send_sems.at[n_sem - 1])

        return x, y, c, ici, ici_small, d2d, local

    def start(ins, outs, send_sems, recv_sems):
        x, y, c, ici, ici_small, _, local = copies(ins, outs, send_sems, recv_sems)
        if ns:
            local().start()
        for j, (cx, cy) in enumerate(_other_chips(x, y)):
            for k in range(nw):
                ici(j, k, 2 * x + y, (cx, cy, c)).start()
            if ns:
                ici_small(j, 2 * x + y, (cx, cy, c)).start()

    def finish(ins, outs, send_sems, recv_sems):
        x, y, c, ici, ici_small, d2d, local = copies(ins, outs, send_sems, recv_sems)
        chips = _other_chips(x, y)
        for j, (cx, cy) in enumerate(chips):
            for k in range(nw):
                ici(j, k, 2 * cx + cy, (x, y, c)).wait_recv()
                d2d(j, k, 2 * cx + cy, c).start()
        for j, (cx, cy) in enumerate(chips):
            if ns:
                ici_small(j, 2 * cx + cy, (x, y, c)).wait_recv()
            for k in range(nw):
                d2d(j, k, 2 * cx + cy, 1 - c).wait_recv()
        for j, (cx, cy) in enumerate(chips):
            for k in range(nw):
                ici(j, k, 2 * x + y, (cx, cy, c)).wait_send()
                d2d(j, k, 2 * cx + cy, c).wait_send()
            if ns:
                ici_small(j, 2 * x + y, (cx, cy, c)).wait_send()
        if ns:
            local().wait()

    outs = [jax.ShapeDtypeStruct(s.shape, s.dtype) for s in slots]
    if ns:
        outs.append(jax.ShapeDtypeStruct((N_CHIPS,) + small.shape, small.dtype))
    return _Phase(list(slots) + ([small] if ns else []), outs, {k: k for k in range(nw)}, n_sem, start, finish)


def _swap_phase(grads):
    nw = len(grads)

    def swaps(ins, outs, send_sems, recv_sems):
        x, y, c = _position()
        cps = []
        for k in range(nw):
            n = ins[k].shape[1] // 2
            cps.append(pltpu.make_async_remote_copy(
                src_ref=ins[k].at[pl.ds(0, N_CHIPS), pl.ds((1 - c) * n, n)], dst_ref=outs[k],
                send_sem=send_sems.at[k], recv_sem=recv_sems.at[k],
                device_id=(x, y, 1 - c), device_id_type=MESH))
        return cps

    def start(*refs):
        for cp in swaps(*refs):
            cp.start()

    def finish(*refs):
        for cp in swaps(*refs):
            cp.wait()

    halves = [jax.ShapeDtypeStruct((g.shape[0], g.shape[1] // 2, g.shape[2]), F32) for g in grads]
    return _Phase(grads, halves, {}, nw, start, finish)


def _add_halves(name, grad, theirs, place):
    s, half, w = theirs.shape
    tr, tc = _tile(half, 256), _tile(w, 2048)
    nrt = half // tr

    def body(place_ref, a_ref, b_ref, h_ref, f_ref):
        v = a_ref[...] + b_ref[...]
        h_ref[...] = v.astype(BF16)

        @pl.when(pl.program_id(2) == place_ref[0])
        def _():
            f_ref[...] = v

    return pl.pallas_call(
        body, name=name,
        grid_spec=pltpu.PrefetchScalarGridSpec(
            num_scalar_prefetch=1, grid=(nrt, w // tc, s),
            in_specs=[pl.BlockSpec((None, tr, tc), lambda i, j, q, p: (q, p[1] * nrt + i, j)),
                      pl.BlockSpec((None, tr, tc), lambda i, j, q, p: (q, i, j))],
            out_specs=[pl.BlockSpec((None, tr, tc), lambda i, j, q, p: (q, i, j)),
                       pl.BlockSpec((tr, tc), lambda i, j, q, p: (p[1] * nrt + i, j))]),
        out_shape=[jax.ShapeDtypeStruct(theirs.shape, BF16), jax.ShapeDtypeStruct((2 * half, w), F32)],
        compiler_params=_params("parallel", "parallel", "arbitrary"))(place, grad, theirs)


def _scatter_phase(own, pb, part=(0, 1), got=None):
    nw = len(own)

    def copies(ins, outs, send_sems, recv_sems):
        own_in, pbs = ins[:nw], ins[nw:2 * nw]
        own_out, gots = outs[:nw], outs[nw:]
        x, y, c = _position()
        sibling = (x, y, 1 - c)

        def d2d_own(k, half):
            return pltpu.make_async_remote_copy(
                src_ref=_part(own_in[k], half, part), dst_ref=_part(own_out[k], half, part),
                send_sem=send_sems.at[k], recv_sem=recv_sems.at[k], device_id=sibling, device_id_type=MESH)

        def ici(j, k, shard, to):
            n = pbs[k].shape[1] // part[1]
            return pltpu.make_async_remote_copy(
                src_ref=pbs[k].at[shard, pl.ds(part[0] * n, n)], dst_ref=_part(gots[k].at[j], c, part),
                send_sem=send_sems.at[nw + j * nw + k], recv_sem=recv_sems.at[nw + j * nw + k],
                device_id=to, device_id_type=MESH)

        def d2d(j, k, half):
            return pltpu.make_async_remote_copy(
                src_ref=_part(gots[k].at[j], half, part), dst_ref=_part(gots[k].at[j], half, part),
                send_sem=send_sems.at[4 * nw + j * nw + k], recv_sem=recv_sems.at[4 * nw + j * nw + k],
                device_id=sibling, device_id_type=MESH)

        return x, y, c, d2d_own, ici, d2d

    def start(*refs):
        x, y, c, d2d_own, ici, _ = copies(*refs)
        for k in range(nw):
            d2d_own(k, c).start()
        for j, (cx, cy) in enumerate(_other_chips(x, y)):
            for k in range(nw):
                ici(j, k, 2 * cx + cy, (cx, cy, c)).start()

    def finish(*refs):
        x, y, c, d2d_own, ici, d2d = copies(*refs)
        for j in range(3):
            for k in range(nw):
                ici(j, k, 2 * x + y, (x, y, c)).wait_recv()
                d2d(j, k, c).start()
        for k in range(nw):
            d2d_own(k, 1 - c).wait_recv()
        for j in range(3):
            for k in range(nw):
                d2d(j, k, 1 - c).wait_recv()
        for k in range(nw):
            d2d_own(k, c).wait_send()
        for j, (cx, cy) in enumerate(_other_chips(x, y)):
            for k in range(nw):
                ici(j, k, 2 * cx + cy, (cx, cy, c)).wait_send()
                d2d(j, k, c).wait_send()

    own_shape = [jax.ShapeDtypeStruct(o.shape, F32) for o in own]
    got_shape = [jax.ShapeDtypeStruct((3,) + o.shape, BF16) for o in own]
    aliases = {k: k for k in range(nw)}
    if got is not None:
        aliases.update({2 * nw + k: nw + k for k in range(nw)})
    return _Phase(list(own) + list(pb) + list(got or []), own_shape + got_shape, aliases, 7 * nw, start, finish)


def _gather_small(part):
    def body(p_ref, o_ref, send_sems, recv_sems, loc_sem):
        x, y, c = _position()
        me = 4 * x + 2 * y + c
        mine = pltpu.make_async_copy(p_ref, o_ref.at[me], loc_sem)
        mine.start()
        sent = []
        for mask in range(1, N_DEV):
            dx, dy, dc = (mask >> 2) & 1, (mask >> 1) & 1, mask & 1
            px, py, pc = x ^ dx, y ^ dy, c ^ dc
            cp = pltpu.make_async_remote_copy(
                src_ref=p_ref, dst_ref=o_ref.at[me], send_sem=send_sems.at[mask - 1],
                recv_sem=recv_sems.at[mask - 1], device_id=(px, py, pc), device_id_type=MESH)
            cp.start()
            sent.append((cp, 4 * px + 2 * py + pc))
        for mask in range(1, N_DEV):
            cp, peer = sent[mask - 1]
            pltpu.make_async_remote_copy(
                src_ref=p_ref, dst_ref=o_ref.at[peer], send_sem=send_sems.at[mask - 1],
                recv_sem=recv_sems.at[mask - 1], device_id=(x, y, c), device_id_type=MESH).wait_recv()
        for cp, _ in sent:
            cp.wait_send()
        mine.wait()

    return pl.pallas_call(
        body, name="gather_small_grads", in_specs=[ANY], out_specs=ANY,
        out_shape=jax.ShapeDtypeStruct((N_DEV,) + part.shape, F32),
        scratch_shapes=[pltpu.SemaphoreType.DMA((N_DEV - 1,)), pltpu.SemaphoreType.DMA((N_DEV - 1,)),
                        pltpu.SemaphoreType.DMA(())],
    )(part)


def _sum_slots(name, parts):
    s, r, c = parts.shape
    tr = _tile(r, 512, 8)

    def body(p_ref, o_ref):
        acc = p_ref[0]
        for q in range(1, s):
            acc = acc + p_ref[q]
        o_ref[...] = acc

    return pl.pallas_call(
        body, name=name, grid=(r // tr,), in_specs=[pl.BlockSpec((s, tr, c), lambda i: (0, i, 0))],
        out_specs=pl.BlockSpec((tr, c), lambda i: (i, 0)), out_shape=jax.ShapeDtypeStruct((r, c), F32),
        compiler_params=_params("parallel"))(parts)


def _adamw_math(w, g, m, v):
    m = ADAM_B1 * m + (1.0 - ADAM_B1) * g
    v = ADAM_B2 * v + (1.0 - ADAM_B2) * (g * g)
    m_hat = m / (1.0 - ADAM_B1 ** ADAM_STEP)
    v_hat = v / (1.0 - ADAM_B2 ** ADAM_STEP)
    delta = -ADAM_LR * (m_hat / (jnp.sqrt(v_hat) + ADAM_EPS) + ADAM_WD * w)
    return delta, m, v


def _adamw(name, w, m, v, own, got=None, odd=None, into=None):
    r, c = w.shape
    rows = r if odd is None else r // 2
    tr, tc = _tile(r // 4, 256, 8) if odd is not None else _tile(r, 256, 8), _tile(c, 1024)
    bpq = (r // 4) // tr if odd is not None else 1
    pick = (lambda i: i) if odd is None else (lambda i: (2 * (i // bpq) + odd) * bpq + i % bpq)
    n_in = 4 + (0 if got is None else 1)

    def body(*refs):
        w_ref, m_ref, v_ref, own_ref = refs[:4]
        outs = refs[n_in + (0 if into is None else 4):]
        g = own_ref[...]
        if got is not None:
            for j in range(3):
                g = g + refs[4][j].astype(F32)
        delta, mn, vn = _adamw_math(w_ref[...], g, m_ref[...], v_ref[...])
        outs[0][...] = g
        outs[1][...] = delta
        outs[2][...] = mn
        outs[3][...] = vn

    full = pl.BlockSpec((tr, tc), lambda i, j: (pick(i), j))
    ins, args = [full] * 3 + [pl.BlockSpec((tr, tc), lambda i, j: (i, j))], [w, m, v, own]
    if got is not None:
        ins.append(pl.BlockSpec((3, tr, tc), lambda i, j: (0, i, j)))
        args.append(got)
    aliases = {}
    if into is not None:
        ins += [ANY] * 4
        args += list(into)
        aliases = {n_in + q: q for q in range(4)}
    return pl.pallas_call(
        body, name=name, grid=(rows // tr, c // tc), in_specs=ins, out_specs=[full] * 4,
        out_shape=[jax.ShapeDtypeStruct((r, c), F32)] * 4, input_output_aliases=aliases,
        compiler_params=_params("parallel", "parallel"))(*args)


def _pack(parts):
    return jnp.concatenate([p.reshape(-1) for p in parts]).reshape(-1, LANES)


def _unpack(packed, shapes):
    flat = packed.reshape(-1)
    out, off = [], 0
    for s in shapes:
        n = 1
        for d in s:
            n *= d
        out.append(flat[off:off + n].reshape(s))
        off += n
    return out


def kernel(x, pre_norm, post_norm, a_w_in, a_ln_g, a_ln_b, a_w_s, a_b_s, a_w_out, b_w_in, b_w_grp, b_b_grp, b_scale, b_w_out, loss_target, m_pre_norm, m_post_norm, m_a_w_in, m_a_ln_g, m_a_ln_b, m_a_w_s, m_a_b_s, m_a_w_out, m_b_w_in, m_b_w_grp, m_b_b_grp, m_b_scale, m_b_w_out, v_pre_norm, v_post_norm, v_a_w_in, v_a_ln_g, v_a_ln_b, v_a_w_s, v_a_b_s, v_a_w_out, v_b_w_in, v_b_w_grp, v_b_b_grp, v_b_scale, v_b_w_out):
    nb, seq, d = x.shape
    t = nb * seq
    e = a_ln_g.shape[1]
    nh, chunk = a_w_s.shape[1], a_w_s.shape[2]
    ng, rs, gw = b_w_grp.shape[1], b_w_grp.shape[2], b_w_grp.shape[3]
    wa, wb = a_w_in.shape[2], b_w_in.shape[2]
    cx, cy = lax.axis_index("x"), lax.axis_index("y")
    chip = 2 * cx + cy

    big_w = [a_w_in.reshape(d, wa), a_w_out.reshape(e // N_CHIPS, d), b_w_in.reshape(d, wb),
             b_w_grp.reshape(ng * rs, gw), b_w_out.reshape(e // N_CHIPS, d)]
    big_m = [m_a_w_in.reshape(d, wa), m_a_w_out.reshape(e // N_CHIPS, d), m_b_w_in.reshape(d, wb),
             m_b_w_grp.reshape(ng * rs, gw), m_b_w_out.reshape(e // N_CHIPS, d)]
    big_v = [v_a_w_in.reshape(d, wa), v_a_w_out.reshape(e // N_CHIPS, d), v_b_w_in.reshape(d, wb),
             v_b_w_grp.reshape(ng * rs, gw), v_b_w_out.reshape(e // N_CHIPS, d)]
    names = ["a_w_in", "a_w_out", "b_w_in", "b_w_grp", "b_w_out"]
    place = jnp.stack([chip, lax.axis_index("c")]).astype(jnp.int32)
    slots = [_cast_into_slot("cast_" + n, w, place) for n, w in zip(names, big_w)]
    small_w = jnp.concatenate([b_b_grp.reshape(ng, rs), b_scale.reshape(ng, rs)], axis=0)
    wa_g, gsmall = _comm_only("gather_a_w_in", [_gather_phase(slots[:1], small_w)])
    b_grp_full = jnp.transpose(gsmall[:, :ng, :], (1, 0, 2)).reshape(1, e)
    scale_full = gsmall[:, ng:, :].reshape(1, e)

    xf = x.reshape(t, d)
    tgt = loss_target.reshape(t, d)
    g_pre0, g_pre1 = pre_norm[0:1], pre_norm[1:2]
    g_post0, g_post1 = post_norm[0:1], post_norm[1:2]
    w_s = a_w_s.reshape(nh, chunk, chunk)
    b_s3 = a_b_s.reshape(nh, chunk, 1)

    h0 = _rms_fwd("pre_norm0", xf, g_pre0)
    pa, (wao_g, wb_half) = _mm_nn_cols(
        "a_in_proj", h0, wa_g, [_gather_phase(slots[1:2]), _gather_phase(slots[2:3], part=(0, 2))])
    wao_g = wao_g.reshape(e, d)
    mu, rstd = _a_stats("a_ln_stats", pa, e)
    y0 = _a_gate_fwd("a_gate", pa, mu, rstd, a_ln_g, a_ln_b, w_s, b_s3)
    m0, (wb_g,) = _mm_nn("a_out_proj", y0, wao_g, phases=[_gather_phase([wb_half], part=(1, 2))])
    x1, h1 = _post_pre_fwd("post0_pre1", xf, m0, g_post0, g_pre1)
    pb, (wg_g, wbo_g) = _mm_nn_cols("b_in_proj", h1, wb_g, [_gather_phase(slots[3:5])])
    wg_g = wg_g.reshape(N_CHIPS, ng, rs, gw)
    wbo_g = wbo_g.reshape(e, d)
    pooled = _pool("b_pool", pb, e, seq, backward=False)
    mm = _grp_fwd("b_grp_proj", pooled, wg_g)
    y1 = _b_gate_fwd("b_gate", mm, pb, b_grp_full, scale_full)
    m1 = _mm_nn("b_out_proj", y1, wbo_g)
    loss, dx2, dm1, dg_post1 = _loss_post_bwd("loss_post1_bwd", x1, m1, g_post1, tgt)

    def chip_sum(n, g, theirs):
        pbk, ownk = _add_halves("chip_sum_" + n, g, theirs, place)
        return [ownk], [pbk]

    reduced = {}
    dy1 = _mm_nt("b_out_dx", dm1, wbo_g)
    g_wbo = _mm_tn("b_out_dw", y1, dm1).reshape(N_CHIPS, e // N_CHIPS, d)
    dmm, dzb, db_grp, dscale = _b_gate_bwd("b_gate_bwd", dy1, mm, pb, b_grp_full, scale_full)
    dpooled, (th,) = _grp_bwd_x("b_grp_dx", dmm, wg_g, [_swap_phase([g_wbo])])
    s_bo = chip_sum("b_w_out", g_wbo, th)
    g_wg = _grp_bwd_w("b_grp_dw", pooled, dmm, ng).reshape(N_CHIPS, ng * rs, gw)
    dp = _pool("b_pool_bwd", dpooled, e, seq, backward=True)
    dpb = jnp.concatenate([dp, dzb], axis=1)
    dh1, (own, got, th) = _mm_nt_cols("b_in_dx", dpb, wb_g, [_scatter_phase(*s_bo), _swap_phase([g_wg])])
    reduced["b_w_out"] = (own, got)
    s_g = chip_sum("b_w_grp", g_wg, th)
    g_wb, reduced["b_w_grp"] = _mm_tn_cols("b_in_dw", h1, dpb, wb, [_scatter_phase(*s_g)])
    dx1, dm0, dg_pre1, dg_post0 = _mid_bwd("pre1_post0_bwd", dx2, dh1, x1, g_pre1, m0, g_post0)
    dy0, (th,) = _mm_nt("a_out_dx", dm0, wao_g, phases=[_swap_phase([g_wb])])
    s_b = chip_sum("b_w_in", g_wb, th)
    g_wao, (own, got) = _mm_tn("a_out_dw", y0, dm0, phases=[_scatter_phase(*s_b, part=(0, 2))])
    g_wao = g_wao.reshape(N_CHIPS, e // N_CHIPS, d)
    du, dz, dsv, c1, c2 = _a_gate_bwd1("a_gate_bwd1", pa, dy0, mu, rstd, a_ln_g, a_ln_b, w_s, b_s3)
    dv, dw_s, db_s, dln_g, dln_b = _a_gate_bwd2("a_gate_bwd2", pa, dsv, mu, rstd, c1, c2, a_ln_g, a_ln_b, w_s)
    dpa = jnp.concatenate([du, dv, dz], axis=1)
    g_lo, (own, got, th) = _mm_tn_cols(
        "a_in_dw_even", h0, dpa, wa, [_scatter_phase([own], s_b[1], part=(1, 2), got=[got]), _swap_phase([g_wao])],
        odd=0)
    reduced["b_w_in"] = (own, got)
    s_ao = chip_sum("a_w_out", g_wao, th)
    g_hi, (th, own, got) = _mm_tn_cols(
        "a_in_dw_odd", h0, dpa, wa, [_swap_phase([g_lo]), _scatter_phase(*s_ao)], odd=1)
    reduced["a_w_out"] = (own, got)
    s_lo = chip_sum("a_w_in_even", g_lo, th)
    dh0, (th, own_lo, got_lo) = _mm_nt_cols("a_in_dx", dpa, wa_g, [_swap_phase([g_hi]), _scatter_phase(*s_lo)])
    s_hi = chip_sum("a_w_in_odd", g_hi, th)
    grad_x, dg_pre0 = _pre_bwd("pre0_bwd", dx1, dh0, xf, g_pre0)
    own_hi, got_hi = _comm_only("scatter_a_w_in_odd", [_scatter_phase(*s_hi)])
    big_out = []
    for k, n in enumerate(names):
        if n == "a_w_in":
            even = _adamw("adamw_a_w_in_even", big_w[k], big_m[k], big_v[k], own_lo, got_lo, odd=0)
            big_out.append(_adamw("adamw_a_w_in_odd", big_w[k], big_m[k], big_v[k], own_hi, got_hi, odd=1, into=even))
        else:
            big_out.append(_adamw("adamw_" + n, big_w[k], big_m[k], big_v[k], *reduced[n]))

    small_shapes = [(2, d), (2, d), (1, e), (1, e), (1, nh, chunk, chunk), (1, nh, chunk), (1, e), (1, e)]
    part = _pack([jnp.concatenate([dg_pre0, dg_pre1], axis=0), jnp.concatenate([dg_post0, dg_post1], axis=0),
                  dln_g, dln_b, dw_s, db_s, db_grp, dscale])
    g_small = _sum_slots("sum_small_grads", _gather_small(part))
    g_pre, g_post, g_lng, g_lnb, g_ws, g_bs, g_bgrp_full, g_scale_full = _unpack(g_small, small_shapes)
    g_bgrp = lax.dynamic_slice_in_dim(g_bgrp_full.reshape(ng, N_CHIPS, rs), chip, 1, axis=1).reshape(1, ng, rs)
    g_scale = lax.dynamic_slice_in_dim(g_scale_full.reshape(N_CHIPS, gw), chip, 1, axis=0)
    small_names = ["pre_norm", "post_norm", "a_ln_g", "a_ln_b", "a_w_s", "a_b_s", "b_b_grp", "b_scale"]
    small_g = [g_pre, g_post, g_lng, g_lnb, g_ws, g_bs, g_bgrp, g_scale]
    small_ws = [pre_norm, post_norm, a_ln_g, a_ln_b, a_w_s, a_b_s, b_b_grp, b_scale]
    small_ms = [m_pre_norm, m_post_norm, m_a_ln_g, m_a_ln_b, m_a_w_s, m_a_b_s, m_b_b_grp, m_b_scale]
    small_vs = [v_pre_norm, v_post_norm, v_a_ln_g, v_a_ln_b, v_a_w_s, v_a_b_s, v_b_b_grp, v_b_scale]
    packed = _adamw("adamw_small", _pack(small_ws), _pack(small_ms), _pack(small_vs), _pack(small_g))
    small_out = [_unpack(p, [w.shape for w in small_ws]) for p in packed]

    loss = lax.psum(loss[0, 0], ("x", "y", "c"))
    order = ["pre_norm", "post_norm", "a_w_in", "a_ln_g", "a_ln_b", "a_w_s", "a_b_s", "a_w_out", "b_w_in",
             "b_w_grp", "b_b_grp", "b_scale", "b_w_out"]
    big_shapes = dict(zip(names, [a_w_in.shape, a_w_out.shape, b_w_in.shape, b_w_grp.shape, b_w_out.shape]))
    outs = [loss, grad_x.reshape(nb, seq, d)]
    for kind in range(4):
        for n in order:
            if n in big_shapes:
                outs.append(big_out[names.index(n)][kind].reshape(big_shapes[n]))
            else:
                outs.append(small_out[kind][small_names.index(n)])
    return tuple(outs)
```

```python
import functools

import jax
import jax.numpy as jnp
from jax import lax
from jax.experimental import pallas as pl
from jax.experimental.pallas import tpu as pltpu

F32 = jnp.float32
BF16 = jnp.bfloat16
NORM_EPS = 1e-6
POOL_WINDOWS = (2, 4, 8, 16)
ADAM_LR = 0.001
ADAM_B1 = 0.9
ADAM_B2 = 0.999
ADAM_EPS = 1e-08
ADAM_WD = 0.01
ADAM_STEP = 10
N_CHIPS = 4
N_DEV = 8
V7X_VMEM_LIMIT_BYTES = 56 * 1024 * 1024
LANES = 128
MESH = pl.DeviceIdType.MESH
ANY = pl.BlockSpec(memory_space=pl.ANY)
SQRT_HALF = 0.7071067811865476
INV_SQRT_2PI = 0.3989422804014327


def _tile(dim, pref, mult=LANES):
    if dim <= pref:
        return dim
    t = (pref // mult) * mult
    while t >= mult:
        if dim % t == 0:
            return t
        t -= mult
    return dim


def _params(*sem):
    return pltpu.CompilerParams(dimension_semantics=sem or None, vmem_limit_bytes=V7X_VMEM_LIMIT_BYTES)


def _gelu(x):
    cdf = 0.5 * (1.0 + lax.erf(x * SQRT_HALF))
    pdf = jnp.exp(-0.5 * x * x) * INV_SQRT_2PI
    return x * cdf, cdf + x * pdf


def _silu(z):
    s = 1.0 / (1.0 + jnp.exp(-z))
    return z * s, s * (1.0 + z * (1.0 - s))


class _Phase:
    def __init__(self, ins, outs, aliases, n_sem, start, finish, relay=None):
        self.ins, self.outs, self.aliases, self.n_sem = list(ins), list(outs), dict(aliases), n_sem
        self.start, self.finish, self.relay = start, finish, relay


RELAY_AT = 0.85


def _hosted(body, phases, grid, n_in, n_out, n_scratch=0):
    ins, outs, aliases, sems = [], [], {}, []
    for ph in phases:
        for i, o in ph.aliases.items():
            aliases[n_in + len(ins) + i] = n_out + len(outs) + o
        ins += ph.ins
        outs += ph.outs
        sems += [pltpu.SemaphoreType.DMA((ph.n_sem,)), pltpu.SemaphoreType.DMA((ph.n_sem,))]
    if not phases:
        return body, ins, outs, aliases, sems
    steps = 1
    for n in grid:
        steps *= n
    relay_at = min(steps - 1, int(RELAY_AT * steps))

    def wrapped(*refs):
        own_in, refs = refs[:n_in], refs[n_in:]
        in_refs, refs = refs[:len(ins)], refs[len(ins):]
        own_out, refs = refs[:n_out], refs[n_out:]
        out_refs, refs = refs[:len(outs)], refs[len(outs):]
        own_scratch, sem_refs = refs[:n_scratch], refs[n_scratch:]

        def run(which):
            i = o = 0
            for p, ph in enumerate(phases):
                fn = getattr(ph, which)
                if fn is not None:
                    fn(in_refs[i:i + len(ph.ins)], out_refs[o:o + len(ph.outs)], sem_refs[2 * p], sem_refs[2 * p + 1])
                i += len(ph.ins)
                o += len(ph.outs)

        if not grid:
            run("start")
            run("relay")
            run("finish")
            return
        step = 0
        for d, n in enumerate(grid):
            step = step * n + pl.program_id(d)
        pl.when(step == 0)(lambda: run("start"))
        body(*own_in, *own_out, *own_scratch)
        pl.when(step == relay_at)(lambda: run("relay"))
        pl.when(step == steps - 1)(lambda: run("finish"))

    return wrapped, ins, outs, aliases, sems


def _comm_only(name, phases):
    body, ins, outs, aliases, sems = _hosted(None, phases, (), 0, 0)
    return pl.pallas_call(
        body, name=name, in_specs=[ANY] * len(ins), out_specs=[ANY] * len(outs), out_shape=outs,
        input_output_aliases=aliases, scratch_shapes=sems)(*ins)


def _matmul(name, a, b, *, dims, grid, a_spec, b_spec, o_spec, out_shape, out_dtype, phases=(), into=None):
    nk = grid[2]
    contract = {"nn": ((1,), (0,)), "nt": ((1,), (1,)), "tn": ((0,), (0,))}[dims]
    acc_in_out = out_dtype == F32 or nk == 1
    blk = tuple(d for d in o_spec.block_shape if d is not None)
    n_in = 2 if into is None else 3

    def body(*refs):
        a_ref, b_ref, o_ref = refs[0], refs[1], refs[n_in]
        part = lax.dot_general(a_ref[...], b_ref[...], (contract, ((), ())), preferred_element_type=F32)
        if nk == 1:
            o_ref[...] = part.astype(out_dtype)
            return
        acc = o_ref if acc_in_out else refs[n_in + 1]
        k = pl.program_id(2)

        @pl.when(k == 0)
        def _():
            acc[...] = part

        @pl.when(k > 0)
        def _():
            acc[...] += part

        if not acc_in_out:

            @pl.when(k == nk - 1)
            def _():
                o_ref[...] = acc[...].astype(out_dtype)

    own_scratch = [] if acc_in_out else [pltpu.VMEM(blk, F32)]
    body, x_ins, x_outs, aliases, sems = _hosted(body, phases, grid, n_in, 1, len(own_scratch))
    if into is not None:
        aliases[2] = 0
    sem = ("arbitrary",) * 3 if phases else ("parallel", "parallel", "arbitrary")
    res = pl.pallas_call(
        body,
        name=name,
        grid=grid,
        in_specs=[a_spec, b_spec] + ([] if into is None else [ANY]) + [ANY] * len(x_ins),
        out_specs=[o_spec] + [ANY] * len(x_outs),
        out_shape=[jax.ShapeDtypeStruct(out_shape, out_dtype)] + x_outs,
        input_output_aliases=aliases,
        scratch_shapes=own_scratch + sems,
        compiler_params=_params(*sem),
    )(a, b, *([] if into is None else [into]), *x_ins)
    return (res[0], list(res[1:])) if phases else res[0]


MM_K_WHOLE = 4096


def _tiles(m, n, k, k_total=None):
    if (k if k_total is None else k_total) <= MM_K_WHOLE:
        return _tile(m, 1024), _tile(n, 1024), k
    return _tile(m, 1024), _tile(n, 2048), _tile(k, 2048)


def _mm_nn(name, a, b, out_dtype=F32, phases=()):
    m, k = a.shape
    n = b.shape[1]
    tm, tn, tk = _tiles(m, n, k)
    return _matmul(
        name, a, b, dims="nn", grid=(m // tm, n // tn, k // tk),
        a_spec=pl.BlockSpec((tm, tk), lambda i, j, l: (i, l)),
        b_spec=pl.BlockSpec((tk, tn), lambda i, j, l: (l, j)),
        o_spec=pl.BlockSpec((tm, tn), lambda i, j, l: (i, j)),
        out_shape=(m, n), out_dtype=out_dtype, phases=phases)


def _mm_nt(name, a, b, out_dtype=F32, phases=()):
    m, k = a.shape
    n = b.shape[0]
    tm, tn, tk = _tiles(m, n, k)
    return _matmul(
        name, a, b, dims="nt", grid=(m // tm, n // tn, k // tk),
        a_spec=pl.BlockSpec((tm, tk), lambda i, j, l: (i, l)),
        b_spec=pl.BlockSpec((tn, tk), lambda i, j, l: (j, l)),
        o_spec=pl.BlockSpec((tm, tn), lambda i, j, l: (i, j)),
        out_shape=(m, n), out_dtype=out_dtype, phases=phases)


def _mm_tn(name, a, b, out_dtype=F32, phases=()):
    k, m = a.shape
    n = b.shape[1]
    tm, tn, tk = _tiles(m, n, k)
    return _matmul(
        name, a, b, dims="tn", grid=(m // tm, n // tn, k // tk),
        a_spec=pl.BlockSpec((tk, tm), lambda i, j, l: (l, i)),
        b_spec=pl.BlockSpec((tk, tn), lambda i, j, l: (l, j)),
        o_spec=pl.BlockSpec((tm, tn), lambda i, j, l: (i, j)),
        out_shape=(m, n), out_dtype=out_dtype, phases=phases)


def _mm_nn_cols(name, a, wg, phases=()):
    m, k = a.shape
    ws = wg.shape[2]
    tm, tn, tk = _tiles(m, ws, k)
    npb = ws // tn
    return _matmul(
        name, a, wg, dims="nn", grid=(m // tm, N_CHIPS * npb, k // tk),
        a_spec=pl.BlockSpec((tm, tk), lambda i, j, l: (i, l)),
        b_spec=pl.BlockSpec((None, tk, tn), lambda i, j, l: (j // npb, l, j % npb)),
        o_spec=pl.BlockSpec((tm, tn), lambda i, j, l: (i, j)),
        out_shape=(m, N_CHIPS * ws), out_dtype=F32, phases=phases)


def _mm_nt_cols(name, a, wg, phases=(), rows=None, into=None):
    m = a.shape[0]
    n, ws = wg.shape[1], wg.shape[2]
    q, nq = rows or (0, 1)
    tm, tn, tk = _tiles(m // nq, n, ws, N_CHIPS * ws)
    kpb, off = ws // tk, q * (m // nq // tm)
    return _matmul(
        name, a, wg, dims="nt", grid=(m // nq // tm, n // tn, N_CHIPS * kpb),
        a_spec=pl.BlockSpec((tm, tk), lambda i, j, l: (i + off, l)),
        b_spec=pl.BlockSpec((None, tn, tk), lambda i, j, l: (l // kpb, j, l % kpb)),
        o_spec=pl.BlockSpec((tm, tn), lambda i, j, l: (i + off, j)),
        out_shape=(m, n), out_dtype=F32, phases=phases, into=into)


def _mm_tn_cols(name, a, b, ws, phases=(), odd=None):
    k, m = a.shape
    rows = m if odd is None else m // 2
    tm, tn, tk = _tiles(m // 4, ws, k)
    npb, bpq = ws // tn, (m // 4) // tm
    pick = (lambda i: i) if odd is None else (lambda i: (2 * (i // bpq) + odd) * bpq + i % bpq)
    return _matmul(
        name, a, b, dims="tn", grid=(rows // tm, N_CHIPS * npb, k // tk),
        a_spec=pl.BlockSpec((tk, tm), lambda i, j, l: (l, pick(i))),
        b_spec=pl.BlockSpec((tk, tn), lambda i, j, l: (l, j)),
        o_spec=pl.BlockSpec((None, tm, tn), lambda i, j, l: (j // npb, i, j % npb)),
        out_shape=(N_CHIPS, rows, ws), out_dtype=F32, phases=phases)


def _grp_fwd(name, pooled, wgg):
    t, e = pooled.shape
    _, ng, rs, gw = wgg.shape
    tm, tn, tk = _tiles(t, gw, rs, gw)
    npb, kps = gw // tn, rs // tk
    return _matmul(
        name, pooled, wgg, dims="nn", grid=(t // tm, ng * npb, N_CHIPS * kps),
        a_spec=pl.BlockSpec((tm, tk), lambda i, j, l: (i, (j // npb) * (gw // tk) + l)),
        b_spec=pl.BlockSpec((None, None, tk, tn), lambda i, j, l: (l // kps, j // npb, l % kps, j % npb)),
        o_spec=pl.BlockSpec((tm, tn), lambda i, j, l: (i, j)),
        out_shape=(t, e), out_dtype=F32)


def _grp_bwd_x(name, dmm, wgg, phases=()):
    t, e = dmm.shape
    _, ng, rs, gw = wgg.shape
    tm, tn, tk = _tiles(t, rs, gw)
    npr, kpg = rs // tn, gw // tk
    return _matmul(
        name, dmm, wgg, dims="nt", grid=(t // tm, ng * N_CHIPS * npr, kpg),
        a_spec=pl.BlockSpec((tm, tk), lambda i, j, l: (i, (j // (N_CHIPS * npr)) * kpg + l)),
        b_spec=pl.BlockSpec(
            (None, None, tn, tk),
            lambda i, j, l: ((j % (N_CHIPS * npr)) // npr, j // (N_CHIPS * npr), j % npr, l)),
        o_spec=pl.BlockSpec((tm, tn), lambda i, j, l: (i, j)),
        out_shape=(t, e), out_dtype=F32, phases=phases)


def _grp_bwd_w(name, pooled, dmm, ng):
    t, e = pooled.shape
    gw = e // ng
    rs = gw // N_CHIPS
    _, tn, tk = _tiles(rs, gw, t)
    npb = gw // tn
    return _matmul(
        name, pooled, dmm, dims="tn", grid=(ng * N_CHIPS, npb, t // tk),
        a_spec=pl.BlockSpec((tk, rs), lambda i, j, l: (l, i)),
        b_spec=pl.BlockSpec((tk, tn), lambda i, j, l: (l, (i // N_CHIPS) * npb + j)),
        o_spec=pl.BlockSpec((None, None, rs, tn), lambda i, j, l: (i % N_CHIPS, i // N_CHIPS, 0, j)),
        out_shape=(N_CHIPS, ng, rs, gw), out_dtype=F32)


def _cast_into_slot(name, w, place):
    r, c = w.shape
    tr, tc = _tile(r, 512), _tile(c, 2048)

    def body(place_ref, w_ref, o_ref):
        o_ref[...] = w_ref[...].astype(BF16)

    return pl.pallas_call(
        body, name=name,
        grid_spec=pltpu.PrefetchScalarGridSpec(
            num_scalar_prefetch=1, grid=(r // tr, c // tc),
            in_specs=[pl.BlockSpec((tr, tc), lambda i, j, p: (i, j))],
            out_specs=pl.BlockSpec((None, tr, tc), lambda i, j, p: (p[0], i, j))),
        out_shape=jax.ShapeDtypeStruct((N_CHIPS, r, c), BF16),
        compiler_params=_params("parallel", "parallel"))(place, w)


def _rms(v):
    return lax.rsqrt(jnp.mean(v * v, axis=-1, keepdims=True) + NORM_EPS)


def _rms_fwd(name, x, g):
    t, d = x.shape
    tr = _tile(t, 256)

    def body(x_ref, g_ref, h_ref):
        xv = x_ref[...]
        h_ref[...] = (xv * _rms(xv) * g_ref[...]).astype(BF16)

    row = pl.BlockSpec((tr, d), lambda i: (i, 0))
    vec = pl.BlockSpec((1, d), lambda i: (0, 0))
    return pl.pallas_call(
        body, name=name, grid=(t // tr,), in_specs=[row, vec], out_specs=row,
        out_shape=jax.ShapeDtypeStruct((t, d), BF16), compiler_params=_params("parallel"))(x, g)


def _post_pre_fwd(name, x, m, g_post, g_pre):
    t, d = x.shape
    tr = _tile(t, 256)

    def body(x_ref, m_ref, gp_ref, gn_ref, x1_ref, h_ref):
        mv = m_ref[...]
        x1 = x_ref[...] + mv * _rms(mv) * gp_ref[...]
        x1_ref[...] = x1
        h_ref[...] = (x1 * _rms(x1) * gn_ref[...]).astype(BF16)

    row = pl.BlockSpec((tr, d), lambda i: (i, 0))
    vec = pl.BlockSpec((1, d), lambda i: (0, 0))
    return pl.pallas_call(
        body, name=name, grid=(t // tr,), in_specs=[row, row, vec, vec], out_specs=[row, row],
        out_shape=[jax.ShapeDtypeStruct((t, d), F32), jax.ShapeDtypeStruct((t, d), BF16)],
        compiler_params=_params("parallel"))(x, m, g_post, g_pre)


def _norm_bwd(dout, nrm, r, g):
    gd = dout * g
    return r * (gd - nrm * jnp.mean(gd * nrm, axis=-1, keepdims=True))


def _loss_post_bwd(name, x1, m, g_post, target):
    t, d = x1.shape
    tr = _tile(t, 128)

    def body(x_ref, m_ref, g_ref, t_ref, loss_ref, dx_ref, dm_ref, dg_ref):
        i = pl.program_id(0)
        mv = m_ref[...]
        r = _rms(mv)
        nrm = mv * r
        err = x_ref[...] + nrm * g_ref[...] - t_ref[...]
        part = 0.5 * jnp.sum(jnp.mean(err * err, axis=-1, keepdims=True), axis=0, keepdims=True)
        dx = err / d
        dx_ref[...] = dx
        dm_ref[...] = _norm_bwd(dx, nrm, r, g_ref[...]).astype(BF16)
        dg = jnp.sum(dx * nrm, axis=0, keepdims=True)

        @pl.when(i == 0)
        def _():
            loss_ref[...] = part
            dg_ref[...] = dg

        @pl.when(i > 0)
        def _():
            loss_ref[...] += part
            dg_ref[...] += dg

    row = pl.BlockSpec((tr, d), lambda i: (i, 0))
    vec = pl.BlockSpec((1, d), lambda i: (0, 0))
    one = pl.BlockSpec((1, 1), lambda i: (0, 0))
    return pl.pallas_call(
        body, name=name, grid=(t // tr,), in_specs=[row, row, vec, row], out_specs=[one, row, row, vec],
        out_shape=[jax.ShapeDtypeStruct((1, 1), F32), jax.ShapeDtypeStruct((t, d), F32),
                   jax.ShapeDtypeStruct((t, d), BF16), jax.ShapeDtypeStruct((1, d), F32)],
        compiler_params=_params("arbitrary"))(x1, m, g_post, target)


def _mid_bwd(name, dx2, dh1, x1, g_pre, m0, g_post):
    t, d = x1.shape
    tr = _tile(t, 128)

    def body(dx2_ref, dh_ref, x_ref, gn_ref, m_ref, gp_ref, dx_ref, dm_ref, dgn_ref, dgp_ref):
        i = pl.program_id(0)
        xv = x_ref[...]
        r1 = _rms(xv)
        n1 = xv * r1
        dh = dh_ref[...]
        dx = dx2_ref[...] + _norm_bwd(dh, n1, r1, gn_ref[...])
        dx_ref[...] = dx
        mv = m_ref[...]
        r0 = _rms(mv)
        n0 = mv * r0
        dm_ref[...] = _norm_bwd(dx, n0, r0, gp_ref[...]).astype(BF16)
        dgn = jnp.sum(dh * n1, axis=0, keepdims=True)
        dgp = jnp.sum(dx * n0, axis=0, keepdims=True)

        @pl.when(i == 0)
        def _():
            dgn_ref[...] = dgn
            dgp_ref[...] = dgp

        @pl.when(i > 0)
        def _():
            dgn_ref[...] += dgn
            dgp_ref[...] += dgp

    row = pl.BlockSpec((tr, d), lambda i: (i, 0))
    vec = pl.BlockSpec((1, d), lambda i: (0, 0))
    return pl.pallas_call(
        body, name=name, grid=(t // tr,), in_specs=[row, row, row, vec, row, vec],
        out_specs=[row, row, vec, vec],
        out_shape=[jax.ShapeDtypeStruct((t, d), F32), jax.ShapeDtypeStruct((t, d), BF16),
                   jax.ShapeDtypeStruct((1, d), F32), jax.ShapeDtypeStruct((1, d), F32)],
        compiler_params=_params("arbitrary"))(dx2, dh1, x1, g_pre, m0, g_post)


def _pre_bwd(name, dx1, dh0, x, g_pre):
    t, d = x.shape
    tr = _tile(t, 128)

    def body(dx1_ref, dh_ref, x_ref, g_ref, dx_ref, dg_ref):
        i = pl.program_id(0)
        xv = x_ref[...]
        r = _rms(xv)
        nrm = xv * r
        dh = dh_ref[...]
        dx_ref[...] = dx1_ref[...] + _norm_bwd(dh, nrm, r, g_ref[...])
        dg = jnp.sum(dh * nrm, axis=0, keepdims=True)

        @pl.when(i == 0)
        def _():
            dg_ref[...] = dg

        @pl.when(i > 0)
        def _():
            dg_ref[...] += dg

    row = pl.BlockSpec((tr, d), lambda i: (i, 0))
    vec = pl.BlockSpec((1, d), lambda i: (0, 0))
    return pl.pallas_call(
        body, name=name, grid=(t // tr,), in_specs=[row, row, row, vec], out_specs=[row, vec],
        out_shape=[jax.ShapeDtypeStruct((t, d), F32), jax.ShapeDtypeStruct((1, d), F32)],
        compiler_params=_params("arbitrary"))(dx1, dh0, x, g_pre)


def _a_stats(name, pa, e):
    t = pa.shape[0]
    tr = _tile(t, 64, 8)

    def body(v_ref, mu_ref, rs_ref):
        vg, _ = _gelu(v_ref[...])
        mu = jnp.mean(vg, axis=-1, keepdims=True)
        xc = vg - mu
        mu_ref[...] = mu
        rs_ref[...] = lax.rsqrt(jnp.mean(xc * xc, axis=-1, keepdims=True) + NORM_EPS)

    col = pl.BlockSpec((tr, 1), lambda i: (i, 0))
    return pl.pallas_call(
        body, name=name, grid=(t // tr,), in_specs=[pl.BlockSpec((tr, e), lambda i: (i, 1))],
        out_specs=[col, col],
        out_shape=[jax.ShapeDtypeStruct((t, 1), F32), jax.ShapeDtypeStruct((t, 1), F32)],
        compiler_params=_params("parallel"))(pa)


def _causal(w):
    c = w.shape[0]
    keep = lax.broadcasted_iota(jnp.int32, (c, c), 0) >= lax.broadcasted_iota(jnp.int32, (c, c), 1)
    return jnp.where(keep, w, 0.0), keep


def _a_gate_fwd(name, pa, mu, rs, ln_g, ln_b, w_s, b_s3):
    t = pa.shape[0]
    nh, c, _ = w_s.shape
    e = ln_g.shape[1]
    dh = e // nh
    rb = 2 * c if t % (2 * c) == 0 else c

    def body(u_ref, v_ref, z_ref, mu_ref, rs_ref, g_ref, b_ref, w_ref, bs_ref, y_ref):
        wc = _causal(w_ref[...])[0].astype(BF16)
        vg, _ = _gelu(v_ref[...])
        vn = ((vg - mu_ref[...]) * rs_ref[...] * g_ref[...] + b_ref[...]).astype(BF16)
        for ci in range(rb // c):
            rows = pl.ds(ci * c, c)
            sv = jnp.dot(wc, vn[ci * c:(ci + 1) * c], preferred_element_type=F32) + bs_ref[...]
            u, _ = _gelu(u_ref[rows, :])
            sz, _ = _silu(z_ref[rows, :])
            y_ref[rows, :] = (u * sv * sz).astype(BF16)

    blk = lambda off: pl.BlockSpec((rb, dh), lambda i, h: (i, off + h))
    col = pl.BlockSpec((rb, 1), lambda i, h: (i, 0))
    vec = pl.BlockSpec((1, dh), lambda i, h: (0, h))
    return pl.pallas_call(
        body, name=name, grid=(t // rb, nh),
        in_specs=[blk(0), blk(nh), blk(2 * nh), col, col, vec, vec,
                  pl.BlockSpec((None, c, c), lambda i, h: (h, 0, 0)),
                  pl.BlockSpec((None, c, 1), lambda i, h: (h, 0, 0))],
        out_specs=pl.BlockSpec((rb, dh), lambda i, h: (i, h)),
        out_shape=jax.ShapeDtypeStruct((t, e), BF16),
        compiler_params=_params("parallel", "parallel"))(pa, pa, pa, mu, rs, ln_g, ln_b, w_s, b_s3)


def _a_gate_bwd1(name, pa, dy, mu, rs, ln_g, ln_b, w_s, b_s3, phases=()):
    t = pa.shape[0]
    nh, c, _ = w_s.shape
    e = ln_g.shape[1]
    dh = e // nh
    rb = 2 * c if t % (2 * c) == 0 else c

    def body(u_ref, v_ref, z_ref, dy_ref, mu_ref, rs_ref, g_ref, b_ref, w_ref, bs_ref,
             du_ref, dz_ref, dsv_ref, c1_ref, c2_ref):
        h = pl.program_id(1)
        wc = _causal(w_ref[...])[0].astype(BF16)
        vg, _ = _gelu(v_ref[...])
        xh = (vg - mu_ref[...]) * rs_ref[...]
        vn = (xh * g_ref[...] + b_ref[...]).astype(BF16)
        s1 = []
        s2 = []
        for ci in range(rb // c):
            rows = pl.ds(ci * c, c)
            lo, hi = ci * c, (ci + 1) * c
            sv = jnp.dot(wc, vn[lo:hi], preferred_element_type=F32) + bs_ref[...]
            u, du = _gelu(u_ref[rows, :])
            zv = z_ref[rows, :]
            sz, dsz = _silu(zv)
            dyv = dy_ref[rows, :]
            du_ref[rows, :] = (dyv * sv * sz * du).astype(BF16)
            dz_ref[rows, :] = (dyv * u * sv * dsz).astype(BF16)
            dsv = (dyv * u * sz).astype(BF16)
            dsv_ref[rows, :] = dsv
            dvn = lax.dot_general(wc, dsv, (((0,), (0,)), ((), ())), preferred_element_type=F32)
            dxh = dvn * g_ref[...]
            s1.append(jnp.sum(dxh, axis=-1, keepdims=True))
            s2.append(jnp.sum(dxh * xh[lo:hi], axis=-1, keepdims=True))
        p1 = jnp.concatenate(s1, axis=0)
        p2 = jnp.concatenate(s2, axis=0)

        @pl.when(h == 0)
        def _():
            c1_ref[...] = p1
            c2_ref[...] = p2

        @pl.when(h > 0)
        def _():
            c1_ref[...] += p1
            c2_ref[...] += p2

    blk = lambda off: pl.BlockSpec((rb, dh), lambda i, h: (i, off + h))
    col = pl.BlockSpec((rb, 1), lambda i, h: (i, 0))
    vec = pl.BlockSpec((1, dh), lambda i, h: (0, h))
    act = jax.ShapeDtypeStruct((t, e), BF16)
    stat = jax.ShapeDtypeStruct((t, 1), F32)
    grid = (t // rb, nh)
    body, x_ins, x_outs, aliases, sems = _hosted(body, phases, grid, 10, 5)
    res = pl.pallas_call(
        body, name=name, grid=grid,
        in_specs=[blk(0), blk(nh), blk(2 * nh), blk(0), col, col, vec, vec,
                  pl.BlockSpec((None, c, c), lambda i, h: (h, 0, 0)),
                  pl.BlockSpec((None, c, 1), lambda i, h: (h, 0, 0))] + [ANY] * len(x_ins),
        out_specs=[blk(0), blk(0), blk(0), col, col] + [ANY] * len(x_outs),
        out_shape=[act, act, act, stat, stat] + x_outs, input_output_aliases=aliases, scratch_shapes=sems,
        compiler_params=_params("arbitrary", "arbitrary"))(pa, pa, pa, dy, mu, rs, ln_g, ln_b, w_s, b_s3, *x_ins)
    return (res[:5], list(res[5:])) if phases else res


def _a_gate_bwd2(name, pa, dsv, mu, rs, c1, c2, ln_g, ln_b, w_s, phases=()):
    t = pa.shape[0]
    nh, c, _ = w_s.shape
    e = ln_g.shape[1]
    dh = e // nh
    rb = 2 * c if t % (2 * c) == 0 else c

    def body(v_ref, dsv_ref, mu_ref, rs_ref, c1_ref, c2_ref, g_ref, b_ref, w_ref,
             dv_ref, dw_ref, dbs_ref, dg_ref, db_ref):
        i = pl.program_id(1)
        wcf, keep = _causal(w_ref[...])
        wc = wcf.astype(BF16)
        vg, dvg = _gelu(v_ref[...])
        rsv = rs_ref[...]
        xh = (vg - mu_ref[...]) * rsv
        vn = (xh * g_ref[...] + b_ref[...]).astype(BF16)
        dw = jnp.zeros((c, c), F32)
        dbs = jnp.zeros((c, 1), F32)
        dvns = []
        for ci in range(rb // c):
            lo, hi = ci * c, (ci + 1) * c
            dsv = dsv_ref[pl.ds(lo, c), :]
            dvns.append(lax.dot_general(wc, dsv, (((0,), (0,)), ((), ())), preferred_element_type=F32))
            dw += lax.dot_general(dsv, vn[lo:hi], (((1,), (1,)), ((), ())), preferred_element_type=F32)
            dbs += jnp.sum(dsv.astype(F32), axis=-1, keepdims=True)
        dvn = jnp.concatenate(dvns, axis=0)
        dxh = dvn * g_ref[...]
        dvv = rsv * (dxh - c1_ref[...] * (1.0 / e) - xh * (c2_ref[...] * (1.0 / e)))
        dv_ref[...] = (dvv * dvg).astype(BF16)
        dw = jnp.where(keep, dw, 0.0)
        dg = jnp.sum(dvn * xh, axis=0, keepdims=True)
        db = jnp.sum(dvn, axis=0, keepdims=True)

        @pl.when(i == 0)
        def _():
            dw_ref[...] = dw
            dbs_ref[...] = dbs
            dg_ref[...] = dg
            db_ref[...] = db

        @pl.when(i > 0)
        def _():
            dw_ref[...] += dw
            dbs_ref[...] += dbs
            dg_ref[...] += dg
            db_ref[...] += db

    col = pl.BlockSpec((rb, 1), lambda h, i: (i, 0))
    vec = pl.BlockSpec((1, dh), lambda h, i: (0, h))
    hblk = pl.BlockSpec((rb, dh), lambda h, i: (i, h))
    grid = (nh, t // rb)
    body, x_ins, x_outs, aliases, sems = _hosted(body, phases, grid, 9, 5)
    res = pl.pallas_call(
        body, name=name, grid=grid,
        in_specs=[pl.BlockSpec((rb, dh), lambda h, i: (i, nh + h)), hblk, col, col, col, col, vec, vec,
                  pl.BlockSpec((None, c, c), lambda h, i: (h, 0, 0))] + [ANY] * len(x_ins),
        out_specs=[hblk, pl.BlockSpec((None, c, c), lambda h, i: (h, 0, 0)),
                   pl.BlockSpec((None, c, 1), lambda h, i: (h, 0, 0)), vec, vec] + [ANY] * len(x_outs),
        out_shape=[jax.ShapeDtypeStruct((t, e), BF16), jax.ShapeDtypeStruct((nh, c, c), F32),
                   jax.ShapeDtypeStruct((nh, c, 1), F32), jax.ShapeDtypeStruct((1, e), F32),
                   jax.ShapeDtypeStruct((1, e), F32)] + x_outs,
        input_output_aliases=aliases, scratch_shapes=sems,
        compiler_params=_params("arbitrary", "arbitrary"))(pa, dsv, mu, rs, c1, c2, ln_g, ln_b, w_s, *x_ins)
    return (res[:5], list(res[5:])) if phases else res


def _pool(name, src, e, seq, backward):
    t = src.shape[0]
    ng = len(POOL_WINDOWS)
    gw = e // ng
    cw = _tile(gw, 256)

    def shifted(a, j, pos):
        if backward:
            return jnp.where(pos < seq - j, pltpu.roll(a, seq - j, 0), 0.0)
        return jnp.where(pos >= j, pltpu.roll(a, j, 0), 0.0)

    def body(p_ref, o_ref):
        grp = (pl.program_id(1) * cw) // gw
        pos = lax.broadcasted_iota(jnp.int32, (seq, cw), 0)
        posf = (pos + 1).astype(F32)
        for k, w in enumerate(POOL_WINDOWS):

            @pl.when(grp == k)
            def _(w=w):
                pv = p_ref[...]
                cnt = jnp.minimum(posf, float(w))
                acc = pv / cnt if backward else pv
                j = 1
                while j < w:
                    acc = acc + shifted(acc, j, pos)
                    j *= 2
                out = acc - pv if backward else acc / cnt - pv
                o_ref[...] = out.astype(BF16)

    spec = pl.BlockSpec((seq, cw), lambda s, j: (s, j))
    return pl.pallas_call(
        body, name=name, grid=(t // seq, e // cw), in_specs=[spec], out_specs=spec,
        out_shape=jax.ShapeDtypeStruct((t, e), BF16),
        compiler_params=_params("parallel", "parallel"))(src)


def _b_gate_fwd(name, mm, pb, b_grp, scale):
    t, e = mm.shape
    tr, tc = _tile(t, 512), _tile(e, 1024)
    nc = e // tc

    def body(mm_ref, z_ref, b_ref, s_ref, y_ref):
        sz, _ = _silu(z_ref[...])
        y_ref[...] = ((mm_ref[...] + b_ref[...]) * s_ref[...] * sz).astype(BF16)

    blk = pl.BlockSpec((tr, tc), lambda i, j: (i, j))
    vec = pl.BlockSpec((1, tc), lambda i, j: (0, j))
    return pl.pallas_call(
        body, name=name, grid=(t // tr, nc),
        in_specs=[blk, pl.BlockSpec((tr, tc), lambda i, j: (i, nc + j)), vec, vec], out_specs=blk,
        out_shape=jax.ShapeDtypeStruct((t, e), BF16),
        compiler_params=_params("parallel", "parallel"))(mm, pb, b_grp, scale)


def _b_gate_bwd(name, dy, mm, pb, b_grp, scale):
    t, e = mm.shape
    tr, tc = _tile(t, 512), _tile(e, 1024)
    nc = e // tc

    def body(dy_ref, mm_ref, z_ref, b_ref, s_ref, dmm_ref, dz_ref, db_ref, ds_ref):
        i = pl.program_id(1)
        sz, dsz = _silu(z_ref[...])
        dyv = dy_ref[...]
        mb = mm_ref[...] + b_ref[...]
        dmixed = dyv * sz
        dmm = dmixed * s_ref[...]
        dmm_ref[...] = dmm.astype(BF16)
        dz_ref[...] = (dyv * (mb * s_ref[...]) * dsz).astype(BF16)
        db = jnp.sum(dmm, axis=0, keepdims=True)
        ds = jnp.sum(dmixed * mb, axis=0, keepdims=True)

        @pl.when(i == 0)
        def _():
            db_ref[...] = db
            ds_ref[...] = ds

        @pl.when(i > 0)
        def _():
            db_ref[...] += db
            ds_ref[...] += ds

    blk = pl.BlockSpec((tr, tc), lambda j, i: (i, j))
    vec = pl.BlockSpec((1, tc), lambda j, i: (0, j))
    act = jax.ShapeDtypeStruct((t, e), BF16)
    stat = jax.ShapeDtypeStruct((1, e), F32)
    return pl.pallas_call(
        body, name=name, grid=(nc, t // tr),
        in_specs=[blk, blk, pl.BlockSpec((tr, tc), lambda j, i: (i, nc + j)), vec, vec],
        out_specs=[blk, blk, vec, vec], out_shape=[act, act, stat, stat],
        compiler_params=_params("parallel", "arbitrary"))(dy, mm, pb, b_grp, scale)


def _position():
    return lax.axis_index("x"), lax.axis_index("y"), lax.axis_index("c")


def _other_chips(x, y):
    return [(1 - x, y), (x, 1 - y), (1 - x, 1 - y)]


def _half(ref, c):
    n = ref.shape[0] // 2
    return ref.at[pl.ds(c * n, n)]


def _part(ref, c, part):
    q, nq = part
    n = ref.shape[0] // (2 * nq)
    return ref.at[pl.ds(c * nq * n + q * n, n)]


def _gather_phase(slots, small=None, part=(0, 1)):
    nw = len(slots)
    ns = 0 if small is None else 1
    n_ici = 3 * (nw + ns)
    n_sem = n_ici + 3 * nw + ns

    def copies(ins, outs, send_sems, recv_sems):
        x, y, c = _position()
        sibling = (x, y, 1 - c)

        def ici(j, k, slot, to):
            return pltpu.make_async_remote_copy(
                src_ref=_part(ins[k].at[slot], c, part), dst_ref=_part(outs[k].at[slot], c, part),
                send_sem=send_sems.at[j * nw + k], recv_sem=recv_sems.at[j * nw + k],
                device_id=to, device_id_type=MESH)

        def ici_small(j, slot, to):
            return pltpu.make_async_remote_copy(
                src_ref=ins[nw], dst_ref=outs[nw].at[slot], send_sem=send_sems.at[3 * nw + j],
                recv_sem=recv_sems.at[3 * nw + j], device_id=to, device_id_type=MESH)

        def d2d(j, k, slot, half):
            return pltpu.make_async_remote_copy(
                src_ref=_part(outs[k].at[slot], half, part), dst_ref=_part(outs[k].at[slot], half, part),
                send_sem=send_sems.at[n_ici + j * nw + k], recv_sem=recv_sems.at[n_ici + j * nw + k],
                device_id=sibling, device_id_type=MESH)

        def local():
            return pltpu.make_async_copy(ins[nw], outs[nw].at[2 * x + y], send_sems.at[n_sem - 1])

        return x, y, c, ici, ici_small, d2d, local

    def start(ins, outs, send_sems, recv_sems):
        x, y, c, ici, ici_small, _, local = copies(ins, outs, send_sems, recv_sems)
        if ns:
            local().start()
        for j, (cx, cy) in enumerate(_other_chips(x, y)):
            for k in range(nw):
                ici(j, k, 2 * x + y, (cx, cy, c)).start()
            if ns:
                ici_small(j, 2 * x + y, (cx, cy, c)).start()

    def relay(ins, outs, send_sems, recv_sems):
        x, y, c, ici, _, d2d, _ = copies(ins, outs, send_sems, recv_sems)
        for j, (cx, cy) in enumerate(_other_chips(x, y)):
            for k in range(nw):
                ici(j, k, 2 * cx + cy, (x, y, c)).wait_recv()
                d2d(j, k, 2 * cx + cy, c).start()

    def finish(ins, outs, send_sems, recv_sems):
        x, y, c, ici, ici_small, d2d, local = copies(ins, outs, send_sems, recv_sems)
        chips = _other_chips(x, y)
        for j, (cx, cy) in enumerate(chips):
            if ns:
                ici_small(j, 2 * cx + cy, (x, y, c)).wait_recv()
            for k in range(nw):
                d2d(j, k, 2 * cx + cy, 1 - c).wait_recv()
        for j, (cx, cy) in enumerate(chips):
            for k in range(nw):
                ici(j, k, 2 * x + y, (cx, cy, c)).wait_send()
                d2d(j, k, 2 * cx + cy, c).wait_send()
            if ns:
                ici_small(j, 2 * x + y, (cx, cy, c)).wait_send()
        if ns:
            local().wait()

    outs = [jax.ShapeDtypeStruct(s.shape, s.dtype) for s in slots]
    if ns:
        outs.append(jax.ShapeDtypeStruct((N_CHIPS,) + small.shape, small.dtype))
    return _Phase(list(slots) + ([small] if ns else []), outs, {k: k for k in range(nw)}, n_sem, start, finish, relay)


def _swap_phase(grads):
    nw = len(grads)

    def swaps(ins, outs, send_sems, recv_sems):
        x, y, c = _position()
        cps = []
        for k in range(nw):
            n = ins[k].shape[1] // 2
            cps.append(pltpu.make_async_remote_copy(
                src_ref=ins[k].at[pl.ds(0, N_CHIPS), pl.ds((1 - c) * n, n)], dst_ref=outs[k],
                send_sem=send_sems.at[k], recv_sem=recv_sems.at[k],
                device_id=(x, y, 1 - c), device_id_type=MESH))
        return cps

    def start(*refs):
        for cp in swaps(*refs):
            cp.start()

    def finish(*refs):
        for cp in swaps(*refs):
            cp.wait()

    halves = [jax.ShapeDtypeStruct((g.shape[0], g.shape[1] // 2, g.shape[2]), F32) for g in grads]
    return _Phase(grads, halves, {}, nw, start, finish)


def _add_halves(name, grad, theirs, place):
    s, half, w = theirs.shape
    tr, tc = _tile(half, 256), _tile(w, 2048)
    nrt = half // tr

    def body(place_ref, a_ref, b_ref, h_ref, f_ref):
        v = a_ref[...] + b_ref[...]
        h_ref[...] = v.astype(BF16)

        @pl.when(pl.program_id(2) == place_ref[0])
        def _():
            f_ref[...] = v

    return pl.pallas_call(
        body, name=name,
        grid_spec=pltpu.PrefetchScalarGridSpec(
            num_scalar_prefetch=1, grid=(nrt, w // tc, s),
            in_specs=[pl.BlockSpec((None, tr, tc), lambda i, j, q, p: (q, p[1] * nrt + i, j)),
                      pl.BlockSpec((None, tr, tc), lambda i, j, q, p: (q, i, j))],
            out_specs=[pl.BlockSpec((None, tr, tc), lambda i, j, q, p: (q, i, j)),
                       pl.BlockSpec((tr, tc), lambda i, j, q, p: (p[1] * nrt + i, j))]),
        out_shape=[jax.ShapeDtypeStruct(theirs.shape, BF16), jax.ShapeDtypeStruct((2 * half, w), F32)],
        compiler_params=_params("parallel", "parallel", "arbitrary"))(place, grad, theirs)


def _scatter_phase(own, pb, part=(0, 1), got=None):
    nw = len(own)

    def copies(ins, outs, send_sems, recv_sems):
        own_in, pbs = ins[:nw], ins[nw:2 * nw]
        own_out, gots = outs[:nw], outs[nw:]
        x, y, c = _position()
        sibling = (x, y, 1 - c)

        def d2d_own(k, half):
            return pltpu.make_async_remote_copy(
                src_ref=_part(own_in[k], half, part), dst_ref=_part(own_out[k], half, part),
                send_sem=send_sems.at[k], recv_sem=recv_sems.at[k], device_id=sibling, device_id_type=MESH)

        def ici(j, k, shard, to):
            n = pbs[k].shape[1] // part[1]
            return pltpu.make_async_remote_copy(
                src_ref=pbs[k].at[shard, pl.ds(part[0] * n, n)], dst_ref=_part(gots[k].at[j], c, part),
                send_sem=send_sems.at[nw + j * nw + k], recv_sem=recv_sems.at[nw + j * nw + k],
                device_id=to, device_id_type=MESH)

        def d2d(j, k, half):
            return pltpu.make_async_remote_copy(
                src_ref=_part(gots[k].at[j], half, part), dst_ref=_part(gots[k].at[j], half, part),
                send_sem=send_sems.at[4 * nw + j * nw + k], recv_sem=recv_sems.at[4 * nw + j * nw + k],
                device_id=sibling, device_id_type=MESH)

        return x, y, c, d2d_own, ici, d2d

    def start(*refs):
        x, y, c, d2d_own, ici, _ = copies(*refs)
        for k in range(nw):
            d2d_own(k, c).start()
        for j, (cx, cy) in enumerate(_other_chips(x, y)):
            for k in range(nw):
                ici(j, k, 2 * cx + cy, (cx, cy, c)).start()

    def relay(*refs):
        x, y, c, _, ici, d2d = copies(*refs)
        for j in range(3):
            for k in range(nw):
                ici(j, k, 2 * x + y, (x, y, c)).wait_recv()
                d2d(j, k, c).start()

    def finish(*refs):
        x, y, c, d2d_own, ici, d2d = copies(*refs)
        for k in range(nw):
            d2d_own(k, 1 - c).wait_recv()
        for j in range(3):
            for k in range(nw):
                d2d(j, k, 1 - c).wait_recv()
        for k in range(nw):
            d2d_own(k, c).wait_send()
        for j, (cx, cy) in enumerate(_other_chips(x, y)):
            for k in range(nw):
                ici(j, k, 2 * cx + cy, (cx, cy, c)).wait_send()
                d2d(j, k, c).wait_send()

    own_shape = [jax.ShapeDtypeStruct(o.shape, F32) for o in own]
    got_shape = [jax.ShapeDtypeStruct((3,) + o.shape, BF16) for o in own]
    aliases = {k: k for k in range(nw)}
    if got is not None:
        aliases.update({2 * nw + k: nw + k for k in range(nw)})
    return _Phase(list(own) + list(pb) + list(got or []), own_shape + got_shape, aliases, 7 * nw, start, finish, relay)


def _gather_small(part):
    def body(p_ref, o_ref, send_sems, recv_sems, loc_sem):
        x, y, c = _position()
        me = 4 * x + 2 * y + c
        mine = pltpu.make_async_copy(p_ref, o_ref.at[me], loc_sem)
        mine.start()
        sent = []
        for mask in range(1, N_DEV):
            dx, dy, dc = (mask >> 2) & 1, (mask >> 1) & 1, mask & 1
            px, py, pc = x ^ dx, y ^ dy, c ^ dc
            cp = pltpu.make_async_remote_copy(
                src_ref=p_ref, dst_ref=o_ref.at[me], send_sem=send_sems.at[mask - 1],
                recv_sem=recv_sems.at[mask - 1], device_id=(px, py, pc), device_id_type=MESH)
            cp.start()
            sent.append((cp, 4 * px + 2 * py + pc))
        for mask in range(1, N_DEV):
            cp, peer = sent[mask - 1]
            pltpu.make_async_remote_copy(
                src_ref=p_ref, dst_ref=o_ref.at[peer], send_sem=send_sems.at[mask - 1],
                recv_sem=recv_sems.at[mask - 1], device_id=(x, y, c), device_id_type=MESH).wait_recv()
        for cp, _ in sent:
            cp.wait_send()
        mine.wait()

    return pl.pallas_call(
        body, name="gather_small_grads", in_specs=[ANY], out_specs=ANY,
        out_shape=jax.ShapeDtypeStruct((N_DEV,) + part.shape, F32),
        scratch_shapes=[pltpu.SemaphoreType.DMA((N_DEV - 1,)), pltpu.SemaphoreType.DMA((N_DEV - 1,)),
                        pltpu.SemaphoreType.DMA(())],
    )(part)


def _sum_slots(name, parts):
    s, r, c = parts.shape
    tr = _tile(r, 512, 8)

    def body(p_ref, o_ref):
        acc = p_ref[0]
        for q in range(1, s):
            acc = acc + p_ref[q]
        o_ref[...] = acc

    return pl.pallas_call(
        body, name=name, grid=(r // tr,), in_specs=[pl.BlockSpec((s, tr, c), lambda i: (0, i, 0))],
        out_specs=pl.BlockSpec((tr, c), lambda i: (i, 0)), out_shape=jax.ShapeDtypeStruct((r, c), F32),
        compiler_params=_params("parallel"))(parts)


def _adamw_math(w, g, m, v):
    m = ADAM_B1 * m + (1.0 - ADAM_B1) * g
    v = ADAM_B2 * v + (1.0 - ADAM_B2) * (g * g)
    m_hat = m / (1.0 - ADAM_B1 ** ADAM_STEP)
    v_hat = v / (1.0 - ADAM_B2 ** ADAM_STEP)
    delta = -ADAM_LR * (m_hat / (jnp.sqrt(v_hat) + ADAM_EPS) + ADAM_WD * w)
    return delta, m, v


def _adamw(name, w, m, v, own, got=None, odd=None, into=None):
    r, c = w.shape
    rows = r if odd is None else r // 2
    tr, tc = _tile(r // 4, 256, 8) if odd is not None else _tile(r, 256, 8), _tile(c, 1024)
    bpq = (r // 4) // tr if odd is not None else 1
    pick = (lambda i: i) if odd is None else (lambda i: (2 * (i // bpq) + odd) * bpq + i % bpq)
    n_in = 4 + (0 if got is None else 1)

    def body(*refs):
        w_ref, m_ref, v_ref, own_ref = refs[:4]
        outs = refs[n_in + (0 if into is None else 4):]
        g = own_ref[...]
        if got is not None:
            for j in range(3):
                g = g + refs[4][j].astype(F32)
        delta, mn, vn = _adamw_math(w_ref[...], g, m_ref[...], v_ref[...])
        outs[0][...] = g
        outs[1][...] = delta
        outs[2][...] = mn
        outs[3][...] = vn

    full = pl.BlockSpec((tr, tc), lambda i, j: (pick(i), j))
    ins, args = [full] * 3 + [pl.BlockSpec((tr, tc), lambda i, j: (i, j))], [w, m, v, own]
    if got is not None:
        ins.append(pl.BlockSpec((3, tr, tc), lambda i, j: (0, i, j)))
        args.append(got)
    aliases = {}
    if into is not None:
        ins += [ANY] * 4
        args += list(into)
        aliases = {n_in + q: q for q in range(4)}
    return pl.pallas_call(
        body, name=name, grid=(rows // tr, c // tc), in_specs=ins, out_specs=[full] * 4,
        out_shape=[jax.ShapeDtypeStruct((r, c), F32)] * 4, input_output_aliases=aliases,
        compiler_params=_params("parallel", "parallel"))(*args)


def _pack(parts):
    return jnp.concatenate([p.reshape(-1) for p in parts]).reshape(-1, LANES)


def _unpack(packed, shapes):
    flat = packed.reshape(-1)
    out, off = [], 0
    for s in shapes:
        n = 1
        for d in s:
            n *= d
        out.append(flat[off:off + n].reshape(s))
        off += n
    return out


def kernel(x, pre_norm, post_norm, a_w_in, a_ln_g, a_ln_b, a_w_s, a_b_s, a_w_out, b_w_in, b_w_grp, b_b_grp, b_scale, b_w_out, loss_target, m_pre_norm, m_post_norm, m_a_w_in, m_a_ln_g, m_a_ln_b, m_a_w_s, m_a_b_s, m_a_w_out, m_b_w_in, m_b_w_grp, m_b_b_grp, m_b_scale, m_b_w_out, v_pre_norm, v_post_norm, v_a_w_in, v_a_ln_g, v_a_ln_b, v_a_w_s, v_a_b_s, v_a_w_out, v_b_w_in, v_b_w_grp, v_b_b_grp, v_b_scale, v_b_w_out):
    nb, seq, d = x.shape
    t = nb * seq
    e = a_ln_g.shape[1]
    nh, chunk = a_w_s.shape[1], a_w_s.shape[2]
    ng, rs, gw = b_w_grp.shape[1], b_w_grp.shape[2], b_w_grp.shape[3]
    wa, wb = a_w_in.shape[2], b_w_in.shape[2]
    cx, cy = lax.axis_index("x"), lax.axis_index("y")
    chip = 2 * cx + cy

    big_w = [a_w_in.reshape(d, wa), a_w_out.reshape(e // N_CHIPS, d), b_w_in.reshape(d, wb),
             b_w_grp.reshape(ng * rs, gw), b_w_out.reshape(e // N_CHIPS, d)]
    big_m = [m_a_w_in.reshape(d, wa), m_a_w_out.reshape(e // N_CHIPS, d), m_b_w_in.reshape(d, wb),
             m_b_w_grp.reshape(ng * rs, gw), m_b_w_out.reshape(e // N_CHIPS, d)]
    big_v = [v_a_w_in.reshape(d, wa), v_a_w_out.reshape(e // N_CHIPS, d), v_b_w_in.reshape(d, wb),
             v_b_w_grp.reshape(ng * rs, gw), v_b_w_out.reshape(e // N_CHIPS, d)]
    names = ["a_w_in", "a_w_out", "b_w_in", "b_w_grp", "b_w_out"]
    place = jnp.stack([chip, lax.axis_index("c")]).astype(jnp.int32)
    slots = [_cast_into_slot("cast_" + n, w, place) for n, w in zip(names, big_w)]
    small_w = jnp.concatenate([b_b_grp.reshape(ng, rs), b_scale.reshape(ng, rs)], axis=0)
    wa_g, gsmall = _comm_only("gather_a_w_in", [_gather_phase(slots[:1], small_w)])
    b_grp_full = jnp.transpose(gsmall[:, :ng, :], (1, 0, 2)).reshape(1, e)
    scale_full = gsmall[:, ng:, :].reshape(1, e)

    xf = x.reshape(t, d)
    tgt = loss_target.reshape(t, d)
    g_pre0, g_pre1 = pre_norm[0:1], pre_norm[1:2]
    g_post0, g_post1 = post_norm[0:1], post_norm[1:2]
    w_s = a_w_s.reshape(nh, chunk, chunk)
    b_s3 = a_b_s.reshape(nh, chunk, 1)

    h0 = _rms_fwd("pre_norm0", xf, g_pre0)
    pa, (wao_g, wb_half) = _mm_nn_cols(
        "a_in_proj", h0, wa_g, [_gather_phase(slots[1:2]), _gather_phase(slots[2:3], part=(0, 2))])
    wao_g = wao_g.reshape(e, d)
    mu, rstd = _a_stats("a_ln_stats", pa, e)
    y0 = _a_gate_fwd("a_gate", pa, mu, rstd, a_ln_g, a_ln_b, w_s, b_s3)
    m0, (wb_g,) = _mm_nn("a_out_proj", y0, wao_g, phases=[_gather_phase([wb_half], part=(1, 2))])
    x1, h1 = _post_pre_fwd("post0_pre1", xf, m0, g_post0, g_pre1)
    pb, (wg_g, wbo_g) = _mm_nn_cols("b_in_proj", h1, wb_g, [_gather_phase(slots[3:5])])
    wg_g = wg_g.reshape(N_CHIPS, ng, rs, gw)
    wbo_g = wbo_g.reshape(e, d)
    pooled = _pool("b_pool", pb, e, seq, backward=False)
    mm = _grp_fwd("b_grp_proj", pooled, wg_g)
    y1 = _b_gate_fwd("b_gate", mm, pb, b_grp_full, scale_full)
    m1 = _mm_nn("b_out_proj", y1, wbo_g)
    loss, dx2, dm1, dg_post1 = _loss_post_bwd("loss_post1_bwd", x1, m1, g_post1, tgt)

    def chip_sum(n, g, theirs):
        pbk, ownk = _add_halves("chip_sum_" + n, g, theirs, place)
        return [ownk], [pbk]

    reduced = {}
    dy1 = _mm_nt("b_out_dx", dm1, wbo_g)
    g_wbo = _mm_tn("b_out_dw", y1, dm1).reshape(N_CHIPS, e // N_CHIPS, d)
    dmm, dzb, db_grp, dscale = _b_gate_bwd("b_gate_bwd", dy1, mm, pb, b_grp_full, scale_full)
    dpooled, (th,) = _grp_bwd_x("b_grp_dx", dmm, wg_g, [_swap_phase([g_wbo])])
    s_bo = chip_sum("b_w_out", g_wbo, th)
    g_wg = _grp_bwd_w("b_grp_dw", pooled, dmm, ng).reshape(N_CHIPS, ng * rs, gw)
    dp = _pool("b_pool_bwd", dpooled, e, seq, backward=True)
    dpb = jnp.concatenate([dp, dzb], axis=1)
    dh1, (own, got, th) = _mm_nt_cols("b_in_dx", dpb, wb_g, [_scatter_phase(*s_bo), _swap_phase([g_wg])])
    reduced["b_w_out"] = (own, got)
    s_g = chip_sum("b_w_grp", g_wg, th)
    g_wb, reduced["b_w_grp"] = _mm_tn_cols("b_in_dw", h1, dpb, wb, [_scatter_phase(*s_g)])
    dx1, dm0, dg_pre1, dg_post0 = _mid_bwd("pre1_post0_bwd", dx2, dh1, x1, g_pre1, m0, g_post0)
    dy0, (th,) = _mm_nt("a_out_dx", dm0, wao_g, phases=[_swap_phase([g_wb])])
    s_b = chip_sum("b_w_in", g_wb, th)
    g_wao, (own, got) = _mm_tn("a_out_dw", y0, dm0, phases=[_scatter_phase(*s_b, part=(0, 2))])
    g_wao = g_wao.reshape(N_CHIPS, e // N_CHIPS, d)
    (du, dz, dsv, c1, c2), (own, got, th) = _a_gate_bwd1(
        "a_gate_bwd1", pa, dy0, mu, rstd, a_ln_g, a_ln_b, w_s, b_s3,
        [_scatter_phase([own], s_b[1], part=(1, 2), got=[got]), _swap_phase([g_wao])])
    reduced["b_w_in"] = (own, got)
    s_ao = chip_sum("a_w_out", g_wao, th)
    (dv, dw_s, db_s, dln_g, dln_b), reduced["a_w_out"] = _a_gate_bwd2(
        "a_gate_bwd2", pa, dsv, mu, rstd, c1, c2, a_ln_g, a_ln_b, w_s, [_scatter_phase(*s_ao)])
    dpa = jnp.concatenate([du, dv, dz], axis=1)
    g_lo = _mm_tn_cols("a_in_dw_even", h0, dpa, wa, odd=0)
    g_hi, (th,) = _mm_tn_cols("a_in_dw_odd", h0, dpa, wa, [_swap_phase([g_lo])], odd=1)
    s_lo = chip_sum("a_w_in_even", g_lo, th)
    dh0, (th, own_lo, got_lo) = _mm_nt_cols(
        "a_in_dx_top", dpa, wa_g, [_swap_phase([g_hi]), _scatter_phase(*s_lo)], rows=(0, 2))
    s_hi = chip_sum("a_w_in_odd", g_hi, th)
    dh0, (own_hi, got_hi) = _mm_nt_cols(
        "a_in_dx_bottom", dpa, wa_g, [_scatter_phase(*s_hi)], rows=(1, 2), into=dh0)
    grad_x, dg_pre0 = _pre_bwd("pre0_bwd", dx1, dh0, xf, g_pre0)
    big_out = []
    for k, n in enumerate(names):
        if n == "a_w_in":
            even = _adamw("adamw_a_w_in_even", big_w[k], big_m[k], big_v[k], own_lo, got_lo, odd=0)
            big_out.append(_adamw("adamw_a_w_in_odd", big_w[k], big_m[k], big_v[k], own_hi, got_hi, odd=1, into=even))
        else:
            big_out.append(_adamw("adamw_" + n, big_w[k], big_m[k], big_v[k], *reduced[n]))

    small_shapes = [(2, d), (2, d), (1, e), (1, e), (1, nh, chunk, chunk), (1, nh, chunk), (1, e), (1, e)]
    part = _pack([jnp.concatenate([dg_pre0, dg_pre1], axis=0), jnp.concatenate([dg_post0, dg_post1], axis=0),
                  dln_g, dln_b, dw_s, db_s, db_grp, dscale])
    g_small = _sum_slots("sum_small_grads", _gather_small(part))
    g_pre, g_post, g_lng, g_lnb, g_ws, g_bs, g_bgrp_full, g_scale_full = _unpack(g_small, small_shapes)
    g_bgrp = lax.dynamic_slice_in_dim(g_bgrp_full.reshape(ng, N_CHIPS, rs), chip, 1, axis=1).reshape(1, ng, rs)
    g_scale = lax.dynamic_slice_in_dim(g_scale_full.reshape(N_CHIPS, gw), chip, 1, axis=0)
    small_names = ["pre_norm", "post_norm", "a_ln_g", "a_ln_b", "a_w_s", "a_b_s", "b_b_grp", "b_scale"]
    small_g = [g_pre, g_post, g_lng, g_lnb, g_ws, g_bs, g_bgrp, g_scale]
    small_ws = [pre_norm, post_norm, a_ln_g, a_ln_b, a_w_s, a_b_s, b_b_grp, b_scale]
    small_ms = [m_pre_norm, m_post_norm, m_a_ln_g, m_a_ln_b, m_a_w_s, m_a_b_s, m_b_b_grp, m_b_scale]
    small_vs = [v_pre_norm, v_post_norm, v_a_ln_g, v_a_ln_b, v_a_w_s, v_a_b_s, v_b_b_grp, v_b_scale]
    packed = _adamw("adamw_small", _pack(small_ws), _pack(small_ms), _pack(small_vs), _pack(small_g))
    small_out = [_unpack(p, [w.shape for w in small_ws]) for p in packed]

    loss = lax.psum(loss[0, 0], ("x", "y", "c"))
    order = ["pre_norm", "post_norm", "a_w_in", "a_ln_g", "a_ln_b", "a_w_s", "a_b_s", "a_w_out", "b_w_in",
             "b_w_grp", "b_b_grp", "b_scale", "b_w_out"]
    big_shapes = dict(zip(names, [a_w_in.shape, a_w_out.shape, b_w_in.shape, b_w_grp.shape, b_w_out.shape]))
    outs = [loss, grad_x.reshape(nb, seq, d)]
    for kind in range(4):
        for n in order:
            if n in big_shapes:
                outs.append(big_out[names.index(n)][kind].reshape(big_shapes[n]))
            else:
                outs.append(small_out[kind][small_names.index(n)])
    return tuple(outs)
```

```python
import functools

import jax
import jax.numpy as jnp
from jax import lax
from jax.experimental import pallas as pl
from jax.experimental.pallas import tpu as pltpu

F32 = jnp.float32
BF16 = jnp.bfloat16
NORM_EPS = 1e-6
POOL_WINDOWS = (2, 4, 8, 16)
ADAM_LR = 0.001
ADAM_B1 = 0.9
ADAM_B2 = 0.999
ADAM_EPS = 1e-08
ADAM_WD = 0.01
ADAM_STEP = 10
N_CHIPS = 4
N_DEV = 8
V7X_VMEM_LIMIT_BYTES = 56 * 1024 * 1024
LANES = 128
MESH = pl.DeviceIdType.MESH
ANY = pl.BlockSpec(memory_space=pl.ANY)
SQRT_HALF = 0.7071067811865476
INV_SQRT_2PI = 0.3989422804014327


def _tile(dim, pref, mult=LANES):
    if dim <= pref:
        return dim
    t = (pref // mult) * mult
    while t >= mult:
        if dim % t == 0:
            return t
        t -= mult
    return dim


def _params(*sem):
    return pltpu.CompilerParams(dimension_semantics=sem or None, vmem_limit_bytes=V7X_VMEM_LIMIT_BYTES)


def _gelu(x):
    cdf = 0.5 * (1.0 + lax.erf(x * SQRT_HALF))
    pdf = jnp.exp(-0.5 * x * x) * INV_SQRT_2PI
    return x * cdf, cdf + x * pdf


def _silu(z):
    s = 1.0 / (1.0 + jnp.exp(-z))
    return z * s, s * (1.0 + z * (1.0 - s))


class _Phase:
    def __init__(self, ins, outs, aliases, n_sem, start, finish, relay=None):
        self.ins, self.outs, self.aliases, self.n_sem = list(ins), list(outs), dict(aliases), n_sem
        self.start, self.finish, self.relay = start, finish, relay


RELAY_AT = 0.85


def _hosted(body, phases, grid, n_in, n_out, n_scratch=0):
    ins, outs, aliases, sems = [], [], {}, []
    for ph in phases:
        for i, o in ph.aliases.items():
            aliases[n_in + len(ins) + i] = n_out + len(outs) + o
        ins += ph.ins
        outs += ph.outs
        sems += [pltpu.SemaphoreType.DMA((ph.n_sem,)), pltpu.SemaphoreType.DMA((ph.n_sem,))]
    if not phases:
        return body, ins, outs, aliases, sems
    steps = 1
    for n in grid:
        steps *= n
    relay_at = min(steps - 1, int(RELAY_AT * steps))

    def wrapped(*refs):
        own_in, refs = refs[:n_in], refs[n_in:]
        in_refs, refs = refs[:len(ins)], refs[len(ins):]
        own_out, refs = refs[:n_out], refs[n_out:]
        out_refs, refs = refs[:len(outs)], refs[len(outs):]
        own_scratch, sem_refs = refs[:n_scratch], refs[n_scratch:]

        def run(which):
            i = o = 0
            for p, ph in enumerate(phases):
                fn = getattr(ph, which)
                if fn is not None:
                    fn(in_refs[i:i + len(ph.ins)], out_refs[o:o + len(ph.outs)], sem_refs[2 * p], sem_refs[2 * p + 1])
                i += len(ph.ins)
                o += len(ph.outs)

        if not grid:
            run("start")
            run("relay")
            run("finish")
            return
        step = 0
        for d, n in enumerate(grid):
            step = step * n + pl.program_id(d)
        pl.when(step == 0)(lambda: run("start"))
        body(*own_in, *own_out, *own_scratch)
        pl.when(step == relay_at)(lambda: run("relay"))
        pl.when(step == steps - 1)(lambda: run("finish"))

    return wrapped, ins, outs, aliases, sems


def _comm_only(name, phases):
    body, ins, outs, aliases, sems = _hosted(None, phases, (), 0, 0)
    return pl.pallas_call(
        body, name=name, in_specs=[ANY] * len(ins), out_specs=[ANY] * len(outs), out_shape=outs,
        input_output_aliases=aliases, scratch_shapes=sems)(*ins)


def _matmul(name, a, b, *, dims, grid, a_spec, b_spec, o_spec, out_shape, out_dtype, phases=(), into=None):
    nk = grid[2]
    contract = {"nn": ((1,), (0,)), "nt": ((1,), (1,)), "tn": ((0,), (0,))}[dims]
    acc_in_out = out_dtype == F32 or nk == 1
    blk = tuple(d for d in o_spec.block_shape if d is not None)
    n_in = 2 if into is None else 3

    def body(*refs):
        a_ref, b_ref, o_ref = refs[0], refs[1], refs[n_in]
        part = lax.dot_general(a_ref[...], b_ref[...], (contract, ((), ())), preferred_element_type=F32)
        if nk == 1:
            o_ref[...] = part.astype(out_dtype)
            return
        acc = o_ref if acc_in_out else refs[n_in + 1]
        k = pl.program_id(2)

        @pl.when(k == 0)
        def _():
            acc[...] = part

        @pl.when(k > 0)
        def _():
            acc[...] += part

        if not acc_in_out:

            @pl.when(k == nk - 1)
            def _():
                o_ref[...] = acc[...].astype(out_dtype)

    own_scratch = [] if acc_in_out else [pltpu.VMEM(blk, F32)]
    body, x_ins, x_outs, aliases, sems = _hosted(body, phases, grid, n_in, 1, len(own_scratch))
    if into is not None:
        aliases[2] = 0
    sem = ("arbitrary",) * 3 if phases else ("parallel", "parallel", "arbitrary")
    res = pl.pallas_call(
        body,
        name=name,
        grid=grid,
        in_specs=[a_spec, b_spec] + ([] if into is None else [ANY]) + [ANY] * len(x_ins),
        out_specs=[o_spec] + [ANY] * len(x_outs),
        out_shape=[jax.ShapeDtypeStruct(out_shape, out_dtype)] + x_outs,
        input_output_aliases=aliases,
        scratch_shapes=own_scratch + sems,
        compiler_params=_params(*sem),
    )(a, b, *([] if into is None else [into]), *x_ins)
    return (res[0], list(res[1:])) if phases else res[0]


MM_K_WHOLE = 4096


def _tiles(m, n, k, k_total=None):
    if (k if k_total is None else k_total) <= MM_K_WHOLE:
        return _tile(m, 1024), _tile(n, 1024), k
    return _tile(m, 1024), _tile(n, 2048), _tile(k, 2048)


def _mm_nn(name, a, b, out_dtype=F32, phases=()):
    m, k = a.shape
    n = b.shape[1]
    tm, tn, tk = _tiles(m, n, k)
    return _matmul(
        name, a, b, dims="nn", grid=(m // tm, n // tn, k // tk),
        a_spec=pl.BlockSpec((tm, tk), lambda i, j, l: (i, l)),
        b_spec=pl.BlockSpec((tk, tn), lambda i, j, l: (l, j)),
        o_spec=pl.BlockSpec((tm, tn), lambda i, j, l: (i, j)),
        out_shape=(m, n), out_dtype=out_dtype, phases=phases)


def _mm_nt(name, a, b, out_dtype=F32, phases=()):
    m, k = a.shape
    n = b.shape[0]
    tm, tn, tk = _tiles(m, n, k)
    return _matmul(
        name, a, b, dims="nt", grid=(m // tm, n // tn, k // tk),
        a_spec=pl.BlockSpec((tm, tk), lambda i, j, l: (i, l)),
        b_spec=pl.BlockSpec((tn, tk), lambda i, j, l: (j, l)),
        o_spec=pl.BlockSpec((tm, tn), lambda i, j, l: (i, j)),
        out_shape=(m, n), out_dtype=out_dtype, phases=phases)


def _mm_tn(name, a, b, out_dtype=F32, phases=()):
    k, m = a.shape
    n = b.shape[1]
    tm, tn, tk = _tiles(m, n, k)
    return _matmul(
        name, a, b, dims="tn", grid=(m // tm, n // tn, k // tk),
        a_spec=pl.BlockSpec((tk, tm), lambda i, j, l: (l, i)),
        b_spec=pl.BlockSpec((tk, tn), lambda i, j, l: (l, j)),
        o_spec=pl.BlockSpec((tm, tn), lambda i, j, l: (i, j)),
        out_shape=(m, n), out_dtype=out_dtype, phases=phases)


def _mm_nn_cols(name, a, wg, phases=()):
    m, k = a.shape
    ws = wg.shape[2]
    tm, tn, tk = _tiles(m, ws, k)
    npb = ws // tn
    return _matmul(
        name, a, wg, dims="nn", grid=(m // tm, N_CHIPS * npb, k // tk),
        a_spec=pl.BlockSpec((tm, tk), lambda i, j, l: (i, l)),
        b_spec=pl.BlockSpec((None, tk, tn), lambda i, j, l: (j // npb, l, j % npb)),
        o_spec=pl.BlockSpec((tm, tn), lambda i, j, l: (i, j)),
        out_shape=(m, N_CHIPS * ws), out_dtype=F32, phases=phases)


def _mm_nt_cols(name, a, wg, phases=(), rows=None, into=None):
    m = a.shape[0]
    n, ws = wg.shape[1], wg.shape[2]
    q, nq = rows or (0, 1)
    tm, tn, tk = _tiles(m // nq, n, ws, N_CHIPS * ws)
    kpb, off = ws // tk, q * (m // nq // tm)
    return _matmul(
        name, a, wg, dims="nt", grid=(m // nq // tm, n // tn, N_CHIPS * kpb),
        a_spec=pl.BlockSpec((tm, tk), lambda i, j, l: (i + off, l)),
        b_spec=pl.BlockSpec((None, tn, tk), lambda i, j, l: (l // kpb, j, l % kpb)),
        o_spec=pl.BlockSpec((tm, tn), lambda i, j, l: (i + off, j)),
        out_shape=(m, n), out_dtype=F32, phases=phases, into=into)


def _mm_tn_cols(name, a, b, ws, phases=(), odd=None):
    k, m = a.shape
    rows = m if odd is None else m // 2
    tm, tn, tk = _tiles(m // 4, ws, k)
    npb, bpq = ws // tn, (m // 4) // tm
    pick = (lambda i: i) if odd is None else (lambda i: (2 * (i // bpq) + odd) * bpq + i % bpq)
    return _matmul(
        name, a, b, dims="tn", grid=(rows // tm, N_CHIPS * npb, k // tk),
        a_spec=pl.BlockSpec((tk, tm), lambda i, j, l: (l, pick(i))),
        b_spec=pl.BlockSpec((tk, tn), lambda i, j, l: (l, j)),
        o_spec=pl.BlockSpec((None, tm, tn), lambda i, j, l: (j // npb, i, j % npb)),
        out_shape=(N_CHIPS, rows, ws), out_dtype=F32, phases=phases)


def _grp_fwd(name, pooled, wgg):
    t, e = pooled.shape
    _, ng, rs, gw = wgg.shape
    tm, tn, tk = _tiles(t, gw, rs, gw)
    npb, kps = gw // tn, rs // tk
    return _matmul(
        name, pooled, wgg, dims="nn", grid=(t // tm, ng * npb, N_CHIPS * kps),
        a_spec=pl.BlockSpec((tm, tk), lambda i, j, l: (i, (j // npb) * (gw // tk) + l)),
        b_spec=pl.BlockSpec((None, None, tk, tn), lambda i, j, l: (l // kps, j // npb, l % kps, j % npb)),
        o_spec=pl.BlockSpec((tm, tn), lambda i, j, l: (i, j)),
        out_shape=(t, e), out_dtype=F32)


def _grp_bwd_x(name, dmm, wgg, phases=()):
    t, e = dmm.shape
    _, ng, rs, gw = wgg.shape
    tm, tn, tk = _tiles(t, rs, gw)
    npr, kpg = rs // tn, gw // tk
    return _matmul(
        name, dmm, wgg, dims="nt", grid=(t // tm, ng * N_CHIPS * npr, kpg),
        a_spec=pl.BlockSpec((tm, tk), lambda i, j, l: (i, (j // (N_CHIPS * npr)) * kpg + l)),
        b_spec=pl.BlockSpec(
            (None, None, tn, tk),
            lambda i, j, l: ((j % (N_CHIPS * npr)) // npr, j // (N_CHIPS * npr), j % npr, l)),
        o_spec=pl.BlockSpec((tm, tn), lambda i, j, l: (i, j)),
        out_shape=(t, e), out_dtype=F32, phases=phases)


def _grp_bwd_w(name, pooled, dmm, ng):
    t, e = pooled.shape
    gw = e // ng
    rs = gw // N_CHIPS
    _, tn, tk = _tiles(rs, gw, t)
    npb = gw // tn
    return _matmul(
        name, pooled, dmm, dims="tn", grid=(ng * N_CHIPS, npb, t // tk),
        a_spec=pl.BlockSpec((tk, rs), lambda i, j, l: (l, i)),
        b_spec=pl.BlockSpec((tk, tn), lambda i, j, l: (l, (i // N_CHIPS) * npb + j)),
        o_spec=pl.BlockSpec((None, None, rs, tn), lambda i, j, l: (i % N_CHIPS, i // N_CHIPS, 0, j)),
        out_shape=(N_CHIPS, ng, rs, gw), out_dtype=F32)


def _cast_into_slot(name, w, place):
    r, c = w.shape
    tr, tc = _tile(r, 512), _tile(c, 2048)

    def body(place_ref, w_ref, o_ref):
        o_ref[...] = w_ref[...].astype(BF16)

    return pl.pallas_call(
        body, name=name,
        grid_spec=pltpu.PrefetchScalarGridSpec(
            num_scalar_prefetch=1, grid=(r // tr, c // tc),
            in_specs=[pl.BlockSpec((tr, tc), lambda i, j, p: (i, j))],
            out_specs=pl.BlockSpec((None, tr, tc), lambda i, j, p: (p[0], i, j))),
        out_shape=jax.ShapeDtypeStruct((N_CHIPS, r, c), BF16),
        compiler_params=_params("parallel", "parallel"))(place, w)


def _rms(v):
    return lax.rsqrt(jnp.mean(v * v, axis=-1, keepdims=True) + NORM_EPS)


def _rms_fwd(name, x, g):
    t, d = x.shape
    tr = _tile(t, 256)

    def body(x_ref, g_ref, h_ref):
        xv = x_ref[...]
        h_ref[...] = (xv * _rms(xv) * g_ref[...]).astype(BF16)

    row = pl.BlockSpec((tr, d), lambda i: (i, 0))
    vec = pl.BlockSpec((1, d), lambda i: (0, 0))
    return pl.pallas_call(
        body, name=name, grid=(t // tr,), in_specs=[row, vec], out_specs=row,
        out_shape=jax.ShapeDtypeStruct((t, d), BF16), compiler_params=_params("parallel"))(x, g)


def _post_pre_fwd(name, x, m, g_post, g_pre):
    t, d = x.shape
    tr = _tile(t, 256)

    def body(x_ref, m_ref, gp_ref, gn_ref, x1_ref, h_ref):
        mv = m_ref[...]
        x1 = x_ref[...] + mv * _rms(mv) * gp_ref[...]
        x1_ref[...] = x1
        h_ref[...] = (x1 * _rms(x1) * gn_ref[...]).astype(BF16)

    row = pl.BlockSpec((tr, d), lambda i: (i, 0))
    vec = pl.BlockSpec((1, d), lambda i: (0, 0))
    return pl.pallas_call(
        body, name=name, grid=(t // tr,), in_specs=[row, row, vec, vec], out_specs=[row, row],
        out_shape=[jax.ShapeDtypeStruct((t, d), F32), jax.ShapeDtypeStruct((t, d), BF16)],
        compiler_params=_params("parallel"))(x, m, g_post, g_pre)


def _norm_bwd(dout, nrm, r, g):
    gd = dout * g
    return r * (gd - nrm * jnp.mean(gd * nrm, axis=-1, keepdims=True))


def _loss_post_bwd(name, x1, m, g_post, target):
    t, d = x1.shape
    tr = _tile(t, 128)

    def body(x_ref, m_ref, g_ref, t_ref, loss_ref, dx_ref, dm_ref, dg_ref):
        i = pl.program_id(0)
        mv = m_ref[...]
        r = _rms(mv)
        nrm = mv * r
        err = x_ref[...] + nrm * g_ref[...] - t_ref[...]
        part = 0.5 * jnp.sum(jnp.mean(err * err, axis=-1, keepdims=True), axis=0, keepdims=True)
        dx = err / d
        dx_ref[...] = dx
        dm_ref[...] = _norm_bwd(dx, nrm, r, g_ref[...]).astype(BF16)
        dg = jnp.sum(dx * nrm, axis=0, keepdims=True)

        @pl.when(i == 0)
        def _():
            loss_ref[...] = part
            dg_ref[...] = dg

        @pl.when(i > 0)
        def _():
            loss_ref[...] += part
            dg_ref[...] += dg

    row = pl.BlockSpec((tr, d), lambda i: (i, 0))
    vec = pl.BlockSpec((1, d), lambda i: (0, 0))
    one = pl.BlockSpec((1, 1), lambda i: (0, 0))
    return pl.pallas_call(
        body, name=name, grid=(t // tr,), in_specs=[row, row, vec, row], out_specs=[one, row, row, vec],
        out_shape=[jax.ShapeDtypeStruct((1, 1), F32), jax.ShapeDtypeStruct((t, d), F32),
                   jax.ShapeDtypeStruct((t, d), BF16), jax.ShapeDtypeStruct((1, d), F32)],
        compiler_params=_params("arbitrary"))(x1, m, g_post, target)


def _mid_bwd(name, dx2, dh1, x1, g_pre, m0, g_post):
    t, d = x1.shape
    tr = _tile(t, 128)

    def body(dx2_ref, dh_ref, x_ref, gn_ref, m_ref, gp_ref, dx_ref, dm_ref, dgn_ref, dgp_ref):
        i = pl.program_id(0)
        xv = x_ref[...]
        r1 = _rms(xv)
        n1 = xv * r1
        dh = dh_ref[...]
        dx = dx2_ref[...] + _norm_bwd(dh, n1, r1, gn_ref[...])
        dx_ref[...] = dx
        mv = m_ref[...]
        r0 = _rms(mv)
        n0 = mv * r0
        dm_ref[...] = _norm_bwd(dx, n0, r0, gp_ref[...]).astype(BF16)
        dgn = jnp.sum(dh * n1, axis=0, keepdims=True)
        dgp = jnp.sum(dx * n0, axis=0, keepdims=True)

        @pl.when(i == 0)
        def _():
            dgn_ref[...] = dgn
            dgp_ref[...] = dgp

        @pl.when(i > 0)
        def _():
            dgn_ref[...] += dgn
            dgp_ref[...] += dgp

    row = pl.BlockSpec((tr, d), lambda i: (i, 0))
    vec = pl.BlockSpec((1, d), lambda i: (0, 0))
    return pl.pallas_call(
        body, name=name, grid=(t // tr,), in_specs=[row, row, row, vec, row, vec],
        out_specs=[row, row, vec, vec],
        out_shape=[jax.ShapeDtypeStruct((t, d), F32), jax.ShapeDtypeStruct((t, d), BF16),
                   jax.ShapeDtypeStruct((1, d), F32), jax.ShapeDtypeStruct((1, d), F32)],
        compiler_params=_params("arbitrary"))(dx2, dh1, x1, g_pre, m0, g_post)


def _pre_bwd(name, dx1, dh0, x, g_pre):
    t, d = x.shape
    tr = _tile(t, 128)

    def body(dx1_ref, dh_ref, x_ref, g_ref, dx_ref, dg_ref):
        i = pl.program_id(0)
        xv = x_ref[...]
        r = _rms(xv)
        nrm = xv * r
        dh = dh_ref[...]
        dx_ref[...] = dx1_ref[...] + _norm_bwd(dh, nrm, r, g_ref[...])
        dg = jnp.sum(dh * nrm, axis=0, keepdims=True)

        @pl.when(i == 0)
        def _():
            dg_ref[...] = dg

        @pl.when(i > 0)
        def _():
            dg_ref[...] += dg

    row = pl.BlockSpec((tr, d), lambda i: (i, 0))
    vec = pl.BlockSpec((1, d), lambda i: (0, 0))
    return pl.pallas_call(
        body, name=name, grid=(t // tr,), in_specs=[row, row, row, vec], out_specs=[row, vec],
        out_shape=[jax.ShapeDtypeStruct((t, d), F32), jax.ShapeDtypeStruct((1, d), F32)],
        compiler_params=_params("arbitrary"))(dx1, dh0, x, g_pre)


def _a_stats(name, pa, e):
    t = pa.shape[0]
    tr = _tile(t, 64, 8)

    def body(v_ref, mu_ref, rs_ref):
        vg, _ = _gelu(v_ref[...])
        mu = jnp.mean(vg, axis=-1, keepdims=True)
        xc = vg - mu
        mu_ref[...] = mu
        rs_ref[...] = lax.rsqrt(jnp.mean(xc * xc, axis=-1, keepdims=True) + NORM_EPS)

    col = pl.BlockSpec((tr, 1), lambda i: (i, 0))
    return pl.pallas_call(
        body, name=name, grid=(t // tr,), in_specs=[pl.BlockSpec((tr, e), lambda i: (i, 1))],
        out_specs=[col, col],
        out_shape=[jax.ShapeDtypeStruct((t, 1), F32), jax.ShapeDtypeStruct((t, 1), F32)],
        compiler_params=_params("parallel"))(pa)


def _causal(w):
    c = w.shape[0]
    keep = lax.broadcasted_iota(jnp.int32, (c, c), 0) >= lax.broadcasted_iota(jnp.int32, (c, c), 1)
    return jnp.where(keep, w, 0.0), keep


def _a_gate_fwd(name, pa, mu, rs, ln_g, ln_b, w_s, b_s3):
    t = pa.shape[0]
    nh, c, _ = w_s.shape
    e = ln_g.shape[1]
    dh = e // nh
    rb = 2 * c if t % (2 * c) == 0 else c

    def body(u_ref, v_ref, z_ref, mu_ref, rs_ref, g_ref, b_ref, w_ref, bs_ref, y_ref):
        wc = _causal(w_ref[...])[0].astype(BF16)
        vg, _ = _gelu(v_ref[...])
        vn = ((vg - mu_ref[...]) * rs_ref[...] * g_ref[...] + b_ref[...]).astype(BF16)
        for ci in range(rb // c):
            rows = pl.ds(ci * c, c)
            sv = jnp.dot(wc, vn[ci * c:(ci + 1) * c], preferred_element_type=F32) + bs_ref[...]
            u, _ = _gelu(u_ref[rows, :])
            sz, _ = _silu(z_ref[rows, :])
            y_ref[rows, :] = (u * sv * sz).astype(BF16)

    blk = lambda off: pl.BlockSpec((rb, dh), lambda i, h: (i, off + h))
    col = pl.BlockSpec((rb, 1), lambda i, h: (i, 0))
    vec = pl.BlockSpec((1, dh), lambda i, h: (0, h))
    return pl.pallas_call(
        body, name=name, grid=(t // rb, nh),
        in_specs=[blk(0), blk(nh), blk(2 * nh), col, col, vec, vec,
                  pl.BlockSpec((None, c, c), lambda i, h: (h, 0, 0)),
                  pl.BlockSpec((None, c, 1), lambda i, h: (h, 0, 0))],
        out_specs=pl.BlockSpec((rb, dh), lambda i, h: (i, h)),
        out_shape=jax.ShapeDtypeStruct((t, e), BF16),
        compiler_params=_params("parallel", "parallel"))(pa, pa, pa, mu, rs, ln_g, ln_b, w_s, b_s3)


def _a_gate_bwd1(name, pa, dy, mu, rs, ln_g, ln_b, w_s, b_s3, phases=()):
    t = pa.shape[0]
    nh, c, _ = w_s.shape
    e = ln_g.shape[1]
    dh = e // nh
    rb = 2 * c if t % (2 * c) == 0 else c

    def body(u_ref, v_ref, z_ref, dy_ref, mu_ref, rs_ref, g_ref, b_ref, w_ref, bs_ref,
             du_ref, dz_ref, dsv_ref, c1_ref, c2_ref):
        h = pl.program_id(1)
        wc = _causal(w_ref[...])[0].astype(BF16)
        vg, _ = _gelu(v_ref[...])
        xh = (vg - mu_ref[...]) * rs_ref[...]
        vn = (xh * g_ref[...] + b_ref[...]).astype(BF16)
        s1 = []
        s2 = []
        for ci in range(rb // c):
            rows = pl.ds(ci * c, c)
            lo, hi = ci * c, (ci + 1) * c
            sv = jnp.dot(wc, vn[lo:hi], preferred_element_type=F32) + bs_ref[...]
            u, du = _gelu(u_ref[rows, :])
            zv = z_ref[rows, :]
            sz, dsz = _silu(zv)
            dyv = dy_ref[rows, :]
            du_ref[rows, :] = (dyv * sv * sz * du).astype(BF16)
            dz_ref[rows, :] = (dyv * u * sv * dsz).astype(BF16)
            dsv = (dyv * u * sz).astype(BF16)
            dsv_ref[rows, :] = dsv
            dvn = lax.dot_general(wc, dsv, (((0,), (0,)), ((), ())), preferred_element_type=F32)
            dxh = dvn * g_ref[...]
            s1.append(jnp.sum(dxh, axis=-1, keepdims=True))
            s2.append(jnp.sum(dxh * xh[lo:hi], axis=-1, keepdims=True))
        p1 = jnp.concatenate(s1, axis=0)
        p2 = jnp.concatenate(s2, axis=0)

        @pl.when(h == 0)
        def _():
            c1_ref[...] = p1
            c2_ref[...] = p2

        @pl.when(h > 0)
        def _():
            c1_ref[...] += p1
            c2_ref[...] += p2

    blk = lambda off: pl.BlockSpec((rb, dh), lambda i, h: (i, off + h))
    col = pl.BlockSpec((rb, 1), lambda i, h: (i, 0))
    vec = pl.BlockSpec((1, dh), lambda i, h: (0, h))
    act = jax.ShapeDtypeStruct((t, e), BF16)
    stat = jax.ShapeDtypeStruct((t, 1), F32)
    grid = (t // rb, nh)
    body, x_ins, x_outs, aliases, sems = _hosted(body, phases, grid, 10, 5)
    res = pl.pallas_call(
        body, name=name, grid=grid,
        in_specs=[blk(0), blk(nh), blk(2 * nh), blk(0), col, col, vec, vec,
                  pl.BlockSpec((None, c, c), lambda i, h: (h, 0, 0)),
                  pl.BlockSpec((None, c, 1), lambda i, h: (h, 0, 0))] + [ANY] * len(x_ins),
        out_specs=[blk(0), blk(0), blk(0), col, col] + [ANY] * len(x_outs),
        out_shape=[act, act, act, stat, stat] + x_outs, input_output_aliases=aliases, scratch_shapes=sems,
        compiler_params=_params("arbitrary", "arbitrary"))(pa, pa, pa, dy, mu, rs, ln_g, ln_b, w_s, b_s3, *x_ins)
    return (res[:5], list(res[5:])) if phases else res


def _a_gate_bwd2(name, pa, dsv, mu, rs, c1, c2, ln_g, ln_b, w_s, phases=()):
    t = pa.shape[0]
    nh, c, _ = w_s.shape
    e = ln_g.shape[1]
    dh = e // nh
    rb = 2 * c if t % (2 * c) == 0 else c

    def body(v_ref, dsv_ref, mu_ref, rs_ref, c1_ref, c2_ref, g_ref, b_ref, w_ref,
             dv_ref, dw_ref, dbs_ref, dg_ref, db_ref):
        i = pl.program_id(1)
        wcf, keep = _causal(w_ref[...])
        wc = wcf.astype(BF16)
        vg, dvg = _gelu(v_ref[...])
        rsv = rs_ref[...]
        xh = (vg - mu_ref[...]) * rsv
        vn = (xh * g_ref[...] + b_ref[...]).astype(BF16)
        dw = jnp.zeros((c, c), F32)
        dbs = jnp.zeros((c, 1), F32)
        dvns = []
        for ci in range(rb // c):
            lo, hi = ci * c, (ci + 1) * c
            dsv = dsv_ref[pl.ds(lo, c), :]
            dvns.append(lax.dot_general(wc, dsv, (((0,), (0,)), ((), ())), preferred_element_type=F32))
            dw += lax.dot_general(dsv, vn[lo:hi], (((1,), (1,)), ((), ())), preferred_element_type=F32)
            dbs += jnp.sum(dsv.astype(F32), axis=-1, keepdims=True)
        dvn = jnp.concatenate(dvns, axis=0)
        dxh = dvn * g_ref[...]
        dvv = rsv * (dxh - c1_ref[...] * (1.0 / e) - xh * (c2_ref[...] * (1.0 / e)))
        dv_ref[...] = (dvv * dvg).astype(BF16)
        dw = jnp.where(keep, dw, 0.0)
        dg = jnp.sum(dvn * xh, axis=0, keepdims=True)
        db = jnp.sum(dvn, axis=0, keepdims=True)

        @pl.when(i == 0)
        def _():
            dw_ref[...] = dw
            dbs_ref[...] = dbs
            dg_ref[...] = dg
            db_ref[...] = db

        @pl.when(i > 0)
        def _():
            dw_ref[...] += dw
            dbs_ref[...] += dbs
            dg_ref[...] += dg
            db_ref[...] += db

    col = pl.BlockSpec((rb, 1), lambda h, i: (i, 0))
    vec = pl.BlockSpec((1, dh), lambda h, i: (0, h))
    hblk = pl.BlockSpec((rb, dh), lambda h, i: (i, h))
    grid = (nh, t // rb)
    body, x_ins, x_outs, aliases, sems = _hosted(body, phases, grid, 9, 5)
    res = pl.pallas_call(
        body, name=name, grid=grid,
        in_specs=[pl.BlockSpec((rb, dh), lambda h, i: (i, nh + h)), hblk, col, col, col, col, vec, vec,
                  pl.BlockSpec((None, c, c), lambda h, i: (h, 0, 0))] + [ANY] * len(x_ins),
        out_specs=[hblk, pl.BlockSpec((None, c, c), lambda h, i: (h, 0, 0)),
                   pl.BlockSpec((None, c, 1), lambda h, i: (h, 0, 0)), vec, vec] + [ANY] * len(x_outs),
        out_shape=[jax.ShapeDtypeStruct((t, e), BF16), jax.ShapeDtypeStruct((nh, c, c), F32),
                   jax.ShapeDtypeStruct((nh, c, 1), F32), jax.ShapeDtypeStruct((1, e), F32),
                   jax.ShapeDtypeStruct((1, e), F32)] + x_outs,
        input_output_aliases=aliases, scratch_shapes=sems,
        compiler_params=_params("arbitrary", "arbitrary"))(pa, dsv, mu, rs, c1, c2, ln_g, ln_b, w_s, *x_ins)
    return (res[:5], list(res[5:])) if phases else res


def _pool(name, src, e, seq, backward):
    t = src.shape[0]
    ng = len(POOL_WINDOWS)
    gw = e // ng
    cw = _tile(gw, 256)

    def shifted(a, j, pos):
        if backward:
            return jnp.where(pos < seq - j, pltpu.roll(a, seq - j, 0), 0.0)
        return jnp.where(pos >= j, pltpu.roll(a, j, 0), 0.0)

    def body(p_ref, o_ref):
        grp = (pl.program_id(1) * cw) // gw
        pos = lax.broadcasted_iota(jnp.int32, (seq, cw), 0)
        posf = (pos + 1).astype(F32)
        for k, w in enumerate(POOL_WINDOWS):

            @pl.when(grp == k)
            def _(w=w):
                pv = p_ref[...]
                cnt = jnp.minimum(posf, float(w))
                acc = pv / cnt if backward else pv
                j = 1
                while j < w:
                    acc = acc + shifted(acc, j, pos)
                    j *= 2
                out = acc - pv if backward else acc / cnt - pv
                o_ref[...] = out.astype(BF16)

    spec = pl.BlockSpec((seq, cw), lambda s, j: (s, j))
    return pl.pallas_call(
        body, name=name, grid=(t // seq, e // cw), in_specs=[spec], out_specs=spec,
        out_shape=jax.ShapeDtypeStruct((t, e), BF16),
        compiler_params=_params("parallel", "parallel"))(src)


def _b_gate_fwd(name, mm, pb, b_grp, scale):
    t, e = mm.shape
    tr, tc = _tile(t, 512), _tile(e, 1024)
    nc = e // tc

    def body(mm_ref, z_ref, b_ref, s_ref, y_ref):
        sz, _ = _silu(z_ref[...])
        y_ref[...] = ((mm_ref[...] + b_ref[...]) * s_ref[...] * sz).astype(BF16)

    blk = pl.BlockSpec((tr, tc), lambda i, j: (i, j))
    vec = pl.BlockSpec((1, tc), lambda i, j: (0, j))
    return pl.pallas_call(
        body, name=name, grid=(t // tr, nc),
        in_specs=[blk, pl.BlockSpec((tr, tc), lambda i, j: (i, nc + j)), vec, vec], out_specs=blk,
        out_shape=jax.ShapeDtypeStruct((t, e), BF16),
        compiler_params=_params("parallel", "parallel"))(mm, pb, b_grp, scale)


def _b_gate_bwd(name, dy, mm, pb, b_grp, scale):
    t, e = mm.shape
    tr, tc = _tile(t, 512), _tile(e, 1024)
    nc = e // tc

    def body(dy_ref, mm_ref, z_ref, b_ref, s_ref, dmm_ref, dz_ref, db_ref, ds_ref):
        i = pl.program_id(1)
        sz, dsz = _silu(z_ref[...])
        dyv = dy_ref[...]
        mb = mm_ref[...] + b_ref[...]
        dmixed = dyv * sz
        dmm = dmixed * s_ref[...]
        dmm_ref[...] = dmm.astype(BF16)
        dz_ref[...] = (dyv * (mb * s_ref[...]) * dsz).astype(BF16)
        db = jnp.sum(dmm, axis=0, keepdims=True)
        ds = jnp.sum(dmixed * mb, axis=0, keepdims=True)

        @pl.when(i == 0)
        def _():
            db_ref[...] = db
            ds_ref[...] = ds

        @pl.when(i > 0)
        def _():
            db_ref[...] += db
            ds_ref[...] += ds

    blk = pl.BlockSpec((tr, tc), lambda j, i: (i, j))
    vec = pl.BlockSpec((1, tc), lambda j, i: (0, j))
    act = jax.ShapeDtypeStruct((t, e), BF16)
    stat = jax.ShapeDtypeStruct((1, e), F32)
    return pl.pallas_call(
        body, name=name, grid=(nc, t // tr),
        in_specs=[blk, blk, pl.BlockSpec((tr, tc), lambda j, i: (i, nc + j)), vec, vec],
        out_specs=[blk, blk, vec, vec], out_shape=[act, act, stat, stat],
        compiler_params=_params("parallel", "arbitrary"))(dy, mm, pb, b_grp, scale)


def _position():
    return lax.axis_index("x"), lax.axis_index("y"), lax.axis_index("c")


def _other_chips(x, y):
    return [(1 - x, y), (x, 1 - y), (1 - x, 1 - y)]


def _half(ref, c):
    n = ref.shape[0] // 2
    return ref.at[pl.ds(c * n, n)]


def _part(ref, c, part):
    q, nq, count = part if len(part) == 3 else (*part, 1)
    n = ref.shape[0] // (2 * nq)
    return ref.at[pl.ds(c * nq * n + q * n, count * n)]


def _gather_phase(slots, small=None, part=(0, 1)):
    nw = len(slots)
    ns = 0 if small is None else 1
    n_ici = 3 * (nw + ns)
    n_sem = n_ici + 3 * nw + ns

    def copies(ins, outs, send_sems, recv_sems):
        x, y, c = _position()
        sibling = (x, y, 1 - c)

        def ici(j, k, slot, to):
            return pltpu.make_async_remote_copy(
                src_ref=_part(ins[k].at[slot], c, part), dst_ref=_part(outs[k].at[slot], c, part),
                send_sem=send_sems.at[j * nw + k], recv_sem=recv_sems.at[j * nw + k],
                device_id=to, device_id_type=MESH)

        def ici_small(j, slot, to):
            return pltpu.make_async_remote_copy(
                src_ref=ins[nw], dst_ref=outs[nw].at[slot], send_sem=send_sems.at[3 * nw + j],
                recv_sem=recv_sems.at[3 * nw + j], device_id=to, device_id_type=MESH)

        def d2d(j, k, slot, half):
            return pltpu.make_async_remote_copy(
                src_ref=_part(outs[k].at[slot], half, part), dst_ref=_part(outs[k].at[slot], half, part),
                send_sem=send_sems.at[n_ici + j * nw + k], recv_sem=recv_sems.at[n_ici + j * nw + k],
                device_id=sibling, device_id_type=MESH)

        def local():
            return pltpu.make_async_copy(ins[nw], outs[nw].at[2 * x + y], send_sems.at[n_sem - 1])

        return x, y, c, ici, ici_small, d2d, local

    def start(ins, outs, send_sems, recv_sems):
        x, y, c, ici, ici_small, _, local = copies(ins, outs, send_sems, recv_sems)
        if ns:
            local().start()
        for j, (cx, cy) in enumerate(_other_chips(x, y)):
            for k in range(nw):
                ici(j, k, 2 * x + y, (cx, cy, c)).start()
            if ns:
                ici_small(j, 2 * x + y, (cx, cy, c)).start()

    def relay(ins, outs, send_sems, recv_sems):
        x, y, c, ici, _, d2d, _ = copies(ins, outs, send_sems, recv_sems)
        for j, (cx, cy) in enumerate(_other_chips(x, y)):
            for k in range(nw):
                ici(j, k, 2 * cx + cy, (x, y, c)).wait_recv()
                d2d(j, k, 2 * cx + cy, c).start()

    def finish(ins, outs, send_sems, recv_sems):
        x, y, c, ici, ici_small, d2d, local = copies(ins, outs, send_sems, recv_sems)
        chips = _other_chips(x, y)
        for j, (cx, cy) in enumerate(chips):
            if ns:
                ici_small(j, 2 * cx + cy, (x, y, c)).wait_recv()
            for k in range(nw):
                d2d(j, k, 2 * cx + cy, 1 - c).wait_recv()
        for j, (cx, cy) in enumerate(chips):
            for k in range(nw):
                ici(j, k, 2 * x + y, (cx, cy, c)).wait_send()
                d2d(j, k, 2 * cx + cy, c).wait_send()
            if ns:
                ici_small(j, 2 * x + y, (cx, cy, c)).wait_send()
        if ns:
            local().wait()

    outs = [jax.ShapeDtypeStruct(s.shape, s.dtype) for s in slots]
    if ns:
        outs.append(jax.ShapeDtypeStruct((N_CHIPS,) + small.shape, small.dtype))
    return _Phase(list(slots) + ([small] if ns else []), outs, {k: k for k in range(nw)}, n_sem, start, finish, relay)


def _relayed_gather_phase(slots, small):
    nw = len(slots)
    n_sem = 7 * nw + 4

    def copies(ins, outs, send_sems, recv_sems):
        x, y, c = _position()
        sibling = (x, y, 1 - c)

        def direct(j, k, slot, to):
            return pltpu.make_async_remote_copy(
                src_ref=_half(ins[k].at[slot], c), dst_ref=_half(outs[k].at[slot], c),
                send_sem=send_sems.at[j * nw + k], recv_sem=recv_sems.at[j * nw + k],
                device_id=to, device_id_type=MESH)

        def relayed(p, k, slot, to):
            return pltpu.make_async_remote_copy(
                src_ref=_part(outs[k].at[slot], c, (p, 2)), dst_ref=_part(outs[k].at[slot], c, (p, 2)),
                send_sem=send_sems.at[2 * nw + p * nw + k], recv_sem=recv_sems.at[2 * nw + p * nw + k],
                device_id=to, device_id_type=MESH)

        def d2d(j, k, slot, half):
            return pltpu.make_async_remote_copy(
                src_ref=_half(outs[k].at[slot], half), dst_ref=_half(outs[k].at[slot], half),
                send_sem=send_sems.at[4 * nw + j * nw + k], recv_sem=recv_sems.at[4 * nw + j * nw + k],
                device_id=sibling, device_id_type=MESH)

        def small_copy(j, slot, to):
            return pltpu.make_async_remote_copy(
                src_ref=ins[nw], dst_ref=outs[nw].at[slot], send_sem=send_sems.at[7 * nw + j],
                recv_sem=recv_sems.at[7 * nw + j], device_id=to, device_id_type=MESH)

        def local():
            return pltpu.make_async_copy(ins[nw], outs[nw].at[2 * x + y], send_sems.at[n_sem - 1])

        chips = _other_chips(x, y)
        return x, y, c, chips, [2 * cx + cy for cx, cy in chips], direct, relayed, d2d, small_copy, local

    def start(*refs):
        x, y, c, chips, _, direct, _, _, small_copy, local = copies(*refs)
        local().start()
        for j in range(2):
            for k in range(nw):
                direct(j, k, 2 * x + y, (*chips[j], c)).start()
        for j in range(3):
            small_copy(j, 2 * x + y, (*chips[j], c)).start()

    def relay(*refs):
        x, y, c, chips, slot, direct, relayed, d2d, _, _ = copies(*refs)
        for j in range(2):
            for k in range(nw):
                direct(j, k, slot[j], (x, y, c)).wait_recv()
                relayed(1 - j, k, slot[j], (*chips[1 - j], c)).start()
                d2d(j, k, slot[j], c).start()
        for k in range(nw):
            for p in range(2):
                relayed(p, k, slot[2], (x, y, c)).wait_recv()
            d2d(2, k, slot[2], c).start()

    def finish(*refs):
        x, y, c, chips, slot, direct, relayed, d2d, small_copy, local = copies(*refs)
        for j in range(3):
            small_copy(j, slot[j], (x, y, c)).wait_recv()
            for k in range(nw):
                d2d(j, k, slot[j], 1 - c).wait_recv()
        for j in range(3):
            small_copy(j, 2 * x + y, (*chips[j], c)).wait_send()
            for k in range(nw):
                d2d(j, k, slot[j], c).wait_send()
        for j in range(2):
            for k in range(nw):
                direct(j, k, 2 * x + y, (*chips[j], c)).wait_send()
                relayed(1 - j, k, slot[j], (*chips[1 - j], c)).wait_send()
        local().wait()

    outs = [jax.ShapeDtypeStruct(s.shape, s.dtype) for s in slots]
    outs.append(jax.ShapeDtypeStruct((N_CHIPS,) + small.shape, small.dtype))
    return _Phase(list(slots) + [small], outs, {k: k for k in range(nw)}, n_sem, start, finish, relay)


def _swap_phase(grads):
    nw = len(grads)

    def swaps(ins, outs, send_sems, recv_sems):
        x, y, c = _position()
        cps = []
        for k in range(nw):
            n = ins[k].shape[1] // 2
            cps.append(pltpu.make_async_remote_copy(
                src_ref=ins[k].at[pl.ds(0, N_CHIPS), pl.ds((1 - c) * n, n)], dst_ref=outs[k],
                send_sem=send_sems.at[k], recv_sem=recv_sems.at[k],
                device_id=(x, y, 1 - c), device_id_type=MESH))
        return cps

    def start(*refs):
        for cp in swaps(*refs):
            cp.start()

    def finish(*refs):
        for cp in swaps(*refs):
            cp.wait()

    halves = [jax.ShapeDtypeStruct((g.shape[0], g.shape[1] // 2, g.shape[2]), F32) for g in grads]
    return _Phase(grads, halves, {}, nw, start, finish)


def _add_halves(name, grad, theirs, place):
    s, half, w = theirs.shape
    tr, tc = _tile(half, 256), _tile(w, 2048)
    nrt = half // tr

    def body(place_ref, a_ref, b_ref, h_ref, f_ref):
        v = a_ref[...] + b_ref[...]
        h_ref[...] = v.astype(BF16)

        @pl.when(pl.program_id(2) == place_ref[0])
        def _():
            f_ref[...] = v

    return pl.pallas_call(
        body, name=name,
        grid_spec=pltpu.PrefetchScalarGridSpec(
            num_scalar_prefetch=1, grid=(nrt, w // tc, s),
            in_specs=[pl.BlockSpec((None, tr, tc), lambda i, j, q, p: (q, p[1] * nrt + i, j)),
                      pl.BlockSpec((None, tr, tc), lambda i, j, q, p: (q, i, j))],
            out_specs=[pl.BlockSpec((None, tr, tc), lambda i, j, q, p: (q, i, j)),
                       pl.BlockSpec((tr, tc), lambda i, j, q, p: (p[1] * nrt + i, j))]),
        out_shape=[jax.ShapeDtypeStruct(theirs.shape, BF16), jax.ShapeDtypeStruct((2 * half, w), F32)],
        compiler_params=_params("parallel", "parallel", "arbitrary"))(place, grad, theirs)


def _scatter_phase(own, pb, part=(0, 1), got=None):
    nw = len(own)

    def copies(ins, outs, send_sems, recv_sems):
        own_in, pbs = ins[:nw], ins[nw:2 * nw]
        own_out, gots = outs[:nw], outs[nw:]
        x, y, c = _position()
        sibling = (x, y, 1 - c)

        def d2d_own(k, half):
            return pltpu.make_async_remote_copy(
                src_ref=_part(own_in[k], half, part), dst_ref=_part(own_out[k], half, part),
                send_sem=send_sems.at[k], recv_sem=recv_sems.at[k], device_id=sibling, device_id_type=MESH)

        def ici(j, k, shard, to):
            n = pbs[k].shape[1] // part[1]
            rows = pl.ds(part[0] * n, (part[2] if len(part) == 3 else 1) * n)
            return pltpu.make_async_remote_copy(
                src_ref=pbs[k].at[shard, rows], dst_ref=_part(gots[k].at[j], c, part),
                send_sem=send_sems.at[nw + j * nw + k], recv_sem=recv_sems.at[nw + j * nw + k],
                device_id=to, device_id_type=MESH)

        def d2d(j, k, half):
            return pltpu.make_async_remote_copy(
                src_ref=_part(gots[k].at[j], half, part), dst_ref=_part(gots[k].at[j], half, part),
                send_sem=send_sems.at[4 * nw + j * nw + k], recv_sem=recv_sems.at[4 * nw + j * nw + k],
                device_id=sibling, device_id_type=MESH)

        return x, y, c, d2d_own, ici, d2d

    def start(*refs):
        x, y, c, d2d_own, ici, _ = copies(*refs)
        for k in range(nw):
            d2d_own(k, c).start()
        for j, (cx, cy) in enumerate(_other_chips(x, y)):
            for k in range(nw):
                ici(j, k, 2 * cx + cy, (cx, cy, c)).start()

    def relay(*refs):
        x, y, c, _, ici, d2d = copies(*refs)
        for j in range(3):
            for k in range(nw):
                ici(j, k, 2 * x + y, (x, y, c)).wait_recv()
                d2d(j, k, c).start()

    def finish(*refs):
        x, y, c, d2d_own, ici, d2d = copies(*refs)
        for k in range(nw):
            d2d_own(k, 1 - c).wait_recv()
        for j in range(3):
            for k in range(nw):
                d2d(j, k, 1 - c).wait_recv()
        for k in range(nw):
            d2d_own(k, c).wait_send()
        for j, (cx, cy) in enumerate(_other_chips(x, y)):
            for k in range(nw):
                ici(j, k, 2 * cx + cy, (cx, cy, c)).wait_send()
                d2d(j, k, c).wait_send()

    own_shape = [jax.ShapeDtypeStruct(o.shape, F32) for o in own]
    got_shape = [jax.ShapeDtypeStruct((3,) + o.shape, BF16) for o in own]
    aliases = {k: k for k in range(nw)}
    if got is not None:
        aliases.update({2 * nw + k: nw + k for k in range(nw)})
    return _Phase(list(own) + list(pb) + list(got or []), own_shape + got_shape, aliases, 7 * nw, start, finish, relay)


def _gather_small(part):
    def body(p_ref, o_ref, send_sems, recv_sems, loc_sem):
        x, y, c = _position()
        me = 4 * x + 2 * y + c
        mine = pltpu.make_async_copy(p_ref, o_ref.at[me], loc_sem)
        mine.start()
        sent = []
        for mask in range(1, N_DEV):
            dx, dy, dc = (mask >> 2) & 1, (mask >> 1) & 1, mask & 1
            px, py, pc = x ^ dx, y ^ dy, c ^ dc
            cp = pltpu.make_async_remote_copy(
                src_ref=p_ref, dst_ref=o_ref.at[me], send_sem=send_sems.at[mask - 1],
                recv_sem=recv_sems.at[mask - 1], device_id=(px, py, pc), device_id_type=MESH)
            cp.start()
            sent.append((cp, 4 * px + 2 * py + pc))
        for mask in range(1, N_DEV):
            cp, peer = sent[mask - 1]
            pltpu.make_async_remote_copy(
                src_ref=p_ref, dst_ref=o_ref.at[peer], send_sem=send_sems.at[mask - 1],
                recv_sem=recv_sems.at[mask - 1], device_id=(x, y, c), device_id_type=MESH).wait_recv()
        for cp, _ in sent:
            cp.wait_send()
        mine.wait()

    return pl.pallas_call(
        body, name="gather_small_grads", in_specs=[ANY], out_specs=ANY,
        out_shape=jax.ShapeDtypeStruct((N_DEV,) + part.shape, F32),
        scratch_shapes=[pltpu.SemaphoreType.DMA((N_DEV - 1,)), pltpu.SemaphoreType.DMA((N_DEV - 1,)),
                        pltpu.SemaphoreType.DMA(())],
    )(part)


def _sum_slots(name, parts):
    s, r, c = parts.shape
    tr = _tile(r, 512, 8)

    def body(p_ref, o_ref):
        acc = p_ref[0]
        for q in range(1, s):
            acc = acc + p_ref[q]
        o_ref[...] = acc

    return pl.pallas_call(
        body, name=name, grid=(r // tr,), in_specs=[pl.BlockSpec((s, tr, c), lambda i: (0, i, 0))],
        out_specs=pl.BlockSpec((tr, c), lambda i: (i, 0)), out_shape=jax.ShapeDtypeStruct((r, c), F32),
        compiler_params=_params("parallel"))(parts)


def _adamw_math(w, g, m, v):
    m = ADAM_B1 * m + (1.0 - ADAM_B1) * g
    v = ADAM_B2 * v + (1.0 - ADAM_B2) * (g * g)
    m_hat = m / (1.0 - ADAM_B1 ** ADAM_STEP)
    v_hat = v / (1.0 - ADAM_B2 ** ADAM_STEP)
    delta = -ADAM_LR * (m_hat / (jnp.sqrt(v_hat) + ADAM_EPS) + ADAM_WD * w)
    return delta, m, v


def _adamw(name, w, m, v, own, got=None, odd=None, into=None):
    r, c = w.shape
    rows = r if odd is None else r // 2
    tr, tc = _tile(r // 4, 256, 8) if odd is not None else _tile(r, 256, 8), _tile(c, 1024)
    bpq = (r // 4) // tr if odd is not None else 1
    pick = (lambda i: i) if odd is None else (lambda i: (2 * (i // bpq) + odd) * bpq + i % bpq)
    n_in = 4 + (0 if got is None else 1)

    def body(*refs):
        w_ref, m_ref, v_ref, own_ref = refs[:4]
        outs = refs[n_in + (0 if into is None else 4):]
        g = own_ref[...]
        if got is not None:
            for j in range(3):
                g = g + refs[4][j].astype(F32)
        delta, mn, vn = _adamw_math(w_ref[...], g, m_ref[...], v_ref[...])
        outs[0][...] = g
        outs[1][...] = delta
        outs[2][...] = mn
        outs[3][...] = vn

    full = pl.BlockSpec((tr, tc), lambda i, j: (pick(i), j))
    ins, args = [full] * 3 + [pl.BlockSpec((tr, tc), lambda i, j: (i, j))], [w, m, v, own]
    if got is not None:
        ins.append(pl.BlockSpec((3, tr, tc), lambda i, j: (0, i, j)))
        args.append(got)
    aliases = {}
    if into is not None:
        ins += [ANY] * 4
        args += list(into)
        aliases = {n_in + q: q for q in range(4)}
    return pl.pallas_call(
        body, name=name, grid=(rows // tr, c // tc), in_specs=ins, out_specs=[full] * 4,
        out_shape=[jax.ShapeDtypeStruct((r, c), F32)] * 4, input_output_aliases=aliases,
        compiler_params=_params("parallel", "parallel"))(*args)


def _pack(parts):
    return jnp.concatenate([p.reshape(-1) for p in parts]).reshape(-1, LANES)


def _unpack(packed, shapes):
    flat = packed.reshape(-1)
    out, off = [], 0
    for s in shapes:
        n = 1
        for d in s:
            n *= d
        out.append(flat[off:off + n].reshape(s))
        off += n
    return out


def kernel(x, pre_norm, post_norm, a_w_in, a_ln_g, a_ln_b, a_w_s, a_b_s, a_w_out, b_w_in, b_w_grp, b_b_grp, b_scale, b_w_out, loss_target, m_pre_norm, m_post_norm, m_a_w_in, m_a_ln_g, m_a_ln_b, m_a_w_s, m_a_b_s, m_a_w_out, m_b_w_in, m_b_w_grp, m_b_b_grp, m_b_scale, m_b_w_out, v_pre_norm, v_post_norm, v_a_w_in, v_a_ln_g, v_a_ln_b, v_a_w_s, v_a_b_s, v_a_w_out, v_b_w_in, v_b_w_grp, v_b_b_grp, v_b_scale, v_b_w_out):
    nb, seq, d = x.shape
    t = nb * seq
    e = a_ln_g.shape[1]
    nh, chunk = a_w_s.shape[1], a_w_s.shape[2]
    ng, rs, gw = b_w_grp.shape[1], b_w_grp.shape[2], b_w_grp.shape[3]
    wa, wb = a_w_in.shape[2], b_w_in.shape[2]
    cx, cy = lax.axis_index("x"), lax.axis_index("y")
    chip = 2 * cx + cy

    big_w = [a_w_in.reshape(d, wa), a_w_out.reshape(e // N_CHIPS, d), b_w_in.reshape(d, wb),
             b_w_grp.reshape(ng * rs, gw), b_w_out.reshape(e // N_CHIPS, d)]
    big_m = [m_a_w_in.reshape(d, wa), m_a_w_out.reshape(e // N_CHIPS, d), m_b_w_in.reshape(d, wb),
             m_b_w_grp.reshape(ng * rs, gw), m_b_w_out.reshape(e // N_CHIPS, d)]
    big_v = [v_a_w_in.reshape(d, wa), v_a_w_out.reshape(e // N_CHIPS, d), v_b_w_in.reshape(d, wb),
             v_b_w_grp.reshape(ng * rs, gw), v_b_w_out.reshape(e // N_CHIPS, d)]
    names = ["a_w_in", "a_w_out", "b_w_in", "b_w_grp", "b_w_out"]
    place = jnp.stack([chip, lax.axis_index("c")]).astype(jnp.int32)
    slots = [_cast_into_slot("cast_" + n, w, place) for n, w in zip(names, big_w)]
    small_w = jnp.concatenate([b_b_grp.reshape(ng, rs), b_scale.reshape(ng, rs)], axis=0)
    wa_g, gsmall = _comm_only("gather_a_w_in", [_relayed_gather_phase(slots[:1], small_w)])
    b_grp_full = jnp.transpose(gsmall[:, :ng, :], (1, 0, 2)).reshape(1, e)
    scale_full = gsmall[:, ng:, :].reshape(1, e)

    xf = x.reshape(t, d)
    tgt = loss_target.reshape(t, d)
    g_pre0, g_pre1 = pre_norm[0:1], pre_norm[1:2]
    g_post0, g_post1 = post_norm[0:1], post_norm[1:2]
    w_s = a_w_s.reshape(nh, chunk, chunk)
    b_s3 = a_b_s.reshape(nh, chunk, 1)

    h0 = _rms_fwd("pre_norm0", xf, g_pre0)
    pa, (wao_g, wb_half) = _mm_nn_cols(
        "a_in_proj", h0, wa_g, [_gather_phase(slots[1:2]), _gather_phase(slots[2:3], part=(0, 4, 3))])
    wao_g = wao_g.reshape(e, d)
    mu, rstd = _a_stats("a_ln_stats", pa, e)
    y0 = _a_gate_fwd("a_gate", pa, mu, rstd, a_ln_g, a_ln_b, w_s, b_s3)
    m0, (wb_g,) = _mm_nn("a_out_proj", y0, wao_g, phases=[_gather_phase([wb_half], part=(3, 4))])
    x1, h1 = _post_pre_fwd("post0_pre1", xf, m0, g_post0, g_pre1)
    pb, (wg_g, wbo_g) = _mm_nn_cols("b_in_proj", h1, wb_g, [_gather_phase(slots[3:5])])
    wg_g = wg_g.reshape(N_CHIPS, ng, rs, gw)
    wbo_g = wbo_g.reshape(e, d)
    pooled = _pool("b_pool", pb, e, seq, backward=False)
    mm = _grp_fwd("b_grp_proj", pooled, wg_g)
    y1 = _b_gate_fwd("b_gate", mm, pb, b_grp_full, scale_full)
    m1 = _mm_nn("b_out_proj", y1, wbo_g)
    loss, dx2, dm1, dg_post1 = _loss_post_bwd("loss_post1_bwd", x1, m1, g_post1, tgt)

    def chip_sum(n, g, theirs):
        pbk, ownk = _add_halves("chip_sum_" + n, g, theirs, place)
        return [ownk], [pbk]

    reduced = {}
    dy1 = _mm_nt("b_out_dx", dm1, wbo_g)
    g_wbo = _mm_tn("b_out_dw", y1, dm1).reshape(N_CHIPS, e // N_CHIPS, d)
    dmm, dzb, db_grp, dscale = _b_gate_bwd("b_gate_bwd", dy1, mm, pb, b_grp_full, scale_full)
    dpooled, (th,) = _grp_bwd_x("b_grp_dx", dmm, wg_g, [_swap_phase([g_wbo])])
    s_bo = chip_sum("b_w_out", g_wbo, th)
    g_wg = _grp_bwd_w("b_grp_dw", pooled, dmm, ng).reshape(N_CHIPS, ng * rs, gw)
    dp = _pool("b_pool_bwd", dpooled, e, seq, backward=True)
    dpb = jnp.concatenate([dp, dzb], axis=1)
    dh1, (own, got, th) = _mm_nt_cols("b_in_dx", dpb, wb_g, [_scatter_phase(*s_bo), _swap_phase([g_wg])])
    reduced["b_w_out"] = (own, got)
    s_g = chip_sum("b_w_grp", g_wg, th)
    g_wb, reduced["b_w_grp"] = _mm_tn_cols("b_in_dw", h1, dpb, wb, [_scatter_phase(*s_g)])
    dx1, dm0, dg_pre1, dg_post0 = _mid_bwd("pre1_post0_bwd", dx2, dh1, x1, g_pre1, m0, g_post0)
    dy0, (th,) = _mm_nt("a_out_dx", dm0, wao_g, phases=[_swap_phase([g_wb])])
    s_b = chip_sum("b_w_in", g_wb, th)
    g_wao, (own, got) = _mm_tn("a_out_dw", y0, dm0, phases=[_scatter_phase(*s_b, part=(0, 4))])
    g_wao = g_wao.reshape(N_CHIPS, e // N_CHIPS, d)
    (du, dz, dsv, c1, c2), (own, got, th) = _a_gate_bwd1(
        "a_gate_bwd1", pa, dy0, mu, rstd, a_ln_g, a_ln_b, w_s, b_s3,
        [_scatter_phase([own], s_b[1], part=(1, 4, 2), got=[got]), _swap_phase([g_wao])])
    s_ao = chip_sum("a_w_out", g_wao, th)
    (dv, dw_s, db_s, dln_g, dln_b), reduced["b_w_in"] = _a_gate_bwd2(
        "a_gate_bwd2", pa, dsv, mu, rstd, c1, c2, a_ln_g, a_ln_b, w_s,
        [_scatter_phase([own], s_b[1], part=(3, 4), got=[got])])
    dpa = jnp.concatenate([du, dv, dz], axis=1)
    g_lo, reduced["a_w_out"] = _mm_tn_cols("a_in_dw_even", h0, dpa, wa, [_scatter_phase(*s_ao)], odd=0)
    g_hi, (th,) = _mm_tn_cols("a_in_dw_odd", h0, dpa, wa, [_swap_phase([g_lo])], odd=1)
    s_lo = chip_sum("a_w_in_even", g_lo, th)
    dh0, (th, own_lo, got_lo) = _mm_nt_cols(
        "a_in_dx_top", dpa, wa_g, [_swap_phase([g_hi]), _scatter_phase(*s_lo)], rows=(0, 2))
    s_hi = chip_sum("a_w_in_odd", g_hi, th)
    dh0, (own_hi, got_hi) = _mm_nt_cols(
        "a_in_dx_bottom", dpa, wa_g, [_scatter_phase(*s_hi)], rows=(1, 2), into=dh0)
    grad_x, dg_pre0 = _pre_bwd("pre0_bwd", dx1, dh0, xf, g_pre0)
    big_out = []
    for k, n in enumerate(names):
        if n == "a_w_in":
            even = _adamw("adamw_a_w_in_even", big_w[k], big_m[k], big_v[k], own_lo, got_lo, odd=0)
            big_out.append(_adamw("adamw_a_w_in_odd", big_w[k], big_m[k], big_v[k], own_hi, got_hi, odd=1, into=even))
        else:
            big_out.append(_adamw("adamw_" + n, big_w[k], big_m[k], big_v[k], *reduced[n]))

    small_shapes = [(2, d), (2, d), (1, e), (1, e), (1, nh, chunk, chunk), (1, nh, chunk), (1, e), (1, e)]
    part = _pack([jnp.concatenate([dg_pre0, dg_pre1], axis=0), jnp.concatenate([dg_post0, dg_post1], axis=0),
                  dln_g, dln_b, dw_s, db_s, db_grp, dscale])
    g_small = _sum_slots("sum_small_grads", _gather_small(part))
    g_pre, g_post, g_lng, g_lnb, g_ws, g_bs, g_bgrp_full, g_scale_full = _unpack(g_small, small_shapes)
    g_bgrp = lax.dynamic_slice_in_dim(g_bgrp_full.reshape(ng, N_CHIPS, rs), chip, 1, axis=1).reshape(1, ng, rs)
    g_scale = lax.dynamic_slice_in_dim(g_scale_full.reshape(N_CHIPS, gw), chip, 1, axis=0)
    small_names = ["pre_norm", "post_norm", "a_ln_g", "a_ln_b", "a_w_s", "a_b_s", "b_b_grp", "b_scale"]
    small_g = [g_pre, g_post, g_lng, g_lnb, g_ws, g_bs, g_bgrp, g_scale]
    small_ws = [pre_norm, post_norm, a_ln_g, a_ln_b, a_w_s, a_b_s, b_b_grp, b_scale]
    small_ms = [m_pre_norm, m_post_norm, m_a_ln_g, m_a_ln_b, m_a_w_s, m_a_b_s, m_b_b_grp, m_b_scale]
    small_vs = [v_pre_norm, v_post_norm, v_a_ln_g, v_a_ln_b, v_a_w_s, v_a_b_s, v_b_b_grp, v_b_scale]
    packed = _adamw("adamw_small", _pack(small_ws), _pack(small_ms), _pack(small_vs), _pack(small_g))
    small_out = [_unpack(p, [w.shape for w in small_ws]) for p in packed]

    loss = lax.psum(loss[0, 0], ("x", "y", "c"))
    order = ["pre_norm", "post_norm", "a_w_in", "a_ln_g", "a_ln_b", "a_w_s", "a_b_s", "a_w_out", "b_w_in",
             "b_w_grp", "b_b_grp", "b_scale", "b_w_out"]
    big_shapes = dict(zip(names, [a_w_in.shape, a_w_out.shape, b_w_in.shape, b_w_grp.shape, b_w_out.shape]))
    outs = [loss, grad_x.reshape(nb, seq, d)]
    for kind in range(4):
        for n in order:
            if n in big_shapes:
                outs.append(big_out[names.index(n)][kind].reshape(big_shapes[n]))
            else:
                outs.append(small_out[kind][small_names.index(n)])
    return tuple(outs)
```

```python
import functools

import jax
import jax.numpy as jnp
from jax import lax
from jax.experimental import pallas as pl
from jax.experimental.pallas import tpu as pltpu

F32 = jnp.float32
BF16 = jnp.bfloat16
NORM_EPS = 1e-6
POOL_WINDOWS = (2, 4, 8, 16)
ADAM_LR = 0.001
ADAM_B1 = 0.9
ADAM_B2 = 0.999
ADAM_EPS = 1e-08
ADAM_WD = 0.01
ADAM_STEP = 10
N_CHIPS = 4
N_DEV = 8
V7X_VMEM_LIMIT_BYTES = 56 * 1024 * 1024
LANES = 128
MESH = pl.DeviceIdType.MESH
ANY = pl.BlockSpec(memory_space=pl.ANY)
SQRT_HALF = 0.7071067811865476
INV_SQRT_2PI = 0.3989422804014327


def _tile(dim, pref, mult=LANES):
    if dim <= pref:
        return dim
    t = (pref // mult) * mult
    while t >= mult:
        if dim % t == 0:
            return t
        t -= mult
    return dim


def _params(*sem):
    return pltpu.CompilerParams(dimension_semantics=sem or None, vmem_limit_bytes=V7X_VMEM_LIMIT_BYTES)


def _gelu(x):
    cdf = 0.5 * (1.0 + lax.erf(x * SQRT_HALF))
    pdf = jnp.exp(-0.5 * x * x) * INV_SQRT_2PI
    return x * cdf, cdf + x * pdf


def _silu(z):
    s = 1.0 / (1.0 + jnp.exp(-z))
    return z * s, s * (1.0 + z * (1.0 - s))


class _Phase:
    def __init__(self, ins, outs, aliases, n_sem, start, finish, relay=None):
        self.ins, self.outs, self.aliases, self.n_sem = list(ins), list(outs), dict(aliases), n_sem
        self.start, self.finish, self.relay = start, finish, relay


RELAY_AT = 0.85


def _hosted(body, phases, grid, n_in, n_out, n_scratch=0):
    ins, outs, aliases, sems = [], [], {}, []
    for ph in phases:
        for i, o in ph.aliases.items():
            aliases[n_in + len(ins) + i] = n_out + len(outs) + o
        ins += ph.ins
        outs += ph.outs
        sems += [pltpu.SemaphoreType.DMA((ph.n_sem,)), pltpu.SemaphoreType.DMA((ph.n_sem,))]
    if not phases:
        return body, ins, outs, aliases, sems
    steps = 1
    for n in grid:
        steps *= n
    relay_at = min(steps - 1, int(RELAY_AT * steps))

    def wrapped(*refs):
        own_in, refs = refs[:n_in], refs[n_in:]
        in_refs, refs = refs[:len(ins)], refs[len(ins):]
        own_out, refs = refs[:n_out], refs[n_out:]
        out_refs, refs = refs[:len(outs)], refs[len(outs):]
        own_scratch, sem_refs = refs[:n_scratch], refs[n_scratch:]

        def run(which):
            i = o = 0
            for p, ph in enumerate(phases):
                fn = getattr(ph, which)
                if fn is not None:
                    fn(in_refs[i:i + len(ph.ins)], out_refs[o:o + len(ph.outs)], sem_refs[2 * p], sem_refs[2 * p + 1])
                i += len(ph.ins)
                o += len(ph.outs)

        if not grid:
            run("start")
            run("relay")
            run("finish")
            return
        step = 0
        for d, n in enumerate(grid):
            step = step * n + pl.program_id(d)
        pl.when(step == 0)(lambda: run("start"))
        body(*own_in, *own_out, *own_scratch)
        pl.when(step == relay_at)(lambda: run("relay"))
        pl.when(step == steps - 1)(lambda: run("finish"))

    return wrapped, ins, outs, aliases, sems


def _comm_only(name, phases):
    body, ins, outs, aliases, sems = _hosted(None, phases, (), 0, 0)
    return pl.pallas_call(
        body, name=name, in_specs=[ANY] * len(ins), out_specs=[ANY] * len(outs), out_shape=outs,
        input_output_aliases=aliases, scratch_shapes=sems)(*ins)


def _matmul(name, a, b, *, dims, grid, a_spec, b_spec, o_spec, out_shape, out_dtype, phases=(), into=None,
            prefetch=None):
    nk = grid[2]
    contract = {"nn": ((1,), (0,)), "nt": ((1,), (1,)), "tn": ((0,), (0,))}[dims]
    acc_in_out = out_dtype == F32 or nk == 1
    blk = tuple(d for d in o_spec.block_shape if d is not None)
    first = 0 if prefetch is None else 1
    n_in = first + (2 if into is None else 3)

    def body(*refs):
        a_ref, b_ref, o_ref = refs[first], refs[first + 1], refs[n_in]
        part = lax.dot_general(a_ref[...], b_ref[...], (contract, ((), ())), preferred_element_type=F32)
        if nk == 1:
            o_ref[...] = part.astype(out_dtype)
            return
        acc = o_ref if acc_in_out else refs[n_in + 1]
        k = pl.program_id(2)

        @pl.when(k == 0)
        def _():
            acc[...] = part

        @pl.when(k > 0)
        def _():
            acc[...] += part

        if not acc_in_out:

            @pl.when(k == nk - 1)
            def _():
                o_ref[...] = acc[...].astype(out_dtype)

    own_scratch = [] if acc_in_out else [pltpu.VMEM(blk, F32)]
    body, x_ins, x_outs, aliases, sems = _hosted(body, phases, grid, n_in, 1, len(own_scratch))
    if into is not None:
        aliases[first + 2] = 0
    sem = ("arbitrary",) * 3 if phases else ("parallel", "parallel", "arbitrary")
    in_specs = [a_spec, b_spec] + ([] if into is None else [ANY]) + [ANY] * len(x_ins)
    out_specs = [o_spec] + [ANY] * len(x_outs)
    if prefetch is None:
        layout = dict(grid=grid, in_specs=in_specs, out_specs=out_specs, scratch_shapes=own_scratch + sems)
    else:
        layout = dict(grid_spec=pltpu.PrefetchScalarGridSpec(
            num_scalar_prefetch=1, grid=grid, in_specs=in_specs, out_specs=out_specs,
            scratch_shapes=own_scratch + sems))
    res = pl.pallas_call(
        body,
        name=name,
        out_shape=[jax.ShapeDtypeStruct(out_shape, out_dtype)] + x_outs,
        input_output_aliases=aliases,
        compiler_params=_params(*sem),
        **layout,
    )(*([] if prefetch is None else [prefetch]), a, b, *([] if into is None else [into]), *x_ins)
    return (res[0], list(res[1:])) if phases else res[0]


MM_K_WHOLE = 4096


def _tiles(m, n, k, k_total=None):
    if (k if k_total is None else k_total) <= MM_K_WHOLE:
        return _tile(m, 1024), _tile(n, 1024), k
    return _tile(m, 1024), _tile(n, 2048), _tile(k, 2048)


def _mm_nn(name, a, b, out_dtype=F32, phases=()):
    m, k = a.shape
    n = b.shape[1]
    tm, tn, tk = _tiles(m, n, k)
    return _matmul(
        name, a, b, dims="nn", grid=(m // tm, n // tn, k // tk),
        a_spec=pl.BlockSpec((tm, tk), lambda i, j, l: (i, l)),
        b_spec=pl.BlockSpec((tk, tn), lambda i, j, l: (l, j)),
        o_spec=pl.BlockSpec((tm, tn), lambda i, j, l: (i, j)),
        out_shape=(m, n), out_dtype=out_dtype, phases=phases)


def _mm_nt(name, a, b, out_dtype=F32, phases=()):
    m, k = a.shape
    n = b.shape[0]
    tm, tn, tk = _tiles(m, n, k)
    return _matmul(
        name, a, b, dims="nt", grid=(m // tm, n // tn, k // tk),
        a_spec=pl.BlockSpec((tm, tk), lambda i, j, l: (i, l)),
        b_spec=pl.BlockSpec((tn, tk), lambda i, j, l: (j, l)),
        o_spec=pl.BlockSpec((tm, tn), lambda i, j, l: (i, j)),
        out_shape=(m, n), out_dtype=out_dtype, phases=phases)


def _mm_tn(name, a, b, out_dtype=F32, phases=()):
    k, m = a.shape
    n = b.shape[1]
    tm, tn, tk = _tiles(m, n, k)
    return _matmul(
        name, a, b, dims="tn", grid=(m // tm, n // tn, k // tk),
        a_spec=pl.BlockSpec((tk, tm), lambda i, j, l: (l, i)),
        b_spec=pl.BlockSpec((tk, tn), lambda i, j, l: (l, j)),
        o_spec=pl.BlockSpec((tm, tn), lambda i, j, l: (i, j)),
        out_shape=(m, n), out_dtype=out_dtype, phases=phases)


def _mm_nn_cols(name, a, wg, phases=(), order=None, span=None, into=None):
    m, k = a.shape
    ws = wg.shape[2]
    tm, tn, tk = _tiles(m, ws, k)
    npb = ws // tn
    if order is None:
        return _matmul(
            name, a, wg, dims="nn", grid=(m // tm, N_CHIPS * npb, k // tk),
            a_spec=pl.BlockSpec((tm, tk), lambda i, j, l: (i, l)),
            b_spec=pl.BlockSpec((None, tk, tn), lambda i, j, l: (j // npb, l, j % npb)),
            o_spec=pl.BlockSpec((tm, tn), lambda i, j, l: (i, j)),
            out_shape=(m, N_CHIPS * ws), out_dtype=F32, phases=phases)
    first, count = span
    return _matmul(
        name, a, wg, dims="nn", grid=(m // tm, count * npb, k // tk),
        a_spec=pl.BlockSpec((tm, tk), lambda i, j, l, o: (i, l)),
        b_spec=pl.BlockSpec((None, tk, tn), lambda i, j, l, o: (o[first + j // npb], l, j % npb)),
        o_spec=pl.BlockSpec((tm, tn), lambda i, j, l, o: (i, o[first + j // npb] * npb + j % npb)),
        out_shape=(m, N_CHIPS * ws), out_dtype=F32, phases=phases, into=into, prefetch=order)


def _mm_nt_cols(name, a, wg, phases=(), rows=None, into=None):
    m = a.shape[0]
    n, ws = wg.shape[1], wg.shape[2]
    q, nq = rows or (0, 1)
    tm, tn, tk = _tiles(m // nq, n, ws, N_CHIPS * ws)
    kpb, off = ws // tk, q * (m // nq // tm)
    return _matmul(
        name, a, wg, dims="nt", grid=(m // nq // tm, n // tn, N_CHIPS * kpb),
        a_spec=pl.BlockSpec((tm, tk), lambda i, j, l: (i + off, l)),
        b_spec=pl.BlockSpec((None, tn, tk), lambda i, j, l: (l // kpb, j, l % kpb)),
        o_spec=pl.BlockSpec((tm, tn), lambda i, j, l: (i + off, j)),
        out_shape=(m, n), out_dtype=F32, phases=phases, into=into)


def _mm_tn_cols(name, a, b, ws, phases=(), odd=None):
    k, m = a.shape
    rows = m if odd is None else m // 2
    tm, tn, tk = _tiles(m // 4, ws, k)
    npb, bpq = ws // tn, (m // 4) // tm
    pick = (lambda i: i) if odd is None else (lambda i: (2 * (i // bpq) + odd) * bpq + i % bpq)
    return _matmul(
        name, a, b, dims="tn", grid=(rows // tm, N_CHIPS * npb, k // tk),
        a_spec=pl.BlockSpec((tk, tm), lambda i, j, l: (l, pick(i))),
        b_spec=pl.BlockSpec((tk, tn), lambda i, j, l: (l, j)),
        o_spec=pl.BlockSpec((None, tm, tn), lambda i, j, l: (j // npb, i, j % npb)),
        out_shape=(N_CHIPS, rows, ws), out_dtype=F32, phases=phases)


def _grp_fwd(name, pooled, wgg):
    t, e = pooled.shape
    _, ng, rs, gw = wgg.shape
    tm, tn, tk = _tiles(t, gw, rs, gw)
    npb, kps = gw // tn, rs // tk
    return _matmul(
        name, pooled, wgg, dims="nn", grid=(t // tm, ng * npb, N_CHIPS * kps),
        a_spec=pl.BlockSpec((tm, tk), lambda i, j, l: (i, (j // npb) * (gw // tk) + l)),
        b_spec=pl.BlockSpec((None, None, tk, tn), lambda i, j, l: (l // kps, j // npb, l % kps, j % npb)),
        o_spec=pl.BlockSpec((tm, tn), lambda i, j, l: (i, j)),
        out_shape=(t, e), out_dtype=F32)


def _grp_bwd_x(name, dmm, wgg, phases=()):
    t, e = dmm.shape
    _, ng, rs, gw = wgg.shape
    tm, tn, tk = _tiles(t, rs, gw)
    npr, kpg = rs // tn, gw // tk
    return _matmul(
        name, dmm, wgg, dims="nt", grid=(t // tm, ng * N_CHIPS * npr, kpg),
        a_spec=pl.BlockSpec((tm, tk), lambda i, j, l: (i, (j // (N_CHIPS * npr)) * kpg + l)),
        b_spec=pl.BlockSpec(
            (None, None, tn, tk),
            lambda i, j, l: ((j % (N_CHIPS * npr)) // npr, j // (N_CHIPS * npr), j % npr, l)),
        o_spec=pl.BlockSpec((tm, tn), lambda i, j, l: (i, j)),
        out_shape=(t, e), out_dtype=F32, phases=phases)


def _grp_bwd_w(name, pooled, dmm, ng):
    t, e = pooled.shape
    gw = e // ng
    rs = gw // N_CHIPS
    _, tn, tk = _tiles(rs, gw, t)
    npb = gw // tn
    return _matmul(
        name, pooled, dmm, dims="tn", grid=(ng * N_CHIPS, npb, t // tk),
        a_spec=pl.BlockSpec((tk, rs), lambda i, j, l: (l, i)),
        b_spec=pl.BlockSpec((tk, tn), lambda i, j, l: (l, (i // N_CHIPS) * npb + j)),
        o_spec=pl.BlockSpec((None, None, rs, tn), lambda i, j, l: (i % N_CHIPS, i // N_CHIPS, 0, j)),
        out_shape=(N_CHIPS, ng, rs, gw), out_dtype=F32)


def _cast_into_slot(name, w, place):
    r, c = w.shape
    tr, tc = _tile(r, 512), _tile(c, 2048)

    def body(place_ref, w_ref, o_ref):
        o_ref[...] = w_ref[...].astype(BF16)

    return pl.pallas_call(
        body, name=name,
        grid_spec=pltpu.PrefetchScalarGridSpec(
            num_scalar_prefetch=1, grid=(r // tr, c // tc),
            in_specs=[pl.BlockSpec((tr, tc), lambda i, j, p: (i, j))],
            out_specs=pl.BlockSpec((None, tr, tc), lambda i, j, p: (p[0], i, j))),
        out_shape=jax.ShapeDtypeStruct((N_CHIPS, r, c), BF16),
        compiler_params=_params("parallel", "parallel"))(place, w)


def _rms(v):
    return lax.rsqrt(jnp.mean(v * v, axis=-1, keepdims=True) + NORM_EPS)


def _rms_fwd(name, x, g):
    t, d = x.shape
    tr = _tile(t, 256)

    def body(x_ref, g_ref, h_ref):
        xv = x_ref[...]
        h_ref[...] = (xv * _rms(xv) * g_ref[...]).astype(BF16)

    row = pl.BlockSpec((tr, d), lambda i: (i, 0))
    vec = pl.BlockSpec((1, d), lambda i: (0, 0))
    return pl.pallas_call(
        body, name=name, grid=(t // tr,), in_specs=[row, vec], out_specs=row,
        out_shape=jax.ShapeDtypeStruct((t, d), BF16), compiler_params=_params("parallel"))(x, g)


def _post_pre_fwd(name, x, m, g_post, g_pre):
    t, d = x.shape
    tr = _tile(t, 256)

    def body(x_ref, m_ref, gp_ref, gn_ref, x1_ref, h_ref):
        mv = m_ref[...]
        x1 = x_ref[...] + mv * _rms(mv) * gp_ref[...]
        x1_ref[...] = x1
        h_ref[...] = (x1 * _rms(x1) * gn_ref[...]).astype(BF16)

    row = pl.BlockSpec((tr, d), lambda i: (i, 0))
    vec = pl.BlockSpec((1, d), lambda i: (0, 0))
    return pl.pallas_call(
        body, name=name, grid=(t // tr,), in_specs=[row, row, vec, vec], out_specs=[row, row],
        out_shape=[jax.ShapeDtypeStruct((t, d), F32), jax.ShapeDtypeStruct((t, d), BF16)],
        compiler_params=_params("parallel"))(x, m, g_post, g_pre)


def _norm_bwd(dout, nrm, r, g):
    gd = dout * g
    return r * (gd - nrm * jnp.mean(gd * nrm, axis=-1, keepdims=True))


def _loss_post_bwd(name, x1, m, g_post, target):
    t, d = x1.shape
    tr = _tile(t, 128)

    def body(x_ref, m_ref, g_ref, t_ref, loss_ref, dx_ref, dm_ref, dg_ref):
        i = pl.program_id(0)
        mv = m_ref[...]
        r = _rms(mv)
        nrm = mv * r
        err = x_ref[...] + nrm * g_ref[...] - t_ref[...]
        part = 0.5 * jnp.sum(jnp.mean(err * err, axis=-1, keepdims=True), axis=0, keepdims=True)
        dx = err / d
        dx_ref[...] = dx
        dm_ref[...] = _norm_bwd(dx, nrm, r, g_ref[...]).astype(BF16)
        dg = jnp.sum(dx * nrm, axis=0, keepdims=True)

        @pl.when(i == 0)
        def _():
            loss_ref[...] = part
            dg_ref[...] = dg

        @pl.when(i > 0)
        def _():
            loss_ref[...] += part
            dg_ref[...] += dg

    row = pl.BlockSpec((tr, d), lambda i: (i, 0))
    vec = pl.BlockSpec((1, d), lambda i: (0, 0))
    one = pl.BlockSpec((1, 1), lambda i: (0, 0))
    return pl.pallas_call(
        body, name=name, grid=(t // tr,), in_specs=[row, row, vec, row], out_specs=[one, row, row, vec],
        out_shape=[jax.ShapeDtypeStruct((1, 1), F32), jax.ShapeDtypeStruct((t, d), F32),
                   jax.ShapeDtypeStruct((t, d), BF16), jax.ShapeDtypeStruct((1, d), F32)],
        compiler_params=_params("arbitrary"))(x1, m, g_post, target)


def _mid_bwd(name, dx2, dh1, x1, g_pre, m0, g_post):
    t, d = x1.shape
    tr = _tile(t, 128)

    def body(dx2_ref, dh_ref, x_ref, gn_ref, m_ref, gp_ref, dx_ref, dm_ref, dgn_ref, dgp_ref):
        i = pl.program_id(0)
        xv = x_ref[...]
        r1 = _rms(xv)
        n1 = xv * r1
        dh = dh_ref[...]
        dx = dx2_ref[...] + _norm_bwd(dh, n1, r1, gn_ref[...])
        dx_ref[...] = dx
        mv = m_ref[...]
        r0 = _rms(mv)
        n0 = mv * r0
        dm_ref[...] = _norm_bwd(dx, n0, r0, gp_ref[...]).astype(BF16)
        dgn = jnp.sum(dh * n1, axis=0, keepdims=True)
        dgp = jnp.sum(dx * n0, axis=0, keepdims=True)

        @pl.when(i == 0)
        def _():
            dgn_ref[...] = dgn
            dgp_ref[...] = dgp

        @pl.when(i > 0)
        def _():
            dgn_ref[...] += dgn
            dgp_ref[...] += dgp

    row = pl.BlockSpec((tr, d), lambda i: (i, 0))
    vec = pl.BlockSpec((1, d), lambda i: (0, 0))
    return pl.pallas_call(
        body, name=name, grid=(t // tr,), in_specs=[row, row, row, vec, row, vec],
        out_specs=[row, row, vec, vec],
        out_shape=[jax.ShapeDtypeStruct((t, d), F32), jax.ShapeDtypeStruct((t, d), BF16),
                   jax.ShapeDtypeStruct((1, d), F32), jax.ShapeDtypeStruct((1, d), F32)],
        compiler_params=_params("arbitrary"))(dx2, dh1, x1, g_pre, m0, g_post)


def _pre_bwd(name, dx1, dh0, x, g_pre):
    t, d = x.shape
    tr = _tile(t, 128)

    def body(dx1_ref, dh_ref, x_ref, g_ref, dx_ref, dg_ref):
        i = pl.program_id(0)
        xv = x_ref[...]
        r = _rms(xv)
        nrm = xv * r
        dh = dh_ref[...]
        dx_ref[...] = dx1_ref[...] + _norm_bwd(dh, nrm, r, g_ref[...])
        dg = jnp.sum(dh * nrm, axis=0, keepdims=True)

        @pl.when(i == 0)
        def _():
            dg_ref[...] = dg

        @pl.when(i > 0)
        def _():
            dg_ref[...] += dg

    row = pl.BlockSpec((tr, d), lambda i: (i, 0))
    vec = pl.BlockSpec((1, d), lambda i: (0, 0))
    return pl.pallas_call(
        body, name=name, grid=(t // tr,), in_specs=[row, row, row, vec], out_specs=[row, vec],
        out_shape=[jax.ShapeDtypeStruct((t, d), F32), jax.ShapeDtypeStruct((1, d), F32)],
        compiler_params=_params("arbitrary"))(dx1, dh0, x, g_pre)


def _a_stats(name, pa, e):
    t = pa.shape[0]
    tr = _tile(t, 64, 8)

    def body(v_ref, mu_ref, rs_ref):
        vg, _ = _gelu(v_ref[...])
        mu = jnp.mean(vg, axis=-1, keepdims=True)
        xc = vg - mu
        mu_ref[...] = mu
        rs_ref[...] = lax.rsqrt(jnp.mean(xc * xc, axis=-1, keepdims=True) + NORM_EPS)

    col = pl.BlockSpec((tr, 1), lambda i: (i, 0))
    return pl.pallas_call(
        body, name=name, grid=(t // tr,), in_specs=[pl.BlockSpec((tr, e), lambda i: (i, 1))],
        out_specs=[col, col],
        out_shape=[jax.ShapeDtypeStruct((t, 1), F32), jax.ShapeDtypeStruct((t, 1), F32)],
        compiler_params=_params("parallel"))(pa)


def _causal(w):
    c = w.shape[0]
    keep = lax.broadcasted_iota(jnp.int32, (c, c), 0) >= lax.broadcasted_iota(jnp.int32, (c, c), 1)
    return jnp.where(keep, w, 0.0), keep


def _a_gate_fwd(name, pa, mu, rs, ln_g, ln_b, w_s, b_s3, phases=()):
    t = pa.shape[0]
    nh, c, _ = w_s.shape
    e = ln_g.shape[1]
    dh = e // nh
    rb = 2 * c if t % (2 * c) == 0 else c

    def body(u_ref, v_ref, z_ref, mu_ref, rs_ref, g_ref, b_ref, w_ref, bs_ref, y_ref):
        wc = _causal(w_ref[...])[0].astype(BF16)
        vg, _ = _gelu(v_ref[...])
        vn = ((vg - mu_ref[...]) * rs_ref[...] * g_ref[...] + b_ref[...]).astype(BF16)
        for ci in range(rb // c):
            rows = pl.ds(ci * c, c)
            sv = jnp.dot(wc, vn[ci * c:(ci + 1) * c], preferred_element_type=F32) + bs_ref[...]
            u, _ = _gelu(u_ref[rows, :])
            sz, _ = _silu(z_ref[rows, :])
            y_ref[rows, :] = (u * sv * sz).astype(BF16)

    blk = lambda off: pl.BlockSpec((rb, dh), lambda i, h: (i, off + h))
    col = pl.BlockSpec((rb, 1), lambda i, h: (i, 0))
    vec = pl.BlockSpec((1, dh), lambda i, h: (0, h))
    grid = (t // rb, nh)
    body, x_ins, x_outs, aliases, sems = _hosted(body, phases, grid, 9, 1)
    res = pl.pallas_call(
        body, name=name, grid=grid,
        in_specs=[blk(0), blk(nh), blk(2 * nh), col, col, vec, vec,
                  pl.BlockSpec((None, c, c), lambda i, h: (h, 0, 0)),
                  pl.BlockSpec((None, c, 1), lambda i, h: (h, 0, 0))] + [ANY] * len(x_ins),
        out_specs=[pl.BlockSpec((rb, dh), lambda i, h: (i, h))] + [ANY] * len(x_outs),
        out_shape=[jax.ShapeDtypeStruct((t, e), BF16)] + x_outs, input_output_aliases=aliases,
        scratch_shapes=sems,
        compiler_params=_params("arbitrary", "arbitrary"))(pa, pa, pa, mu, rs, ln_g, ln_b, w_s, b_s3, *x_ins)
    return (res[0], list(res[1:])) if phases else res[0]


def _a_gate_bwd1(name, pa, dy, mu, rs, ln_g, ln_b, w_s, b_s3, phases=()):
    t = pa.shape[0]
    nh, c, _ = w_s.shape
    e = ln_g.shape[1]
    dh = e // nh
    rb = 2 * c if t % (2 * c) == 0 else c

    def body(u_ref, v_ref, z_ref, dy_ref, mu_ref, rs_ref, g_ref, b_ref, w_ref, bs_ref,
             du_ref, dz_ref, dsv_ref, c1_ref, c2_ref):
        h = pl.program_id(1)
        wc = _causal(w_ref[...])[0].astype(BF16)
        vg, _ = _gelu(v_ref[...])
        xh = (vg - mu_ref[...]) * rs_ref[...]
        vn = (xh * g_ref[...] + b_ref[...]).astype(BF16)
        s1 = []
        s2 = []
        for ci in range(rb // c):
            rows = pl.ds(ci * c, c)
            lo, hi = ci * c, (ci + 1) * c
            sv = jnp.dot(wc, vn[lo:hi], preferred_element_type=F32) + bs_ref[...]
            u, du = _gelu(u_ref[rows, :])
            zv = z_ref[rows, :]
            sz, dsz = _silu(zv)
            dyv = dy_ref[rows, :]
            du_ref[rows, :] = (dyv * sv * sz * du).astype(BF16)
            dz_ref[rows, :] = (dyv * u * sv * dsz).astype(BF16)
            dsv = (dyv * u * sz).astype(BF16)
            dsv_ref[rows, :] = dsv
            dvn = lax.dot_general(wc, dsv, (((0,), (0,)), ((), ())), preferred_element_type=F32)
            dxh = dvn * g_ref[...]
            s1.append(jnp.sum(dxh, axis=-1, keepdims=True))
            s2.append(jnp.sum(dxh * xh[lo:hi], axis=-1, keepdims=True))
        p1 = jnp.concatenate(s1, axis=0)
        p2 = jnp.concatenate(s2, axis=0)

        @pl.when(h == 0)
        def _():
            c1_ref[...] = p1
            c2_ref[...] = p2

        @pl.when(h > 0)
        def _():
            c1_ref[...] += p1
            c2_ref[...] += p2

    blk = lambda off: pl.BlockSpec((rb, dh), lambda i, h: (i, off + h))
    col = pl.BlockSpec((rb, 1), lambda i, h: (i, 0))
    vec = pl.BlockSpec((1, dh), lambda i, h: (0, h))
    act = jax.ShapeDtypeStruct((t, e), BF16)
    stat = jax.ShapeDtypeStruct((t, 1), F32)
    grid = (t // rb, nh)
    body, x_ins, x_outs, aliases, sems = _hosted(body, phases, grid, 10, 5)
    res = pl.pallas_call(
        body, name=name, grid=grid,
        in_specs=[blk(0), blk(nh), blk(2 * nh), blk(0), col, col, vec, vec,
                  pl.BlockSpec((None, c, c), lambda i, h: (h, 0, 0)),
                  pl.BlockSpec((None, c, 1), lambda i, h: (h, 0, 0))] + [ANY] * len(x_ins),
        out_specs=[blk(0), blk(0), blk(0), col, col] + [ANY] * len(x_outs),
        out_shape=[act, act, act, stat, stat] + x_outs, input_output_aliases=aliases, scratch_shapes=sems,
        compiler_params=_params("arbitrary", "arbitrary"))(pa, pa, pa, dy, mu, rs, ln_g, ln_b, w_s, b_s3, *x_ins)
    return (res[:5], list(res[5:])) if phases else res


def _a_gate_bwd2(name, pa, dsv, mu, rs, c1, c2, ln_g, ln_b, w_s, phases=()):
    t = pa.shape[0]
    nh, c, _ = w_s.shape
    e = ln_g.shape[1]
    dh = e // nh
    rb = 2 * c if t % (2 * c) == 0 else c

    def body(v_ref, dsv_ref, mu_ref, rs_ref, c1_ref, c2_ref, g_ref, b_ref, w_ref,
             dv_ref, dw_ref, dbs_ref, dg_ref, db_ref):
        i = pl.program_id(1)
        wcf, keep = _causal(w_ref[...])
        wc = wcf.astype(BF16)
        vg, dvg = _gelu(v_ref[...])
        rsv = rs_ref[...]
        xh = (vg - mu_ref[...]) * rsv
        vn = (xh * g_ref[...] + b_ref[...]).astype(BF16)
        dw = jnp.zeros((c, c), F32)
        dbs = jnp.zeros((c, 1), F32)
        dvns = []
        for ci in range(rb // c):
            lo, hi = ci * c, (ci + 1) * c
            dsv = dsv_ref[pl.ds(lo, c), :]
            dvns.append(lax.dot_general(wc, dsv, (((0,), (0,)), ((), ())), preferred_element_type=F32))
            dw += lax.dot_general(dsv, vn[lo:hi], (((1,), (1,)), ((), ())), preferred_element_type=F32)
            dbs += jnp.sum(dsv.astype(F32), axis=-1, keepdims=True)
        dvn = jnp.concatenate(dvns, axis=0)
        dxh = dvn * g_ref[...]
        dvv = rsv * (dxh - c1_ref[...] * (1.0 / e) - xh * (c2_ref[...] * (1.0 / e)))
        dv_ref[...] = (dvv * dvg).astype(BF16)
        dw = jnp.where(keep, dw, 0.0)
        dg = jnp.sum(dvn * xh, axis=0, keepdims=True)
        db = jnp.sum(dvn, axis=0, keepdims=True)

        @pl.when(i == 0)
        def _():
            dw_ref[...] = dw
            dbs_ref[...] = dbs
            dg_ref[...] = dg
            db_ref[...] = db

        @pl.when(i > 0)
        def _():
            dw_ref[...] += dw
            dbs_ref[...] += dbs
            dg_ref[...] += dg
            db_ref[...] += db

    col = pl.BlockSpec((rb, 1), lambda h, i: (i, 0))
    vec = pl.BlockSpec((1, dh), lambda h, i: (0, h))
    hblk = pl.BlockSpec((rb, dh), lambda h, i: (i, h))
    grid = (nh, t // rb)
    body, x_ins, x_outs, aliases, sems = _hosted(body, phases, grid, 9, 5)
    res = pl.pallas_call(
        body, name=name, grid=grid,
        in_specs=[pl.BlockSpec((rb, dh), lambda h, i: (i, nh + h)), hblk, col, col, col, col, vec, vec,
                  pl.BlockSpec((None, c, c), lambda h, i: (h, 0, 0))] + [ANY] * len(x_ins),
        out_specs=[hblk, pl.BlockSpec((None, c, c), lambda h, i: (h, 0, 0)),
                   pl.BlockSpec((None, c, 1), lambda h, i: (h, 0, 0)), vec, vec] + [ANY] * len(x_outs),
        out_shape=[jax.ShapeDtypeStruct((t, e), BF16), jax.ShapeDtypeStruct((nh, c, c), F32),
                   jax.ShapeDtypeStruct((nh, c, 1), F32), jax.ShapeDtypeStruct((1, e), F32),
                   jax.ShapeDtypeStruct((1, e), F32)] + x_outs,
        input_output_aliases=aliases, scratch_shapes=sems,
        compiler_params=_params("arbitrary", "arbitrary"))(pa, dsv, mu, rs, c1, c2, ln_g, ln_b, w_s, *x_ins)
    return (res[:5], list(res[5:])) if phases else res


def _pool(name, src, e, seq, backward):
    t = src.shape[0]
    ng = len(POOL_WINDOWS)
    gw = e // ng
    cw = _tile(gw, 256)

    def shifted(a, j, pos):
        if backward:
            return jnp.where(pos < seq - j, pltpu.roll(a, seq - j, 0), 0.0)
        return jnp.where(pos >= j, pltpu.roll(a, j, 0), 0.0)

    def body(p_ref, o_ref):
        grp = (pl.program_id(1) * cw) // gw
        pos = lax.broadcasted_iota(jnp.int32, (seq, cw), 0)
        posf = (pos + 1).astype(F32)
        for k, w in enumerate(POOL_WINDOWS):

            @pl.when(grp == k)
            def _(w=w):
                pv = p_ref[...]
                cnt = jnp.minimum(posf, float(w))
                acc = pv / cnt if backward else pv
                j = 1
                while j < w:
                    acc = acc + shifted(acc, j, pos)
                    j *= 2
                out = acc - pv if backward else acc / cnt - pv
                o_ref[...] = out.astype(BF16)

    spec = pl.BlockSpec((seq, cw), lambda s, j: (s, j))
    return pl.pallas_call(
        body, name=name, grid=(t // seq, e // cw), in_specs=[spec], out_specs=spec,
        out_shape=jax.ShapeDtypeStruct((t, e), BF16),
        compiler_params=_params("parallel", "parallel"))(src)


def _b_gate_fwd(name, mm, pb, b_grp, scale):
    t, e = mm.shape
    tr, tc = _tile(t, 512), _tile(e, 1024)
    nc = e // tc

    def body(mm_ref, z_ref, b_ref, s_ref, y_ref):
        sz, _ = _silu(z_ref[...])
        y_ref[...] = ((mm_ref[...] + b_ref[...]) * s_ref[...] * sz).astype(BF16)

    blk = pl.BlockSpec((tr, tc), lambda i, j: (i, j))
    vec = pl.BlockSpec((1, tc), lambda i, j: (0, j))
    return pl.pallas_call(
        body, name=name, grid=(t // tr, nc),
        in_specs=[blk, pl.BlockSpec((tr, tc), lambda i, j: (i, nc + j)), vec, vec], out_specs=blk,
        out_shape=jax.ShapeDtypeStruct((t, e), BF16),
        compiler_params=_params("parallel", "parallel"))(mm, pb, b_grp, scale)


def _b_gate_bwd(name, dy, mm, pb, b_grp, scale):
    t, e = mm.shape
    tr, tc = _tile(t, 512), _tile(e, 1024)
    nc = e // tc

    def body(dy_ref, mm_ref, z_ref, b_ref, s_ref, dmm_ref, dz_ref, db_ref, ds_ref):
        i = pl.program_id(1)
        sz, dsz = _silu(z_ref[...])
        dyv = dy_ref[...]
        mb = mm_ref[...] + b_ref[...]
        dmixed = dyv * sz
        dmm = dmixed * s_ref[...]
        dmm_ref[...] = dmm.astype(BF16)
        dz_ref[...] = (dyv * (mb * s_ref[...]) * dsz).astype(BF16)
        db = jnp.sum(dmm, axis=0, keepdims=True)
        ds = jnp.sum(dmixed * mb, axis=0, keepdims=True)

        @pl.when(i == 0)
        def _():
            db_ref[...] = db
            ds_ref[...] = ds

        @pl.when(i > 0)
        def _():
            db_ref[...] += db
            ds_ref[...] += ds

    blk = pl.BlockSpec((tr, tc), lambda j, i: (i, j))
    vec = pl.BlockSpec((1, tc), lambda j, i: (0, j))
    act = jax.ShapeDtypeStruct((t, e), BF16)
    stat = jax.ShapeDtypeStruct((1, e), F32)
    return pl.pallas_call(
        body, name=name, grid=(nc, t // tr),
        in_specs=[blk, blk, pl.BlockSpec((tr, tc), lambda j, i: (i, nc + j)), vec, vec],
        out_specs=[blk, blk, vec, vec], out_shape=[act, act, stat, stat],
        compiler_params=_params("parallel", "arbitrary"))(dy, mm, pb, b_grp, scale)


def _position():
    return lax.axis_index("x"), lax.axis_index("y"), lax.axis_index("c")


def _other_chips(x, y):
    return [(1 - x, y), (x, 1 - y), (1 - x, 1 - y)]


def _half(ref, c):
    n = ref.shape[0] // 2
    return ref.at[pl.ds(c * n, n)]


def _part(ref, c, part):
    q, nq, count = part if len(part) == 3 else (*part, 1)
    n = ref.shape[0] // (2 * nq)
    return ref.at[pl.ds(c * nq * n + q * n, count * n)]


def _gather_phase(slots, small=None, part=(0, 1)):
    nw = len(slots)
    ns = 0 if small is None else 1
    n_ici = 3 * (nw + ns)
    n_sem = n_ici + 3 * nw + ns

    def copies(ins, outs, send_sems, recv_sems):
        x, y, c = _position()
        sibling = (x, y, 1 - c)

        def ici(j, k, slot, to):
            return pltpu.make_async_remote_copy(
                src_ref=_part(ins[k].at[slot], c, part), dst_ref=_part(outs[k].at[slot], c, part),
                send_sem=send_sems.at[j * nw + k], recv_sem=recv_sems.at[j * nw + k],
                device_id=to, device_id_type=MESH)

        def ici_small(j, slot, to):
            return pltpu.make_async_remote_copy(
                src_ref=ins[nw], dst_ref=outs[nw].at[slot], send_sem=send_sems.at[3 * nw + j],
                recv_sem=recv_sems.at[3 * nw + j], device_id=to, device_id_type=MESH)

        def d2d(j, k, slot, half):
            return pltpu.make_async_remote_copy(
                src_ref=_part(outs[k].at[slot], half, part), dst_ref=_part(outs[k].at[slot], half, part),
                send_sem=send_sems.at[n_ici + j * nw + k], recv_sem=recv_sems.at[n_ici + j * nw + k],
                device_id=sibling, device_id_type=MESH)

        def local():
            return pltpu.make_async_copy(ins[nw], outs[nw].at[2 * x + y], send_sems.at[n_sem - 1])

        return x, y, c, ici, ici_small, d2d, local

    def start(ins, outs, send_sems, recv_sems):
        x, y, c, ici, ici_small, _, local = copies(ins, outs, send_sems, recv_sems)
        if ns:
            local().start()
        for j, (cx, cy) in enumerate(_other_chips(x, y)):
            for k in range(nw):
                ici(j, k, 2 * x + y, (cx, cy, c)).start()
            if ns:
                ici_small(j, 2 * x + y, (cx, cy, c)).start()

    def relay(ins, outs, send_sems, recv_sems):
        x, y, c, ici, _, d2d, _ = copies(ins, outs, send_sems, recv_sems)
        for j, (cx, cy) in enumerate(_other_chips(x, y)):
            for k in range(nw):
                ici(j, k, 2 * cx + cy, (x, y, c)).wait_recv()
                d2d(j, k, 2 * cx + cy, c).start()

    def finish(ins, outs, send_sems, recv_sems):
        x, y, c, ici, ici_small, d2d, local = copies(ins, outs, send_sems, recv_sems)
        chips = _other_chips(x, y)
        for j, (cx, cy) in enumerate(chips):
            if ns:
                ici_small(j, 2 * cx + cy, (x, y, c)).wait_recv()
            for k in range(nw):
                d2d(j, k, 2 * cx + cy, 1 - c).wait_recv()
        for j, (cx, cy) in enumerate(chips):
            for k in range(nw):
                ici(j, k, 2 * x + y, (cx, cy, c)).wait_send()
                d2d(j, k, 2 * cx + cy, c).wait_send()
            if ns:
                ici_small(j, 2 * x + y, (cx, cy, c)).wait_send()
        if ns:
            local().wait()

    outs = [jax.ShapeDtypeStruct(s.shape, s.dtype) for s in slots]
    if ns:
        outs.append(jax.ShapeDtypeStruct((N_CHIPS,) + small.shape, small.dtype))
    return _Phase(list(slots) + ([small] if ns else []), outs, {k: k for k in range(nw)}, n_sem, start, finish, relay)


def _staged_gather_phases(slots, small):
    nw = len(slots)

    def copies(ins, outs, send_sems, recv_sems, fwd_base):
        x, y, c = _position()
        sibling = (x, y, 1 - c)

        def direct(j, k, slot, to):
            return pltpu.make_async_remote_copy(
                src_ref=_half(ins[k].at[slot], c), dst_ref=_half(outs[k].at[slot], c),
                send_sem=send_sems.at[j * nw + k], recv_sem=recv_sems.at[j * nw + k],
                device_id=to, device_id_type=MESH)

        def relayed(p, k, slot, to):
            return pltpu.make_async_remote_copy(
                src_ref=_part(ins[k].at[slot], c, (p, 2)), dst_ref=_part(outs[k].at[slot], c, (p, 2)),
                send_sem=send_sems.at[p * nw + k], recv_sem=recv_sems.at[p * nw + k],
                device_id=to, device_id_type=MESH)

        def d2d(j, k, slot, half):
            return pltpu.make_async_remote_copy(
                src_ref=_half(outs[k].at[slot], half), dst_ref=_half(outs[k].at[slot], half),
                send_sem=send_sems.at[fwd_base + j * nw + k], recv_sem=recv_sems.at[fwd_base + j * nw + k],
                device_id=sibling, device_id_type=MESH)

        def small_copy(j, slot, to):
            return pltpu.make_async_remote_copy(
                src_ref=ins[nw], dst_ref=outs[nw].at[slot], send_sem=send_sems.at[4 * nw + j],
                recv_sem=recv_sems.at[4 * nw + j], device_id=to, device_id_type=MESH)

        def local():
            return pltpu.make_async_copy(ins[nw], outs[nw].at[2 * x + y], send_sems.at[4 * nw + 3])

        chips = _other_chips(x, y)
        return x, y, c, chips, [2 * cx + cy for cx, cy in chips], direct, relayed, d2d, small_copy, local

    def start_1(*refs):
        x, y, c, chips, _, direct, _, _, small_copy, local = copies(*refs, 2 * nw)
        local().start()
        for j in range(2):
            for k in range(nw):
                direct(j, k, 2 * x + y, (*chips[j], c)).start()
        for j in range(3):
            small_copy(j, 2 * x + y, (*chips[j], c)).start()

    def relay_1(*refs):
        x, y, c, _, slot, direct, _, d2d, _, _ = copies(*refs, 2 * nw)
        for j in range(2):
            for k in range(nw):
                direct(j, k, slot[j], (x, y, c)).wait_recv()
                d2d(j, k, slot[j], c).start()

    def finish_1(*refs):
        x, y, c, chips, slot, direct, _, d2d, small_copy, local = copies(*refs, 2 * nw)
        for j in range(3):
            small_copy(j, slot[j], (x, y, c)).wait_recv()
        for j in range(2):
            for k in range(nw):
                d2d(j, k, slot[j], 1 - c).wait_recv()
        for j in range(3):
            small_copy(j, 2 * x + y, (*chips[j], c)).wait_send()
        for j in range(2):
            for k in range(nw):
                direct(j, k, 2 * x + y, (*chips[j], c)).wait_send()
                d2d(j, k, slot[j], c).wait_send()
        local().wait()

    def start_2(*refs):
        x, y, c, chips, slot, _, relayed, _, _, _ = copies(*refs, 2 * nw)
        for j in range(2):
            for k in range(nw):
                relayed(1 - j, k, slot[j], (*chips[1 - j], c)).start()

    def relay_2(*refs):
        x, y, c, _, slot, _, relayed, d2d, _, _ = copies(*refs, 2 * nw)
        for k in range(nw):
            for p in range(2):
                relayed(p, k, slot[2], (x, y, c)).wait_recv()
            d2d(0, k, slot[2], c).start()

    def finish_2(*refs):
        x, y, c, chips, slot, _, relayed, d2d, _, _ = copies(*refs, 2 * nw)
        for k in range(nw):
            d2d(0, k, slot[2], 1 - c).wait_recv()
        for k in range(nw):
            d2d(0, k, slot[2], c).wait_send()
            for j in range(2):
                relayed(1 - j, k, slot[j], (*chips[1 - j], c)).wait_send()

    shapes = [jax.ShapeDtypeStruct(s.shape, s.dtype) for s in slots]
    small_shape = jax.ShapeDtypeStruct((N_CHIPS,) + small.shape, small.dtype)
    keep = {k: k for k in range(nw)}

    def second(landed):
        return _Phase(list(landed), shapes, keep, 3 * nw, start_2, finish_2, relay_2)

    return _Phase(list(slots) + [small], shapes + [small_shape], keep, 4 * nw + 4, start_1, finish_1, relay_1), second


def _swap_phase(grads):
    nw = len(grads)

    def swaps(ins, outs, send_sems, recv_sems):
        x, y, c = _position()
        cps = []
        for k in range(nw):
            n = ins[k].shape[1] // 2
            cps.append(pltpu.make_async_remote_copy(
                src_ref=ins[k].at[pl.ds(0, N_CHIPS), pl.ds((1 - c) * n, n)], dst_ref=outs[k],
                send_sem=send_sems.at[k], recv_sem=recv_sems.at[k],
                device_id=(x, y, 1 - c), device_id_type=MESH))
        return cps

    def start(*refs):
        for cp in swaps(*refs):
            cp.start()

    def finish(*refs):
        for cp in swaps(*refs):
            cp.wait()

    halves = [jax.ShapeDtypeStruct((g.shape[0], g.shape[1] // 2, g.shape[2]), F32) for g in grads]
    return _Phase(grads, halves, {}, nw, start, finish)


def _add_halves(name, grad, theirs, place):
    s, half, w = theirs.shape
    tr, tc = _tile(half, 256), _tile(w, 2048)
    nrt = half // tr

    def body(place_ref, a_ref, b_ref, h_ref, f_ref):
        v = a_ref[...] + b_ref[...]
        h_ref[...] = v.astype(BF16)

        @pl.when(pl.program_id(2) == place_ref[0])
        def _():
            f_ref[...] = v

    return pl.pallas_call(
        body, name=name,
        grid_spec=pltpu.PrefetchScalarGridSpec(
            num_scalar_prefetch=1, grid=(nrt, w // tc, s),
            in_specs=[pl.BlockSpec((None, tr, tc), lambda i, j, q, p: (q, p[1] * nrt + i, j)),
                      pl.BlockSpec((None, tr, tc), lambda i, j, q, p: (q, i, j))],
            out_specs=[pl.BlockSpec((None, tr, tc), lambda i, j, q, p: (q, i, j)),
                       pl.BlockSpec((tr, tc), lambda i, j, q, p: (p[1] * nrt + i, j))]),
        out_shape=[jax.ShapeDtypeStruct(theirs.shape, BF16), jax.ShapeDtypeStruct((2 * half, w), F32)],
        compiler_params=_params("parallel", "parallel", "arbitrary"))(place, grad, theirs)


def _scatter_phase(own, pb, part=(0, 1), got=None):
    nw = len(own)

    def copies(ins, outs, send_sems, recv_sems):
        own_in, pbs = ins[:nw], ins[nw:2 * nw]
        own_out, gots = outs[:nw], outs[nw:]
        x, y, c = _position()
        sibling = (x, y, 1 - c)

        def d2d_own(k, half):
            return pltpu.make_async_remote_copy(
                src_ref=_part(own_in[k], half, part), dst_ref=_part(own_out[k], half, part),
                send_sem=send_sems.at[k], recv_sem=recv_sems.at[k], device_id=sibling, device_id_type=MESH)

        def ici(j, k, shard, to):
            n = pbs[k].shape[1] // part[1]
            rows = pl.ds(part[0] * n, (part[2] if len(part) == 3 else 1) * n)
            return pltpu.make_async_remote_copy(
                src_ref=pbs[k].at[shard, rows], dst_ref=_part(gots[k].at[j], c, part),
                send_sem=send_sems.at[nw + j * nw + k], recv_sem=recv_sems.at[nw + j * nw + k],
                device_id=to, device_id_type=MESH)

        def d2d(j, k, half):
            return pltpu.make_async_remote_copy(
                src_ref=_part(gots[k].at[j], half, part), dst_ref=_part(gots[k].at[j], half, part),
                send_sem=send_sems.at[4 * nw + j * nw + k], recv_sem=recv_sems.at[4 * nw + j * nw + k],
                device_id=sibling, device_id_type=MESH)

        return x, y, c, d2d_own, ici, d2d

    def start(*refs):
        x, y, c, d2d_own, ici, _ = copies(*refs)
        for k in range(nw):
            d2d_own(k, c).start()
        for j, (cx, cy) in enumerate(_other_chips(x, y)):
            for k in range(nw):
                ici(j, k, 2 * cx + cy, (cx, cy, c)).start()

    def relay(*refs):
        x, y, c, _, ici, d2d = copies(*refs)
        for j in range(3):
            for k in range(nw):
                ici(j, k, 2 * x + y, (x, y, c)).wait_recv()
                d2d(j, k, c).start()

    def finish(*refs):
        x, y, c, d2d_own, ici, d2d = copies(*refs)
        for k in range(nw):
            d2d_own(k, 1 - c).wait_recv()
        for j in range(3):
            for k in range(nw):
                d2d(j, k, 1 - c).wait_recv()
        for k in range(nw):
            d2d_own(k, c).wait_send()
        for j, (cx, cy) in enumerate(_other_chips(x, y)):
            for k in range(nw):
                ici(j, k, 2 * cx + cy, (cx, cy, c)).wait_send()
                d2d(j, k, c).wait_send()

    own_shape = [jax.ShapeDtypeStruct(o.shape, F32) for o in own]
    got_shape = [jax.ShapeDtypeStruct((3,) + o.shape, BF16) for o in own]
    aliases = {k: k for k in range(nw)}
    if got is not None:
        aliases.update({2 * nw + k: nw + k for k in range(nw)})
    return _Phase(list(own) + list(pb) + list(got or []), own_shape + got_shape, aliases, 7 * nw, start, finish, relay)


def _gather_small(part):
    def body(p_ref, o_ref, send_sems, recv_sems, loc_sem):
        x, y, c = _position()
        me = 4 * x + 2 * y + c
        mine = pltpu.make_async_copy(p_ref, o_ref.at[me], loc_sem)
        mine.start()
        sent = []
        for mask in range(1, N_DEV):
            dx, dy, dc = (mask >> 2) & 1, (mask >> 1) & 1, mask & 1
            px, py, pc = x ^ dx, y ^ dy, c ^ dc
            cp = pltpu.make_async_remote_copy(
                src_ref=p_ref, dst_ref=o_ref.at[me], send_sem=send_sems.at[mask - 1],
                recv_sem=recv_sems.at[mask - 1], device_id=(px, py, pc), device_id_type=MESH)
            cp.start()
            sent.append((cp, 4 * px + 2 * py + pc))
        for mask in range(1, N_DEV):
            cp, peer = sent[mask - 1]
            pltpu.make_async_remote_copy(
                src_ref=p_ref, dst_ref=o_ref.at[peer], send_sem=send_sems.at[mask - 1],
                recv_sem=recv_sems.at[mask - 1], device_id=(x, y, c), device_id_type=MESH).wait_recv()
        for cp, _ in sent:
            cp.wait_send()
        mine.wait()

    return pl.pallas_call(
        body, name="gather_small_grads", in_specs=[ANY], out_specs=ANY,
        out_shape=jax.ShapeDtypeStruct((N_DEV,) + part.shape, F32),
        scratch_shapes=[pltpu.SemaphoreType.DMA((N_DEV - 1,)), pltpu.SemaphoreType.DMA((N_DEV - 1,)),
                        pltpu.SemaphoreType.DMA(())],
    )(part)


def _sum_slots(name, parts):
    s, r, c = parts.shape
    tr = _tile(r, 512, 8)

    def body(p_ref, o_ref):
        acc = p_ref[0]
        for q in range(1, s):
            acc = acc + p_ref[q]
        o_ref[...] = acc

    return pl.pallas_call(
        body, name=name, grid=(r // tr,), in_specs=[pl.BlockSpec((s, tr, c), lambda i: (0, i, 0))],
        out_specs=pl.BlockSpec((tr, c), lambda i: (i, 0)), out_shape=jax.ShapeDtypeStruct((r, c), F32),
        compiler_params=_params("parallel"))(parts)


def _adamw_math(w, g, m, v):
    m = ADAM_B1 * m + (1.0 - ADAM_B1) * g
    v = ADAM_B2 * v + (1.0 - ADAM_B2) * (g * g)
    m_hat = m / (1.0 - ADAM_B1 ** ADAM_STEP)
    v_hat = v / (1.0 - ADAM_B2 ** ADAM_STEP)
    delta = -ADAM_LR * (m_hat / (jnp.sqrt(v_hat) + ADAM_EPS) + ADAM_WD * w)
    return delta, m, v


def _adamw(name, w, m, v, own, got=None, odd=None, into=None):
    r, c = w.shape
    rows = r if odd is None else r // 2
    tr, tc = _tile(r // 4, 256, 8) if odd is not None else _tile(r, 256, 8), _tile(c, 1024)
    bpq = (r // 4) // tr if odd is not None else 1
    pick = (lambda i: i) if odd is None else (lambda i: (2 * (i // bpq) + odd) * bpq + i % bpq)
    n_in = 4 + (0 if got is None else 1)

    def body(*refs):
        w_ref, m_ref, v_ref, own_ref = refs[:4]
        outs = refs[n_in + (0 if into is None else 4):]
        g = own_ref[...]
        if got is not None:
            for j in range(3):
                g = g + refs[4][j].astype(F32)
        delta, mn, vn = _adamw_math(w_ref[...], g, m_ref[...], v_ref[...])
        outs[0][...] = g
        outs[1][...] = delta
        outs[2][...] = mn
        outs[3][...] = vn

    full = pl.BlockSpec((tr, tc), lambda i, j: (pick(i), j))
    ins, args = [full] * 3 + [pl.BlockSpec((tr, tc), lambda i, j: (i, j))], [w, m, v, own]
    if got is not None:
        ins.append(pl.BlockSpec((3, tr, tc), lambda i, j: (0, i, j)))
        args.append(got)
    aliases = {}
    if into is not None:
        ins += [ANY] * 4
        args += list(into)
        aliases = {n_in + q: q for q in range(4)}
    return pl.pallas_call(
        body, name=name, grid=(rows // tr, c // tc), in_specs=ins, out_specs=[full] * 4,
        out_shape=[jax.ShapeDtypeStruct((r, c), F32)] * 4, input_output_aliases=aliases,
        compiler_params=_params("parallel", "parallel"))(*args)


def _pack(parts):
    return jnp.concatenate([p.reshape(-1) for p in parts]).reshape(-1, LANES)


def _unpack(packed, shapes):
    flat = packed.reshape(-1)
    out, off = [], 0
    for s in shapes:
        n = 1
        for d in s:
            n *= d
        out.append(flat[off:off + n].reshape(s))
        off += n
    return out


def kernel(x, pre_norm, post_norm, a_w_in, a_ln_g, a_ln_b, a_w_s, a_b_s, a_w_out, b_w_in, b_w_grp, b_b_grp, b_scale, b_w_out, loss_target, m_pre_norm, m_post_norm, m_a_w_in, m_a_ln_g, m_a_ln_b, m_a_w_s, m_a_b_s, m_a_w_out, m_b_w_in, m_b_w_grp, m_b_b_grp, m_b_scale, m_b_w_out, v_pre_norm, v_post_norm, v_a_w_in, v_a_ln_g, v_a_ln_b, v_a_w_s, v_a_b_s, v_a_w_out, v_b_w_in, v_b_w_grp, v_b_b_grp, v_b_scale, v_b_w_out):
    nb, seq, d = x.shape
    t = nb * seq
    e = a_ln_g.shape[1]
    nh, chunk = a_w_s.shape[1], a_w_s.shape[2]
    ng, rs, gw = b_w_grp.shape[1], b_w_grp.shape[2], b_w_grp.shape[3]
    wa, wb = a_w_in.shape[2], b_w_in.shape[2]
    cx, cy = lax.axis_index("x"), lax.axis_index("y")
    chip = 2 * cx + cy

    big_w = [a_w_in.reshape(d, wa), a_w_out.reshape(e // N_CHIPS, d), b_w_in.reshape(d, wb),
             b_w_grp.reshape(ng * rs, gw), b_w_out.reshape(e // N_CHIPS, d)]
    big_m = [m_a_w_in.reshape(d, wa), m_a_w_out.reshape(e // N_CHIPS, d), m_b_w_in.reshape(d, wb),
             m_b_w_grp.reshape(ng * rs, gw), m_b_w_out.reshape(e // N_CHIPS, d)]
    big_v = [v_a_w_in.reshape(d, wa), v_a_w_out.reshape(e // N_CHIPS, d), v_b_w_in.reshape(d, wb),
             v_b_w_grp.reshape(ng * rs, gw), v_b_w_out.reshape(e // N_CHIPS, d)]
    names = ["a_w_in", "a_w_out", "b_w_in", "b_w_grp", "b_w_out"]
    place = jnp.stack([chip, lax.axis_index("c")]).astype(jnp.int32)
    slots = [_cast_into_slot("cast_" + n, w, place) for n, w in zip(names, big_w)]
    small_w = jnp.concatenate([b_b_grp.reshape(ng, rs), b_scale.reshape(ng, rs)], axis=0)
    xf = x.reshape(t, d)
    tgt = loss_target.reshape(t, d)
    g_pre0, g_pre1 = pre_norm[0:1], pre_norm[1:2]
    g_post0, g_post1 = post_norm[0:1], post_norm[1:2]
    w_s = a_w_s.reshape(nh, chunk, chunk)
    b_s3 = a_b_s.reshape(nh, chunk, 1)

    h0 = _rms_fwd("pre_norm0", xf, g_pre0)
    order = jnp.stack([chip, chip ^ 2, chip ^ 1, chip ^ 3]).astype(jnp.int32)
    near, far = _staged_gather_phases(slots[:1], small_w)
    pa, (wa_g, gsmall) = _mm_nn_cols("a_in_proj_own", h0, slots[0], [near], order=order, span=(0, 1))
    pa, (wa_g, wao_g) = _mm_nn_cols(
        "a_in_proj_near", h0, wa_g, [far([wa_g]), _gather_phase(slots[1:2], part=(0, 2))],
        order=order, span=(1, 2), into=pa)
    pa, (wao_g,) = _mm_nn_cols(
        "a_in_proj_far", h0, wa_g, [_gather_phase([wao_g], part=(1, 2))], order=order, span=(3, 1), into=pa)
    wao_g = wao_g.reshape(e, d)
    b_grp_full = jnp.transpose(gsmall[:, :ng, :], (1, 0, 2)).reshape(1, e)
    scale_full = gsmall[:, ng:, :].reshape(1, e)
    mu, rstd = _a_stats("a_ln_stats", pa, e)
    y0, (wb_half,) = _a_gate_fwd(
        "a_gate", pa, mu, rstd, a_ln_g, a_ln_b, w_s, b_s3, [_gather_phase(slots[2:3], part=(0, 2))])
    m0, (wb_g,) = _mm_nn("a_out_proj", y0, wao_g, phases=[_gather_phase([wb_half], part=(1, 2))])
    x1, h1 = _post_pre_fwd("post0_pre1", xf, m0, g_post0, g_pre1)
    pb, (wg_g, wbo_g) = _mm_nn_cols("b_in_proj", h1, wb_g, [_gather_phase(slots[3:5])])
    wg_g = wg_g.reshape(N_CHIPS, ng, rs, gw)
    wbo_g = wbo_g.reshape(e, d)
    pooled = _pool("b_pool", pb, e, seq, backward=False)
    mm = _grp_fwd("b_grp_proj", pooled, wg_g)
    y1 = _b_gate_fwd("b_gate", mm, pb, b_grp_full, scale_full)
    m1 = _mm_nn("b_out_proj", y1, wbo_g)
    loss, dx2, dm1, dg_post1 = _loss_post_bwd("loss_post1_bwd", x1, m1, g_post1, tgt)

    def chip_sum(n, g, theirs):
        pbk, ownk = _add_halves("chip_sum_" + n, g, theirs, place)
        return [ownk], [pbk]

    reduced = {}
    dy1 = _mm_nt("b_out_dx", dm1, wbo_g)
    g_wbo = _mm_tn("b_out_dw", y1, dm1).reshape(N_CHIPS, e // N_CHIPS, d)
    dmm, dzb, db_grp, dscale = _b_gate_bwd("b_gate_bwd", dy1, mm, pb, b_grp_full, scale_full)
    dpooled, (th,) = _grp_bwd_x("b_grp_dx", dmm, wg_g, [_swap_phase([g_wbo])])
    s_bo = chip_sum("b_w_out", g_wbo, th)
    g_wg = _grp_bwd_w("b_grp_dw", pooled, dmm, ng).reshape(N_CHIPS, ng * rs, gw)
    dp = _pool("b_pool_bwd", dpooled, e, seq, backward=True)
    dpb = jnp.concatenate([dp, dzb], axis=1)
    dh1, (own, got, th) = _mm_nt_cols("b_in_dx", dpb, wb_g, [_scatter_phase(*s_bo), _swap_phase([g_wg])])
    reduced["b_w_out"] = (own, got)
    s_g = chip_sum("b_w_grp", g_wg, th)
    g_wb, reduced["b_w_grp"] = _mm_tn_cols("b_in_dw", h1, dpb, wb, [_scatter_phase(*s_g)])
    dx1, dm0, dg_pre1, dg_post0 = _mid_bwd("pre1_post0_bwd", dx2, dh1, x1, g_pre1, m0, g_post0)
    dy0, (th,) = _mm_nt("a_out_dx", dm0, wao_g, phases=[_swap_phase([g_wb])])
    s_b = chip_sum("b_w_in", g_wb, th)
    g_wao, (own, got) = _mm_tn("a_out_dw", y0, dm0, phases=[_scatter_phase(*s_b, part=(0, 4))])
    g_wao = g_wao.reshape(N_CHIPS, e // N_CHIPS, d)
    (du, dz, dsv, c1, c2), (own, got, th) = _a_gate_bwd1(
        "a_gate_bwd1", pa, dy0, mu, rstd, a_ln_g, a_ln_b, w_s, b_s3,
        [_scatter_phase([own], s_b[1], part=(1, 4, 2), got=[got]), _swap_phase([g_wao])])
    s_ao = chip_sum("a_w_out", g_wao, th)
    (dv, dw_s, db_s, dln_g, dln_b), reduced["b_w_in"] = _a_gate_bwd2(
        "a_gate_bwd2", pa, dsv, mu, rstd, c1, c2, a_ln_g, a_ln_b, w_s,
        [_scatter_phase([own], s_b[1], part=(3, 4), got=[got])])
    dpa = jnp.concatenate([du, dv, dz], axis=1)
    g_lo, reduced["a_w_out"] = _mm_tn_cols("a_in_dw_even", h0, dpa, wa, [_scatter_phase(*s_ao)], odd=0)
    g_hi, (th,) = _mm_tn_cols("a_in_dw_odd", h0, dpa, wa, [_swap_phase([g_lo])], odd=1)
    s_lo = chip_sum("a_w_in_even", g_lo, th)
    dh0, (th, own_lo, got_lo) = _mm_nt_cols(
        "a_in_dx_top", dpa, wa_g, [_swap_phase([g_hi]), _scatter_phase(*s_lo)], rows=(0, 2))
    s_hi = chip_sum("a_w_in_odd", g_hi, th)
    dh0, (own_hi, got_hi) = _mm_nt_cols(
        "a_in_dx_bottom", dpa, wa_g, [_scatter_phase(*s_hi)], rows=(1, 2), into=dh0)
    grad_x, dg_pre0 = _pre_bwd("pre0_bwd", dx1, dh0, xf, g_pre0)
    big_out = []
    for k, n in enumerate(names):
        if n == "a_w_in":
            even = _adamw("adamw_a_w_in_even", big_w[k], big_m[k], big_v[k], own_lo, got_lo, odd=0)
            big_out.append(_adamw("adamw_a_w_in_odd", big_w[k], big_m[k], big_v[k], own_hi, got_hi, odd=1, into=even))
        else:
            big_out.append(_adamw("adamw_" + n, big_w[k], big_m[k], big_v[k], *reduced[n]))

    small_shapes = [(2, d), (2, d), (1, e), (1, e), (1, nh, chunk, chunk), (1, nh, chunk), (1, e), (1, e)]
    part = _pack([jnp.concatenate([dg_pre0, dg_pre1], axis=0), jnp.concatenate([dg_post0, dg_post1], axis=0),
                  dln_g, dln_b, dw_s, db_s, db_grp, dscale])
    g_small = _sum_slots("sum_small_grads", _gather_small(part))
    g_pre, g_post, g_lng, g_lnb, g_ws, g_bs, g_bgrp_full, g_scale_full = _unpack(g_small, small_shapes)
    g_bgrp = lax.dynamic_slice_in_dim(g_bgrp_full.reshape(ng, N_CHIPS, rs), chip, 1, axis=1).reshape(1, ng, rs)
    g_scale = lax.dynamic_slice_in_dim(g_scale_full.reshape(N_CHIPS, gw), chip, 1, axis=0)
    small_names = ["pre_norm", "post_norm", "a_ln_g", "a_ln_b", "a_w_s", "a_b_s", "b_b_grp", "b_scale"]
    small_g = [g_pre, g_post, g_lng, g_lnb, g_ws, g_bs, g_bgrp, g_scale]
    small_ws = [pre_norm, post_norm, a_ln_g, a_ln_b, a_w_s, a_b_s, b_b_grp, b_scale]
    small_ms = [m_pre_norm, m_post_norm, m_a_ln_g, m_a_ln_b, m_a_w_s, m_a_b_s, m_b_b_grp, m_b_scale]
    small_vs = [v_pre_norm, v_post_norm, v_a_ln_g, v_a_ln_b, v_a_w_s, v_a_b_s, v_b_b_grp, v_b_scale]
    packed = _adamw("adamw_small", _pack(small_ws), _pack(small_ms), _pack(small_vs), _pack(small_g))
    small_out = [_unpack(p, [w.shape for w in small_ws]) for p in packed]

    loss = lax.psum(loss[0, 0], ("x", "y", "c"))
    order = ["pre_norm", "post_norm", "a_w_in", "a_ln_g", "a_ln_b", "a_w_s", "a_b_s", "a_w_out", "b_w_in",
             "b_w_grp", "b_b_grp", "b_scale", "b_w_out"]
    big_shapes = dict(zip(names, [a_w_in.shape, a_w_out.shape, b_w_in.shape, b_w_grp.shape, b_w_out.shape]))
    outs = [loss, grad_x.reshape(nb, seq, d)]
    for kind in range(4):
        for n in order:
            if n in big_shapes:
                outs.append(big_out[names.index(n)][kind].reshape(big_shapes[n]))
            else:
                outs.append(small_out[kind][small_names.index(n)])
    return tuple(outs)
```

```python
import functools

import jax
import jax.numpy as jnp
from jax import lax
from jax.experimental import pallas as pl
from jax.experimental.pallas import tpu as pltpu

F32 = jnp.float32
BF16 = jnp.bfloat16
NORM_EPS = 1e-6
POOL_WINDOWS = (2, 4, 8, 16)
ADAM_LR = 0.001
ADAM_B1 = 0.9
ADAM_B2 = 0.999
ADAM_EPS = 1e-08
ADAM_WD = 0.01
ADAM_STEP = 10
N_CHIPS = 4
N_DEV = 8
V7X_VMEM_LIMIT_BYTES = 56 * 1024 * 1024
LANES = 128
MESH = pl.DeviceIdType.MESH
ANY = pl.BlockSpec(memory_space=pl.ANY)
SQRT_HALF = 0.7071067811865476
INV_SQRT_2PI = 0.3989422804014327


def _tile(dim, pref, mult=LANES):
    if dim <= pref:
        return dim
    t = (pref // mult) * mult
    while t >= mult:
        if dim % t == 0:
            return t
        t -= mult
    return dim


def _params(*sem):
    return pltpu.CompilerParams(dimension_semantics=sem or None, vmem_limit_bytes=V7X_VMEM_LIMIT_BYTES)


def _gelu(x):
    cdf = 0.5 * (1.0 + lax.erf(x * SQRT_HALF))
    pdf = jnp.exp(-0.5 * x * x) * INV_SQRT_2PI
    return x * cdf, cdf + x * pdf


def _silu(z):
    s = 1.0 / (1.0 + jnp.exp(-z))
    return z * s, s * (1.0 + z * (1.0 - s))


class _Phase:
    def __init__(self, ins, outs, aliases, n_sem, start, finish, relay=None):
        self.ins, self.outs, self.aliases, self.n_sem = list(ins), list(outs), dict(aliases), n_sem
        self.start, self.finish, self.relay = start, finish, relay


RELAY_AT = 0.85


def _hosted(body, phases, grid, n_in, n_out, n_scratch=0):
    ins, outs, aliases, sems = [], [], {}, []
    for ph in phases:
        for i, o in ph.aliases.items():
            aliases[n_in + len(ins) + i] = n_out + len(outs) + o
        ins += ph.ins
        outs += ph.outs
        sems += [pltpu.SemaphoreType.DMA((ph.n_sem,)), pltpu.SemaphoreType.DMA((ph.n_sem,))]
    if not phases:
        return body, ins, outs, aliases, sems
    steps = 1
    for n in grid:
        steps *= n
    relay_at = min(steps - 1, int(RELAY_AT * steps))

    def wrapped(*refs):
        own_in, refs = refs[:n_in], refs[n_in:]
        in_refs, refs = refs[:len(ins)], refs[len(ins):]
        own_out, refs = refs[:n_out], refs[n_out:]
        out_refs, refs = refs[:len(outs)], refs[len(outs):]
        own_scratch, sem_refs = refs[:n_scratch], refs[n_scratch:]

        def run(which):
            i = o = 0
            for p, ph in enumerate(phases):
                fn = getattr(ph, which)
                if fn is not None:
                    fn(in_refs[i:i + len(ph.ins)], out_refs[o:o + len(ph.outs)], sem_refs[2 * p], sem_refs[2 * p + 1])
                i += len(ph.ins)
                o += len(ph.outs)

        if not grid:
            run("start")
            run("relay")
            run("finish")
            return
        step = 0
        for d, n in enumerate(grid):
            step = step * n + pl.program_id(d)
        pl.when(step == 0)(lambda: run("start"))
        body(*own_in, *own_out, *own_scratch)
        pl.when(step == relay_at)(lambda: run("relay"))
        pl.when(step == steps - 1)(lambda: run("finish"))

    return wrapped, ins, outs, aliases, sems


def _comm_only(name, phases):
    body, ins, outs, aliases, sems = _hosted(None, phases, (), 0, 0)
    return pl.pallas_call(
        body, name=name, in_specs=[ANY] * len(ins), out_specs=[ANY] * len(outs), out_shape=outs,
        input_output_aliases=aliases, scratch_shapes=sems)(*ins)


def _matmul(name, a, b, *, dims, grid, a_spec, b_spec, o_spec, out_shape, out_dtype, phases=(), into=None,
            prefetch=None, other_b=None):
    nk = grid[2]
    contract = {"nn": ((1,), (0,)), "nt": ((1,), (1,)), "tn": ((0,), (0,))}[dims]
    acc_in_out = out_dtype == F32 or nk == 1
    blk = tuple(d for d in o_spec.block_shape if d is not None)
    first = 0 if prefetch is None else 1
    n_b = 1 if other_b is None else 2
    n_in = first + 1 + n_b + (0 if into is None else 1)

    def body(*refs):
        a_ref, b_ref, o_ref = refs[first], refs[first + 1], refs[n_in]
        b_val = b_ref[...]
        if other_b is not None:
            b_val = jnp.where(other_b[2](refs[0]), refs[first + 2][...], b_val)
        part = lax.dot_general(a_ref[...], b_val, (contract, ((), ())), preferred_element_type=F32)
        if nk == 1:
            o_ref[...] = part.astype(out_dtype)
            return
        acc = o_ref if acc_in_out else refs[n_in + 1]
        k = pl.program_id(2)

        @pl.when(k == 0)
        def _():
            acc[...] = part

        @pl.when(k > 0)
        def _():
            acc[...] += part

        if not acc_in_out:

            @pl.when(k == nk - 1)
            def _():
                o_ref[...] = acc[...].astype(out_dtype)

    own_scratch = [] if acc_in_out else [pltpu.VMEM(blk, F32)]
    body, x_ins, x_outs, aliases, sems = _hosted(body, phases, grid, n_in, 1, len(own_scratch))
    if into is not None:
        aliases[n_in - 1] = 0
    sem = ("arbitrary",) * 3 if phases else ("parallel", "parallel", "arbitrary")
    in_specs = [a_spec, b_spec] + ([] if other_b is None else [other_b[1]])
    in_specs += ([] if into is None else [ANY]) + [ANY] * len(x_ins)
    out_specs = [o_spec] + [ANY] * len(x_outs)
    if prefetch is None:
        layout = dict(grid=grid, in_specs=in_specs, out_specs=out_specs, scratch_shapes=own_scratch + sems)
    else:
        layout = dict(grid_spec=pltpu.PrefetchScalarGridSpec(
            num_scalar_prefetch=1, grid=grid, in_specs=in_specs, out_specs=out_specs,
            scratch_shapes=own_scratch + sems))
    res = pl.pallas_call(
        body,
        name=name,
        out_shape=[jax.ShapeDtypeStruct(out_shape, out_dtype)] + x_outs,
        input_output_aliases=aliases,
        compiler_params=_params(*sem),
        **layout,
    )(*([] if prefetch is None else [prefetch]), a, b, *([] if other_b is None else [other_b[0]]),
      *([] if into is None else [into]), *x_ins)
    return (res[0], list(res[1:])) if phases else res[0]


MM_K_WHOLE = 4096


def _tiles(m, n, k, k_total=None):
    if (k if k_total is None else k_total) <= MM_K_WHOLE:
        return _tile(m, 1024), _tile(n, 1024), k
    return _tile(m, 1024), _tile(n, 2048), _tile(k, 2048)


def _mm_nn(name, a, b, out_dtype=F32, phases=()):
    m, k = a.shape
    n = b.shape[1]
    tm, tn, tk = _tiles(m, n, k)
    return _matmul(
        name, a, b, dims="nn", grid=(m // tm, n // tn, k // tk),
        a_spec=pl.BlockSpec((tm, tk), lambda i, j, l: (i, l)),
        b_spec=pl.BlockSpec((tk, tn), lambda i, j, l: (l, j)),
        o_spec=pl.BlockSpec((tm, tn), lambda i, j, l: (i, j)),
        out_shape=(m, n), out_dtype=out_dtype, phases=phases)


def _mm_nt(name, a, b, out_dtype=F32, phases=()):
    m, k = a.shape
    n = b.shape[0]
    tm, tn, tk = _tiles(m, n, k)
    return _matmul(
        name, a, b, dims="nt", grid=(m // tm, n // tn, k // tk),
        a_spec=pl.BlockSpec((tm, tk), lambda i, j, l: (i, l)),
        b_spec=pl.BlockSpec((tn, tk), lambda i, j, l: (j, l)),
        o_spec=pl.BlockSpec((tm, tn), lambda i, j, l: (i, j)),
        out_shape=(m, n), out_dtype=out_dtype, phases=phases)


def _mm_tn(name, a, b, out_dtype=F32, phases=()):
    k, m = a.shape
    n = b.shape[1]
    tm, tn, tk = _tiles(m, n, k)
    return _matmul(
        name, a, b, dims="tn", grid=(m // tm, n // tn, k // tk),
        a_spec=pl.BlockSpec((tk, tm), lambda i, j, l: (l, i)),
        b_spec=pl.BlockSpec((tk, tn), lambda i, j, l: (l, j)),
        o_spec=pl.BlockSpec((tm, tn), lambda i, j, l: (i, j)),
        out_shape=(m, n), out_dtype=out_dtype, phases=phases)


def _mm_nn_cols(name, a, wg, phases=(), order=None, span=None, into=None):
    m, k = a.shape
    ws = wg.shape[-1]
    tm, tn, tk = _tiles(m, ws, k)
    npb = ws // tn
    if order is None:
        return _matmul(
            name, a, wg, dims="nn", grid=(m // tm, N_CHIPS * npb, k // tk),
            a_spec=pl.BlockSpec((tm, tk), lambda i, j, l: (i, l)),
            b_spec=pl.BlockSpec((None, tk, tn), lambda i, j, l: (j // npb, l, j % npb)),
            o_spec=pl.BlockSpec((tm, tn), lambda i, j, l: (i, j)),
            out_shape=(m, N_CHIPS * ws), out_dtype=F32, phases=phases)
    first, count = span
    if wg.ndim == 2:
        b_spec = pl.BlockSpec((tk, tn), lambda i, j, l, o: (l, j % npb))
    else:
        b_spec = pl.BlockSpec((None, tk, tn), lambda i, j, l, o: (o[first + j // npb], l, j % npb))
    return _matmul(
        name, a, wg, dims="nn", grid=(m // tm, count * npb, k // tk),
        a_spec=pl.BlockSpec((tm, tk), lambda i, j, l, o: (i, l)), b_spec=b_spec,
        o_spec=pl.BlockSpec((tm, tn), lambda i, j, l, o: (i, o[first + j // npb] * npb + j % npb)),
        out_shape=(m, N_CHIPS * ws), out_dtype=F32, phases=phases, into=into, prefetch=order)


def _mm_nt_cols(name, a, wg, phases=(), rows=None, into=None, elsewhere=None):
    m = a.shape[0]
    n, ws = wg.shape[1], wg.shape[2]
    q, nq = rows or (0, 1)
    tm, tn, tk = _tiles(m // nq, n, ws, N_CHIPS * ws)
    if elsewhere is not None:
        tn = _tile(n, 1024)
    kpb, off = ws // tk, q * (m // nq // tm)
    if elsewhere is None:
        return _matmul(
            name, a, wg, dims="nt", grid=(m // nq // tm, n // tn, N_CHIPS * kpb),
            a_spec=pl.BlockSpec((tm, tk), lambda i, j, l: (i + off, l)),
            b_spec=pl.BlockSpec((None, tn, tk), lambda i, j, l: (l // kpb, j, l % kpb)),
            o_spec=pl.BlockSpec((tm, tn), lambda i, j, l: (i + off, j)),
            out_shape=(m, n), out_dtype=F32, phases=phases, into=into)
    order, shard = elsewhere
    there = lambda l, o: l // kpb == o[3]
    return _matmul(
        name, a, wg, dims="nt", grid=(m // nq // tm, n // tn, N_CHIPS * kpb),
        a_spec=pl.BlockSpec((tm, tk), lambda i, j, l, o: (i + off, l)),
        b_spec=pl.BlockSpec((None, tn, tk), lambda i, j, l, o: (jnp.where(there(l, o), o[0], l // kpb), j, l % kpb)),
        o_spec=pl.BlockSpec((tm, tn), lambda i, j, l, o: (i + off, j)),
        out_shape=(m, n), out_dtype=F32, phases=phases, into=into, prefetch=order,
        other_b=(shard, pl.BlockSpec((tn, tk), lambda i, j, l, o: (j, jnp.where(there(l, o), l % kpb, 0))),
                 lambda o: there(pl.program_id(2), o)))


def _mm_tn_cols(name, a, b, ws, phases=(), odd=None):
    k, m = a.shape
    rows = m if odd is None else m // 2
    tm, tn, tk = _tiles(m // 4, ws, k)
    npb, bpq = ws // tn, (m // 4) // tm
    pick = (lambda i: i) if odd is None else (lambda i: (2 * (i // bpq) + odd) * bpq + i % bpq)
    return _matmul(
        name, a, b, dims="tn", grid=(rows // tm, N_CHIPS * npb, k // tk),
        a_spec=pl.BlockSpec((tk, tm), lambda i, j, l: (l, pick(i))),
        b_spec=pl.BlockSpec((tk, tn), lambda i, j, l: (l, j)),
        o_spec=pl.BlockSpec((None, tm, tn), lambda i, j, l: (j // npb, i, j % npb)),
        out_shape=(N_CHIPS, rows, ws), out_dtype=F32, phases=phases)


def _grp_fwd(name, pooled, wgg):
    t, e = pooled.shape
    _, ng, rs, gw = wgg.shape
    tm, tn, tk = _tiles(t, gw, rs, gw)
    npb, kps = gw // tn, rs // tk
    return _matmul(
        name, pooled, wgg, dims="nn", grid=(t // tm, ng * npb, N_CHIPS * kps),
        a_spec=pl.BlockSpec((tm, tk), lambda i, j, l: (i, (j // npb) * (gw // tk) + l)),
        b_spec=pl.BlockSpec((None, None, tk, tn), lambda i, j, l: (l // kps, j // npb, l % kps, j % npb)),
        o_spec=pl.BlockSpec((tm, tn), lambda i, j, l: (i, j)),
        out_shape=(t, e), out_dtype=F32)


def _grp_bwd_x(name, dmm, wgg, phases=()):
    t, e = dmm.shape
    _, ng, rs, gw = wgg.shape
    tm, tn, tk = _tiles(t, rs, gw)
    npr, kpg = rs // tn, gw // tk
    return _matmul(
        name, dmm, wgg, dims="nt", grid=(t // tm, ng * N_CHIPS * npr, kpg),
        a_spec=pl.BlockSpec((tm, tk), lambda i, j, l: (i, (j // (N_CHIPS * npr)) * kpg + l)),
        b_spec=pl.BlockSpec(
            (None, None, tn, tk),
            lambda i, j, l: ((j % (N_CHIPS * npr)) // npr, j // (N_CHIPS * npr), j % npr, l)),
        o_spec=pl.BlockSpec((tm, tn), lambda i, j, l: (i, j)),
        out_shape=(t, e), out_dtype=F32, phases=phases)


def _grp_bwd_w(name, pooled, dmm, ng):
    t, e = pooled.shape
    gw = e // ng
    rs = gw // N_CHIPS
    _, tn, tk = _tiles(rs, gw, t)
    npb = gw // tn
    return _matmul(
        name, pooled, dmm, dims="tn", grid=(ng * N_CHIPS, npb, t // tk),
        a_spec=pl.BlockSpec((tk, rs), lambda i, j, l: (l, i)),
        b_spec=pl.BlockSpec((tk, tn), lambda i, j, l: (l, (i // N_CHIPS) * npb + j)),
        o_spec=pl.BlockSpec((None, None, rs, tn), lambda i, j, l: (i % N_CHIPS, i // N_CHIPS, 0, j)),
        out_shape=(N_CHIPS, ng, rs, gw), out_dtype=F32)


def _cast_into_slot(name, w, place, also_alone=False):
    r, c = w.shape
    tr, tc = _tile(r, 512), _tile(c, 2048)

    def body(place_ref, w_ref, o_ref, *alone):
        v = w_ref[...].astype(BF16)
        o_ref[...] = v
        for a_ref in alone:
            a_ref[...] = v

    slot = pl.BlockSpec((None, tr, tc), lambda i, j, p: (p[0], i, j))
    plain = pl.BlockSpec((tr, tc), lambda i, j, p: (i, j))
    return pl.pallas_call(
        body, name=name,
        grid_spec=pltpu.PrefetchScalarGridSpec(
            num_scalar_prefetch=1, grid=(r // tr, c // tc), in_specs=[plain],
            out_specs=[slot, plain] if also_alone else slot),
        out_shape=([jax.ShapeDtypeStruct((N_CHIPS, r, c), BF16), jax.ShapeDtypeStruct((r, c), BF16)]
                   if also_alone else jax.ShapeDtypeStruct((N_CHIPS, r, c), BF16)),
        compiler_params=_params("parallel", "parallel"))(place, w)


def _rms(v):
    return lax.rsqrt(jnp.mean(v * v, axis=-1, keepdims=True) + NORM_EPS)


def _rms_fwd(name, x, g):
    t, d = x.shape
    tr = _tile(t, 256)

    def body(x_ref, g_ref, h_ref):
        xv = x_ref[...]
        h_ref[...] = (xv * _rms(xv) * g_ref[...]).astype(BF16)

    row = pl.BlockSpec((tr, d), lambda i: (i, 0))
    vec = pl.BlockSpec((1, d), lambda i: (0, 0))
    return pl.pallas_call(
        body, name=name, grid=(t // tr,), in_specs=[row, vec], out_specs=row,
        out_shape=jax.ShapeDtypeStruct((t, d), BF16), compiler_params=_params("parallel"))(x, g)


def _post_pre_fwd(name, x, m, g_post, g_pre):
    t, d = x.shape
    tr = _tile(t, 256)

    def body(x_ref, m_ref, gp_ref, gn_ref, x1_ref, h_ref):
        mv = m_ref[...]
        x1 = x_ref[...] + mv * _rms(mv) * gp_ref[...]
        x1_ref[...] = x1
        h_ref[...] = (x1 * _rms(x1) * gn_ref[...]).astype(BF16)

    row = pl.BlockSpec((tr, d), lambda i: (i, 0))
    vec = pl.BlockSpec((1, d), lambda i: (0, 0))
    return pl.pallas_call(
        body, name=name, grid=(t // tr,), in_specs=[row, row, vec, vec], out_specs=[row, row],
        out_shape=[jax.ShapeDtypeStruct((t, d), F32), jax.ShapeDtypeStruct((t, d), BF16)],
        compiler_params=_params("parallel"))(x, m, g_post, g_pre)


def _norm_bwd(dout, nrm, r, g):
    gd = dout * g
    return r * (gd - nrm * jnp.mean(gd * nrm, axis=-1, keepdims=True))


def _loss_post_bwd(name, x1, m, g_post, target):
    t, d = x1.shape
    tr = _tile(t, 128)

    def body(x_ref, m_ref, g_ref, t_ref, loss_ref, dx_ref, dm_ref, dg_ref):
        i = pl.program_id(0)
        mv = m_ref[...]
        r = _rms(mv)
        nrm = mv * r
        err = x_ref[...] + nrm * g_ref[...] - t_ref[...]
        part = 0.5 * jnp.sum(jnp.mean(err * err, axis=-1, keepdims=True), axis=0, keepdims=True)
        dx = err / d
        dx_ref[...] = dx
        dm_ref[...] = _norm_bwd(dx, nrm, r, g_ref[...]).astype(BF16)
        dg = jnp.sum(dx * nrm, axis=0, keepdims=True)

        @pl.when(i == 0)
        def _():
            loss_ref[...] = part
            dg_ref[...] = dg

        @pl.when(i > 0)
        def _():
            loss_ref[...] += part
            dg_ref[...] += dg

    row = pl.BlockSpec((tr, d), lambda i: (i, 0))
    vec = pl.BlockSpec((1, d), lambda i: (0, 0))
    one = pl.BlockSpec((1, 1), lambda i: (0, 0))
    return pl.pallas_call(
        body, name=name, grid=(t // tr,), in_specs=[row, row, vec, row], out_specs=[one, row, row, vec],
        out_shape=[jax.ShapeDtypeStruct((1, 1), F32), jax.ShapeDtypeStruct((t, d), F32),
                   jax.ShapeDtypeStruct((t, d), BF16), jax.ShapeDtypeStruct((1, d), F32)],
        compiler_params=_params("arbitrary"))(x1, m, g_post, target)


def _mid_bwd(name, dx2, dh1, x1, g_pre, m0, g_post):
    t, d = x1.shape
    tr = _tile(t, 128)

    def body(dx2_ref, dh_ref, x_ref, gn_ref, m_ref, gp_ref, dx_ref, dm_ref, dgn_ref, dgp_ref):
        i = pl.program_id(0)
        xv = x_ref[...]
        r1 = _rms(xv)
        n1 = xv * r1
        dh = dh_ref[...]
        dx = dx2_ref[...] + _norm_bwd(dh, n1, r1, gn_ref[...])
        dx_ref[...] = dx
        mv = m_ref[...]
        r0 = _rms(mv)
        n0 = mv * r0
        dm_ref[...] = _norm_bwd(dx, n0, r0, gp_ref[...]).astype(BF16)
        dgn = jnp.sum(dh * n1, axis=0, keepdims=True)
        dgp = jnp.sum(dx * n0, axis=0, keepdims=True)

        @pl.when(i == 0)
        def _():
            dgn_ref[...] = dgn
            dgp_ref[...] = dgp

        @pl.when(i > 0)
        def _():
            dgn_ref[...] += dgn
            dgp_ref[...] += dgp

    row = pl.BlockSpec((tr, d), lambda i: (i, 0))
    vec = pl.BlockSpec((1, d), lambda i: (0, 0))
    return pl.pallas_call(
        body, name=name, grid=(t // tr,), in_specs=[row, row, row, vec, row, vec],
        out_specs=[row, row, vec, vec],
        out_shape=[jax.ShapeDtypeStruct((t, d), F32), jax.ShapeDtypeStruct((t, d), BF16),
                   jax.ShapeDtypeStruct((1, d), F32), jax.ShapeDtypeStruct((1, d), F32)],
        compiler_params=_params("arbitrary"))(dx2, dh1, x1, g_pre, m0, g_post)


def _pre_bwd(name, dx1, dh0, x, g_pre):
    t, d = x.shape
    tr = _tile(t, 128)

    def body(dx1_ref, dh_ref, x_ref, g_ref, dx_ref, dg_ref):
        i = pl.program_id(0)
        xv = x_ref[...]
        r = _rms(xv)
        nrm = xv * r
        dh = dh_ref[...]
        dx_ref[...] = dx1_ref[...] + _norm_bwd(dh, nrm, r, g_ref[...])
        dg = jnp.sum(dh * nrm, axis=0, keepdims=True)

        @pl.when(i == 0)
        def _():
            dg_ref[...] = dg

        @pl.when(i > 0)
        def _():
            dg_ref[...] += dg

    row = pl.BlockSpec((tr, d), lambda i: (i, 0))
    vec = pl.BlockSpec((1, d), lambda i: (0, 0))
    return pl.pallas_call(
        body, name=name, grid=(t // tr,), in_specs=[row, row, row, vec], out_specs=[row, vec],
        out_shape=[jax.ShapeDtypeStruct((t, d), F32), jax.ShapeDtypeStruct((1, d), F32)],
        compiler_params=_params("arbitrary"))(dx1, dh0, x, g_pre)


def _a_stats(name, pa, e):
    t = pa.shape[0]
    tr = _tile(t, 64, 8)

    def body(v_ref, mu_ref, rs_ref):
        vg, _ = _gelu(v_ref[...])
        mu = jnp.mean(vg, axis=-1, keepdims=True)
        xc = vg - mu
        mu_ref[...] = mu
        rs_ref[...] = lax.rsqrt(jnp.mean(xc * xc, axis=-1, keepdims=True) + NORM_EPS)

    col = pl.BlockSpec((tr, 1), lambda i: (i, 0))
    return pl.pallas_call(
        body, name=name, grid=(t // tr,), in_specs=[pl.BlockSpec((tr, e), lambda i: (i, 1))],
        out_specs=[col, col],
        out_shape=[jax.ShapeDtypeStruct((t, 1), F32), jax.ShapeDtypeStruct((t, 1), F32)],
        compiler_params=_params("parallel"))(pa)


def _causal(w):
    c = w.shape[0]
    keep = lax.broadcasted_iota(jnp.int32, (c, c), 0) >= lax.broadcasted_iota(jnp.int32, (c, c), 1)
    return jnp.where(keep, w, 0.0), keep


def _a_gate_fwd(name, pa, mu, rs, ln_g, ln_b, w_s, b_s3, phases=()):
    t = pa.shape[0]
    nh, c, _ = w_s.shape
    e = ln_g.shape[1]
    dh = e // nh
    rb = 2 * c if t % (2 * c) == 0 else c

    def body(u_ref, v_ref, z_ref, mu_ref, rs_ref, g_ref, b_ref, w_ref, bs_ref, y_ref):
        wc = _causal(w_ref[...])[0].astype(BF16)
        vg, _ = _gelu(v_ref[...])
        vn = ((vg - mu_ref[...]) * rs_ref[...] * g_ref[...] + b_ref[...]).astype(BF16)
        for ci in range(rb // c):
            rows = pl.ds(ci * c, c)
            sv = jnp.dot(wc, vn[ci * c:(ci + 1) * c], preferred_element_type=F32) + bs_ref[...]
            u, _ = _gelu(u_ref[rows, :])
            sz, _ = _silu(z_ref[rows, :])
            y_ref[rows, :] = (u * sv * sz).astype(BF16)

    blk = lambda off: pl.BlockSpec((rb, dh), lambda i, h: (i, off + h))
    col = pl.BlockSpec((rb, 1), lambda i, h: (i, 0))
    vec = pl.BlockSpec((1, dh), lambda i, h: (0, h))
    grid = (t // rb, nh)
    body, x_ins, x_outs, aliases, sems = _hosted(body, phases, grid, 9, 1)
    res = pl.pallas_call(
        body, name=name, grid=grid,
        in_specs=[blk(0), blk(nh), blk(2 * nh), col, col, vec, vec,
                  pl.BlockSpec((None, c, c), lambda i, h: (h, 0, 0)),
                  pl.BlockSpec((None, c, 1), lambda i, h: (h, 0, 0))] + [ANY] * len(x_ins),
        out_specs=[pl.BlockSpec((rb, dh), lambda i, h: (i, h))] + [ANY] * len(x_outs),
        out_shape=[jax.ShapeDtypeStruct((t, e), BF16)] + x_outs, input_output_aliases=aliases,
        scratch_shapes=sems,
        compiler_params=_params("arbitrary", "arbitrary"))(pa, pa, pa, mu, rs, ln_g, ln_b, w_s, b_s3, *x_ins)
    return (res[0], list(res[1:])) if phases else res[0]


def _a_gate_bwd1(name, pa, dy, mu, rs, ln_g, ln_b, w_s, b_s3, phases=()):
    t = pa.shape[0]
    nh, c, _ = w_s.shape
    e = ln_g.shape[1]
    dh = e // nh
    rb = 2 * c if t % (2 * c) == 0 else c

    def body(u_ref, v_ref, z_ref, dy_ref, mu_ref, rs_ref, g_ref, b_ref, w_ref, bs_ref,
             du_ref, dz_ref, dsv_ref, c1_ref, c2_ref):
        h = pl.program_id(1)
        wc = _causal(w_ref[...])[0].astype(BF16)
        vg, _ = _gelu(v_ref[...])
        xh = (vg - mu_ref[...]) * rs_ref[...]
        vn = (xh * g_ref[...] + b_ref[...]).astype(BF16)
        s1 = []
        s2 = []
        for ci in range(rb // c):
            rows = pl.ds(ci * c, c)
            lo, hi = ci * c, (ci + 1) * c
            sv = jnp.dot(wc, vn[lo:hi], preferred_element_type=F32) + bs_ref[...]
            u, du = _gelu(u_ref[rows, :])
            zv = z_ref[rows, :]
            sz, dsz = _silu(zv)
            dyv = dy_ref[rows, :]
            du_ref[rows, :] = (dyv * sv * sz * du).astype(BF16)
            dz_ref[rows, :] = (dyv * u * sv * dsz).astype(BF16)
            dsv = (dyv * u * sz).astype(BF16)
            dsv_ref[rows, :] = dsv
            dvn = lax.dot_general(wc, dsv, (((0,), (0,)), ((), ())), preferred_element_type=F32)
            dxh = dvn * g_ref[...]
            s1.append(jnp.sum(dxh, axis=-1, keepdims=True))
            s2.append(jnp.sum(dxh * xh[lo:hi], axis=-1, keepdims=True))
        p1 = jnp.concatenate(s1, axis=0)
        p2 = jnp.concatenate(s2, axis=0)

        @pl.when(h == 0)
        def _():
            c1_ref[...] = p1
            c2_ref[...] = p2

        @pl.when(h > 0)
        def _():
            c1_ref[...] += p1
            c2_ref[...] += p2

    blk = lambda off: pl.BlockSpec((rb, dh), lambda i, h: (i, off + h))
    col = pl.BlockSpec((rb, 1), lambda i, h: (i, 0))
    vec = pl.BlockSpec((1, dh), lambda i, h: (0, h))
    act = jax.ShapeDtypeStruct((t, e), BF16)
    stat = jax.ShapeDtypeStruct((t, 1), F32)
    grid = (t // rb, nh)
    body, x_ins, x_outs, aliases, sems = _hosted(body, phases, grid, 10, 5)
    res = pl.pallas_call(
        body, name=name, grid=grid,
        in_specs=[blk(0), blk(nh), blk(2 * nh), blk(0), col, col, vec, vec,
                  pl.BlockSpec((None, c, c), lambda i, h: (h, 0, 0)),
                  pl.BlockSpec((None, c, 1), lambda i, h: (h, 0, 0))] + [ANY] * len(x_ins),
        out_specs=[blk(0), blk(0), blk(0), col, col] + [ANY] * len(x_outs),
        out_shape=[act, act, act, stat, stat] + x_outs, input_output_aliases=aliases, scratch_shapes=sems,
        compiler_params=_params("arbitrary", "arbitrary"))(pa, pa, pa, dy, mu, rs, ln_g, ln_b, w_s, b_s3, *x_ins)
    return (res[:5], list(res[5:])) if phases else res


def _a_gate_bwd2(name, pa, dsv, mu, rs, c1, c2, ln_g, ln_b, w_s, phases=()):
    t = pa.shape[0]
    nh, c, _ = w_s.shape
    e = ln_g.shape[1]
    dh = e // nh
    rb = 2 * c if t % (2 * c) == 0 else c

    def body(v_ref, dsv_ref, mu_ref, rs_ref, c1_ref, c2_ref, g_ref, b_ref, w_ref,
             dv_ref, dw_ref, dbs_ref, dg_ref, db_ref):
        i = pl.program_id(1)
        wcf, keep = _causal(w_ref[...])
        wc = wcf.astype(BF16)
        vg, dvg = _gelu(v_ref[...])
        rsv = rs_ref[...]
        xh = (vg - mu_ref[...]) * rsv
        vn = (xh * g_ref[...] + b_ref[...]).astype(BF16)
        dw = jnp.zeros((c, c), F32)
        dbs = jnp.zeros((c, 1), F32)
        dvns = []
        for ci in range(rb // c):
            lo, hi = ci * c, (ci + 1) * c
            dsv = dsv_ref[pl.ds(lo, c), :]
            dvns.append(lax.dot_general(wc, dsv, (((0,), (0,)), ((), ())), preferred_element_type=F32))
            dw += lax.dot_general(dsv, vn[lo:hi], (((1,), (1,)), ((), ())), preferred_element_type=F32)
            dbs += jnp.sum(dsv.astype(F32), axis=-1, keepdims=True)
        dvn = jnp.concatenate(dvns, axis=0)
        dxh = dvn * g_ref[...]
        dvv = rsv * (dxh - c1_ref[...] * (1.0 / e) - xh * (c2_ref[...] * (1.0 / e)))
        dv_ref[...] = (dvv * dvg).astype(BF16)
        dw = jnp.where(keep, dw, 0.0)
        dg = jnp.sum(dvn * xh, axis=0, keepdims=True)
        db = jnp.sum(dvn, axis=0, keepdims=True)

        @pl.when(i == 0)
        def _():
            dw_ref[...] = dw
            dbs_ref[...] = dbs
            dg_ref[...] = dg
            db_ref[...] = db

        @pl.when(i > 0)
        def _():
            dw_ref[...] += dw
            dbs_ref[...] += dbs
            dg_ref[...] += dg
            db_ref[...] += db

    col = pl.BlockSpec((rb, 1), lambda h, i: (i, 0))
    vec = pl.BlockSpec((1, dh), lambda h, i: (0, h))
    hblk = pl.BlockSpec((rb, dh), lambda h, i: (i, h))
    grid = (nh, t // rb)
    body, x_ins, x_outs, aliases, sems = _hosted(body, phases, grid, 9, 5)
    res = pl.pallas_call(
        body, name=name, grid=grid,
        in_specs=[pl.BlockSpec((rb, dh), lambda h, i: (i, nh + h)), hblk, col, col, col, col, vec, vec,
                  pl.BlockSpec((None, c, c), lambda h, i: (h, 0, 0))] + [ANY] * len(x_ins),
        out_specs=[hblk, pl.BlockSpec((None, c, c), lambda h, i: (h, 0, 0)),
                   pl.BlockSpec((None, c, 1), lambda h, i: (h, 0, 0)), vec, vec] + [ANY] * len(x_outs),
        out_shape=[jax.ShapeDtypeStruct((t, e), BF16), jax.ShapeDtypeStruct((nh, c, c), F32),
                   jax.ShapeDtypeStruct((nh, c, 1), F32), jax.ShapeDtypeStruct((1, e), F32),
                   jax.ShapeDtypeStruct((1, e), F32)] + x_outs,
        input_output_aliases=aliases, scratch_shapes=sems,
        compiler_params=_params("arbitrary", "arbitrary"))(pa, dsv, mu, rs, c1, c2, ln_g, ln_b, w_s, *x_ins)
    return (res[:5], list(res[5:])) if phases else res


def _pool(name, src, e, seq, backward):
    t = src.shape[0]
    ng = len(POOL_WINDOWS)
    gw = e // ng
    cw = _tile(gw, 256)

    def shifted(a, j, pos):
        if backward:
            return jnp.where(pos < seq - j, pltpu.roll(a, seq - j, 0), 0.0)
        return jnp.where(pos >= j, pltpu.roll(a, j, 0), 0.0)

    def body(p_ref, o_ref):
        grp = (pl.program_id(1) * cw) // gw
        pos = lax.broadcasted_iota(jnp.int32, (seq, cw), 0)
        posf = (pos + 1).astype(F32)
        for k, w in enumerate(POOL_WINDOWS):

            @pl.when(grp == k)
            def _(w=w):
                pv = p_ref[...]
                cnt = jnp.minimum(posf, float(w))
                acc = pv / cnt if backward else pv
                j = 1
                while j < w:
                    acc = acc + shifted(acc, j, pos)
                    j *= 2
                out = acc - pv if backward else acc / cnt - pv
                o_ref[...] = out.astype(BF16)

    spec = pl.BlockSpec((seq, cw), lambda s, j: (s, j))
    return pl.pallas_call(
        body, name=name, grid=(t // seq, e // cw), in_specs=[spec], out_specs=spec,
        out_shape=jax.ShapeDtypeStruct((t, e), BF16),
        compiler_params=_params("parallel", "parallel"))(src)


def _b_gate_fwd(name, mm, pb, b_grp, scale):
    t, e = mm.shape
    tr, tc = _tile(t, 512), _tile(e, 1024)
    nc = e // tc

    def body(mm_ref, z_ref, b_ref, s_ref, y_ref):
        sz, _ = _silu(z_ref[...])
        y_ref[...] = ((mm_ref[...] + b_ref[...]) * s_ref[...] * sz).astype(BF16)

    blk = pl.BlockSpec((tr, tc), lambda i, j: (i, j))
    vec = pl.BlockSpec((1, tc), lambda i, j: (0, j))
    return pl.pallas_call(
        body, name=name, grid=(t // tr, nc),
        in_specs=[blk, pl.BlockSpec((tr, tc), lambda i, j: (i, nc + j)), vec, vec], out_specs=blk,
        out_shape=jax.ShapeDtypeStruct((t, e), BF16),
        compiler_params=_params("parallel", "parallel"))(mm, pb, b_grp, scale)


def _b_gate_bwd(name, dy, mm, pb, b_grp, scale):
    t, e = mm.shape
    tr, tc = _tile(t, 512), _tile(e, 1024)
    nc = e // tc

    def body(dy_ref, mm_ref, z_ref, b_ref, s_ref, dmm_ref, dz_ref, db_ref, ds_ref):
        i = pl.program_id(1)
        sz, dsz = _silu(z_ref[...])
        dyv = dy_ref[...]
        mb = mm_ref[...] + b_ref[...]
        dmixed = dyv * sz
        dmm = dmixed * s_ref[...]
        dmm_ref[...] = dmm.astype(BF16)
        dz_ref[...] = (dyv * (mb * s_ref[...]) * dsz).astype(BF16)
        db = jnp.sum(dmm, axis=0, keepdims=True)
        ds = jnp.sum(dmixed * mb, axis=0, keepdims=True)

        @pl.when(i == 0)
        def _():
            db_ref[...] = db
            ds_ref[...] = ds

        @pl.when(i > 0)
        def _():
            db_ref[...] += db
            ds_ref[...] += ds

    blk = pl.BlockSpec((tr, tc), lambda j, i: (i, j))
    vec = pl.BlockSpec((1, tc), lambda j, i: (0, j))
    act = jax.ShapeDtypeStruct((t, e), BF16)
    stat = jax.ShapeDtypeStruct((1, e), F32)
    return pl.pallas_call(
        body, name=name, grid=(nc, t // tr),
        in_specs=[blk, blk, pl.BlockSpec((tr, tc), lambda j, i: (i, nc + j)), vec, vec],
        out_specs=[blk, blk, vec, vec], out_shape=[act, act, stat, stat],
        compiler_params=_params("parallel", "arbitrary"))(dy, mm, pb, b_grp, scale)


def _position():
    return lax.axis_index("x"), lax.axis_index("y"), lax.axis_index("c")


def _other_chips(x, y):
    return [(1 - x, y), (x, 1 - y), (1 - x, 1 - y)]


def _half(ref, c):
    n = ref.shape[0] // 2
    return ref.at[pl.ds(c * n, n)]


def _part(ref, c, part):
    q, nq, count = part if len(part) == 3 else (*part, 1)
    n = ref.shape[0] // (2 * nq)
    return ref.at[pl.ds(c * nq * n + q * n, count * n)]


def _gather_phase(slots, small=None, part=(0, 1)):
    nw = len(slots)
    ns = 0 if small is None else 1
    n_ici = 3 * (nw + ns)
    n_sem = n_ici + 3 * nw + ns

    def copies(ins, outs, send_sems, recv_sems):
        x, y, c = _position()
        sibling = (x, y, 1 - c)

        def ici(j, k, slot, to):
            return pltpu.make_async_remote_copy(
                src_ref=_part(ins[k].at[slot], c, part), dst_ref=_part(outs[k].at[slot], c, part),
                send_sem=send_sems.at[j * nw + k], recv_sem=recv_sems.at[j * nw + k],
                device_id=to, device_id_type=MESH)

        def ici_small(j, slot, to):
            return pltpu.make_async_remote_copy(
                src_ref=ins[nw], dst_ref=outs[nw].at[slot], send_sem=send_sems.at[3 * nw + j],
                recv_sem=recv_sems.at[3 * nw + j], device_id=to, device_id_type=MESH)

        def d2d(j, k, slot, half):
            return pltpu.make_async_remote_copy(
                src_ref=_part(outs[k].at[slot], half, part), dst_ref=_part(outs[k].at[slot], half, part),
                send_sem=send_sems.at[n_ici + j * nw + k], recv_sem=recv_sems.at[n_ici + j * nw + k],
                device_id=sibling, device_id_type=MESH)

        def local():
            return pltpu.make_async_copy(ins[nw], outs[nw].at[2 * x + y], send_sems.at[n_sem - 1])

        return x, y, c, ici, ici_small, d2d, local

    def start(ins, outs, send_sems, recv_sems):
        x, y, c, ici, ici_small, _, local = copies(ins, outs, send_sems, recv_sems)
        if ns:
            local().start()
        for j, (cx, cy) in enumerate(_other_chips(x, y)):
            for k in range(nw):
                ici(j, k, 2 * x + y, (cx, cy, c)).start()
            if ns:
                ici_small(j, 2 * x + y, (cx, cy, c)).start()

    def relay(ins, outs, send_sems, recv_sems):
        x, y, c, ici, _, d2d, _ = copies(ins, outs, send_sems, recv_sems)
        for j, (cx, cy) in enumerate(_other_chips(x, y)):
            for k in range(nw):
                ici(j, k, 2 * cx + cy, (x, y, c)).wait_recv()
                d2d(j, k, 2 * cx + cy, c).start()

    def finish(ins, outs, send_sems, recv_sems):
        x, y, c, ici, ici_small, d2d, local = copies(ins, outs, send_sems, recv_sems)
        chips = _other_chips(x, y)
        for j, (cx, cy) in enumerate(chips):
            if ns:
                ici_small(j, 2 * cx + cy, (x, y, c)).wait_recv()
            for k in range(nw):
                d2d(j, k, 2 * cx + cy, 1 - c).wait_recv()
        for j, (cx, cy) in enumerate(chips):
            for k in range(nw):
                ici(j, k, 2 * x + y, (cx, cy, c)).wait_send()
                d2d(j, k, 2 * cx + cy, c).wait_send()
            if ns:
                ici_small(j, 2 * x + y, (cx, cy, c)).wait_send()
        if ns:
            local().wait()

    outs = [jax.ShapeDtypeStruct(s.shape, s.dtype) for s in slots]
    if ns:
        outs.append(jax.ShapeDtypeStruct((N_CHIPS,) + small.shape, small.dtype))
    return _Phase(list(slots) + ([small] if ns else []), outs, {k: k for k in range(nw)}, n_sem, start, finish, relay)


def _staged_gather_phases(slots, small=None):
    nw = len(slots)
    ns = 0 if small is None else 1
    fwd_base = 2 * nw

    def copies(ins, outs, send_sems, recv_sems, alone=False):
        x, y, c = _position()
        sibling = (x, y, 1 - c)
        home = (lambda k, slot: outs[k]) if alone else (lambda k, slot: outs[k].at[slot])

        def direct(j, k, slot, to):
            return pltpu.make_async_remote_copy(
                src_ref=_half(ins[k].at[slot], c), dst_ref=_half(outs[k].at[slot], c),
                send_sem=send_sems.at[j * nw + k], recv_sem=recv_sems.at[j * nw + k],
                device_id=to, device_id_type=MESH)

        def relayed(p, k, slot, to):
            return pltpu.make_async_remote_copy(
                src_ref=_part(ins[k].at[slot], c, (p, 2)), dst_ref=_part(home(k, slot), c, (p, 2)),
                send_sem=send_sems.at[p * nw + k], recv_sem=recv_sems.at[p * nw + k],
                device_id=to, device_id_type=MESH)

        def d2d(j, k, slot, half):
            return pltpu.make_async_remote_copy(
                src_ref=_half(home(k, slot), half), dst_ref=_half(home(k, slot), half),
                send_sem=send_sems.at[fwd_base + j * nw + k], recv_sem=recv_sems.at[fwd_base + j * nw + k],
                device_id=sibling, device_id_type=MESH)

        def small_copy(j, slot, to):
            return pltpu.make_async_remote_copy(
                src_ref=ins[nw], dst_ref=outs[nw].at[slot], send_sem=send_sems.at[4 * nw + j],
                recv_sem=recv_sems.at[4 * nw + j], device_id=to, device_id_type=MESH)

        def local():
            return pltpu.make_async_copy(ins[nw], outs[nw].at[2 * x + y], send_sems.at[4 * nw + 3])

        chips = _other_chips(x, y)
        return x, y, c, chips, [2 * cx + cy for cx, cy in chips], direct, relayed, d2d, small_copy, local

    def start_1(*refs):
        x, y, c, chips, _, direct, _, _, small_copy, local = copies(*refs)
        for j in range(2):
            for k in range(nw):
                direct(j, k, 2 * x + y, (*chips[j], c)).start()
        if ns:
            local().start()
            for j in range(3):
                small_copy(j, 2 * x + y, (*chips[j], c)).start()

    def relay_1(*refs):
        x, y, c, _, slot, direct, _, d2d, _, _ = copies(*refs)
        for j in range(2):
            for k in range(nw):
                direct(j, k, slot[j], (x, y, c)).wait_recv()
                d2d(j, k, slot[j], c).start()

    def finish_1(*refs):
        x, y, c, chips, slot, direct, _, d2d, small_copy, local = copies(*refs)
        if ns:
            for j in range(3):
                small_copy(j, slot[j], (x, y, c)).wait_recv()
        for j in range(2):
            for k in range(nw):
                d2d(j, k, slot[j], 1 - c).wait_recv()
        for j in range(2):
            for k in range(nw):
                direct(j, k, 2 * x + y, (*chips[j], c)).wait_send()
                d2d(j, k, slot[j], c).wait_send()
        if ns:
            for j in range(3):
                small_copy(j, 2 * x + y, (*chips[j], c)).wait_send()
            local().wait()

    shapes = [jax.ShapeDtypeStruct(s.shape, s.dtype) for s in slots]
    keep = {k: k for k in range(nw)}

    def second(landed, alone=False):
        def start_2(*refs):
            x, y, c, chips, slot, _, relayed, _, _, _ = copies(*refs, alone)
            for j in range(2):
                for k in range(nw):
                    relayed(1 - j, k, slot[j], (*chips[1 - j], c)).start()

        def relay_2(*refs):
            x, y, c, _, slot, _, relayed, d2d, _, _ = copies(*refs, alone)
            for k in range(nw):
                for p in range(2):
                    relayed(p, k, slot[2], (x, y, c)).wait_recv()
                d2d(0, k, slot[2], c).start()

        def finish_2(*refs):
            x, y, c, chips, slot, _, relayed, d2d, _, _ = copies(*refs, alone)
            for k in range(nw):
                d2d(0, k, slot[2], 1 - c).wait_recv()
            for k in range(nw):
                d2d(0, k, slot[2], c).wait_send()
                for j in range(2):
                    relayed(1 - j, k, slot[j], (*chips[1 - j], c)).wait_send()

        outs = [jax.ShapeDtypeStruct(s.shape[1:], s.dtype) for s in slots] if alone else shapes
        return _Phase(list(landed), outs, {} if alone else keep, 3 * nw, start_2, finish_2, relay_2)

    first_outs = shapes + ([jax.ShapeDtypeStruct((N_CHIPS,) + small.shape, small.dtype)] if ns else [])
    first = _Phase(list(slots) + ([small] if ns else []), first_outs, keep, 4 * nw + 4, start_1, finish_1, relay_1)
    return first, second


def _swap_phase(grads):
    nw = len(grads)

    def swaps(ins, outs, send_sems, recv_sems):
        x, y, c = _position()
        cps = []
        for k in range(nw):
            n = ins[k].shape[1] // 2
            cps.append(pltpu.make_async_remote_copy(
                src_ref=ins[k].at[pl.ds(0, N_CHIPS), pl.ds((1 - c) * n, n)], dst_ref=outs[k],
                send_sem=send_sems.at[k], recv_sem=recv_sems.at[k],
                device_id=(x, y, 1 - c), device_id_type=MESH))
        return cps

    def start(*refs):
        for cp in swaps(*refs):
            cp.start()

    def finish(*refs):
        for cp in swaps(*refs):
            cp.wait()

    halves = [jax.ShapeDtypeStruct((g.shape[0], g.shape[1] // 2, g.shape[2]), F32) for g in grads]
    return _Phase(grads, halves, {}, nw, start, finish)


def _add_halves(name, grad, theirs, place):
    s, half, w = theirs.shape
    tr, tc = _tile(half, 256), _tile(w, 2048)
    nrt = half // tr

    def body(place_ref, a_ref, b_ref, h_ref, f_ref):
        v = a_ref[...] + b_ref[...]
        h_ref[...] = v.astype(BF16)

        @pl.when(pl.program_id(2) == place_ref[0])
        def _():
            f_ref[...] = v

    return pl.pallas_call(
        body, name=name,
        grid_spec=pltpu.PrefetchScalarGridSpec(
            num_scalar_prefetch=1, grid=(nrt, w // tc, s),
            in_specs=[pl.BlockSpec((None, tr, tc), lambda i, j, q, p: (q, p[1] * nrt + i, j)),
                      pl.BlockSpec((None, tr, tc), lambda i, j, q, p: (q, i, j))],
            out_specs=[pl.BlockSpec((None, tr, tc), lambda i, j, q, p: (q, i, j)),
                       pl.BlockSpec((tr, tc), lambda i, j, q, p: (p[1] * nrt + i, j))]),
        out_shape=[jax.ShapeDtypeStruct(theirs.shape, BF16), jax.ShapeDtypeStruct((2 * half, w), F32)],
        compiler_params=_params("parallel", "parallel", "arbitrary"))(place, grad, theirs)


def _scatter_phase(own, pb, part=(0, 1), got=None):
    nw = len(own)

    def copies(ins, outs, send_sems, recv_sems):
        own_in, pbs = ins[:nw], ins[nw:2 * nw]
        own_out, gots = outs[:nw], outs[nw:]
        x, y, c = _position()
        sibling = (x, y, 1 - c)

        def d2d_own(k, half):
            return pltpu.make_async_remote_copy(
                src_ref=_part(own_in[k], half, part), dst_ref=_part(own_out[k], half, part),
                send_sem=send_sems.at[k], recv_sem=recv_sems.at[k], device_id=sibling, device_id_type=MESH)

        def ici(j, k, shard, to):
            n = pbs[k].shape[1] // part[1]
            rows = pl.ds(part[0] * n, (part[2] if len(part) == 3 else 1) * n)
            return pltpu.make_async_remote_copy(
                src_ref=pbs[k].at[shard, rows], dst_ref=_part(gots[k].at[j], c, part),
                send_sem=send_sems.at[nw + j * nw + k], recv_sem=recv_sems.at[nw + j * nw + k],
                device_id=to, device_id_type=MESH)

        def d2d(j, k, half):
            return pltpu.make_async_remote_copy(
                src_ref=_part(gots[k].at[j], half, part), dst_ref=_part(gots[k].at[j], half, part),
                send_sem=send_sems.at[4 * nw + j * nw + k], recv_sem=recv_sems.at[4 * nw + j * nw + k],
                device_id=sibling, device_id_type=MESH)

        return x, y, c, d2d_own, ici, d2d

    def start(*refs):
        x, y, c, d2d_own, ici, _ = copies(*refs)
        for k in range(nw):
            d2d_own(k, c).start()
        for j, (cx, cy) in enumerate(_other_chips(x, y)):
            for k in range(nw):
                ici(j, k, 2 * cx + cy, (cx, cy, c)).start()

    def relay(*refs):
        x, y, c, _, ici, d2d = copies(*refs)
        for j in range(3):
            for k in range(nw):
                ici(j, k, 2 * x + y, (x, y, c)).wait_recv()
                d2d(j, k, c).start()

    def finish(*refs):
        x, y, c, d2d_own, ici, d2d = copies(*refs)
        for k in range(nw):
            d2d_own(k, 1 - c).wait_recv()
        for j in range(3):
            for k in range(nw):
                d2d(j, k, 1 - c).wait_recv()
        for k in range(nw):
            d2d_own(k, c).wait_send()
        for j, (cx, cy) in enumerate(_other_chips(x, y)):
            for k in range(nw):
                ici(j, k, 2 * cx + cy, (cx, cy, c)).wait_send()
                d2d(j, k, c).wait_send()

    own_shape = [jax.ShapeDtypeStruct(o.shape, F32) for o in own]
    got_shape = [jax.ShapeDtypeStruct((3,) + o.shape, BF16) for o in own]
    aliases = {k: k for k in range(nw)}
    if got is not None:
        aliases.update({2 * nw + k: nw + k for k in range(nw)})
    return _Phase(list(own) + list(pb) + list(got or []), own_shape + got_shape, aliases, 7 * nw, start, finish, relay)


def _gather_small(part):
    def body(p_ref, o_ref, send_sems, recv_sems, loc_sem):
        x, y, c = _position()
        me = 4 * x + 2 * y + c
        mine = pltpu.make_async_copy(p_ref, o_ref.at[me], loc_sem)
        mine.start()
        sent = []
        for mask in range(1, N_DEV):
            dx, dy, dc = (mask >> 2) & 1, (mask >> 1) & 1, mask & 1
            px, py, pc = x ^ dx, y ^ dy, c ^ dc
            cp = pltpu.make_async_remote_copy(
                src_ref=p_ref, dst_ref=o_ref.at[me], send_sem=send_sems.at[mask - 1],
                recv_sem=recv_sems.at[mask - 1], device_id=(px, py, pc), device_id_type=MESH)
            cp.start()
            sent.append((cp, 4 * px + 2 * py + pc))
        for mask in range(1, N_DEV):
            cp, peer = sent[mask - 1]
            pltpu.make_async_remote_copy(
                src_ref=p_ref, dst_ref=o_ref.at[peer], send_sem=send_sems.at[mask - 1],
                recv_sem=recv_sems.at[mask - 1], device_id=(x, y, c), device_id_type=MESH).wait_recv()
        for cp, _ in sent:
            cp.wait_send()
        mine.wait()

    return pl.pallas_call(
        body, name="gather_small_grads", in_specs=[ANY], out_specs=ANY,
        out_shape=jax.ShapeDtypeStruct((N_DEV,) + part.shape, F32),
        scratch_shapes=[pltpu.SemaphoreType.DMA((N_DEV - 1,)), pltpu.SemaphoreType.DMA((N_DEV - 1,)),
                        pltpu.SemaphoreType.DMA(())],
    )(part)


def _sum_slots(name, parts):
    s, r, c = parts.shape
    tr = _tile(r, 512, 8)

    def body(p_ref, o_ref):
        acc = p_ref[0]
        for q in range(1, s):
            acc = acc + p_ref[q]
        o_ref[...] = acc

    return pl.pallas_call(
        body, name=name, grid=(r // tr,), in_specs=[pl.BlockSpec((s, tr, c), lambda i: (0, i, 0))],
        out_specs=pl.BlockSpec((tr, c), lambda i: (i, 0)), out_shape=jax.ShapeDtypeStruct((r, c), F32),
        compiler_params=_params("parallel"))(parts)


def _adamw_math(w, g, m, v):
    m = ADAM_B1 * m + (1.0 - ADAM_B1) * g
    v = ADAM_B2 * v + (1.0 - ADAM_B2) * (g * g)
    m_hat = m / (1.0 - ADAM_B1 ** ADAM_STEP)
    v_hat = v / (1.0 - ADAM_B2 ** ADAM_STEP)
    delta = -ADAM_LR * (m_hat / (jnp.sqrt(v_hat) + ADAM_EPS) + ADAM_WD * w)
    return delta, m, v


def _adamw(name, w, m, v, own, got=None, odd=None, into=None):
    r, c = w.shape
    rows = r if odd is None else r // 2
    tr, tc = _tile(r // 4, 256, 8) if odd is not None else _tile(r, 256, 8), _tile(c, 1024)
    bpq = (r // 4) // tr if odd is not None else 1
    pick = (lambda i: i) if odd is None else (lambda i: (2 * (i // bpq) + odd) * bpq + i % bpq)
    n_in = 4 + (0 if got is None else 1)

    def body(*refs):
        w_ref, m_ref, v_ref, own_ref = refs[:4]
        outs = refs[n_in + (0 if into is None else 4):]
        g = own_ref[...]
        if got is not None:
            for j in range(3):
                g = g + refs[4][j].astype(F32)
        delta, mn, vn = _adamw_math(w_ref[...], g, m_ref[...], v_ref[...])
        outs[0][...] = g
        outs[1][...] = delta
        outs[2][...] = mn
        outs[3][...] = vn

    full = pl.BlockSpec((tr, tc), lambda i, j: (pick(i), j))
    ins, args = [full] * 3 + [pl.BlockSpec((tr, tc), lambda i, j: (i, j))], [w, m, v, own]
    if got is not None:
        ins.append(pl.BlockSpec((3, tr, tc), lambda i, j: (0, i, j)))
        args.append(got)
    aliases = {}
    if into is not None:
        ins += [ANY] * 4
        args += list(into)
        aliases = {n_in + q: q for q in range(4)}
    return pl.pallas_call(
        body, name=name, grid=(rows // tr, c // tc), in_specs=ins, out_specs=[full] * 4,
        out_shape=[jax.ShapeDtypeStruct((r, c), F32)] * 4, input_output_aliases=aliases,
        compiler_params=_params("parallel", "parallel"))(*args)


def _pack(parts):
    return jnp.concatenate([p.reshape(-1) for p in parts]).reshape(-1, LANES)


def _unpack(packed, shapes):
    flat = packed.reshape(-1)
    out, off = [], 0
    for s in shapes:
        n = 1
        for d in s:
            n *= d
        out.append(flat[off:off + n].reshape(s))
        off += n
    return out


def kernel(x, pre_norm, post_norm, a_w_in, a_ln_g, a_ln_b, a_w_s, a_b_s, a_w_out, b_w_in, b_w_grp, b_b_grp, b_scale, b_w_out, loss_target, m_pre_norm, m_post_norm, m_a_w_in, m_a_ln_g, m_a_ln_b, m_a_w_s, m_a_b_s, m_a_w_out, m_b_w_in, m_b_w_grp, m_b_b_grp, m_b_scale, m_b_w_out, v_pre_norm, v_post_norm, v_a_w_in, v_a_ln_g, v_a_ln_b, v_a_w_s, v_a_b_s, v_a_w_out, v_b_w_in, v_b_w_grp, v_b_b_grp, v_b_scale, v_b_w_out):
    nb, seq, d = x.shape
    t = nb * seq
    e = a_ln_g.shape[1]
    nh, chunk = a_w_s.shape[1], a_w_s.shape[2]
    ng, rs, gw = b_w_grp.shape[1], b_w_grp.shape[2], b_w_grp.shape[3]
    wa, wb = a_w_in.shape[2], b_w_in.shape[2]
    cx, cy = lax.axis_index("x"), lax.axis_index("y")
    chip = 2 * cx + cy

    big_w = [a_w_in.reshape(d, wa), a_w_out.reshape(e // N_CHIPS, d), b_w_in.reshape(d, wb),
             b_w_grp.reshape(ng * rs, gw), b_w_out.reshape(e // N_CHIPS, d)]
    big_m = [m_a_w_in.reshape(d, wa), m_a_w_out.reshape(e // N_CHIPS, d), m_b_w_in.reshape(d, wb),
             m_b_w_grp.reshape(ng * rs, gw), m_b_w_out.reshape(e // N_CHIPS, d)]
    big_v = [v_a_w_in.reshape(d, wa), v_a_w_out.reshape(e // N_CHIPS, d), v_b_w_in.reshape(d, wb),
             v_b_w_grp.reshape(ng * rs, gw), v_b_w_out.reshape(e // N_CHIPS, d)]
    names = ["a_w_in", "a_w_out", "b_w_in", "b_w_grp", "b_w_out"]
    place = jnp.stack([chip, lax.axis_index("c")]).astype(jnp.int32)
    slots = [_cast_into_slot("cast_" + n, w, place, also_alone=(n == "a_w_in")) for n, w in zip(names, big_w)]
    slots[0], wa_own = slots[0]
    small_w = jnp.concatenate([b_b_grp.reshape(ng, rs), b_scale.reshape(ng, rs)], axis=0)
    xf = x.reshape(t, d)
    tgt = loss_target.reshape(t, d)
    g_pre0, g_pre1 = pre_norm[0:1], pre_norm[1:2]
    g_post0, g_post1 = post_norm[0:1], post_norm[1:2]
    w_s = a_w_s.reshape(nh, chunk, chunk)
    b_s3 = a_b_s.reshape(nh, chunk, 1)

    h0 = _rms_fwd("pre_norm0", xf, g_pre0)
    order = jnp.stack([chip, chip ^ 2, chip ^ 1, chip ^ 3]).astype(jnp.int32)
    near, far = _staged_gather_phases(slots[:1], small_w)
    near_o, far_o = _staged_gather_phases(slots[1:2])
    near_b, far_b = _staged_gather_phases(slots[2:3])
    pa, (wa_g, gsmall) = _mm_nn_cols("a_in_proj_own", h0, wa_own, [near], order=order, span=(0, 1))
    pa, (wa_far, wao_g) = _mm_nn_cols(
        "a_in_proj_near", h0, wa_g, [far([wa_g], alone=True), near_o], order=order, span=(1, 2), into=pa)
    pa, (wao_g,) = _mm_nn_cols("a_in_proj_far", h0, wa_far, [far_o([wao_g])], order=order, span=(3, 1), into=pa)
    wao_g = wao_g.reshape(e, d)
    b_grp_full = jnp.transpose(gsmall[:, :ng, :], (1, 0, 2)).reshape(1, e)
    scale_full = gsmall[:, ng:, :].reshape(1, e)
    mu, rstd = _a_stats("a_ln_stats", pa, e)
    y0, (wb_g,) = _a_gate_fwd("a_gate", pa, mu, rstd, a_ln_g, a_ln_b, w_s, b_s3, [near_b])
    m0, (wb_g,) = _mm_nn("a_out_proj", y0, wao_g, phases=[far_b([wb_g])])
    x1, h1 = _post_pre_fwd("post0_pre1", xf, m0, g_post0, g_pre1)
    pb, (wg_g, wbo_g) = _mm_nn_cols("b_in_proj", h1, wb_g, [_gather_phase(slots[3:5])])
    wg_g = wg_g.reshape(N_CHIPS, ng, rs, gw)
    wbo_g = wbo_g.reshape(e, d)
    pooled = _pool("b_pool", pb, e, seq, backward=False)
    mm = _grp_fwd("b_grp_proj", pooled, wg_g)
    y1 = _b_gate_fwd("b_gate", mm, pb, b_grp_full, scale_full)
    m1 = _mm_nn("b_out_proj", y1, wbo_g)
    loss, dx2, dm1, dg_post1 = _loss_post_bwd("loss_post1_bwd", x1, m1, g_post1, tgt)

    def chip_sum(n, g, theirs):
        pbk, ownk = _add_halves("chip_sum_" + n, g, theirs, place)
        return [ownk], [pbk]

    reduced = {}
    dy1 = _mm_nt("b_out_dx", dm1, wbo_g)
    g_wbo = _mm_tn("b_out_dw", y1, dm1).reshape(N_CHIPS, e // N_CHIPS, d)
    dmm, dzb, db_grp, dscale = _b_gate_bwd("b_gate_bwd", dy1, mm, pb, b_grp_full, scale_full)
    dpooled, (th,) = _grp_bwd_x("b_grp_dx", dmm, wg_g, [_swap_phase([g_wbo])])
    s_bo = chip_sum("b_w_out", g_wbo, th)
    g_wg = _grp_bwd_w("b_grp_dw", pooled, dmm, ng).reshape(N_CHIPS, ng * rs, gw)
    dp = _pool("b_pool_bwd", dpooled, e, seq, backward=True)
    dpb = jnp.concatenate([dp, dzb], axis=1)
    dh1, (own, got, th) = _mm_nt_cols("b_in_dx", dpb, wb_g, [_scatter_phase(*s_bo), _swap_phase([g_wg])])
    reduced["b_w_out"] = (own, got)
    s_g = chip_sum("b_w_grp", g_wg, th)
    g_wb, reduced["b_w_grp"] = _mm_tn_cols("b_in_dw", h1, dpb, wb, [_scatter_phase(*s_g)])
    dx1, dm0, dg_pre1, dg_post0 = _mid_bwd("pre1_post0_bwd", dx2, dh1, x1, g_pre1, m0, g_post0)
    dy0, (th,) = _mm_nt("a_out_dx", dm0, wao_g, phases=[_swap_phase([g_wb])])
    s_b = chip_sum("b_w_in", g_wb, th)
    g_wao, (own, got) = _mm_tn("a_out_dw", y0, dm0, phases=[_scatter_phase(*s_b, part=(0, 4))])
    g_wao = g_wao.reshape(N_CHIPS, e // N_CHIPS, d)
    (du, dz, dsv, c1, c2), (own, got, th) = _a_gate_bwd1(
        "a_gate_bwd1", pa, dy0, mu, rstd, a_ln_g, a_ln_b, w_s, b_s3,
        [_scatter_phase([own], s_b[1], part=(1, 4, 2), got=[got]), _swap_phase([g_wao])])
    s_ao = chip_sum("a_w_out", g_wao, th)
    (dv, dw_s, db_s, dln_g, dln_b), reduced["b_w_in"] = _a_gate_bwd2(
        "a_gate_bwd2", pa, dsv, mu, rstd, c1, c2, a_ln_g, a_ln_b, w_s,
        [_scatter_phase([own], s_b[1], part=(3, 4), got=[got])])
    dpa = jnp.concatenate([du, dv, dz], axis=1)
    g_lo, reduced["a_w_out"] = _mm_tn_cols("a_in_dw_even", h0, dpa, wa, [_scatter_phase(*s_ao)], odd=0)
    g_hi, (th,) = _mm_tn_cols("a_in_dw_odd", h0, dpa, wa, [_swap_phase([g_lo])], odd=1)
    s_lo = chip_sum("a_w_in_even", g_lo, th)
    dh0, (th, own_lo, got_lo) = _mm_nt_cols(
        "a_in_dx_top", dpa, wa_g, [_swap_phase([g_hi]), _scatter_phase(*s_lo)], rows=(0, 2),
        elsewhere=(order, wa_far))
    s_hi = chip_sum("a_w_in_odd", g_hi, th)
    dh0, (own_hi, got_hi) = _mm_nt_cols(
        "a_in_dx_bottom", dpa, wa_g, [_scatter_phase(*s_hi)], rows=(1, 2), into=dh0, elsewhere=(order, wa_far))
    grad_x, dg_pre0 = _pre_bwd("pre0_bwd", dx1, dh0, xf, g_pre0)
    big_out = []
    for k, n in enumerate(names):
        if n == "a_w_in":
            even = _adamw("adamw_a_w_in_even", big_w[k], big_m[k], big_v[k], own_lo, got_lo, odd=0)
            big_out.append(_adamw("adamw_a_w_in_odd", big_w[k], big_m[k], big_v[k], own_hi, got_hi, odd=1, into=even))
        else:
            big_out.append(_adamw("adamw_" + n, big_w[k], big_m[k], big_v[k], *reduced[n]))

    small_shapes = [(2, d), (2, d), (1, e), (1, e), (1, nh, chunk, chunk), (1, nh, chunk), (1, e), (1, e)]
    part = _pack([jnp.concatenate([dg_pre0, dg_pre1], axis=0), jnp.concatenate([dg_post0, dg_post1], axis=0),
                  dln_g, dln_b, dw_s, db_s, db_grp, dscale])
    g_small = _sum_slots("sum_small_grads", _gather_small(part))
    g_pre, g_post, g_lng, g_lnb, g_ws, g_bs, g_bgrp_full, g_scale_full = _unpack(g_small, small_shapes)
    g_bgrp = lax.dynamic_slice_in_dim(g_bgrp_full.reshape(ng, N_CHIPS, rs), chip, 1, axis=1).reshape(1, ng, rs)
    g_scale = lax.dynamic_slice_in_dim(g_scale_full.reshape(N_CHIPS, gw), chip, 1, axis=0)
    small_names = ["pre_norm", "post_norm", "a_ln_g", "a_ln_b", "a_w_s", "a_b_s", "b_b_grp", "b_scale"]
    small_g = [g_pre, g_post, g_lng, g_lnb, g_ws, g_bs, g_bgrp, g_scale]
    small_ws = [pre_norm, post_norm, a_ln_g, a_ln_b, a_w_s, a_b_s, b_b_grp, b_scale]
    small_ms = [m_pre_norm, m_post_norm, m_a_ln_g, m_a_ln_b, m_a_w_s, m_a_b_s, m_b_b_grp, m_b_scale]
    small_vs = [v_pre_norm, v_post_norm, v_a_ln_g, v_a_ln_b, v_a_w_s, v_a_b_s, v_b_b_grp, v_b_scale]
    packed = _adamw("adamw_small", _pack(small_ws), _pack(small_ms), _pack(small_vs), _pack(small_g))
    small_out = [_unpack(p, [w.shape for w in small_ws]) for p in packed]

    loss = lax.psum(loss[0, 0], ("x", "y", "c"))
    order = ["pre_norm", "post_norm", "a_w_in", "a_ln_g", "a_ln_b", "a_w_s", "a_b_s", "a_w_out", "b_w_in",
             "b_w_grp", "b_b_grp", "b_scale", "b_w_out"]
    big_shapes = dict(zip(names, [a_w_in.shape, a_w_out.shape, b_w_in.shape, b_w_grp.shape, b_w_out.shape]))
    outs = [loss, grad_x.reshape(nb, seq, d)]
    for kind in range(4):
        for n in order:
            if n in big_shapes:
                outs.append(big_out[names.index(n)][kind].reshape(big_shapes[n]))
            else:
                outs.append(small_out[kind][small_names.index(n)])
    return tuple(outs)
```

```python
import functools

import jax
import jax.numpy as jnp
from jax import lax
from jax.experimental import pallas as pl
from jax.experimental.pallas import tpu as pltpu

F32 = jnp.float32
BF16 = jnp.bfloat16
NORM_EPS = 1e-6
POOL_WINDOWS = (2, 4, 8, 16)
ADAM_LR = 0.001
ADAM_B1 = 0.9
ADAM_B2 = 0.999
ADAM_EPS = 1e-08
ADAM_WD = 0.01
ADAM_STEP = 10
N_CHIPS = 4
N_DEV = 8
V7X_VMEM_LIMIT_BYTES = 56 * 1024 * 1024
LANES = 128
MESH = pl.DeviceIdType.MESH
ANY = pl.BlockSpec(memory_space=pl.ANY)
SQRT_HALF = 0.7071067811865476
INV_SQRT_2PI = 0.3989422804014327


def _tile(dim, pref, mult=LANES):
    if dim <= pref:
        return dim
    t = (pref // mult) * mult
    while t >= mult:
        if dim % t == 0:
            return t
        t -= mult
    return dim


def _params(*sem):
    return pltpu.CompilerParams(dimension_semantics=sem or None, vmem_limit_bytes=V7X_VMEM_LIMIT_BYTES)


def _gelu(x):
    cdf = 0.5 * (1.0 + lax.erf(x * SQRT_HALF))
    pdf = jnp.exp(-0.5 * x * x) * INV_SQRT_2PI
    return x * cdf, cdf + x * pdf


def _silu(z):
    s = 1.0 / (1.0 + jnp.exp(-z))
    return z * s, s * (1.0 + z * (1.0 - s))


class _Phase:
    def __init__(self, ins, outs, aliases, n_sem, start, finish, relay=None):
        self.ins, self.outs, self.aliases, self.n_sem = list(ins), list(outs), dict(aliases), n_sem
        self.start, self.finish, self.relay = start, finish, relay


RELAY_AT = 0.85


def _hosted(body, phases, grid, n_in, n_out, n_scratch=0):
    ins, outs, aliases, sems = [], [], {}, []
    for ph in phases:
        for i, o in ph.aliases.items():
            aliases[n_in + len(ins) + i] = n_out + len(outs) + o
        ins += ph.ins
        outs += ph.outs
        sems += [pltpu.SemaphoreType.DMA((ph.n_sem,)), pltpu.SemaphoreType.DMA((ph.n_sem,))]
    if not phases:
        return body, ins, outs, aliases, sems
    steps = 1
    for n in grid:
        steps *= n
    relay_at = min(steps - 1, int(RELAY_AT * steps))

    def wrapped(*refs):
        own_in, refs = refs[:n_in], refs[n_in:]
        in_refs, refs = refs[:len(ins)], refs[len(ins):]
        own_out, refs = refs[:n_out], refs[n_out:]
        out_refs, refs = refs[:len(outs)], refs[len(outs):]
        own_scratch, sem_refs = refs[:n_scratch], refs[n_scratch:]

        def run(which):
            i = o = 0
            for p, ph in enumerate(phases):
                fn = getattr(ph, which)
                if fn is not None:
                    fn(in_refs[i:i + len(ph.ins)], out_refs[o:o + len(ph.outs)], sem_refs[2 * p], sem_refs[2 * p + 1])
                i += len(ph.ins)
                o += len(ph.outs)

        if not grid:
            run("start")
            run("relay")
            run("finish")
            return
        step = 0
        for d, n in enumerate(grid):
            step = step * n + pl.program_id(d)
        pl.when(step == 0)(lambda: run("start"))
        body(*own_in, *own_out, *own_scratch)
        pl.when(step == relay_at)(lambda: run("relay"))
        pl.when(step == steps - 1)(lambda: run("finish"))

    return wrapped, ins, outs, aliases, sems


def _comm_only(name, phases):
    body, ins, outs, aliases, sems = _hosted(None, phases, (), 0, 0)
    return pl.pallas_call(
        body, name=name, in_specs=[ANY] * len(ins), out_specs=[ANY] * len(outs), out_shape=outs,
        input_output_aliases=aliases, scratch_shapes=sems)(*ins)


def _matmul(name, a, b, *, dims, grid, a_spec, b_spec, o_spec, out_shape, out_dtype, phases=(), into=None,
            prefetch=None, other_b=None):
    nk = grid[2]
    contract = {"nn": ((1,), (0,)), "nt": ((1,), (1,)), "tn": ((0,), (0,))}[dims]
    acc_in_out = out_dtype == F32 or nk == 1
    blk = tuple(d for d in o_spec.block_shape if d is not None)
    first = 0 if prefetch is None else 1
    n_b = 1 if other_b is None else 2
    n_in = first + 1 + n_b + (0 if into is None else 1)

    def body(*refs):
        a_ref, b_ref, o_ref = refs[first], refs[first + 1], refs[n_in]
        b_val = b_ref[...]
        if other_b is not None:
            b_val = jnp.where(other_b[2](refs[0]), refs[first + 2][...], b_val)
        part = lax.dot_general(a_ref[...], b_val, (contract, ((), ())), preferred_element_type=F32)
        if nk == 1:
            o_ref[...] = part.astype(out_dtype)
            return
        acc = o_ref if acc_in_out else refs[n_in + 1]
        k = pl.program_id(2)

        @pl.when(k == 0)
        def _():
            acc[...] = part

        @pl.when(k > 0)
        def _():
            acc[...] += part

        if not acc_in_out:

            @pl.when(k == nk - 1)
            def _():
                o_ref[...] = acc[...].astype(out_dtype)

    own_scratch = [] if acc_in_out else [pltpu.VMEM(blk, F32)]
    body, x_ins, x_outs, aliases, sems = _hosted(body, phases, grid, n_in, 1, len(own_scratch))
    if into is not None:
        aliases[n_in - 1] = 0
    sem = ("arbitrary",) * 3 if phases else ("parallel", "parallel", "arbitrary")
    in_specs = [a_spec, b_spec] + ([] if other_b is None else [other_b[1]])
    in_specs += ([] if into is None else [ANY]) + [ANY] * len(x_ins)
    out_specs = [o_spec] + [ANY] * len(x_outs)
    if prefetch is None:
        layout = dict(grid=grid, in_specs=in_specs, out_specs=out_specs, scratch_shapes=own_scratch + sems)
    else:
        layout = dict(grid_spec=pltpu.PrefetchScalarGridSpec(
            num_scalar_prefetch=1, grid=grid, in_specs=in_specs, out_specs=out_specs,
            scratch_shapes=own_scratch + sems))
    res = pl.pallas_call(
        body,
        name=name,
        out_shape=[jax.ShapeDtypeStruct(out_shape, out_dtype)] + x_outs,
        input_output_aliases=aliases,
        compiler_params=_params(*sem),
        **layout,
    )(*([] if prefetch is None else [prefetch]), a, b, *([] if other_b is None else [other_b[0]]),
      *([] if into is None else [into]), *x_ins)
    return (res[0], list(res[1:])) if phases else res[0]


MM_K_WHOLE = 4096


def _tiles(m, n, k, k_total=None):
    if (k if k_total is None else k_total) <= MM_K_WHOLE:
        return _tile(m, 1024), _tile(n, 1024), k
    return _tile(m, 1024), _tile(n, 2048), _tile(k, 2048)


def _mm_nn(name, a, b, out_dtype=F32, phases=()):
    m, k = a.shape
    n = b.shape[1]
    tm, tn, tk = _tiles(m, n, k)
    return _matmul(
        name, a, b, dims="nn", grid=(m // tm, n // tn, k // tk),
        a_spec=pl.BlockSpec((tm, tk), lambda i, j, l: (i, l)),
        b_spec=pl.BlockSpec((tk, tn), lambda i, j, l: (l, j)),
        o_spec=pl.BlockSpec((tm, tn), lambda i, j, l: (i, j)),
        out_shape=(m, n), out_dtype=out_dtype, phases=phases)


def _mm_nt(name, a, b, out_dtype=F32, phases=()):
    m, k = a.shape
    n = b.shape[0]
    tm, tn, tk = _tiles(m, n, k)
    return _matmul(
        name, a, b, dims="nt", grid=(m // tm, n // tn, k // tk),
        a_spec=pl.BlockSpec((tm, tk), lambda i, j, l: (i, l)),
        b_spec=pl.BlockSpec((tn, tk), lambda i, j, l: (j, l)),
        o_spec=pl.BlockSpec((tm, tn), lambda i, j, l: (i, j)),
        out_shape=(m, n), out_dtype=out_dtype, phases=phases)


def _mm_tn(name, a, b, out_dtype=F32, phases=()):
    k, m = a.shape
    n = b.shape[1]
    tm, tn, tk = _tiles(m, n, k)
    return _matmul(
        name, a, b, dims="tn", grid=(m // tm, n // tn, k // tk),
        a_spec=pl.BlockSpec((tk, tm), lambda i, j, l: (l, i)),
        b_spec=pl.BlockSpec((tk, tn), lambda i, j, l: (l, j)),
        o_spec=pl.BlockSpec((tm, tn), lambda i, j, l: (i, j)),
        out_shape=(m, n), out_dtype=out_dtype, phases=phases)


def _mm_nn_cols(name, a, wg, phases=(), order=None, span=None, into=None):
    m, k = a.shape
    ws = wg.shape[-1]
    tm, tn, tk = _tiles(m, ws, k)
    npb = ws // tn
    if order is None:
        return _matmul(
            name, a, wg, dims="nn", grid=(m // tm, N_CHIPS * npb, k // tk),
            a_spec=pl.BlockSpec((tm, tk), lambda i, j, l: (i, l)),
            b_spec=pl.BlockSpec((None, tk, tn), lambda i, j, l: (j // npb, l, j % npb)),
            o_spec=pl.BlockSpec((tm, tn), lambda i, j, l: (i, j)),
            out_shape=(m, N_CHIPS * ws), out_dtype=F32, phases=phases)
    first, count = span
    if wg.ndim == 2:
        b_spec = pl.BlockSpec((tk, tn), lambda i, j, l, o: (l, j % npb))
    else:
        b_spec = pl.BlockSpec((None, tk, tn), lambda i, j, l, o: (o[first + j // npb], l, j % npb))
    return _matmul(
        name, a, wg, dims="nn", grid=(m // tm, count * npb, k // tk),
        a_spec=pl.BlockSpec((tm, tk), lambda i, j, l, o: (i, l)), b_spec=b_spec,
        o_spec=pl.BlockSpec((tm, tn), lambda i, j, l, o: (i, o[first + j // npb] * npb + j % npb)),
        out_shape=(m, N_CHIPS * ws), out_dtype=F32, phases=phases, into=into, prefetch=order)


def _mm_nt_cols(name, a, wg, phases=(), rows=None, into=None, elsewhere=None):
    m = a.shape[0]
    n, ws = wg.shape[1], wg.shape[2]
    q, nq = rows or (0, 1)
    tm, tn, tk = _tiles(m // nq, n, ws, N_CHIPS * ws)
    if elsewhere is not None:
        tn = _tile(n, 1024)
    kpb, off = ws // tk, q * (m // nq // tm)
    if elsewhere is None:
        return _matmul(
            name, a, wg, dims="nt", grid=(m // nq // tm, n // tn, N_CHIPS * kpb),
            a_spec=pl.BlockSpec((tm, tk), lambda i, j, l: (i + off, l)),
            b_spec=pl.BlockSpec((None, tn, tk), lambda i, j, l: (l // kpb, j, l % kpb)),
            o_spec=pl.BlockSpec((tm, tn), lambda i, j, l: (i + off, j)),
            out_shape=(m, n), out_dtype=F32, phases=phases, into=into)
    order, shard = elsewhere
    there = lambda l, o: l // kpb == o[3]
    return _matmul(
        name, a, wg, dims="nt", grid=(m // nq // tm, n // tn, N_CHIPS * kpb),
        a_spec=pl.BlockSpec((tm, tk), lambda i, j, l, o: (i + off, l)),
        b_spec=pl.BlockSpec((None, tn, tk), lambda i, j, l, o: (jnp.where(there(l, o), o[0], l // kpb), j, l % kpb)),
        o_spec=pl.BlockSpec((tm, tn), lambda i, j, l, o: (i + off, j)),
        out_shape=(m, n), out_dtype=F32, phases=phases, into=into, prefetch=order,
        other_b=(shard, pl.BlockSpec((tn, tk), lambda i, j, l, o: (j, jnp.where(there(l, o), l % kpb, 0))),
                 lambda o: there(pl.program_id(2), o)))


def _mm_tn_cols(name, a, b, ws, phases=(), odd=None):
    k, m = a.shape
    rows = m if odd is None else m // 2
    tm, tn, tk = _tiles(m // 4, ws, k)
    npb, bpq = ws // tn, (m // 4) // tm
    pick = (lambda i: i) if odd is None else (lambda i: (2 * (i // bpq) + odd) * bpq + i % bpq)
    return _matmul(
        name, a, b, dims="tn", grid=(rows // tm, N_CHIPS * npb, k // tk),
        a_spec=pl.BlockSpec((tk, tm), lambda i, j, l: (l, pick(i))),
        b_spec=pl.BlockSpec((tk, tn), lambda i, j, l: (l, j)),
        o_spec=pl.BlockSpec((None, tm, tn), lambda i, j, l: (j // npb, i, j % npb)),
        out_shape=(N_CHIPS, rows, ws), out_dtype=F32, phases=phases)


def _grp_fwd(name, pooled, wgg):
    t, e = pooled.shape
    _, ng, rs, gw = wgg.shape
    tm, tn, tk = _tiles(t, gw, rs, gw)
    npb, kps = gw // tn, rs // tk
    return _matmul(
        name, pooled, wgg, dims="nn", grid=(t // tm, ng * npb, N_CHIPS * kps),
        a_spec=pl.BlockSpec((tm, tk), lambda i, j, l: (i, (j // npb) * (gw // tk) + l)),
        b_spec=pl.BlockSpec((None, None, tk, tn), lambda i, j, l: (l // kps, j // npb, l % kps, j % npb)),
        o_spec=pl.BlockSpec((tm, tn), lambda i, j, l: (i, j)),
        out_shape=(t, e), out_dtype=F32)


def _grp_bwd_x(name, dmm, wgg, phases=()):
    t, e = dmm.shape
    _, ng, rs, gw = wgg.shape
    tm, tn, tk = _tiles(t, rs, gw)
    npr, kpg = rs // tn, gw // tk
    return _matmul(
        name, dmm, wgg, dims="nt", grid=(t // tm, ng * N_CHIPS * npr, kpg),
        a_spec=pl.BlockSpec((tm, tk), lambda i, j, l: (i, (j // (N_CHIPS * npr)) * kpg + l)),
        b_spec=pl.BlockSpec(
            (None, None, tn, tk),
            lambda i, j, l: ((j % (N_CHIPS * npr)) // npr, j // (N_CHIPS * npr), j % npr, l)),
        o_spec=pl.BlockSpec((tm, tn), lambda i, j, l: (i, j)),
        out_shape=(t, e), out_dtype=F32, phases=phases)


def _grp_bwd_w(name, pooled, dmm, ng):
    t, e = pooled.shape
    gw = e // ng
    rs = gw // N_CHIPS
    _, tn, tk = _tiles(rs, gw, t)
    npb = gw // tn
    return _matmul(
        name, pooled, dmm, dims="tn", grid=(ng * N_CHIPS, npb, t // tk),
        a_spec=pl.BlockSpec((tk, rs), lambda i, j, l: (l, i)),
        b_spec=pl.BlockSpec((tk, tn), lambda i, j, l: (l, (i // N_CHIPS) * npb + j)),
        o_spec=pl.BlockSpec((None, None, rs, tn), lambda i, j, l: (i % N_CHIPS, i // N_CHIPS, 0, j)),
        out_shape=(N_CHIPS, ng, rs, gw), out_dtype=F32)


def _cast_into_slot(name, w, place, also_alone=False):
    r, c = w.shape
    tr, tc = _tile(r, 512), _tile(c, 2048)

    def body(place_ref, w_ref, o_ref, *alone):
        v = w_ref[...].astype(BF16)
        o_ref[...] = v
        for a_ref in alone:
            a_ref[...] = v

    slot = pl.BlockSpec((None, tr, tc), lambda i, j, p: (p[0], i, j))
    plain = pl.BlockSpec((tr, tc), lambda i, j, p: (i, j))
    return pl.pallas_call(
        body, name=name,
        grid_spec=pltpu.PrefetchScalarGridSpec(
            num_scalar_prefetch=1, grid=(r // tr, c // tc), in_specs=[plain],
            out_specs=[slot, plain] if also_alone else slot),
        out_shape=([jax.ShapeDtypeStruct((N_CHIPS, r, c), BF16), jax.ShapeDtypeStruct((r, c), BF16)]
                   if also_alone else jax.ShapeDtypeStruct((N_CHIPS, r, c), BF16)),
        compiler_params=_params("parallel", "parallel"))(place, w)


def _rms(v):
    return lax.rsqrt(jnp.mean(v * v, axis=-1, keepdims=True) + NORM_EPS)


def _rms_fwd(name, x, g):
    t, d = x.shape
    tr = _tile(t, 256)

    def body(x_ref, g_ref, h_ref):
        xv = x_ref[...]
        h_ref[...] = (xv * _rms(xv) * g_ref[...]).astype(BF16)

    row = pl.BlockSpec((tr, d), lambda i: (i, 0))
    vec = pl.BlockSpec((1, d), lambda i: (0, 0))
    return pl.pallas_call(
        body, name=name, grid=(t // tr,), in_specs=[row, vec], out_specs=row,
        out_shape=jax.ShapeDtypeStruct((t, d), BF16), compiler_params=_params("parallel"))(x, g)


def _post_pre_fwd(name, x, m, g_post, g_pre):
    t, d = x.shape
    tr = _tile(t, 256)

    def body(x_ref, m_ref, gp_ref, gn_ref, x1_ref, h_ref):
        mv = m_ref[...]
        x1 = x_ref[...] + mv * _rms(mv) * gp_ref[...]
        x1_ref[...] = x1
        h_ref[...] = (x1 * _rms(x1) * gn_ref[...]).astype(BF16)

    row = pl.BlockSpec((tr, d), lambda i: (i, 0))
    vec = pl.BlockSpec((1, d), lambda i: (0, 0))
    return pl.pallas_call(
        body, name=name, grid=(t // tr,), in_specs=[row, row, vec, vec], out_specs=[row, row],
        out_shape=[jax.ShapeDtypeStruct((t, d), F32), jax.ShapeDtypeStruct((t, d), BF16)],
        compiler_params=_params("parallel"))(x, m, g_post, g_pre)


def _norm_bwd(dout, nrm, r, g):
    gd = dout * g
    return r * (gd - nrm * jnp.mean(gd * nrm, axis=-1, keepdims=True))


def _loss_post_bwd(name, x1, m, g_post, target):
    t, d = x1.shape
    tr = _tile(t, 128)

    def body(x_ref, m_ref, g_ref, t_ref, loss_ref, dx_ref, dm_ref, dg_ref):
        i = pl.program_id(0)
        mv = m_ref[...]
        r = _rms(mv)
        nrm = mv * r
        err = x_ref[...] + nrm * g_ref[...] - t_ref[...]
        part = 0.5 * jnp.sum(jnp.mean(err * err, axis=-1, keepdims=True), axis=0, keepdims=True)
        dx = err / d
        dx_ref[...] = dx
        dm_ref[...] = _norm_bwd(dx, nrm, r, g_ref[...]).astype(BF16)
        dg = jnp.sum(dx * nrm, axis=0, keepdims=True)

        @pl.when(i == 0)
        def _():
            loss_ref[...] = part
            dg_ref[...] = dg

        @pl.when(i > 0)
        def _():
            loss_ref[...] += part
            dg_ref[...] += dg

    row = pl.BlockSpec((tr, d), lambda i: (i, 0))
    vec = pl.BlockSpec((1, d), lambda i: (0, 0))
    one = pl.BlockSpec((1, 1), lambda i: (0, 0))
    return pl.pallas_call(
        body, name=name, grid=(t // tr,), in_specs=[row, row, vec, row], out_specs=[one, row, row, vec],
        out_shape=[jax.ShapeDtypeStruct((1, 1), F32), jax.ShapeDtypeStruct((t, d), F32),
                   jax.ShapeDtypeStruct((t, d), BF16), jax.ShapeDtypeStruct((1, d), F32)],
        compiler_params=_params("arbitrary"))(x1, m, g_post, target)


def _mid_bwd(name, dx2, dh1, x1, g_pre, m0, g_post):
    t, d = x1.shape
    tr = _tile(t, 128)

    def body(dx2_ref, dh_ref, x_ref, gn_ref, m_ref, gp_ref, dx_ref, dm_ref, dgn_ref, dgp_ref):
        i = pl.program_id(0)
        xv = x_ref[...]
        r1 = _rms(xv)
        n1 = xv * r1
        dh = dh_ref[...]
        dx = dx2_ref[...] + _norm_bwd(dh, n1, r1, gn_ref[...])
        dx_ref[...] = dx
        mv = m_ref[...]
        r0 = _rms(mv)
        n0 = mv * r0
        dm_ref[...] = _norm_bwd(dx, n0, r0, gp_ref[...]).astype(BF16)
        dgn = jnp.sum(dh * n1, axis=0, keepdims=True)
        dgp = jnp.sum(dx * n0, axis=0, keepdims=True)

        @pl.when(i == 0)
        def _():
            dgn_ref[...] = dgn
            dgp_ref[...] = dgp

        @pl.when(i > 0)
        def _():
            dgn_ref[...] += dgn
            dgp_ref[...] += dgp

    row = pl.BlockSpec((tr, d), lambda i: (i, 0))
    vec = pl.BlockSpec((1, d), lambda i: (0, 0))
    return pl.pallas_call(
        body, name=name, grid=(t // tr,), in_specs=[row, row, row, vec, row, vec],
        out_specs=[row, row, vec, vec],
        out_shape=[jax.ShapeDtypeStruct((t, d), F32), jax.ShapeDtypeStruct((t, d), BF16),
                   jax.ShapeDtypeStruct((1, d), F32), jax.ShapeDtypeStruct((1, d), F32)],
        compiler_params=_params("arbitrary"))(dx2, dh1, x1, g_pre, m0, g_post)


def _pre_bwd(name, dx1, dh0, x, g_pre):
    t, d = x.shape
    tr = _tile(t, 128)

    def body(dx1_ref, dh_ref, x_ref, g_ref, dx_ref, dg_ref):
        i = pl.program_id(0)
        xv = x_ref[...]
        r = _rms(xv)
        nrm = xv * r
        dh = dh_ref[...]
        dx_ref[...] = dx1_ref[...] + _norm_bwd(dh, nrm, r, g_ref[...])
        dg = jnp.sum(dh * nrm, axis=0, keepdims=True)

        @pl.when(i == 0)
        def _():
            dg_ref[...] = dg

        @pl.when(i > 0)
        def _():
            dg_ref[...] += dg

    row = pl.BlockSpec((tr, d), lambda i: (i, 0))
    vec = pl.BlockSpec((1, d), lambda i: (0, 0))
    return pl.pallas_call(
        body, name=name, grid=(t // tr,), in_specs=[row, row, row, vec], out_specs=[row, vec],
        out_shape=[jax.ShapeDtypeStruct((t, d), F32), jax.ShapeDtypeStruct((1, d), F32)],
        compiler_params=_params("arbitrary"))(dx1, dh0, x, g_pre)


def _a_stats(name, pa, e):
    t = pa.shape[0]
    tr = _tile(t, 64, 8)

    def body(v_ref, mu_ref, rs_ref):
        vg, _ = _gelu(v_ref[...])
        mu = jnp.mean(vg, axis=-1, keepdims=True)
        xc = vg - mu
        mu_ref[...] = mu
        rs_ref[...] = lax.rsqrt(jnp.mean(xc * xc, axis=-1, keepdims=True) + NORM_EPS)

    col = pl.BlockSpec((tr, 1), lambda i: (i, 0))
    return pl.pallas_call(
        body, name=name, grid=(t // tr,), in_specs=[pl.BlockSpec((tr, e), lambda i: (i, 1))],
        out_specs=[col, col],
        out_shape=[jax.ShapeDtypeStruct((t, 1), F32), jax.ShapeDtypeStruct((t, 1), F32)],
        compiler_params=_params("parallel"))(pa)


def _causal(w):
    c = w.shape[0]
    keep = lax.broadcasted_iota(jnp.int32, (c, c), 0) >= lax.broadcasted_iota(jnp.int32, (c, c), 1)
    return jnp.where(keep, w, 0.0), keep


def _a_gate_fwd(name, pa, mu, rs, ln_g, ln_b, w_s, b_s3, phases=()):
    t = pa.shape[0]
    nh, c, _ = w_s.shape
    e = ln_g.shape[1]
    dh = e // nh
    rb = 2 * c if t % (2 * c) == 0 else c

    def body(u_ref, v_ref, z_ref, mu_ref, rs_ref, g_ref, b_ref, w_ref, bs_ref, y_ref):
        wc = _causal(w_ref[...])[0].astype(BF16)
        vg, _ = _gelu(v_ref[...])
        vn = ((vg - mu_ref[...]) * rs_ref[...] * g_ref[...] + b_ref[...]).astype(BF16)
        for ci in range(rb // c):
            rows = pl.ds(ci * c, c)
            sv = jnp.dot(wc, vn[ci * c:(ci + 1) * c], preferred_element_type=F32) + bs_ref[...]
            u, _ = _gelu(u_ref[rows, :])
            sz, _ = _silu(z_ref[rows, :])
            y_ref[rows, :] = (u * sv * sz).astype(BF16)

    blk = lambda off: pl.BlockSpec((rb, dh), lambda i, h: (i, off + h))
    col = pl.BlockSpec((rb, 1), lambda i, h: (i, 0))
    vec = pl.BlockSpec((1, dh), lambda i, h: (0, h))
    grid = (t // rb, nh)
    body, x_ins, x_outs, aliases, sems = _hosted(body, phases, grid, 9, 1)
    res = pl.pallas_call(
        body, name=name, grid=grid,
        in_specs=[blk(0), blk(nh), blk(2 * nh), col, col, vec, vec,
                  pl.BlockSpec((None, c, c), lambda i, h: (h, 0, 0)),
                  pl.BlockSpec((None, c, 1), lambda i, h: (h, 0, 0))] + [ANY] * len(x_ins),
        out_specs=[pl.BlockSpec((rb, dh), lambda i, h: (i, h))] + [ANY] * len(x_outs),
        out_shape=[jax.ShapeDtypeStruct((t, e), BF16)] + x_outs, input_output_aliases=aliases,
        scratch_shapes=sems,
        compiler_params=_params("arbitrary", "arbitrary"))(pa, pa, pa, mu, rs, ln_g, ln_b, w_s, b_s3, *x_ins)
    return (res[0], list(res[1:])) if phases else res[0]


def _a_gate_bwd1(name, pa, dy, mu, rs, ln_g, ln_b, w_s, b_s3, phases=()):
    t = pa.shape[0]
    nh, c, _ = w_s.shape
    e = ln_g.shape[1]
    dh = e // nh
    rb = 2 * c if t % (2 * c) == 0 else c

    def body(u_ref, v_ref, z_ref, dy_ref, mu_ref, rs_ref, g_ref, b_ref, w_ref, bs_ref,
             du_ref, dz_ref, dsv_ref, c1_ref, c2_ref):
        h = pl.program_id(1)
        wc = _causal(w_ref[...])[0].astype(BF16)
        vg, _ = _gelu(v_ref[...])
        xh = (vg - mu_ref[...]) * rs_ref[...]
        vn = (xh * g_ref[...] + b_ref[...]).astype(BF16)
        s1 = []
        s2 = []
        for ci in range(rb // c):
            rows = pl.ds(ci * c, c)
            lo, hi = ci * c, (ci + 1) * c
            sv = jnp.dot(wc, vn[lo:hi], preferred_element_type=F32) + bs_ref[...]
            u, du = _gelu(u_ref[rows, :])
            zv = z_ref[rows, :]
            sz, dsz = _silu(zv)
            dyv = dy_ref[rows, :]
            du_ref[rows, :] = (dyv * sv * sz * du).astype(BF16)
            dz_ref[rows, :] = (dyv * u * sv * dsz).astype(BF16)
            dsv = (dyv * u * sz).astype(BF16)
            dsv_ref[rows, :] = dsv
            dvn = lax.dot_general(wc, dsv, (((0,), (0,)), ((), ())), preferred_element_type=F32)
            dxh = dvn * g_ref[...]
            s1.append(jnp.sum(dxh, axis=-1, keepdims=True))
            s2.append(jnp.sum(dxh * xh[lo:hi], axis=-1, keepdims=True))
        p1 = jnp.concatenate(s1, axis=0)
        p2 = jnp.concatenate(s2, axis=0)

        @pl.when(h == 0)
        def _():
            c1_ref[...] = p1
            c2_ref[...] = p2

        @pl.when(h > 0)
        def _():
            c1_ref[...] += p1
            c2_ref[...] += p2

    blk = lambda off: pl.BlockSpec((rb, dh), lambda i, h: (i, off + h))
    col = pl.BlockSpec((rb, 1), lambda i, h: (i, 0))
    vec = pl.BlockSpec((1, dh), lambda i, h: (0, h))
    act = jax.ShapeDtypeStruct((t, e), BF16)
    stat = jax.ShapeDtypeStruct((t, 1), F32)
    grid = (t // rb, nh)
    body, x_ins, x_outs, aliases, sems = _hosted(body, phases, grid, 10, 5)
    res = pl.pallas_call(
        body, name=name, grid=grid,
        in_specs=[blk(0), blk(nh), blk(2 * nh), blk(0), col, col, vec, vec,
                  pl.BlockSpec((None, c, c), lambda i, h: (h, 0, 0)),
                  pl.BlockSpec((None, c, 1), lambda i, h: (h, 0, 0))] + [ANY] * len(x_ins),
        out_specs=[blk(0), blk(0), blk(0), col, col] + [ANY] * len(x_outs),
        out_shape=[act, act, act, stat, stat] + x_outs, input_output_aliases=aliases, scratch_shapes=sems,
        compiler_params=_params("arbitrary", "arbitrary"))(pa, pa, pa, dy, mu, rs, ln_g, ln_b, w_s, b_s3, *x_ins)
    return (res[:5], list(res[5:])) if phases else res


def _a_gate_bwd2(name, pa, dsv, mu, rs, c1, c2, ln_g, ln_b, w_s, phases=()):
    t = pa.shape[0]
    nh, c, _ = w_s.shape
    e = ln_g.shape[1]
    dh = e // nh
    rb = 2 * c if t % (2 * c) == 0 else c

    def body(v_ref, dsv_ref, mu_ref, rs_ref, c1_ref, c2_ref, g_ref, b_ref, w_ref,
             dv_ref, dw_ref, dbs_ref, dg_ref, db_ref):
        i = pl.program_id(1)
        wcf, keep = _causal(w_ref[...])
        wc = wcf.astype(BF16)
        vg, dvg = _gelu(v_ref[...])
        rsv = rs_ref[...]
        xh = (vg - mu_ref[...]) * rsv
        vn = (xh * g_ref[...] + b_ref[...]).astype(BF16)
        dw = jnp.zeros((c, c), F32)
        dbs = jnp.zeros((c, 1), F32)
        dvns = []
        for ci in range(rb // c):
            lo, hi = ci * c, (ci + 1) * c
            dsv = dsv_ref[pl.ds(lo, c), :]
            dvns.append(lax.dot_general(wc, dsv, (((0,), (0,)), ((), ())), preferred_element_type=F32))
            dw += lax.dot_general(dsv, vn[lo:hi], (((1,), (1,)), ((), ())), preferred_element_type=F32)
            dbs += jnp.sum(dsv.astype(F32), axis=-1, keepdims=True)
        dvn = jnp.concatenate(dvns, axis=0)
        dxh = dvn * g_ref[...]
        dvv = rsv * (dxh - c1_ref[...] * (1.0 / e) - xh * (c2_ref[...] * (1.0 / e)))
        dv_ref[...] = (dvv * dvg).astype(BF16)
        dw = jnp.where(keep, dw, 0.0)
        dg = jnp.sum(dvn * xh, axis=0, keepdims=True)
        db = jnp.sum(dvn, axis=0, keepdims=True)

        @pl.when(i == 0)
        def _():
            dw_ref[...] = dw
            dbs_ref[...] = dbs
            dg_ref[...] = dg
            db_ref[...] = db

        @pl.when(i > 0)
        def _():
            dw_ref[...] += dw
            dbs_ref[...] += dbs
            dg_ref[...] += dg
            db_ref[...] += db

    col = pl.BlockSpec((rb, 1), lambda h, i: (i, 0))
    vec = pl.BlockSpec((1, dh), lambda h, i: (0, h))
    hblk = pl.BlockSpec((rb, dh), lambda h, i: (i, h))
    grid = (nh, t // rb)
    body, x_ins, x_outs, aliases, sems = _hosted(body, phases, grid, 9, 5)
    res = pl.pallas_call(
        body, name=name, grid=grid,
        in_specs=[pl.BlockSpec((rb, dh), lambda h, i: (i, nh + h)), hblk, col, col, col, col, vec, vec,
                  pl.BlockSpec((None, c, c), lambda h, i: (h, 0, 0))] + [ANY] * len(x_ins),
        out_specs=[hblk, pl.BlockSpec((None, c, c), lambda h, i: (h, 0, 0)),
                   pl.BlockSpec((None, c, 1), lambda h, i: (h, 0, 0)), vec, vec] + [ANY] * len(x_outs),
        out_shape=[jax.ShapeDtypeStruct((t, e), BF16), jax.ShapeDtypeStruct((nh, c, c), F32),
                   jax.ShapeDtypeStruct((nh, c, 1), F32), jax.ShapeDtypeStruct((1, e), F32),
                   jax.ShapeDtypeStruct((1, e), F32)] + x_outs,
        input_output_aliases=aliases, scratch_shapes=sems,
        compiler_params=_params("arbitrary", "arbitrary"))(pa, dsv, mu, rs, c1, c2, ln_g, ln_b, w_s, *x_ins)
    return (res[:5], list(res[5:])) if phases else res


def _pool(name, src, e, seq, backward):
    t = src.shape[0]
    ng = len(POOL_WINDOWS)
    gw = e // ng
    cw = _tile(gw, 256)

    def shifted(a, j, pos):
        if backward:
            return jnp.where(pos < seq - j, pltpu.roll(a, seq - j, 0), 0.0)
        return jnp.where(pos >= j, pltpu.roll(a, j, 0), 0.0)

    def body(p_ref, o_ref):
        grp = (pl.program_id(1) * cw) // gw
        pos = lax.broadcasted_iota(jnp.int32, (seq, cw), 0)
        posf = (pos + 1).astype(F32)
        for k, w in enumerate(POOL_WINDOWS):

            @pl.when(grp == k)
            def _(w=w):
                pv = p_ref[...]
                cnt = jnp.minimum(posf, float(w))
                acc = pv / cnt if backward else pv
                j = 1
                while j < w:
                    acc = acc + shifted(acc, j, pos)
                    j *= 2
                out = acc - pv if backward else acc / cnt - pv
                o_ref[...] = out.astype(BF16)

    spec = pl.BlockSpec((seq, cw), lambda s, j: (s, j))
    return pl.pallas_call(
        body, name=name, grid=(t // seq, e // cw), in_specs=[spec], out_specs=spec,
        out_shape=jax.ShapeDtypeStruct((t, e), BF16),
        compiler_params=_params("parallel", "parallel"))(src)


def _b_gate_fwd(name, mm, pb, b_grp, scale):
    t, e = mm.shape
    tr, tc = _tile(t, 512), _tile(e, 1024)
    nc = e // tc

    def body(mm_ref, z_ref, b_ref, s_ref, y_ref):
        sz, _ = _silu(z_ref[...])
        y_ref[...] = ((mm_ref[...] + b_ref[...]) * s_ref[...] * sz).astype(BF16)

    blk = pl.BlockSpec((tr, tc), lambda i, j: (i, j))
    vec = pl.BlockSpec((1, tc), lambda i, j: (0, j))
    return pl.pallas_call(
        body, name=name, grid=(t // tr, nc),
        in_specs=[blk, pl.BlockSpec((tr, tc), lambda i, j: (i, nc + j)), vec, vec], out_specs=blk,
        out_shape=jax.ShapeDtypeStruct((t, e), BF16),
        compiler_params=_params("parallel", "parallel"))(mm, pb, b_grp, scale)


def _b_gate_bwd(name, dy, mm, pb, b_grp, scale):
    t, e = mm.shape
    tr, tc = _tile(t, 512), _tile(e, 1024)
    nc = e // tc

    def body(dy_ref, mm_ref, z_ref, b_ref, s_ref, dmm_ref, dz_ref, db_ref, ds_ref):
        i = pl.program_id(1)
        sz, dsz = _silu(z_ref[...])
        dyv = dy_ref[...]
        mb = mm_ref[...] + b_ref[...]
        dmixed = dyv * sz
        dmm = dmixed * s_ref[...]
        dmm_ref[...] = dmm.astype(BF16)
        dz_ref[...] = (dyv * (mb * s_ref[...]) * dsz).astype(BF16)
        db = jnp.sum(dmm, axis=0, keepdims=True)
        ds = jnp.sum(dmixed * mb, axis=0, keepdims=True)

        @pl.when(i == 0)
        def _():
            db_ref[...] = db
            ds_ref[...] = ds

        @pl.when(i > 0)
        def _():
            db_ref[...] += db
            ds_ref[...] += ds

    blk = pl.BlockSpec((tr, tc), lambda j, i: (i, j))
    vec = pl.BlockSpec((1, tc), lambda j, i: (0, j))
    act = jax.ShapeDtypeStruct((t, e), BF16)
    stat = jax.ShapeDtypeStruct((1, e), F32)
    return pl.pallas_call(
        body, name=name, grid=(nc, t // tr),
        in_specs=[blk, blk, pl.BlockSpec((tr, tc), lambda j, i: (i, nc + j)), vec, vec],
        out_specs=[blk, blk, vec, vec], out_shape=[act, act, stat, stat],
        compiler_params=_params("parallel", "arbitrary"))(dy, mm, pb, b_grp, scale)


def _position():
    return lax.axis_index("x"), lax.axis_index("y"), lax.axis_index("c")


def _other_chips(x, y):
    return [(1 - x, y), (x, 1 - y), (1 - x, 1 - y)]


def _half(ref, c):
    n = ref.shape[0] // 2
    return ref.at[pl.ds(c * n, n)]


def _part(ref, c, part):
    q, nq, count = part if len(part) == 3 else (*part, 1)
    n = ref.shape[0] // (2 * nq)
    return ref.at[pl.ds(c * nq * n + q * n, count * n)]


def _gather_phase(slots, small=None, part=(0, 1)):
    nw = len(slots)
    ns = 0 if small is None else 1
    n_ici = 3 * (nw + ns)
    n_sem = n_ici + 3 * nw + ns

    def copies(ins, outs, send_sems, recv_sems):
        x, y, c = _position()
        sibling = (x, y, 1 - c)

        def ici(j, k, slot, to):
            return pltpu.make_async_remote_copy(
                src_ref=_part(ins[k].at[slot], c, part), dst_ref=_part(outs[k].at[slot], c, part),
                send_sem=send_sems.at[j * nw + k], recv_sem=recv_sems.at[j * nw + k],
                device_id=to, device_id_type=MESH)

        def ici_small(j, slot, to):
            return pltpu.make_async_remote_copy(
                src_ref=ins[nw], dst_ref=outs[nw].at[slot], send_sem=send_sems.at[3 * nw + j],
                recv_sem=recv_sems.at[3 * nw + j], device_id=to, device_id_type=MESH)

        def d2d(j, k, slot, half):
            return pltpu.make_async_remote_copy(
                src_ref=_part(outs[k].at[slot], half, part), dst_ref=_part(outs[k].at[slot], half, part),
                send_sem=send_sems.at[n_ici + j * nw + k], recv_sem=recv_sems.at[n_ici + j * nw + k],
                device_id=sibling, device_id_type=MESH)

        def local():
            return pltpu.make_async_copy(ins[nw], outs[nw].at[2 * x + y], send_sems.at[n_sem - 1])

        return x, y, c, ici, ici_small, d2d, local

    def start(ins, outs, send_sems, recv_sems):
        x, y, c, ici, ici_small, _, local = copies(ins, outs, send_sems, recv_sems)
        if ns:
            local().start()
        for j, (cx, cy) in enumerate(_other_chips(x, y)):
            for k in range(nw):
                ici(j, k, 2 * x + y, (cx, cy, c)).start()
            if ns:
                ici_small(j, 2 * x + y, (cx, cy, c)).start()

    def relay(ins, outs, send_sems, recv_sems):
        x, y, c, ici, _, d2d, _ = copies(ins, outs, send_sems, recv_sems)
        for j, (cx, cy) in enumerate(_other_chips(x, y)):
            for k in range(nw):
                ici(j, k, 2 * cx + cy, (x, y, c)).wait_recv()
                d2d(j, k, 2 * cx + cy, c).start()

    def finish(ins, outs, send_sems, recv_sems):
        x, y, c, ici, ici_small, d2d, local = copies(ins, outs, send_sems, recv_sems)
        chips = _other_chips(x, y)
        for j, (cx, cy) in enumerate(chips):
            if ns:
                ici_small(j, 2 * cx + cy, (x, y, c)).wait_recv()
            for k in range(nw):
                d2d(j, k, 2 * cx + cy, 1 - c).wait_recv()
        for j, (cx, cy) in enumerate(chips):
            for k in range(nw):
                ici(j, k, 2 * x + y, (cx, cy, c)).wait_send()
                d2d(j, k, 2 * cx + cy, c).wait_send()
            if ns:
                ici_small(j, 2 * x + y, (cx, cy, c)).wait_send()
        if ns:
            local().wait()

    outs = [jax.ShapeDtypeStruct(s.shape, s.dtype) for s in slots]
    if ns:
        outs.append(jax.ShapeDtypeStruct((N_CHIPS,) + small.shape, small.dtype))
    return _Phase(list(slots) + ([small] if ns else []), outs, {k: k for k in range(nw)}, n_sem, start, finish, relay)


def _staged_gather_phases(slots, small=None):
    nw = len(slots)
    ns = 0 if small is None else 1
    fwd_base = 2 * nw

    def copies(ins, outs, send_sems, recv_sems, alone=False):
        x, y, c = _position()
        sibling = (x, y, 1 - c)
        home = (lambda k, slot: outs[k]) if alone else (lambda k, slot: outs[k].at[slot])

        def direct(j, k, slot, to):
            return pltpu.make_async_remote_copy(
                src_ref=_half(ins[k].at[slot], c), dst_ref=_half(outs[k].at[slot], c),
                send_sem=send_sems.at[j * nw + k], recv_sem=recv_sems.at[j * nw + k],
                device_id=to, device_id_type=MESH)

        def relayed(p, k, slot, to):
            return pltpu.make_async_remote_copy(
                src_ref=_part(ins[k].at[slot], c, (p, 2)), dst_ref=_part(home(k, slot), c, (p, 2)),
                send_sem=send_sems.at[p * nw + k], recv_sem=recv_sems.at[p * nw + k],
                device_id=to, device_id_type=MESH)

        def d2d(j, k, slot, half):
            return pltpu.make_async_remote_copy(
                src_ref=_half(home(k, slot), half), dst_ref=_half(home(k, slot), half),
                send_sem=send_sems.at[fwd_base + j * nw + k], recv_sem=recv_sems.at[fwd_base + j * nw + k],
                device_id=sibling, device_id_type=MESH)

        def small_copy(j, slot, to):
            return pltpu.make_async_remote_copy(
                src_ref=ins[nw], dst_ref=outs[nw].at[slot], send_sem=send_sems.at[4 * nw + j],
                recv_sem=recv_sems.at[4 * nw + j], device_id=to, device_id_type=MESH)

        def local():
            return pltpu.make_async_copy(ins[nw], outs[nw].at[2 * x + y], send_sems.at[4 * nw + 3])

        chips = _other_chips(x, y)
        return x, y, c, chips, [2 * cx + cy for cx, cy in chips], direct, relayed, d2d, small_copy, local

    def start_1(*refs):
        x, y, c, chips, _, direct, _, _, small_copy, local = copies(*refs)
        for j in range(2):
            for k in range(nw):
                direct(j, k, 2 * x + y, (*chips[j], c)).start()
        if ns:
            local().start()
            for j in range(3):
                small_copy(j, 2 * x + y, (*chips[j], c)).start()

    def relay_1(*refs):
        x, y, c, _, slot, direct, _, d2d, _, _ = copies(*refs)
        for j in range(2):
            for k in range(nw):
                direct(j, k, slot[j], (x, y, c)).wait_recv()
                d2d(j, k, slot[j], c).start()

    def finish_1(*refs):
        x, y, c, chips, slot, direct, _, d2d, small_copy, local = copies(*refs)
        if ns:
            for j in range(3):
                small_copy(j, slot[j], (x, y, c)).wait_recv()
        for j in range(2):
            for k in range(nw):
                d2d(j, k, slot[j], 1 - c).wait_recv()
        for j in range(2):
            for k in range(nw):
                direct(j, k, 2 * x + y, (*chips[j], c)).wait_send()
                d2d(j, k, slot[j], c).wait_send()
        if ns:
            for j in range(3):
                small_copy(j, 2 * x + y, (*chips[j], c)).wait_send()
            local().wait()

    shapes = [jax.ShapeDtypeStruct(s.shape, s.dtype) for s in slots]
    keep = {k: k for k in range(nw)}

    def second(landed, alone=False):
        def start_2(*refs):
            x, y, c, chips, slot, _, relayed, _, _, _ = copies(*refs, alone)
            for j in range(2):
                for k in range(nw):
                    relayed(1 - j, k, slot[j], (*chips[1 - j], c)).start()

        def relay_2(*refs):
            x, y, c, _, slot, _, relayed, d2d, _, _ = copies(*refs, alone)
            for k in range(nw):
                for p in range(2):
                    relayed(p, k, slot[2], (x, y, c)).wait_recv()
                d2d(0, k, slot[2], c).start()

        def finish_2(*refs):
            x, y, c, chips, slot, _, relayed, d2d, _, _ = copies(*refs, alone)
            for k in range(nw):
                d2d(0, k, slot[2], 1 - c).wait_recv()
            for k in range(nw):
                d2d(0, k, slot[2], c).wait_send()
                for j in range(2):
                    relayed(1 - j, k, slot[j], (*chips[1 - j], c)).wait_send()

        outs = [jax.ShapeDtypeStruct(s.shape[1:], s.dtype) for s in slots] if alone else shapes
        return _Phase(list(landed), outs, {} if alone else keep, 3 * nw, start_2, finish_2, relay_2)

    first_outs = shapes + ([jax.ShapeDtypeStruct((N_CHIPS,) + small.shape, small.dtype)] if ns else [])
    first = _Phase(list(slots) + ([small] if ns else []), first_outs, keep, 4 * nw + 4, start_1, finish_1, relay_1)
    return first, second


def _swap_phase(grads):
    nw = len(grads)

    def swaps(ins, outs, send_sems, recv_sems):
        x, y, c = _position()
        cps = []
        for k in range(nw):
            n = ins[k].shape[1] // 2
            cps.append(pltpu.make_async_remote_copy(
                src_ref=ins[k].at[pl.ds(0, N_CHIPS), pl.ds((1 - c) * n, n)], dst_ref=outs[k],
                send_sem=send_sems.at[k], recv_sem=recv_sems.at[k],
                device_id=(x, y, 1 - c), device_id_type=MESH))
        return cps

    def start(*refs):
        for cp in swaps(*refs):
            cp.start()

    def finish(*refs):
        for cp in swaps(*refs):
            cp.wait()

    halves = [jax.ShapeDtypeStruct((g.shape[0], g.shape[1] // 2, g.shape[2]), F32) for g in grads]
    return _Phase(grads, halves, {}, nw, start, finish)


def _add_halves(name, grad, theirs, place):
    s, half, w = theirs.shape
    tr, tc = _tile(half, 256), _tile(w, 2048)
    nrt = half // tr

    def body(place_ref, a_ref, b_ref, h_ref, f_ref):
        v = a_ref[...] + b_ref[...]
        h_ref[...] = v.astype(BF16)

        @pl.when(pl.program_id(2) == place_ref[0])
        def _():
            f_ref[...] = v

    return pl.pallas_call(
        body, name=name,
        grid_spec=pltpu.PrefetchScalarGridSpec(
            num_scalar_prefetch=1, grid=(nrt, w // tc, s),
            in_specs=[pl.BlockSpec((None, tr, tc), lambda i, j, q, p: (q, p[1] * nrt + i, j)),
                      pl.BlockSpec((None, tr, tc), lambda i, j, q, p: (q, i, j))],
            out_specs=[pl.BlockSpec((None, tr, tc), lambda i, j, q, p: (q, i, j)),
                       pl.BlockSpec((tr, tc), lambda i, j, q, p: (p[1] * nrt + i, j))]),
        out_shape=[jax.ShapeDtypeStruct(theirs.shape, BF16), jax.ShapeDtypeStruct((2 * half, w), F32)],
        compiler_params=_params("parallel", "parallel", "arbitrary"))(place, grad, theirs)


def _scatter_phase(own, pb, part=(0, 1), got=None):
    nw = len(own)

    def copies(ins, outs, send_sems, recv_sems):
        own_in, pbs = ins[:nw], ins[nw:2 * nw]
        own_out, gots = outs[:nw], outs[nw:]
        x, y, c = _position()
        sibling = (x, y, 1 - c)

        def d2d_own(k, half):
            return pltpu.make_async_remote_copy(
                src_ref=_part(own_in[k], half, part), dst_ref=_part(own_out[k], half, part),
                send_sem=send_sems.at[k], recv_sem=recv_sems.at[k], device_id=sibling, device_id_type=MESH)

        def ici(j, k, shard, to):
            n = pbs[k].shape[1] // part[1]
            rows = pl.ds(part[0] * n, (part[2] if len(part) == 3 else 1) * n)
            return pltpu.make_async_remote_copy(
                src_ref=pbs[k].at[shard, rows], dst_ref=_part(gots[k].at[j], c, part),
                send_sem=send_sems.at[nw + j * nw + k], recv_sem=recv_sems.at[nw + j * nw + k],
                device_id=to, device_id_type=MESH)

        def d2d(j, k, half):
            return pltpu.make_async_remote_copy(
                src_ref=_part(gots[k].at[j], half, part), dst_ref=_part(gots[k].at[j], half, part),
                send_sem=send_sems.at[4 * nw + j * nw + k], recv_sem=recv_sems.at[4 * nw + j * nw + k],
                device_id=sibling, device_id_type=MESH)

        return x, y, c, d2d_own, ici, d2d

    def start(*refs):
        x, y, c, d2d_own, ici, _ = copies(*refs)
        for k in range(nw):
            d2d_own(k, c).start()
        for j, (cx, cy) in enumerate(_other_chips(x, y)):
            for k in range(nw):
                ici(j, k, 2 * cx + cy, (cx, cy, c)).start()

    def relay(*refs):
        x, y, c, _, ici, d2d = copies(*refs)
        for j in range(3):
            for k in range(nw):
                ici(j, k, 2 * x + y, (x, y, c)).wait_recv()
                d2d(j, k, c).start()

    def finish(*refs):
        x, y, c, d2d_own, ici, d2d = copies(*refs)
        for k in range(nw):
            d2d_own(k, 1 - c).wait_recv()
        for j in range(3):
            for k in range(nw):
                d2d(j, k, 1 - c).wait_recv()
        for k in range(nw):
            d2d_own(k, c).wait_send()
        for j, (cx, cy) in enumerate(_other_chips(x, y)):
            for k in range(nw):
                ici(j, k, 2 * cx + cy, (cx, cy, c)).wait_send()
                d2d(j, k, c).wait_send()

    own_shape = [jax.ShapeDtypeStruct(o.shape, F32) for o in own]
    got_shape = [jax.ShapeDtypeStruct((3,) + o.shape, BF16) for o in own]
    aliases = {k: k for k in range(nw)}
    if got is not None:
        aliases.update({2 * nw + k: nw + k for k in range(nw)})
    return _Phase(list(own) + list(pb) + list(got or []), own_shape + got_shape, aliases, 7 * nw, start, finish, relay)


def _small_gather_phase(part):
    def copies(ins, outs, send_sems, recv_sems):
        x, y, c = _position()

        def peer(mask):
            return x ^ ((mask >> 2) & 1), y ^ ((mask >> 1) & 1), c ^ (mask & 1)

        def copy(mask, origin, to):
            ox, oy, oc = origin
            return pltpu.make_async_remote_copy(
                src_ref=ins[0], dst_ref=outs[0].at[4 * ox + 2 * oy + oc], send_sem=send_sems.at[mask - 1],
                recv_sem=recv_sems.at[mask - 1], device_id=to, device_id_type=MESH)

        local = pltpu.make_async_copy(ins[0], outs[0].at[4 * x + 2 * y + c], send_sems.at[N_DEV - 1])
        return (x, y, c), peer, copy, local

    def start(*refs):
        me, peer, copy, local = copies(*refs)
        local.start()
        for mask in range(1, N_DEV):
            copy(mask, me, peer(mask)).start()

    def finish(*refs):
        me, peer, copy, local = copies(*refs)
        for mask in range(1, N_DEV):
            copy(mask, peer(mask), me).wait_recv()
        for mask in range(1, N_DEV):
            copy(mask, me, peer(mask)).wait_send()
        local.wait()

    return _Phase([part], [jax.ShapeDtypeStruct((N_DEV,) + part.shape, F32)], {}, N_DEV, start, finish)


def _sum_slots(name, parts):
    s, r, c = parts.shape
    tr = _tile(r, 512, 8)

    def body(p_ref, o_ref):
        acc = p_ref[0]
        for q in range(1, s):
            acc = acc + p_ref[q]
        o_ref[...] = acc

    return pl.pallas_call(
        body, name=name, grid=(r // tr,), in_specs=[pl.BlockSpec((s, tr, c), lambda i: (0, i, 0))],
        out_specs=pl.BlockSpec((tr, c), lambda i: (i, 0)), out_shape=jax.ShapeDtypeStruct((r, c), F32),
        compiler_params=_params("parallel"))(parts)


def _adamw_math(w, g, m, v):
    m = ADAM_B1 * m + (1.0 - ADAM_B1) * g
    v = ADAM_B2 * v + (1.0 - ADAM_B2) * (g * g)
    m_hat = m / (1.0 - ADAM_B1 ** ADAM_STEP)
    v_hat = v / (1.0 - ADAM_B2 ** ADAM_STEP)
    delta = -ADAM_LR * (m_hat / (jnp.sqrt(v_hat) + ADAM_EPS) + ADAM_WD * w)
    return delta, m, v


def _adamw(name, w, m, v, own, got=None, odd=None, into=None, phases=()):
    r, c = w.shape
    rows = r if odd is None else r // 2
    tr, tc = _tile(r // 4, 256, 8) if odd is not None else _tile(r, 256, 8), _tile(c, 1024)
    bpq = (r // 4) // tr if odd is not None else 1
    pick = (lambda i: i) if odd is None else (lambda i: (2 * (i // bpq) + odd) * bpq + i % bpq)
    n_in = 4 + (0 if got is None else 1)

    def body(*refs):
        w_ref, m_ref, v_ref, own_ref = refs[:4]
        outs = refs[n_in + (0 if into is None else 4):]
        g = own_ref[...]
        if got is not None:
            for j in range(3):
                g = g + refs[4][j].astype(F32)
        delta, mn, vn = _adamw_math(w_ref[...], g, m_ref[...], v_ref[...])
        outs[0][...] = g
        outs[1][...] = delta
        outs[2][...] = mn
        outs[3][...] = vn

    full = pl.BlockSpec((tr, tc), lambda i, j: (pick(i), j))
    ins, args = [full] * 3 + [pl.BlockSpec((tr, tc), lambda i, j: (i, j))], [w, m, v, own]
    if got is not None:
        ins.append(pl.BlockSpec((3, tr, tc), lambda i, j: (0, i, j)))
        args.append(got)
    own_aliases = {}
    if into is not None:
        ins += [ANY] * 4
        args += list(into)
        own_aliases = {n_in + q: q for q in range(4)}
    grid = (rows // tr, c // tc)
    body, x_ins, x_outs, aliases, sems = _hosted(body, phases, grid, len(args), 4)
    aliases.update(own_aliases)
    res = pl.pallas_call(
        body, name=name, grid=grid, in_specs=ins + [ANY] * len(x_ins), out_specs=[full] * 4 + [ANY] * len(x_outs),
        out_shape=[jax.ShapeDtypeStruct((r, c), F32)] * 4 + x_outs, input_output_aliases=aliases,
        scratch_shapes=sems,
        compiler_params=_params(*(("arbitrary",) * 2 if phases else ("parallel",) * 2)))(*args, *x_ins)
    return (res[:4], list(res[4:])) if phases else res


def _pack(parts):
    return jnp.concatenate([p.reshape(-1) for p in parts]).reshape(-1, LANES)


def _unpack(packed, shapes):
    flat = packed.reshape(-1)
    out, off = [], 0
    for s in shapes:
        n = 1
        for d in s:
            n *= d
        out.append(flat[off:off + n].reshape(s))
        off += n
    return out


def kernel(x, pre_norm, post_norm, a_w_in, a_ln_g, a_ln_b, a_w_s, a_b_s, a_w_out, b_w_in, b_w_grp, b_b_grp, b_scale, b_w_out, loss_target, m_pre_norm, m_post_norm, m_a_w_in, m_a_ln_g, m_a_ln_b, m_a_w_s, m_a_b_s, m_a_w_out, m_b_w_in, m_b_w_grp, m_b_b_grp, m_b_scale, m_b_w_out, v_pre_norm, v_post_norm, v_a_w_in, v_a_ln_g, v_a_ln_b, v_a_w_s, v_a_b_s, v_a_w_out, v_b_w_in, v_b_w_grp, v_b_b_grp, v_b_scale, v_b_w_out):
    nb, seq, d = x.shape
    t = nb * seq
    e = a_ln_g.shape[1]
    nh, chunk = a_w_s.shape[1], a_w_s.shape[2]
    ng, rs, gw = b_w_grp.shape[1], b_w_grp.shape[2], b_w_grp.shape[3]
    wa, wb = a_w_in.shape[2], b_w_in.shape[2]
    cx, cy = lax.axis_index("x"), lax.axis_index("y")
    chip = 2 * cx + cy

    big_w = [a_w_in.reshape(d, wa), a_w_out.reshape(e // N_CHIPS, d), b_w_in.reshape(d, wb),
             b_w_grp.reshape(ng * rs, gw), b_w_out.reshape(e // N_CHIPS, d)]
    big_m = [m_a_w_in.reshape(d, wa), m_a_w_out.reshape(e // N_CHIPS, d), m_b_w_in.reshape(d, wb),
             m_b_w_grp.reshape(ng * rs, gw), m_b_w_out.reshape(e // N_CHIPS, d)]
    big_v = [v_a_w_in.reshape(d, wa), v_a_w_out.reshape(e // N_CHIPS, d), v_b_w_in.reshape(d, wb),
             v_b_w_grp.reshape(ng * rs, gw), v_b_w_out.reshape(e // N_CHIPS, d)]
    names = ["a_w_in", "a_w_out", "b_w_in", "b_w_grp", "b_w_out"]
    place = jnp.stack([chip, lax.axis_index("c")]).astype(jnp.int32)
    slots = [_cast_into_slot("cast_" + n, w, place, also_alone=(n == "a_w_in")) for n, w in zip(names, big_w)]
    slots[0], wa_own = slots[0]
    small_w = jnp.concatenate([b_b_grp.reshape(ng, rs), b_scale.reshape(ng, rs)], axis=0)
    xf = x.reshape(t, d)
    tgt = loss_target.reshape(t, d)
    g_pre0, g_pre1 = pre_norm[0:1], pre_norm[1:2]
    g_post0, g_post1 = post_norm[0:1], post_norm[1:2]
    w_s = a_w_s.reshape(nh, chunk, chunk)
    b_s3 = a_b_s.reshape(nh, chunk, 1)

    h0 = _rms_fwd("pre_norm0", xf, g_pre0)
    order = jnp.stack([chip, chip ^ 2, chip ^ 1, chip ^ 3]).astype(jnp.int32)
    near, far = _staged_gather_phases(slots[:1], small_w)
    near_o, far_o = _staged_gather_phases(slots[1:2])
    near_b, far_b = _staged_gather_phases(slots[2:3])
    pa, (wa_g, gsmall) = _mm_nn_cols("a_in_proj_own", h0, wa_own, [near], order=order, span=(0, 1))
    pa, (wa_far, wao_g) = _mm_nn_cols(
        "a_in_proj_near", h0, wa_g, [far([wa_g], alone=True), near_o], order=order, span=(1, 2), into=pa)
    pa, (wao_g,) = _mm_nn_cols("a_in_proj_far", h0, wa_far, [far_o([wao_g])], order=order, span=(3, 1), into=pa)
    wao_g = wao_g.reshape(e, d)
    b_grp_full = jnp.transpose(gsmall[:, :ng, :], (1, 0, 2)).reshape(1, e)
    scale_full = gsmall[:, ng:, :].reshape(1, e)
    mu, rstd = _a_stats("a_ln_stats", pa, e)
    y0, (wb_g,) = _a_gate_fwd("a_gate", pa, mu, rstd, a_ln_g, a_ln_b, w_s, b_s3, [near_b])
    m0, (wb_g,) = _mm_nn("a_out_proj", y0, wao_g, phases=[far_b([wb_g])])
    x1, h1 = _post_pre_fwd("post0_pre1", xf, m0, g_post0, g_pre1)
    pb, (wg_g, wbo_g) = _mm_nn_cols("b_in_proj", h1, wb_g, [_gather_phase(slots[3:5])])
    wg_g = wg_g.reshape(N_CHIPS, ng, rs, gw)
    wbo_g = wbo_g.reshape(e, d)
    pooled = _pool("b_pool", pb, e, seq, backward=False)
    mm = _grp_fwd("b_grp_proj", pooled, wg_g)
    y1 = _b_gate_fwd("b_gate", mm, pb, b_grp_full, scale_full)
    m1 = _mm_nn("b_out_proj", y1, wbo_g)
    loss, dx2, dm1, dg_post1 = _loss_post_bwd("loss_post1_bwd", x1, m1, g_post1, tgt)

    def chip_sum(n, g, theirs):
        pbk, ownk = _add_halves("chip_sum_" + n, g, theirs, place)
        return [ownk], [pbk]

    reduced = {}
    dy1 = _mm_nt("b_out_dx", dm1, wbo_g)
    g_wbo = _mm_tn("b_out_dw", y1, dm1).reshape(N_CHIPS, e // N_CHIPS, d)
    dmm, dzb, db_grp, dscale = _b_gate_bwd("b_gate_bwd", dy1, mm, pb, b_grp_full, scale_full)
    dpooled, (th,) = _grp_bwd_x("b_grp_dx", dmm, wg_g, [_swap_phase([g_wbo])])
    s_bo = chip_sum("b_w_out", g_wbo, th)
    g_wg = _grp_bwd_w("b_grp_dw", pooled, dmm, ng).reshape(N_CHIPS, ng * rs, gw)
    dp = _pool("b_pool_bwd", dpooled, e, seq, backward=True)
    dpb = jnp.concatenate([dp, dzb], axis=1)
    dh1, (own, got, th) = _mm_nt_cols("b_in_dx", dpb, wb_g, [_scatter_phase(*s_bo), _swap_phase([g_wg])])
    reduced["b_w_out"] = (own, got)
    s_g = chip_sum("b_w_grp", g_wg, th)
    g_wb, reduced["b_w_grp"] = _mm_tn_cols("b_in_dw", h1, dpb, wb, [_scatter_phase(*s_g)])
    dx1, dm0, dg_pre1, dg_post0 = _mid_bwd("pre1_post0_bwd", dx2, dh1, x1, g_pre1, m0, g_post0)
    dy0, (th,) = _mm_nt("a_out_dx", dm0, wao_g, phases=[_swap_phase([g_wb])])
    s_b = chip_sum("b_w_in", g_wb, th)
    g_wao, (own, got) = _mm_tn("a_out_dw", y0, dm0, phases=[_scatter_phase(*s_b, part=(0, 4))])
    g_wao = g_wao.reshape(N_CHIPS, e // N_CHIPS, d)
    (du, dz, dsv, c1, c2), (own, got, th) = _a_gate_bwd1(
        "a_gate_bwd1", pa, dy0, mu, rstd, a_ln_g, a_ln_b, w_s, b_s3,
        [_scatter_phase([own], s_b[1], part=(1, 4, 2), got=[got]), _swap_phase([g_wao])])
    s_ao = chip_sum("a_w_out", g_wao, th)
    (dv, dw_s, db_s, dln_g, dln_b), reduced["b_w_in"] = _a_gate_bwd2(
        "a_gate_bwd2", pa, dsv, mu, rstd, c1, c2, a_ln_g, a_ln_b, w_s,
        [_scatter_phase([own], s_b[1], part=(3, 4), got=[got])])
    dpa = jnp.concatenate([du, dv, dz], axis=1)
    g_lo, reduced["a_w_out"] = _mm_tn_cols("a_in_dw_even", h0, dpa, wa, [_scatter_phase(*s_ao)], odd=0)
    g_hi, (th,) = _mm_tn_cols("a_in_dw_odd", h0, dpa, wa, [_swap_phase([g_lo])], odd=1)
    s_lo = chip_sum("a_w_in_even", g_lo, th)
    dh0, (th, own_lo, got_lo) = _mm_nt_cols(
        "a_in_dx_top", dpa, wa_g, [_swap_phase([g_hi]), _scatter_phase(*s_lo, part=(0, 4, 3))], rows=(0, 2),
        elsewhere=(order, wa_far))
    s_hi = chip_sum("a_w_in_odd", g_hi, th)
    dh0, (own_lo, got_lo, own_hi, got_hi) = _mm_nt_cols(
        "a_in_dx_bottom", dpa, wa_g,
        [_scatter_phase([own_lo], s_lo[1], part=(3, 4), got=[got_lo]), _scatter_phase(*s_hi, part=(0, 4, 2))],
        rows=(1, 2), into=dh0, elsewhere=(order, wa_far))
    grad_x, dg_pre0 = _pre_bwd("pre0_bwd", dx1, dh0, xf, g_pre0)
    small_shapes = [(2, d), (2, d), (1, e), (1, e), (1, nh, chunk, chunk), (1, nh, chunk), (1, e), (1, e)]
    part = _pack([jnp.concatenate([dg_pre0, dg_pre1], axis=0), jnp.concatenate([dg_post0, dg_post1], axis=0),
                  dln_g, dln_b, dw_s, db_s, db_grp, dscale])
    big_out = {}
    k = names.index("b_w_in")
    big_out["b_w_in"], (own_hi, got_hi) = _adamw(
        "adamw_b_w_in", big_w[k], big_m[k], big_v[k], *reduced["b_w_in"],
        phases=[_scatter_phase([own_hi], s_hi[1], part=(2, 4, 2), got=[got_hi])])
    k = names.index("a_w_out")
    big_out["a_w_out"], (small_parts,) = _adamw(
        "adamw_a_w_out", big_w[k], big_m[k], big_v[k], *reduced["a_w_out"], phases=[_small_gather_phase(part)])
    for n in ("b_w_out", "b_w_grp"):
        k = names.index(n)
        big_out[n] = _adamw("adamw_" + n, big_w[k], big_m[k], big_v[k], *reduced[n])
    k = names.index("a_w_in")
    even = _adamw("adamw_a_w_in_even", big_w[k], big_m[k], big_v[k], own_lo, got_lo, odd=0)
    big_out["a_w_in"] = _adamw("adamw_a_w_in_odd", big_w[k], big_m[k], big_v[k], own_hi, got_hi, odd=1, into=even)
    big_out = [big_out[n] for n in names]

    g_small = _sum_slots("sum_small_grads", small_parts)
    g_pre, g_post, g_lng, g_lnb, g_ws, g_bs, g_bgrp_full, g_scale_full = _unpack(g_small, small_shapes)
    g_bgrp = lax.dynamic_slice_in_dim(g_bgrp_full.reshape(ng, N_CHIPS, rs), chip, 1, axis=1).reshape(1, ng, rs)
    g_scale = lax.dynamic_slice_in_dim(g_scale_full.reshape(N_CHIPS, gw), chip, 1, axis=0)
    small_names = ["pre_norm", "post_norm", "a_ln_g", "a_ln_b", "a_w_s", "a_b_s", "b_b_grp", "b_scale"]
    small_g = [g_pre, g_post, g_lng, g_lnb, g_ws, g_bs, g_bgrp, g_scale]
    small_ws = [pre_norm, post_norm, a_ln_g, a_ln_b, a_w_s, a_b_s, b_b_grp, b_scale]
    small_ms = [m_pre_norm, m_post_norm, m_a_ln_g, m_a_ln_b, m_a_w_s, m_a_b_s, m_b_b_grp, m_b_scale]
    small_vs = [v_pre_norm, v_post_norm, v_a_ln_g, v_a_ln_b, v_a_w_s, v_a_b_s, v_b_b_grp, v_b_scale]
    packed = _adamw("adamw_small", _pack(small_ws), _pack(small_ms), _pack(small_vs), _pack(small_g))
    small_out = [_unpack(p, [w.shape for w in small_ws]) for p in packed]

    loss = lax.psum(loss[0, 0], ("x", "y", "c"))
    order = ["pre_norm", "post_norm", "a_w_in", "a_ln_g", "a_ln_b", "a_w_s", "a_b_s", "a_w_out", "b_w_in",
             "b_w_grp", "b_b_grp", "b_scale", "b_w_out"]
    big_shapes = dict(zip(names, [a_w_in.shape, a_w_out.shape, b_w_in.shape, b_w_grp.shape, b_w_out.shape]))
    outs = [loss, grad_x.reshape(nb, seq, d)]
    for kind in range(4):
        for n in order:
            if n in big_shapes:
                outs.append(big_out[names.index(n)][kind].reshape(big_shapes[n]))
            else:
                outs.append(small_out[kind][small_names.index(n)])
    return tuple(outs)
```

```python
import functools

import jax
import jax.numpy as jnp
from jax import lax
from jax.experimental import pallas as pl
from jax.experimental.pallas import tpu as pltpu

F32 = jnp.float32
BF16 = jnp.bfloat16
NORM_EPS = 1e-6
POOL_WINDOWS = (2, 4, 8, 16)
ADAM_LR = 0.001
ADAM_B1 = 0.9
ADAM_B2 = 0.999
ADAM_EPS = 1e-08
ADAM_WD = 0.01
ADAM_STEP = 10
N_CHIPS = 4
N_DEV = 8
V7X_VMEM_LIMIT_BYTES = 56 * 1024 * 1024
LANES = 128
MESH = pl.DeviceIdType.MESH
ANY = pl.BlockSpec(memory_space=pl.ANY)
SQRT_HALF = 0.7071067811865476
INV_SQRT_2PI = 0.3989422804014327


def _tile(dim, pref, mult=LANES):
    if dim <= pref:
        return dim
    t = (pref // mult) * mult
    while t >= mult:
        if dim % t == 0:
            return t
        t -= mult
    return dim


def _params(*sem):
    return pltpu.CompilerParams(dimension_semantics=sem or None, vmem_limit_bytes=V7X_VMEM_LIMIT_BYTES)


def _gelu(x):
    cdf = 0.5 * (1.0 + lax.erf(x * SQRT_HALF))
    pdf = jnp.exp(-0.5 * x * x) * INV_SQRT_2PI
    return x * cdf, cdf + x * pdf


def _silu(z):
    s = 1.0 / (1.0 + jnp.exp(-z))
    return z * s, s * (1.0 + z * (1.0 - s))


class _Phase:
    def __init__(self, ins, outs, aliases, n_sem, start, finish, relay=None):
        self.ins, self.outs, self.aliases, self.n_sem = list(ins), list(outs), dict(aliases), n_sem
        self.start, self.finish, self.relay = start, finish, relay


RELAY_AT = 0.85


def _hosted(body, phases, grid, n_in, n_out, n_scratch=0):
    ins, outs, aliases, sems = [], [], {}, []
    for ph in phases:
        for i, o in ph.aliases.items():
            aliases[n_in + len(ins) + i] = n_out + len(outs) + o
        ins += ph.ins
        outs += ph.outs
        sems += [pltpu.SemaphoreType.DMA((ph.n_sem,)), pltpu.SemaphoreType.DMA((ph.n_sem,))]
    if not phases:
        return body, ins, outs, aliases, sems
    steps = 1
    for n in grid:
        steps *= n
    relay_at = min(steps - 1, int(RELAY_AT * steps))

    def wrapped(*refs):
        own_in, refs = refs[:n_in], refs[n_in:]
        in_refs, refs = refs[:len(ins)], refs[len(ins):]
        own_out, refs = refs[:n_out], refs[n_out:]
        out_refs, refs = refs[:len(outs)], refs[len(outs):]
        own_scratch, sem_refs = refs[:n_scratch], refs[n_scratch:]

        def run(which):
            i = o = 0
            for p, ph in enumerate(phases):
                fn = getattr(ph, which)
                if fn is not None:
                    fn(in_refs[i:i + len(ph.ins)], out_refs[o:o + len(ph.outs)], sem_refs[2 * p], sem_refs[2 * p + 1])
                i += len(ph.ins)
                o += len(ph.outs)

        if not grid:
            run("start")
            run("relay")
            run("finish")
            return
        step = 0
        for d, n in enumerate(grid):
            step = step * n + pl.program_id(d)
        pl.when(step == 0)(lambda: run("start"))
        body(*own_in, *own_out, *own_scratch)
        pl.when(step == relay_at)(lambda: run("relay"))
        pl.when(step == steps - 1)(lambda: run("finish"))

    return wrapped, ins, outs, aliases, sems


def _comm_only(name, phases):
    body, ins, outs, aliases, sems = _hosted(None, phases, (), 0, 0)
    return pl.pallas_call(
        body, name=name, in_specs=[ANY] * len(ins), out_specs=[ANY] * len(outs), out_shape=outs,
        input_output_aliases=aliases, scratch_shapes=sems)(*ins)


def _matmul(name, a, b, *, dims, grid, a_spec, b_spec, o_spec, out_shape, out_dtype, phases=(), into=None,
            prefetch=None, other_b=None):
    nk = grid[2]
    contract = {"nn": ((1,), (0,)), "nt": ((1,), (1,)), "tn": ((0,), (0,))}[dims]
    acc_in_out = out_dtype == F32 or nk == 1
    blk = tuple(d for d in o_spec.block_shape if d is not None)
    first = 0 if prefetch is None else 1
    n_b = 1 if other_b is None else 2
    n_in = first + 1 + n_b + (0 if into is None else 1)

    def body(*refs):
        a_ref, b_ref, o_ref = refs[first], refs[first + 1], refs[n_in]
        b_val = b_ref[...]
        if other_b is not None:
            b_val = jnp.where(other_b[2](refs[0]), refs[first + 2][...], b_val)
        part = lax.dot_general(a_ref[...], b_val, (contract, ((), ())), preferred_element_type=F32)
        if nk == 1:
            o_ref[...] = part.astype(out_dtype)
            return
        acc = o_ref if acc_in_out else refs[n_in + 1]
        k = pl.program_id(2)

        @pl.when(k == 0)
        def _():
            acc[...] = part

        @pl.when(k > 0)
        def _():
            acc[...] += part

        if not acc_in_out:

            @pl.when(k == nk - 1)
            def _():
                o_ref[...] = acc[...].astype(out_dtype)

    own_scratch = [] if acc_in_out else [pltpu.VMEM(blk, F32)]
    body, x_ins, x_outs, aliases, sems = _hosted(body, phases, grid, n_in, 1, len(own_scratch))
    if into is not None:
        aliases[n_in - 1] = 0
    sem = ("arbitrary",) * 3 if phases else ("parallel", "parallel", "arbitrary")
    in_specs = [a_spec, b_spec] + ([] if other_b is None else [other_b[1]])
    in_specs += ([] if into is None else [ANY]) + [ANY] * len(x_ins)
    out_specs = [o_spec] + [ANY] * len(x_outs)
    if prefetch is None:
        layout = dict(grid=grid, in_specs=in_specs, out_specs=out_specs, scratch_shapes=own_scratch + sems)
    else:
        layout = dict(grid_spec=pltpu.PrefetchScalarGridSpec(
            num_scalar_prefetch=1, grid=grid, in_specs=in_specs, out_specs=out_specs,
            scratch_shapes=own_scratch + sems))
    res = pl.pallas_call(
        body,
        name=name,
        out_shape=[jax.ShapeDtypeStruct(out_shape, out_dtype)] + x_outs,
        input_output_aliases=aliases,
        compiler_params=_params(*sem),
        **layout,
    )(*([] if prefetch is None else [prefetch]), a, b, *([] if other_b is None else [other_b[0]]),
      *([] if into is None else [into]), *x_ins)
    return (res[0], list(res[1:])) if phases else res[0]


MM_K_WHOLE = 4096


def _tiles(m, n, k, k_total=None):
    if (k if k_total is None else k_total) <= MM_K_WHOLE:
        return _tile(m, 1024), _tile(n, 1024), k
    return _tile(m, 1024), _tile(n, 2048), _tile(k, 2048)


def _mm_nn(name, a, b, out_dtype=F32, phases=()):
    m, k = a.shape
    n = b.shape[1]
    tm, tn, tk = _tiles(m, n, k)
    return _matmul(
        name, a, b, dims="nn", grid=(m // tm, n // tn, k // tk),
        a_spec=pl.BlockSpec((tm, tk), lambda i, j, l: (i, l)),
        b_spec=pl.BlockSpec((tk, tn), lambda i, j, l: (l, j)),
        o_spec=pl.BlockSpec((tm, tn), lambda i, j, l: (i, j)),
        out_shape=(m, n), out_dtype=out_dtype, phases=phases)


def _mm_nt(name, a, b, out_dtype=F32, phases=()):
    m, k = a.shape
    n = b.shape[0]
    tm, tn, tk = _tiles(m, n, k)
    return _matmul(
        name, a, b, dims="nt", grid=(m // tm, n // tn, k // tk),
        a_spec=pl.BlockSpec((tm, tk), lambda i, j, l: (i, l)),
        b_spec=pl.BlockSpec((tn, tk), lambda i, j, l: (j, l)),
        o_spec=pl.BlockSpec((tm, tn), lambda i, j, l: (i, j)),
        out_shape=(m, n), out_dtype=out_dtype, phases=phases)


def _mm_tn(name, a, b, out_dtype=F32, phases=()):
    k, m = a.shape
    n = b.shape[1]
    tm, tn, tk = _tiles(m, n, k)
    return _matmul(
        name, a, b, dims="tn", grid=(m // tm, n // tn, k // tk),
        a_spec=pl.BlockSpec((tk, tm), lambda i, j, l: (l, i)),
        b_spec=pl.BlockSpec((tk, tn), lambda i, j, l: (l, j)),
        o_spec=pl.BlockSpec((tm, tn), lambda i, j, l: (i, j)),
        out_shape=(m, n), out_dtype=out_dtype, phases=phases)


def _mm_nn_cols(name, a, wg, phases=(), order=None, span=None, into=None):
    m, k = a.shape
    ws = wg.shape[-1]
    tm, tn, tk = _tiles(m, ws, k)
    npb = ws // tn
    if order is None:
        return _matmul(
            name, a, wg, dims="nn", grid=(m // tm, N_CHIPS * npb, k // tk),
            a_spec=pl.BlockSpec((tm, tk), lambda i, j, l: (i, l)),
            b_spec=pl.BlockSpec((None, tk, tn), lambda i, j, l: (j // npb, l, j % npb)),
            o_spec=pl.BlockSpec((tm, tn), lambda i, j, l: (i, j)),
            out_shape=(m, N_CHIPS * ws), out_dtype=F32, phases=phases)
    first, count = span
    if wg.ndim == 2:
        b_spec = pl.BlockSpec((tk, tn), lambda i, j, l, o: (l, j % npb))
    else:
        b_spec = pl.BlockSpec((None, tk, tn), lambda i, j, l, o: (o[first + j // npb], l, j % npb))
    return _matmul(
        name, a, wg, dims="nn", grid=(m // tm, count * npb, k // tk),
        a_spec=pl.BlockSpec((tm, tk), lambda i, j, l, o: (i, l)), b_spec=b_spec,
        o_spec=pl.BlockSpec((tm, tn), lambda i, j, l, o: (i, o[first + j // npb] * npb + j % npb)),
        out_shape=(m, N_CHIPS * ws), out_dtype=F32, phases=phases, into=into, prefetch=order)


def _mm_nt_cols(name, a, wg, phases=(), rows=None, into=None, elsewhere=None):
    m = a.shape[0]
    n, ws = wg.shape[1], wg.shape[2]
    q, nq = rows or (0, 1)
    tm, tn, tk = _tiles(m // nq, n, ws, N_CHIPS * ws)
    if elsewhere is not None:
        tn = _tile(n, 1024)
    kpb, off = ws // tk, q * (m // nq // tm)
    if elsewhere is None:
        return _matmul(
            name, a, wg, dims="nt", grid=(m // nq // tm, n // tn, N_CHIPS * kpb),
            a_spec=pl.BlockSpec((tm, tk), lambda i, j, l: (i + off, l)),
            b_spec=pl.BlockSpec((None, tn, tk), lambda i, j, l: (l // kpb, j, l % kpb)),
            o_spec=pl.BlockSpec((tm, tn), lambda i, j, l: (i + off, j)),
            out_shape=(m, n), out_dtype=F32, phases=phases, into=into)
    order, shard = elsewhere
    there = lambda l, o: l // kpb == o[3]
    return _matmul(
        name, a, wg, dims="nt", grid=(m // nq // tm, n // tn, N_CHIPS * kpb),
        a_spec=pl.BlockSpec((tm, tk), lambda i, j, l, o: (i + off, l)),
        b_spec=pl.BlockSpec((None, tn, tk), lambda i, j, l, o: (jnp.where(there(l, o), o[0], l // kpb), j, l % kpb)),
        o_spec=pl.BlockSpec((tm, tn), lambda i, j, l, o: (i + off, j)),
        out_shape=(m, n), out_dtype=F32, phases=phases, into=into, prefetch=order,
        other_b=(shard, pl.BlockSpec((tn, tk), lambda i, j, l, o: (j, jnp.where(there(l, o), l % kpb, 0))),
                 lambda o: there(pl.program_id(2), o)))


def _mm_tn_cols(name, a, b, ws, phases=(), odd=None):
    k, m = a.shape
    rows = m if odd is None else m // 2
    tm, tn, tk = _tiles(m // 4, ws, k)
    npb, bpq = ws // tn, (m // 4) // tm
    pick = (lambda i: i) if odd is None else (lambda i: (2 * (i // bpq) + odd) * bpq + i % bpq)
    return _matmul(
        name, a, b, dims="tn", grid=(rows // tm, N_CHIPS * npb, k // tk),
        a_spec=pl.BlockSpec((tk, tm), lambda i, j, l: (l, pick(i))),
        b_spec=pl.BlockSpec((tk, tn), lambda i, j, l: (l, j)),
        o_spec=pl.BlockSpec((None, tm, tn), lambda i, j, l: (j // npb, i, j % npb)),
        out_shape=(N_CHIPS, rows, ws), out_dtype=F32, phases=phases)


def _grp_fwd(name, pooled, wgg):
    t, e = pooled.shape
    _, ng, rs, gw = wgg.shape
    tm, tn, tk = _tiles(t, gw, rs, gw)
    npb, kps = gw // tn, rs // tk
    return _matmul(
        name, pooled, wgg, dims="nn", grid=(t // tm, ng * npb, N_CHIPS * kps),
        a_spec=pl.BlockSpec((tm, tk), lambda i, j, l: (i, (j // npb) * (gw // tk) + l)),
        b_spec=pl.BlockSpec((None, None, tk, tn), lambda i, j, l: (l // kps, j // npb, l % kps, j % npb)),
        o_spec=pl.BlockSpec((tm, tn), lambda i, j, l: (i, j)),
        out_shape=(t, e), out_dtype=F32)


def _grp_bwd_x(name, dmm, wgg, phases=()):
    t, e = dmm.shape
    _, ng, rs, gw = wgg.shape
    tm, tn, tk = _tiles(t, rs, gw)
    npr, kpg = rs // tn, gw // tk
    return _matmul(
        name, dmm, wgg, dims="nt", grid=(t // tm, ng * N_CHIPS * npr, kpg),
        a_spec=pl.BlockSpec((tm, tk), lambda i, j, l: (i, (j // (N_CHIPS * npr)) * kpg + l)),
        b_spec=pl.BlockSpec(
            (None, None, tn, tk),
            lambda i, j, l: ((j % (N_CHIPS * npr)) // npr, j // (N_CHIPS * npr), j % npr, l)),
        o_spec=pl.BlockSpec((tm, tn), lambda i, j, l: (i, j)),
        out_shape=(t, e), out_dtype=F32, phases=phases)


def _grp_bwd_w(name, pooled, dmm, ng):
    t, e = pooled.shape
    gw = e // ng
    rs = gw // N_CHIPS
    _, tn, tk = _tiles(rs, gw, t)
    npb = gw // tn
    return _matmul(
        name, pooled, dmm, dims="tn", grid=(ng * N_CHIPS, npb, t // tk),
        a_spec=pl.BlockSpec((tk, rs), lambda i, j, l: (l, i)),
        b_spec=pl.BlockSpec((tk, tn), lambda i, j, l: (l, (i // N_CHIPS) * npb + j)),
        o_spec=pl.BlockSpec((None, None, rs, tn), lambda i, j, l: (i % N_CHIPS, i // N_CHIPS, 0, j)),
        out_shape=(N_CHIPS, ng, rs, gw), out_dtype=F32)


def _cast_into_slot(name, w, place, also_alone=False):
    r, c = w.shape
    tr, tc = _tile(r, 512), _tile(c, 2048)

    def body(place_ref, w_ref, o_ref, *alone):
        v = w_ref[...].astype(BF16)
        o_ref[...] = v
        for a_ref in alone:
            a_ref[...] = v

    slot = pl.BlockSpec((None, tr, tc), lambda i, j, p: (p[0], i, j))
    plain = pl.BlockSpec((tr, tc), lambda i, j, p: (i, j))
    return pl.pallas_call(
        body, name=name,
        grid_spec=pltpu.PrefetchScalarGridSpec(
            num_scalar_prefetch=1, grid=(r // tr, c // tc), in_specs=[plain],
            out_specs=[slot, plain] if also_alone else slot),
        out_shape=([jax.ShapeDtypeStruct((N_CHIPS, r, c), BF16), jax.ShapeDtypeStruct((r, c), BF16)]
                   if also_alone else jax.ShapeDtypeStruct((N_CHIPS, r, c), BF16)),
        compiler_params=_params("parallel", "parallel"))(place, w)


def _rms(v):
    return lax.rsqrt(jnp.mean(v * v, axis=-1, keepdims=True) + NORM_EPS)


def _rms_fwd(name, x, g):
    t, d = x.shape
    tr = _tile(t, 256)

    def body(x_ref, g_ref, h_ref):
        xv = x_ref[...]
        h_ref[...] = (xv * _rms(xv) * g_ref[...]).astype(BF16)

    row = pl.BlockSpec((tr, d), lambda i: (i, 0))
    vec = pl.BlockSpec((1, d), lambda i: (0, 0))
    return pl.pallas_call(
        body, name=name, grid=(t // tr,), in_specs=[row, vec], out_specs=row,
        out_shape=jax.ShapeDtypeStruct((t, d), BF16), compiler_params=_params("parallel"))(x, g)


def _post_pre_fwd(name, x, m, g_post, g_pre):
    t, d = x.shape
    tr = _tile(t, 256)

    def body(x_ref, m_ref, gp_ref, gn_ref, x1_ref, h_ref):
        mv = m_ref[...]
        x1 = x_ref[...] + mv * _rms(mv) * gp_ref[...]
        x1_ref[...] = x1
        h_ref[...] = (x1 * _rms(x1) * gn_ref[...]).astype(BF16)

    row = pl.BlockSpec((tr, d), lambda i: (i, 0))
    vec = pl.BlockSpec((1, d), lambda i: (0, 0))
    return pl.pallas_call(
        body, name=name, grid=(t // tr,), in_specs=[row, row, vec, vec], out_specs=[row, row],
        out_shape=[jax.ShapeDtypeStruct((t, d), F32), jax.ShapeDtypeStruct((t, d), BF16)],
        compiler_params=_params("parallel"))(x, m, g_post, g_pre)


def _norm_bwd(dout, nrm, r, g):
    gd = dout * g
    return r * (gd - nrm * jnp.mean(gd * nrm, axis=-1, keepdims=True))


def _loss_post_bwd(name, x1, m, g_post, target):
    t, d = x1.shape
    tr = _tile(t, 128)

    def body(x_ref, m_ref, g_ref, t_ref, loss_ref, dx_ref, dm_ref, dg_ref):
        i = pl.program_id(0)
        mv = m_ref[...]
        r = _rms(mv)
        nrm = mv * r
        err = x_ref[...] + nrm * g_ref[...] - t_ref[...]
        part = 0.5 * jnp.sum(jnp.mean(err * err, axis=-1, keepdims=True), axis=0, keepdims=True)
        dx = err / d
        dx_ref[...] = dx
        dm_ref[...] = _norm_bwd(dx, nrm, r, g_ref[...]).astype(BF16)
        dg = jnp.sum(dx * nrm, axis=0, keepdims=True)

        @pl.when(i == 0)
        def _():
            loss_ref[...] = part
            dg_ref[...] = dg

        @pl.when(i > 0)
        def _():
            loss_ref[...] += part
            dg_ref[...] += dg

    row = pl.BlockSpec((tr, d), lambda i: (i, 0))
    vec = pl.BlockSpec((1, d), lambda i: (0, 0))
    one = pl.BlockSpec((1, 1), lambda i: (0, 0))
    return pl.pallas_call(
        body, name=name, grid=(t // tr,), in_specs=[row, row, vec, row], out_specs=[one, row, row, vec],
        out_shape=[jax.ShapeDtypeStruct((1, 1), F32), jax.ShapeDtypeStruct((t, d), F32),
                   jax.ShapeDtypeStruct((t, d), BF16), jax.ShapeDtypeStruct((1, d), F32)],
        compiler_params=_params("arbitrary"))(x1, m, g_post, target)


def _mid_bwd(name, dx2, dh1, x1, g_pre, m0, g_post):
    t, d = x1.shape
    tr = _tile(t, 128)

    def body(dx2_ref, dh_ref, x_ref, gn_ref, m_ref, gp_ref, dx_ref, dm_ref, dgn_ref, dgp_ref):
        i = pl.program_id(0)
        xv = x_ref[...]
        r1 = _rms(xv)
        n1 = xv * r1
        dh = dh_ref[...]
        dx = dx2_ref[...] + _norm_bwd(dh, n1, r1, gn_ref[...])
        dx_ref[...] = dx
        mv = m_ref[...]
        r0 = _rms(mv)
        n0 = mv * r0
        dm_ref[...] = _norm_bwd(dx, n0, r0, gp_ref[...]).astype(BF16)
        dgn = jnp.sum(dh * n1, axis=0, keepdims=True)
        dgp = jnp.sum(dx * n0, axis=0, keepdims=True)

        @pl.when(i == 0)
        def _():
            dgn_ref[...] = dgn
            dgp_ref[...] = dgp

        @pl.when(i > 0)
        def _():
            dgn_ref[...] += dgn
            dgp_ref[...] += dgp

    row = pl.BlockSpec((tr, d), lambda i: (i, 0))
    vec = pl.BlockSpec((1, d), lambda i: (0, 0))
    return pl.pallas_call(
        body, name=name, grid=(t // tr,), in_specs=[row, row, row, vec, row, vec],
        out_specs=[row, row, vec, vec],
        out_shape=[jax.ShapeDtypeStruct((t, d), F32), jax.ShapeDtypeStruct((t, d), BF16),
                   jax.ShapeDtypeStruct((1, d), F32), jax.ShapeDtypeStruct((1, d), F32)],
        compiler_params=_params("arbitrary"))(dx2, dh1, x1, g_pre, m0, g_post)


def _pre_bwd(name, dx1, dh0, x, g_pre):
    t, d = x.shape
    tr = _tile(t, 128)

    def body(dx1_ref, dh_ref, x_ref, g_ref, dx_ref, dg_ref):
        i = pl.program_id(0)
        xv = x_ref[...]
        r = _rms(xv)
        nrm = xv * r
        dh = dh_ref[...]
        dx_ref[...] = dx1_ref[...] + _norm_bwd(dh, nrm, r, g_ref[...])
        dg = jnp.sum(dh * nrm, axis=0, keepdims=True)

        @pl.when(i == 0)
        def _():
            dg_ref[...] = dg

        @pl.when(i > 0)
        def _():
            dg_ref[...] += dg

    row = pl.BlockSpec((tr, d), lambda i: (i, 0))
    vec = pl.BlockSpec((1, d), lambda i: (0, 0))
    return pl.pallas_call(
        body, name=name, grid=(t // tr,), in_specs=[row, row, row, vec], out_specs=[row, vec],
        out_shape=[jax.ShapeDtypeStruct((t, d), F32), jax.ShapeDtypeStruct((1, d), F32)],
        compiler_params=_params("arbitrary"))(dx1, dh0, x, g_pre)


def _a_stats(name, pa, e):
    t = pa.shape[0]
    tr = _tile(t, 64, 8)

    def body(v_ref, mu_ref, rs_ref):
        vg, _ = _gelu(v_ref[...])
        mu = jnp.mean(vg, axis=-1, keepdims=True)
        xc = vg - mu
        mu_ref[...] = mu
        rs_ref[...] = lax.rsqrt(jnp.mean(xc * xc, axis=-1, keepdims=True) + NORM_EPS)

    col = pl.BlockSpec((tr, 1), lambda i: (i, 0))
    return pl.pallas_call(
        body, name=name, grid=(t // tr,), in_specs=[pl.BlockSpec((tr, e), lambda i: (i, 1))],
        out_specs=[col, col],
        out_shape=[jax.ShapeDtypeStruct((t, 1), F32), jax.ShapeDtypeStruct((t, 1), F32)],
        compiler_params=_params("parallel"))(pa)


def _causal(w):
    c = w.shape[0]
    keep = lax.broadcasted_iota(jnp.int32, (c, c), 0) >= lax.broadcasted_iota(jnp.int32, (c, c), 1)
    return jnp.where(keep, w, 0.0), keep


def _a_gate_fwd(name, pa, mu, rs, ln_g, ln_b, w_s, b_s3, phases=()):
    t = pa.shape[0]
    nh, c, _ = w_s.shape
    e = ln_g.shape[1]
    dh = e // nh
    rb = 2 * c if t % (2 * c) == 0 else c

    def body(u_ref, v_ref, z_ref, mu_ref, rs_ref, g_ref, b_ref, w_ref, bs_ref, y_ref):
        wc = _causal(w_ref[...])[0].astype(BF16)
        vg, _ = _gelu(v_ref[...])
        vn = ((vg - mu_ref[...]) * rs_ref[...] * g_ref[...] + b_ref[...]).astype(BF16)
        for ci in range(rb // c):
            rows = pl.ds(ci * c, c)
            sv = jnp.dot(wc, vn[ci * c:(ci + 1) * c], preferred_element_type=F32) + bs_ref[...]
            u, _ = _gelu(u_ref[rows, :])
            sz, _ = _silu(z_ref[rows, :])
            y_ref[rows, :] = (u * sv * sz).astype(BF16)

    blk = lambda off: pl.BlockSpec((rb, dh), lambda i, h: (i, off + h))
    col = pl.BlockSpec((rb, 1), lambda i, h: (i, 0))
    vec = pl.BlockSpec((1, dh), lambda i, h: (0, h))
    grid = (t // rb, nh)
    body, x_ins, x_outs, aliases, sems = _hosted(body, phases, grid, 9, 1)
    res = pl.pallas_call(
        body, name=name, grid=grid,
        in_specs=[blk(0), blk(nh), blk(2 * nh), col, col, vec, vec,
                  pl.BlockSpec((None, c, c), lambda i, h: (h, 0, 0)),
                  pl.BlockSpec((None, c, 1), lambda i, h: (h, 0, 0))] + [ANY] * len(x_ins),
        out_specs=[pl.BlockSpec((rb, dh), lambda i, h: (i, h))] + [ANY] * len(x_outs),
        out_shape=[jax.ShapeDtypeStruct((t, e), BF16)] + x_outs, input_output_aliases=aliases,
        scratch_shapes=sems,
        compiler_params=_params("arbitrary", "arbitrary"))(pa, pa, pa, mu, rs, ln_g, ln_b, w_s, b_s3, *x_ins)
    return (res[0], list(res[1:])) if phases else res[0]


def _a_gate_bwd1(name, pa, dy, mu, rs, ln_g, ln_b, w_s, b_s3, phases=()):
    t = pa.shape[0]
    nh, c, _ = w_s.shape
    e = ln_g.shape[1]
    dh = e // nh
    rb = 2 * c if t % (2 * c) == 0 else c

    def body(u_ref, v_ref, z_ref, dy_ref, mu_ref, rs_ref, g_ref, b_ref, w_ref, bs_ref,
             du_ref, dz_ref, dsv_ref, c1_ref, c2_ref):
        h = pl.program_id(1)
        wc = _causal(w_ref[...])[0].astype(BF16)
        vg, _ = _gelu(v_ref[...])
        xh = (vg - mu_ref[...]) * rs_ref[...]
        vn = (xh * g_ref[...] + b_ref[...]).astype(BF16)
        s1 = []
        s2 = []
        for ci in range(rb // c):
            rows = pl.ds(ci * c, c)
            lo, hi = ci * c, (ci + 1) * c
            sv = jnp.dot(wc, vn[lo:hi], preferred_element_type=F32) + bs_ref[...]
            u, du = _gelu(u_ref[rows, :])
            zv = z_ref[rows, :]
            sz, dsz = _silu(zv)
            dyv = dy_ref[rows, :]
            du_ref[rows, :] = (dyv * sv * sz * du).astype(BF16)
            dz_ref[rows, :] = (dyv * u * sv * dsz).astype(BF16)
            dsv = (dyv * u * sz).astype(BF16)
            dsv_ref[rows, :] = dsv
            dvn = lax.dot_general(wc, dsv, (((0,), (0,)), ((), ())), preferred_element_type=F32)
            dxh = dvn * g_ref[...]
            s1.append(jnp.sum(dxh, axis=-1, keepdims=True))
            s2.append(jnp.sum(dxh * xh[lo:hi], axis=-1, keepdims=True))
        p1 = jnp.concatenate(s1, axis=0)
        p2 = jnp.concatenate(s2, axis=0)

        @pl.when(h == 0)
        def _():
            c1_ref[...] = p1
            c2_ref[...] = p2

        @pl.when(h > 0)
        def _():
            c1_ref[...] += p1
            c2_ref[...] += p2

    blk = lambda off: pl.BlockSpec((rb, dh), lambda i, h: (i, off + h))
    col = pl.BlockSpec((rb, 1), lambda i, h: (i, 0))
    vec = pl.BlockSpec((1, dh), lambda i, h: (0, h))
    act = jax.ShapeDtypeStruct((t, e), BF16)
    stat = jax.ShapeDtypeStruct((t, 1), F32)
    grid = (t // rb, nh)
    body, x_ins, x_outs, aliases, sems = _hosted(body, phases, grid, 10, 5)
    res = pl.pallas_call(
        body, name=name, grid=grid,
        in_specs=[blk(0), blk(nh), blk(2 * nh), blk(0), col, col, vec, vec,
                  pl.BlockSpec((None, c, c), lambda i, h: (h, 0, 0)),
                  pl.BlockSpec((None, c, 1), lambda i, h: (h, 0, 0))] + [ANY] * len(x_ins),
        out_specs=[blk(0), blk(0), blk(0), col, col] + [ANY] * len(x_outs),
        out_shape=[act, act, act, stat, stat] + x_outs, input_output_aliases=aliases, scratch_shapes=sems,
        compiler_params=_params("arbitrary", "arbitrary"))(pa, pa, pa, dy, mu, rs, ln_g, ln_b, w_s, b_s3, *x_ins)
    return (res[:5], list(res[5:])) if phases else res


def _a_gate_bwd2(name, pa, dsv, mu, rs, c1, c2, ln_g, ln_b, w_s, phases=()):
    t = pa.shape[0]
    nh, c, _ = w_s.shape
    e = ln_g.shape[1]
    dh = e // nh
    rb = 2 * c if t % (2 * c) == 0 else c

    def body(v_ref, dsv_ref, mu_ref, rs_ref, c1_ref, c2_ref, g_ref, b_ref, w_ref,
             dv_ref, dw_ref, dbs_ref, dg_ref, db_ref):
        i = pl.program_id(1)
        wcf, keep = _causal(w_ref[...])
        wc = wcf.astype(BF16)
        vg, dvg = _gelu(v_ref[...])
        rsv = rs_ref[...]
        xh = (vg - mu_ref[...]) * rsv
        vn = (xh * g_ref[...] + b_ref[...]).astype(BF16)
        dw = jnp.zeros((c, c), F32)
        dbs = jnp.zeros((c, 1), F32)
        dvns = []
        for ci in range(rb // c):
            lo, hi = ci * c, (ci + 1) * c
            dsv = dsv_ref[pl.ds(lo, c), :]
            dvns.append(lax.dot_general(wc, dsv, (((0,), (0,)), ((), ())), preferred_element_type=F32))
            dw += lax.dot_general(dsv, vn[lo:hi], (((1,), (1,)), ((), ())), preferred_element_type=F32)
            dbs += jnp.sum(dsv.astype(F32), axis=-1, keepdims=True)
        dvn = jnp.concatenate(dvns, axis=0)
        dxh = dvn * g_ref[...]
        dvv = rsv * (dxh - c1_ref[...] * (1.0 / e) - xh * (c2_ref[...] * (1.0 / e)))
        dv_ref[...] = (dvv * dvg).astype(BF16)
        dw = jnp.where(keep, dw, 0.0)
        dg = jnp.sum(dvn * xh, axis=0, keepdims=True)
        db = jnp.sum(dvn, axis=0, keepdims=True)

        @pl.when(i == 0)
        def _():
            dw_ref[...] = dw
            dbs_ref[...] = dbs
            dg_ref[...] = dg
            db_ref[...] = db

        @pl.when(i > 0)
        def _():
            dw_ref[...] += dw
            dbs_ref[...] += dbs
            dg_ref[...] += dg
            db_ref[...] += db

    col = pl.BlockSpec((rb, 1), lambda h, i: (i, 0))
    vec = pl.BlockSpec((1, dh), lambda h, i: (0, h))
    hblk = pl.BlockSpec((rb, dh), lambda h, i: (i, h))
    grid = (nh, t // rb)
    body, x_ins, x_outs, aliases, sems = _hosted(body, phases, grid, 9, 5)
    res = pl.pallas_call(
        body, name=name, grid=grid,
        in_specs=[pl.BlockSpec((rb, dh), lambda h, i: (i, nh + h)), hblk, col, col, col, col, vec, vec,
                  pl.BlockSpec((None, c, c), lambda h, i: (h, 0, 0))] + [ANY] * len(x_ins),
        out_specs=[hblk, pl.BlockSpec((None, c, c), lambda h, i: (h, 0, 0)),
                   pl.BlockSpec((None, c, 1), lambda h, i: (h, 0, 0)), vec, vec] + [ANY] * len(x_outs),
        out_shape=[jax.ShapeDtypeStruct((t, e), BF16), jax.ShapeDtypeStruct((nh, c, c), F32),
                   jax.ShapeDtypeStruct((nh, c, 1), F32), jax.ShapeDtypeStruct((1, e), F32),
                   jax.ShapeDtypeStruct((1, e), F32)] + x_outs,
        input_output_aliases=aliases, scratch_shapes=sems,
        compiler_params=_params("arbitrary", "arbitrary"))(pa, dsv, mu, rs, c1, c2, ln_g, ln_b, w_s, *x_ins)
    return (res[:5], list(res[5:])) if phases else res


def _pool(name, src, e, seq, backward):
    t = src.shape[0]
    ng = len(POOL_WINDOWS)
    gw = e // ng
    cw = _tile(gw, 256)

    def shifted(a, j, pos):
        if backward:
            return jnp.where(pos < seq - j, pltpu.roll(a, seq - j, 0), 0.0)
        return jnp.where(pos >= j, pltpu.roll(a, j, 0), 0.0)

    def body(p_ref, o_ref):
        grp = (pl.program_id(1) * cw) // gw
        pos = lax.broadcasted_iota(jnp.int32, (seq, cw), 0)
        posf = (pos + 1).astype(F32)
        for k, w in enumerate(POOL_WINDOWS):

            @pl.when(grp == k)
            def _(w=w):
                pv = p_ref[...]
                cnt = jnp.minimum(posf, float(w))
                acc = pv / cnt if backward else pv
                j = 1
                while j < w:
                    acc = acc + shifted(acc, j, pos)
                    j *= 2
                out = acc - pv if backward else acc / cnt - pv
                o_ref[...] = out.astype(BF16)

    spec = pl.BlockSpec((seq, cw), lambda s, j: (s, j))
    return pl.pallas_call(
        body, name=name, grid=(t // seq, e // cw), in_specs=[spec], out_specs=spec,
        out_shape=jax.ShapeDtypeStruct((t, e), BF16),
        compiler_params=_params("parallel", "parallel"))(src)


def _b_gate_fwd(name, mm, pb, b_grp, scale):
    t, e = mm.shape
    tr, tc = _tile(t, 512), _tile(e, 1024)
    nc = e // tc

    def body(mm_ref, z_ref, b_ref, s_ref, y_ref):
        sz, _ = _silu(z_ref[...])
        y_ref[...] = ((mm_ref[...] + b_ref[...]) * s_ref[...] * sz).astype(BF16)

    blk = pl.BlockSpec((tr, tc), lambda i, j: (i, j))
    vec = pl.BlockSpec((1, tc), lambda i, j: (0, j))
    return pl.pallas_call(
        body, name=name, grid=(t // tr, nc),
        in_specs=[blk, pl.BlockSpec((tr, tc), lambda i, j: (i, nc + j)), vec, vec], out_specs=blk,
        out_shape=jax.ShapeDtypeStruct((t, e), BF16),
        compiler_params=_params("parallel", "parallel"))(mm, pb, b_grp, scale)


def _b_gate_bwd(name, dy, mm, pb, b_grp, scale):
    t, e = mm.shape
    tr, tc = _tile(t, 512), _tile(e, 1024)
    nc = e // tc

    def body(dy_ref, mm_ref, z_ref, b_ref, s_ref, dmm_ref, dz_ref, db_ref, ds_ref):
        i = pl.program_id(1)
        sz, dsz = _silu(z_ref[...])
        dyv = dy_ref[...]
        mb = mm_ref[...] + b_ref[...]
        dmixed = dyv * sz
        dmm = dmixed * s_ref[...]
        dmm_ref[...] = dmm.astype(BF16)
        dz_ref[...] = (dyv * (mb * s_ref[...]) * dsz).astype(BF16)
        db = jnp.sum(dmm, axis=0, keepdims=True)
        ds = jnp.sum(dmixed * mb, axis=0, keepdims=True)

        @pl.when(i == 0)
        def _():
            db_ref[...] = db
            ds_ref[...] = ds

        @pl.when(i > 0)
        def _():
            db_ref[...] += db
            ds_ref[...] += ds

    blk = pl.BlockSpec((tr, tc), lambda j, i: (i, j))
    vec = pl.BlockSpec((1, tc), lambda j, i: (0, j))
    act = jax.ShapeDtypeStruct((t, e), BF16)
    stat = jax.ShapeDtypeStruct((1, e), F32)
    return pl.pallas_call(
        body, name=name, grid=(nc, t // tr),
        in_specs=[blk, blk, pl.BlockSpec((tr, tc), lambda j, i: (i, nc + j)), vec, vec],
        out_specs=[blk, blk, vec, vec], out_shape=[act, act, stat, stat],
        compiler_params=_params("parallel", "arbitrary"))(dy, mm, pb, b_grp, scale)


def _position():
    return lax.axis_index("x"), lax.axis_index("y"), lax.axis_index("c")


def _other_chips(x, y):
    return [(1 - x, y), (x, 1 - y), (1 - x, 1 - y)]


def _half(ref, c):
    n = ref.shape[0] // 2
    return ref.at[pl.ds(c * n, n)]


def _part(ref, c, part):
    q, nq, count = part if len(part) == 3 else (*part, 1)
    n = ref.shape[0] // (2 * nq)
    return ref.at[pl.ds(c * nq * n + q * n, count * n)]


def _gather_phase(slots, small=None, part=(0, 1)):
    nw = len(slots)
    ns = 0 if small is None else 1
    n_ici = 3 * (nw + ns)
    n_sem = n_ici + 3 * nw + ns

    def copies(ins, outs, send_sems, recv_sems):
        x, y, c = _position()
        sibling = (x, y, 1 - c)

        def ici(j, k, slot, to):
            return pltpu.make_async_remote_copy(
                src_ref=_part(ins[k].at[slot], c, part), dst_ref=_part(outs[k].at[slot], c, part),
                send_sem=send_sems.at[j * nw + k], recv_sem=recv_sems.at[j * nw + k],
                device_id=to, device_id_type=MESH)

        def ici_small(j, slot, to):
            return pltpu.make_async_remote_copy(
                src_ref=ins[nw], dst_ref=outs[nw].at[slot], send_sem=send_sems.at[3 * nw + j],
                recv_sem=recv_sems.at[3 * nw + j], device_id=to, device_id_type=MESH)

        def d2d(j, k, slot, half):
            return pltpu.make_async_remote_copy(
                src_ref=_part(outs[k].at[slot], half, part), dst_ref=_part(outs[k].at[slot], half, part),
                send_sem=send_sems.at[n_ici + j * nw + k], recv_sem=recv_sems.at[n_ici + j * nw + k],
                device_id=sibling, device_id_type=MESH)

        def local():
            return pltpu.make_async_copy(ins[nw], outs[nw].at[2 * x + y], send_sems.at[n_sem - 1])

        return x, y, c, ici, ici_small, d2d, local

    def start(ins, outs, send_sems, recv_sems):
        x, y, c, ici, ici_small, _, local = copies(ins, outs, send_sems, recv_sems)
        if ns:
            local().start()
        for j, (cx, cy) in enumerate(_other_chips(x, y)):
            for k in range(nw):
                ici(j, k, 2 * x + y, (cx, cy, c)).start()
            if ns:
                ici_small(j, 2 * x + y, (cx, cy, c)).start()

    def relay(ins, outs, send_sems, recv_sems):
        x, y, c, ici, _, d2d, _ = copies(ins, outs, send_sems, recv_sems)
        for j, (cx, cy) in enumerate(_other_chips(x, y)):
            for k in range(nw):
                ici(j, k, 2 * cx + cy, (x, y, c)).wait_recv()
                d2d(j, k, 2 * cx + cy, c).start()

    def finish(ins, outs, send_sems, recv_sems):
        x, y, c, ici, ici_small, d2d, local = copies(ins, outs, send_sems, recv_sems)
        chips = _other_chips(x, y)
        for j, (cx, cy) in enumerate(chips):
            if ns:
                ici_small(j, 2 * cx + cy, (x, y, c)).wait_recv()
            for k in range(nw):
                d2d(j, k, 2 * cx + cy, 1 - c).wait_recv()
        for j, (cx, cy) in enumerate(chips):
            for k in range(nw):
                ici(j, k, 2 * x + y, (cx, cy, c)).wait_send()
                d2d(j, k, 2 * cx + cy, c).wait_send()
            if ns:
                ici_small(j, 2 * x + y, (cx, cy, c)).wait_send()
        if ns:
            local().wait()

    outs = [jax.ShapeDtypeStruct(s.shape, s.dtype) for s in slots]
    if ns:
        outs.append(jax.ShapeDtypeStruct((N_CHIPS,) + small.shape, small.dtype))
    return _Phase(list(slots) + ([small] if ns else []), outs, {k: k for k in range(nw)}, n_sem, start, finish, relay)


def _staged_gather_phases(slots, small=None):
    nw = len(slots)
    ns = 0 if small is None else 1
    fwd_base = 2 * nw

    def copies(ins, outs, send_sems, recv_sems, alone=False):
        x, y, c = _position()
        sibling = (x, y, 1 - c)
        home = (lambda k, slot: outs[k]) if alone else (lambda k, slot: outs[k].at[slot])

        def direct(j, k, slot, to):
            return pltpu.make_async_remote_copy(
                src_ref=_half(ins[k].at[slot], c), dst_ref=_half(outs[k].at[slot], c),
                send_sem=send_sems.at[j * nw + k], recv_sem=recv_sems.at[j * nw + k],
                device_id=to, device_id_type=MESH)

        def relayed(p, k, slot, to):
            return pltpu.make_async_remote_copy(
                src_ref=_part(ins[k].at[slot], c, (p, 2)), dst_ref=_part(home(k, slot), c, (p, 2)),
                send_sem=send_sems.at[p * nw + k], recv_sem=recv_sems.at[p * nw + k],
                device_id=to, device_id_type=MESH)

        def d2d(j, k, slot, half):
            return pltpu.make_async_remote_copy(
                src_ref=_half(home(k, slot), half), dst_ref=_half(home(k, slot), half),
                send_sem=send_sems.at[fwd_base + j * nw + k], recv_sem=recv_sems.at[fwd_base + j * nw + k],
                device_id=sibling, device_id_type=MESH)

        def small_copy(j, slot, to):
            return pltpu.make_async_remote_copy(
                src_ref=ins[nw], dst_ref=outs[nw].at[slot], send_sem=send_sems.at[4 * nw + j],
                recv_sem=recv_sems.at[4 * nw + j], device_id=to, device_id_type=MESH)

        def local():
            return pltpu.make_async_copy(ins[nw], outs[nw].at[2 * x + y], send_sems.at[4 * nw + 3])

        chips = _other_chips(x, y)
        return x, y, c, chips, [2 * cx + cy for cx, cy in chips], direct, relayed, d2d, small_copy, local

    def start_1(*refs):
        x, y, c, chips, _, direct, _, _, small_copy, local = copies(*refs)
        for j in range(2):
            for k in range(nw):
                direct(j, k, 2 * x + y, (*chips[j], c)).start()
        if ns:
            local().start()
            for j in range(3):
                small_copy(j, 2 * x + y, (*chips[j], c)).start()

    def relay_1(*refs):
        x, y, c, _, slot, direct, _, d2d, _, _ = copies(*refs)
        for j in range(2):
            for k in range(nw):
                direct(j, k, slot[j], (x, y, c)).wait_recv()
                d2d(j, k, slot[j], c).start()

    def finish_1(*refs):
        x, y, c, chips, slot, direct, _, d2d, small_copy, local = copies(*refs)
        if ns:
            for j in range(3):
                small_copy(j, slot[j], (x, y, c)).wait_recv()
        for j in range(2):
            for k in range(nw):
                d2d(j, k, slot[j], 1 - c).wait_recv()
        for j in range(2):
            for k in range(nw):
                direct(j, k, 2 * x + y, (*chips[j], c)).wait_send()
                d2d(j, k, slot[j], c).wait_send()
        if ns:
            for j in range(3):
                small_copy(j, 2 * x + y, (*chips[j], c)).wait_send()
            local().wait()

    shapes = [jax.ShapeDtypeStruct(s.shape, s.dtype) for s in slots]
    keep = {k: k for k in range(nw)}

    def second(landed, alone=False):
        def start_2(*refs):
            x, y, c, chips, slot, _, relayed, _, _, _ = copies(*refs, alone)
            for j in range(2):
                for k in range(nw):
                    relayed(1 - j, k, slot[j], (*chips[1 - j], c)).start()

        def relay_2(*refs):
            x, y, c, _, slot, _, relayed, d2d, _, _ = copies(*refs, alone)
            for k in range(nw):
                for p in range(2):
                    relayed(p, k, slot[2], (x, y, c)).wait_recv()
                d2d(0, k, slot[2], c).start()

        def finish_2(*refs):
            x, y, c, chips, slot, _, relayed, d2d, _, _ = copies(*refs, alone)
            for k in range(nw):
                d2d(0, k, slot[2], 1 - c).wait_recv()
            for k in range(nw):
                d2d(0, k, slot[2], c).wait_send()
                for j in range(2):
                    relayed(1 - j, k, slot[j], (*chips[1 - j], c)).wait_send()

        outs = [jax.ShapeDtypeStruct(s.shape[1:], s.dtype) for s in slots] if alone else shapes
        return _Phase(list(landed), outs, {} if alone else keep, 3 * nw, start_2, finish_2, relay_2)

    first_outs = shapes + ([jax.ShapeDtypeStruct((N_CHIPS,) + small.shape, small.dtype)] if ns else [])
    first = _Phase(list(slots) + ([small] if ns else []), first_outs, keep, 4 * nw + 4, start_1, finish_1, relay_1)
    return first, second


def _swap_phase(grads):
    nw = len(grads)

    def swaps(ins, outs, send_sems, recv_sems):
        x, y, c = _position()
        cps = []
        for k in range(nw):
            n = ins[k].shape[1] // 2
            cps.append(pltpu.make_async_remote_copy(
                src_ref=ins[k].at[pl.ds(0, N_CHIPS), pl.ds((1 - c) * n, n)], dst_ref=outs[k],
                send_sem=send_sems.at[k], recv_sem=recv_sems.at[k],
                device_id=(x, y, 1 - c), device_id_type=MESH))
        return cps

    def start(*refs):
        for cp in swaps(*refs):
            cp.start()

    def finish(*refs):
        for cp in swaps(*refs):
            cp.wait()

    halves = [jax.ShapeDtypeStruct((g.shape[0], g.shape[1] // 2, g.shape[2]), F32) for g in grads]
    return _Phase(grads, halves, {}, nw, start, finish)


def _add_halves(name, grad, theirs, place):
    s, half, w = theirs.shape
    tr, tc = _tile(half, 256), _tile(w, 2048)
    nrt = half // tr

    def body(place_ref, a_ref, b_ref, h_ref, f_ref):
        v = a_ref[...] + b_ref[...]
        h_ref[...] = v.astype(BF16)

        @pl.when(pl.program_id(2) == place_ref[0])
        def _():
            f_ref[...] = v

    return pl.pallas_call(
        body, name=name,
        grid_spec=pltpu.PrefetchScalarGridSpec(
            num_scalar_prefetch=1, grid=(nrt, w // tc, s),
            in_specs=[pl.BlockSpec((None, tr, tc), lambda i, j, q, p: (q, p[1] * nrt + i, j)),
                      pl.BlockSpec((None, tr, tc), lambda i, j, q, p: (q, i, j))],
            out_specs=[pl.BlockSpec((None, tr, tc), lambda i, j, q, p: (q, i, j)),
                       pl.BlockSpec((tr, tc), lambda i, j, q, p: (p[1] * nrt + i, j))]),
        out_shape=[jax.ShapeDtypeStruct(theirs.shape, BF16), jax.ShapeDtypeStruct((2 * half, w), F32)],
        compiler_params=_params("parallel", "parallel", "arbitrary"))(place, grad, theirs)


def _scatter_phase(own, pb, part=(0, 1), got=None):
    nw = len(own)

    def copies(ins, outs, send_sems, recv_sems):
        own_in, pbs = ins[:nw], ins[nw:2 * nw]
        own_out, gots = outs[:nw], outs[nw:]
        x, y, c = _position()
        sibling = (x, y, 1 - c)

        def d2d_own(k, half):
            return pltpu.make_async_remote_copy(
                src_ref=_part(own_in[k], half, part), dst_ref=_part(own_out[k], half, part),
                send_sem=send_sems.at[k], recv_sem=recv_sems.at[k], device_id=sibling, device_id_type=MESH)

        def ici(j, k, shard, to):
            n = pbs[k].shape[1] // part[1]
            rows = pl.ds(part[0] * n, (part[2] if len(part) == 3 else 1) * n)
            return pltpu.make_async_remote_copy(
                src_ref=pbs[k].at[shard, rows], dst_ref=_part(gots[k].at[j], c, part),
                send_sem=send_sems.at[nw + j * nw + k], recv_sem=recv_sems.at[nw + j * nw + k],
                device_id=to, device_id_type=MESH)

        def d2d(j, k, half):
            return pltpu.make_async_remote_copy(
                src_ref=_part(gots[k].at[j], half, part), dst_ref=_part(gots[k].at[j], half, part),
                send_sem=send_sems.at[4 * nw + j * nw + k], recv_sem=recv_sems.at[4 * nw + j * nw + k],
                device_id=sibling, device_id_type=MESH)

        return x, y, c, d2d_own, ici, d2d

    def start(*refs):
        x, y, c, d2d_own, ici, _ = copies(*refs)
        for k in range(nw):
            d2d_own(k, c).start()
        for j, (cx, cy) in enumerate(_other_chips(x, y)):
            for k in range(nw):
                ici(j, k, 2 * cx + cy, (cx, cy, c)).start()

    def relay(*refs):
        x, y, c, _, ici, d2d = copies(*refs)
        for j in range(3):
            for k in range(nw):
                ici(j, k, 2 * x + y, (x, y, c)).wait_recv()
                d2d(j, k, c).start()

    def finish(*refs):
        x, y, c, d2d_own, ici, d2d = copies(*refs)
        for k in range(nw):
            d2d_own(k, 1 - c).wait_recv()
        for j in range(3):
            for k in range(nw):
                d2d(j, k, 1 - c).wait_recv()
        for k in range(nw):
            d2d_own(k, c).wait_send()
        for j, (cx, cy) in enumerate(_other_chips(x, y)):
            for k in range(nw):
                ici(j, k, 2 * cx + cy, (cx, cy, c)).wait_send()
                d2d(j, k, c).wait_send()

    own_shape = [jax.ShapeDtypeStruct(o.shape, F32) for o in own]
    got_shape = [jax.ShapeDtypeStruct((3,) + o.shape, BF16) for o in own]
    aliases = {k: k for k in range(nw)}
    if got is not None:
        aliases.update({2 * nw + k: nw + k for k in range(nw)})
    return _Phase(list(own) + list(pb) + list(got or []), own_shape + got_shape, aliases, 7 * nw, start, finish, relay)


def _small_gather_phase(part):
    def copies(ins, outs, send_sems, recv_sems):
        x, y, c = _position()

        def peer(mask):
            return x ^ ((mask >> 2) & 1), y ^ ((mask >> 1) & 1), c ^ (mask & 1)

        def copy(mask, origin, to):
            ox, oy, oc = origin
            return pltpu.make_async_remote_copy(
                src_ref=ins[0], dst_ref=outs[0].at[4 * ox + 2 * oy + oc], send_sem=send_sems.at[mask - 1],
                recv_sem=recv_sems.at[mask - 1], device_id=to, device_id_type=MESH)

        local = pltpu.make_async_copy(ins[0], outs[0].at[4 * x + 2 * y + c], send_sems.at[N_DEV - 1])
        return (x, y, c), peer, copy, local

    def start(*refs):
        me, peer, copy, local = copies(*refs)
        local.start()
        for mask in range(1, N_DEV):
            copy(mask, me, peer(mask)).start()

    def finish(*refs):
        me, peer, copy, local = copies(*refs)
        for mask in range(1, N_DEV):
            copy(mask, peer(mask), me).wait_recv()
        for mask in range(1, N_DEV):
            copy(mask, me, peer(mask)).wait_send()
        local.wait()

    return _Phase([part], [jax.ShapeDtypeStruct((N_DEV,) + part.shape, F32)], {}, N_DEV, start, finish)


def _sum_slots(name, parts):
    s, r, c = parts.shape
    tr = _tile(r, 512, 8)

    def body(p_ref, o_ref):
        acc = p_ref[0]
        for q in range(1, s):
            acc = acc + p_ref[q]
        o_ref[...] = acc

    return pl.pallas_call(
        body, name=name, grid=(r // tr,), in_specs=[pl.BlockSpec((s, tr, c), lambda i: (0, i, 0))],
        out_specs=pl.BlockSpec((tr, c), lambda i: (i, 0)), out_shape=jax.ShapeDtypeStruct((r, c), F32),
        compiler_params=_params("parallel"))(parts)


def _adamw_math(w, g, m, v):
    m = ADAM_B1 * m + (1.0 - ADAM_B1) * g
    v = ADAM_B2 * v + (1.0 - ADAM_B2) * (g * g)
    m_hat = m / (1.0 - ADAM_B1 ** ADAM_STEP)
    v_hat = v / (1.0 - ADAM_B2 ** ADAM_STEP)
    delta = -ADAM_LR * (m_hat / (jnp.sqrt(v_hat) + ADAM_EPS) + ADAM_WD * w)
    return delta, m, v


def _adamw(name, w, m, v, own, got=None, odd=None, into=None, phases=()):
    r, c = w.shape
    rows = r if odd is None else r // 2
    tr, tc = _tile(r // 4, 256, 8) if odd is not None else _tile(r, 256, 8), _tile(c, 1024)
    bpq = (r // 4) // tr if odd is not None else 1
    pick = (lambda i: i) if odd is None else (lambda i: (2 * (i // bpq) + odd) * bpq + i % bpq)
    n_in = 4 + (0 if got is None else 1)

    def body(*refs):
        w_ref, m_ref, v_ref, own_ref = refs[:4]
        outs = refs[n_in + (0 if into is None else 4):]
        g = own_ref[...]
        if got is not None:
            for j in range(3):
                g = g + refs[4][j].astype(F32)
        delta, mn, vn = _adamw_math(w_ref[...], g, m_ref[...], v_ref[...])
        outs[0][...] = g
        outs[1][...] = delta
        outs[2][...] = mn
        outs[3][...] = vn

    full = pl.BlockSpec((tr, tc), lambda i, j: (pick(i), j))
    ins, args = [full] * 3 + [pl.BlockSpec((tr, tc), lambda i, j: (i, j))], [w, m, v, own]
    if got is not None:
        ins.append(pl.BlockSpec((3, tr, tc), lambda i, j: (0, i, j)))
        args.append(got)
    own_aliases = {}
    if into is not None:
        ins += [ANY] * 4
        args += list(into)
        own_aliases = {n_in + q: q for q in range(4)}
    grid = (rows // tr, c // tc)
    body, x_ins, x_outs, aliases, sems = _hosted(body, phases, grid, len(args), 4)
    aliases.update(own_aliases)
    res = pl.pallas_call(
        body, name=name, grid=grid, in_specs=ins + [ANY] * len(x_ins), out_specs=[full] * 4 + [ANY] * len(x_outs),
        out_shape=[jax.ShapeDtypeStruct((r, c), F32)] * 4 + x_outs, input_output_aliases=aliases,
        scratch_shapes=sems,
        compiler_params=_params(*(("arbitrary",) * 2 if phases else ("parallel",) * 2)))(*args, *x_ins)
    return (res[:4], list(res[4:])) if phases else res


def _pack(parts):
    return jnp.concatenate([p.reshape(-1) for p in parts]).reshape(-1, LANES)


def _unpack(packed, shapes):
    flat = packed.reshape(-1)
    out, off = [], 0
    for s in shapes:
        n = 1
        for d in s:
            n *= d
        out.append(flat[off:off + n].reshape(s))
        off += n
    return out


def kernel(x, pre_norm, post_norm, a_w_in, a_ln_g, a_ln_b, a_w_s, a_b_s, a_w_out, b_w_in, b_w_grp, b_b_grp, b_scale, b_w_out, loss_target, m_pre_norm, m_post_norm, m_a_w_in, m_a_ln_g, m_a_ln_b, m_a_w_s, m_a_b_s, m_a_w_out, m_b_w_in, m_b_w_grp, m_b_b_grp, m_b_scale, m_b_w_out, v_pre_norm, v_post_norm, v_a_w_in, v_a_ln_g, v_a_ln_b, v_a_w_s, v_a_b_s, v_a_w_out, v_b_w_in, v_b_w_grp, v_b_b_grp, v_b_scale, v_b_w_out):
    nb, seq, d = x.shape
    t = nb * seq
    e = a_ln_g.shape[1]
    nh, chunk = a_w_s.shape[1], a_w_s.shape[2]
    ng, rs, gw = b_w_grp.shape[1], b_w_grp.shape[2], b_w_grp.shape[3]
    wa, wb = a_w_in.shape[2], b_w_in.shape[2]
    cx, cy = lax.axis_index("x"), lax.axis_index("y")
    chip = 2 * cx + cy

    big_w = [a_w_in.reshape(d, wa), a_w_out.reshape(e // N_CHIPS, d), b_w_in.reshape(d, wb),
             b_w_grp.reshape(ng * rs, gw), b_w_out.reshape(e // N_CHIPS, d)]
    big_m = [m_a_w_in.reshape(d, wa), m_a_w_out.reshape(e // N_CHIPS, d), m_b_w_in.reshape(d, wb),
             m_b_w_grp.reshape(ng * rs, gw), m_b_w_out.reshape(e // N_CHIPS, d)]
    big_v = [v_a_w_in.reshape(d, wa), v_a_w_out.reshape(e // N_CHIPS, d), v_b_w_in.reshape(d, wb),
             v_b_w_grp.reshape(ng * rs, gw), v_b_w_out.reshape(e // N_CHIPS, d)]
    names = ["a_w_in", "a_w_out", "b_w_in", "b_w_grp", "b_w_out"]
    place = jnp.stack([chip, lax.axis_index("c")]).astype(jnp.int32)
    slots = [_cast_into_slot("cast_" + n, w, place, also_alone=(n == "a_w_in")) for n, w in zip(names, big_w)]
    slots[0], wa_own = slots[0]
    small_w = jnp.concatenate([b_b_grp.reshape(ng, rs), b_scale.reshape(ng, rs)], axis=0)
    xf = x.reshape(t, d)
    tgt = loss_target.reshape(t, d)
    g_pre0, g_pre1 = pre_norm[0:1], pre_norm[1:2]
    g_post0, g_post1 = post_norm[0:1], post_norm[1:2]
    w_s = a_w_s.reshape(nh, chunk, chunk)
    b_s3 = a_b_s.reshape(nh, chunk, 1)

    h0 = _rms_fwd("pre_norm0", xf, g_pre0)
    order = jnp.stack([chip, chip ^ 2, chip ^ 1, chip ^ 3]).astype(jnp.int32)
    near, far = _staged_gather_phases(slots[:1], small_w)
    near_o, far_o = _staged_gather_phases(slots[1:2])
    near_b, far_b = _staged_gather_phases(slots[2:3])
    pa, (wa_g, gsmall) = _mm_nn_cols("a_in_proj_own", h0, wa_own, [near], order=order, span=(0, 1))
    pa, (wa_far, wao_g) = _mm_nn_cols(
        "a_in_proj_near", h0, wa_g, [far([wa_g], alone=True), near_o], order=order, span=(1, 2), into=pa)
    pa, (wao_g,) = _mm_nn_cols("a_in_proj_far", h0, wa_far, [far_o([wao_g])], order=order, span=(3, 1), into=pa)
    wao_g = wao_g.reshape(e, d)
    b_grp_full = jnp.transpose(gsmall[:, :ng, :], (1, 0, 2)).reshape(1, e)
    scale_full = gsmall[:, ng:, :].reshape(1, e)
    mu, rstd = _a_stats("a_ln_stats", pa, e)
    y0, (wb_g,) = _a_gate_fwd("a_gate", pa, mu, rstd, a_ln_g, a_ln_b, w_s, b_s3, [near_b])
    m0, (wb_g,) = _mm_nn("a_out_proj", y0, wao_g, phases=[far_b([wb_g])])
    x1, h1 = _post_pre_fwd("post0_pre1", xf, m0, g_post0, g_pre1)
    pb, (wg_g, wbo_g) = _mm_nn_cols("b_in_proj", h1, wb_g, [_gather_phase(slots[3:5])])
    wg_g = wg_g.reshape(N_CHIPS, ng, rs, gw)
    wbo_g = wbo_g.reshape(e, d)
    pooled = _pool("b_pool", pb, e, seq, backward=False)
    mm = _grp_fwd("b_grp_proj", pooled, wg_g)
    y1 = _b_gate_fwd("b_gate", mm, pb, b_grp_full, scale_full)
    m1 = _mm_nn("b_out_proj", y1, wbo_g)
    loss, dx2, dm1, dg_post1 = _loss_post_bwd("loss_post1_bwd", x1, m1, g_post1, tgt)

    def chip_sum(n, g, theirs):
        pbk, ownk = _add_halves("chip_sum_" + n, g, theirs, place)
        return [ownk], [pbk]

    reduced = {}
    dy1 = _mm_nt("b_out_dx", dm1, wbo_g)
    g_wbo = _mm_tn("b_out_dw", y1, dm1).reshape(N_CHIPS, e // N_CHIPS, d)
    dmm, dzb, db_grp, dscale = _b_gate_bwd("b_gate_bwd", dy1, mm, pb, b_grp_full, scale_full)
    dpooled, (th,) = _grp_bwd_x("b_grp_dx", dmm, wg_g, [_swap_phase([g_wbo])])
    s_bo = chip_sum("b_w_out", g_wbo, th)
    g_wg = _grp_bwd_w("b_grp_dw", pooled, dmm, ng).reshape(N_CHIPS, ng * rs, gw)
    dp = _pool("b_pool_bwd", dpooled, e, seq, backward=True)
    dpb = jnp.concatenate([dp, dzb], axis=1)
    dh1, (own, got, th) = _mm_nt_cols("b_in_dx", dpb, wb_g, [_scatter_phase(*s_bo), _swap_phase([g_wg])])
    reduced["b_w_out"] = (own, got)
    s_g = chip_sum("b_w_grp", g_wg, th)
    g_wb, reduced["b_w_grp"] = _mm_tn_cols("b_in_dw", h1, dpb, wb, [_scatter_phase(*s_g)])
    dx1, dm0, dg_pre1, dg_post0 = _mid_bwd("pre1_post0_bwd", dx2, dh1, x1, g_pre1, m0, g_post0)
    dy0, (th,) = _mm_nt("a_out_dx", dm0, wao_g, phases=[_swap_phase([g_wb])])
    s_b = chip_sum("b_w_in", g_wb, th)
    g_wao, (own, got) = _mm_tn("a_out_dw", y0, dm0, phases=[_scatter_phase(*s_b, part=(0, 4))])
    g_wao = g_wao.reshape(N_CHIPS, e // N_CHIPS, d)
    (du, dz, dsv, c1, c2), (own, got, th) = _a_gate_bwd1(
        "a_gate_bwd1", pa, dy0, mu, rstd, a_ln_g, a_ln_b, w_s, b_s3,
        [_scatter_phase([own], s_b[1], part=(1, 4, 2), got=[got]), _swap_phase([g_wao])])
    s_ao = chip_sum("a_w_out", g_wao, th)
    (dv, dw_s, db_s, dln_g, dln_b), reduced["b_w_in"] = _a_gate_bwd2(
        "a_gate_bwd2", pa, dsv, mu, rstd, c1, c2, a_ln_g, a_ln_b, w_s,
        [_scatter_phase([own], s_b[1], part=(3, 4), got=[got])])
    dpa = jnp.concatenate([du, dv, dz], axis=1)
    g_lo, reduced["a_w_out"] = _mm_tn_cols("a_in_dw_even", h0, dpa, wa, [_scatter_phase(*s_ao)], odd=0)
    g_hi, (th,) = _mm_tn_cols("a_in_dw_odd", h0, dpa, wa, [_swap_phase([g_lo])], odd=1)
    s_lo = chip_sum("a_w_in_even", g_lo, th)
    dh0, (th, own_lo, got_lo) = _mm_nt_cols(
        "a_in_dx_top", dpa, wa_g, [_swap_phase([g_hi]), _scatter_phase(*s_lo, part=(0, 4, 3))], rows=(0, 2),
        elsewhere=(order, wa_far))
    s_hi = chip_sum("a_w_in_odd", g_hi, th)
    dh0, (own_lo, got_lo, own_hi, got_hi) = _mm_nt_cols(
        "a_in_dx_bottom", dpa, wa_g,
        [_scatter_phase([own_lo], s_lo[1], part=(3, 4), got=[got_lo]), _scatter_phase(*s_hi, part=(0, 4, 2))],
        rows=(1, 2), into=dh0, elsewhere=(order, wa_far))
    grad_x, dg_pre0 = _pre_bwd("pre0_bwd", dx1, dh0, xf, g_pre0)
    small_shapes = [(2, d), (2, d), (1, e), (1, e), (1, nh, chunk, chunk), (1, nh, chunk), (1, e), (1, e)]
    part = _pack([jnp.concatenate([dg_pre0, dg_pre1], axis=0), jnp.concatenate([dg_post0, dg_post1], axis=0),
                  dln_g, dln_b, dw_s, db_s, db_grp, dscale])
    big_out = []
    for k, n in enumerate(names):
        if n == "a_w_in":
            even, (own_hi, got_hi, small_parts) = _adamw(
                "adamw_a_w_in_even", big_w[k], big_m[k], big_v[k], own_lo, got_lo, odd=0,
                phases=[_scatter_phase([own_hi], s_hi[1], part=(2, 4, 2), got=[got_hi]), _small_gather_phase(part)])
            big_out.append(_adamw("adamw_a_w_in_odd", big_w[k], big_m[k], big_v[k], own_hi, got_hi, odd=1, into=even))
        else:
            big_out.append(_adamw("adamw_" + n, big_w[k], big_m[k], big_v[k], *reduced[n]))

    g_small = _sum_slots("sum_small_grads", small_parts)
    g_pre, g_post, g_lng, g_lnb, g_ws, g_bs, g_bgrp_full, g_scale_full = _unpack(g_small, small_shapes)
    g_bgrp = lax.dynamic_slice_in_dim(g_bgrp_full.reshape(ng, N_CHIPS, rs), chip, 1, axis=1).reshape(1, ng, rs)
    g_scale = lax.dynamic_slice_in_dim(g_scale_full.reshape(N_CHIPS, gw), chip, 1, axis=0)
    small_names = ["pre_norm", "post_norm", "a_ln_g", "a_ln_b", "a_w_s", "a_b_s", "b_b_grp", "b_scale"]
    small_g = [g_pre, g_post, g_lng, g_lnb, g_ws, g_bs, g_bgrp, g_scale]
    small_ws = [pre_norm, post_norm, a_ln_g, a_ln_b, a_w_s, a_b_s, b_b_grp, b_scale]
    small_ms = [m_pre_norm, m_post_norm, m_a_ln_g, m_a_ln_b, m_a_w_s, m_a_b_s, m_b_b_grp, m_b_scale]
    small_vs = [v_pre_norm, v_post_norm, v_a_ln_g, v_a_ln_b, v_a_w_s, v_a_b_s, v_b_b_grp, v_b_scale]
    packed = _adamw("adamw_small", _pack(small_ws), _pack(small_ms), _pack(small_vs), _pack(small_g))
    small_out = [_unpack(p, [w.shape for w in small_ws]) for p in packed]

    loss = lax.psum(loss[0, 0], ("x", "y", "c"))
    order = ["pre_norm", "post_norm", "a_w_in", "a_ln_g", "a_ln_b", "a_w_s", "a_b_s", "a_w_out", "b_w_in",
             "b_w_grp", "b_b_grp", "b_scale", "b_w_out"]
    big_shapes = dict(zip(names, [a_w_in.shape, a_w_out.shape, b_w_in.shape, b_w_grp.shape, b_w_out.shape]))
    outs = [loss, grad_x.reshape(nb, seq, d)]
    for kind in range(4):
        for n in order:
            if n in big_shapes:
                outs.append(big_out[names.index(n)][kind].reshape(big_shapes[n]))
            else:
                outs.append(small_out[kind][small_names.index(n)])
    return tuple(outs)
```
